```python
import math
import jax, jax.numpy as jnp
from jax import lax
import numpy as np

D_MODEL = 1024
BATCH = 16
SEQ = 2048
DEPTH = 2

CHUNK = 64
EPS = 1e-6
GM_WIDTH = D_MODEL // 2
GM_GROUPS = 4
GM_GROUP_CH = GM_WIDTH // GM_GROUPS
GM_BLOCK = 128
DA_HEADS = 4
DA_HEAD_DIM = 64
DA_QK_WIDTH = DA_HEADS * 2 * DA_HEAD_DIM
DA_V_DIM = 2 * DA_HEAD_DIM
DA_V_WIDTH = DA_HEADS * DA_V_DIM
DA_QBLOCK = 128
ROPE_THETA = 10000.0
DN_HEADS = 4
DN_HEAD_DIM = 128
DN_WIDTH = DN_HEADS * DN_HEAD_DIM
DN_CONV = 4
N_BRANCH = 3
IN_SPLITS = (GM_WIDTH, GM_WIDTH, DA_QK_WIDTH, DA_QK_WIDTH, DA_V_WIDTH,
             DN_WIDTH, DN_WIDTH, DN_WIDTH, DN_WIDTH, DN_HEADS, DN_HEADS,
             N_BRANCH * D_MODEL)
IN_WIDTH = (2 * GM_WIDTH + 2 * DA_QK_WIDTH + DA_V_WIDTH + 4 * DN_WIDTH
            + 2 * DN_HEADS + N_BRANCH * D_MODEL)
FF_DENSE = 256 * ((8 * D_MODEL // 3 + 255) // 256)
N_EXPERTS = 8
TOP_K = 2
FF_EXPERT = 7 * D_MODEL // 2
MOE_BLOCK = 256
N_DENSE = (DEPTH + 1) // 2
N_MOE = DEPTH // 2

kernel_name = 'hybrid_gmlp_diffattn_gdn_moe_adaln'


def lambda_init_fn(layer):
    return 0.8 - 0.6 * math.exp(-0.3 * layer)


def rmsnorm(x, g):
    xf = x.astype(jnp.float32)
    y = xf * lax.rsqrt(jnp.mean(xf * xf, axis=-1, keepdims=True) + EPS)
    return (y * g.astype(jnp.float32)).astype(x.dtype)


def layernorm(x, g, b):
    xf = x.astype(jnp.float32)
    mu = jnp.mean(xf, axis=-1, keepdims=True)
    xc = xf - mu
    var = jnp.mean(xc * xc, axis=-1, keepdims=True)
    y = xc * lax.rsqrt(var + EPS) * g.astype(jnp.float32) + b.astype(jnp.float32)
    return y.astype(x.dtype)


def rope(x, cos, sin):
    x1, x2 = jnp.split(x, 2, axis=-1)
    return jnp.concatenate([x1 * cos - x2 * sin, x2 * cos + x1 * sin], axis=-1)


def causal_depthwise_conv(x, w):
    C = x.shape[-1]
    return lax.conv_general_dilated(
        x, w[:, None, :].astype(x.dtype), window_strides=(1,),
        padding=[(DN_CONV - 1, 0)], dimension_numbers=('NWC', 'WIO', 'NWC'),
        feature_group_count=C)


def gmlp_branch(u, v, ln_g, ln_b, w_s, b_s):
    B, S, _ = u.shape
    u = jax.nn.gelu(u, approximate=False)
    v = layernorm(jax.nn.gelu(v, approximate=False), ln_g, ln_b)
    vb = v.reshape(B, S // GM_BLOCK, GM_BLOCK, GM_GROUPS, GM_GROUP_CH)
    pos_chunk = jnp.arange(GM_BLOCK) // CHUNK
    allowed = pos_chunk[None, :] <= pos_chunk[:, None]
    w = jnp.where(allowed, w_s, 0.0).astype(v.dtype)
    sv = jnp.einsum('gij,bnjgc->bnigc', w, vb) + b_s.T[:, :, None].astype(v.dtype)
    return u * sv.reshape(B, S, GM_WIDTH)


def diff_attention(q, k, v, cos, sin, lam_p, subln_g, lambda_init):
    B, S, _ = q.shape
    q = q.reshape(B, S, DA_HEADS, 2, DA_HEAD_DIM)
    k = k.reshape(B, S, DA_HEADS, 2, DA_HEAD_DIM)
    v = v.reshape(B, S, DA_HEADS, DA_V_DIM)
    cs = cos[:, :, None, None, :]
    sn = sin[:, :, None, None, :]
    q = rope(q, cs, sn) * (DA_HEAD_DIM ** -0.5)
    k = rope(k, cs, sn)
    lp = lam_p.astype(jnp.float32)
    lam = jnp.exp(jnp.sum(lp[0] * lp[1])) - jnp.exp(jnp.sum(lp[2] * lp[3])) + lambda_init
    outs = []
    for qb in range(S // DA_QBLOCK):
        q0 = qb * DA_QBLOCK
        kend = q0 + DA_QBLOCK
        s = jnp.einsum('bqhpd,bkhpd->bhpqk', q[:, q0:kend], k[:, :kend]).astype(jnp.float32)
        q_chunk = (q0 + jnp.arange(DA_QBLOCK)) // CHUNK
        k_chunk = jnp.arange(kend) // CHUNK
        s = jnp.where(k_chunk[None, :] <= q_chunk[:, None], s, -jnp.inf)
        a = jax.nn.softmax(s, axis=-1)
        a = a[:, :, 0] - lam * a[:, :, 1]
        outs.append(jnp.einsum('bhqk,bkhe->bqhe', a.astype(v.dtype), v[:, :kend]))
    o = jnp.concatenate(outs, axis=1)
    o = rmsnorm(o, subln_g) * (1.0 - lambda_init)
    return o.reshape(B, S, DA_V_WIDTH)


def gated_delta_rule(q, k, v, g, beta):
    B, S, H, dk = q.shape
    dv = v.shape[-1]
    N = S // CHUNK

    def blocks(t):
        return jnp.moveaxis(t.reshape((B, N, CHUNK) + t.shape[2:]), 3, 1)

    q, k, v, g, beta = blocks(q), blocks(k), blocks(v), blocks(g), blocks(beta)
    g = jnp.cumsum(g, axis=-1)
    idx = jnp.arange(CHUNK)
    tril = idx[:, None] >= idx[None, :]
    strict = idx[:, None] > idx[None, :]
    decay = jnp.exp(jnp.where(tril, g[..., :, None] - g[..., None, :], -jnp.inf))
    kb = k * beta[..., None]
    A = jnp.where(strict, jnp.einsum('bhnid,bhnjd->bhnij', kb, k) * decay, 0.0)
    L = A + jnp.eye(CHUNK, dtype=jnp.float32)
    rhs = jnp.concatenate([v * beta[..., None], kb * jnp.exp(g)[..., None]], axis=-1)
    sol = lax.linalg.triangular_solve(L, rhs, left_side=True, lower=True, unit_diagonal=True)
    u, w = sol[..., :dv], sol[..., dv:]
    qk = jnp.einsum('bhnid,bhnjd->bhnij', q, k) * decay
    qg = q * jnp.exp(g)[..., None]
    kd = k * jnp.exp(g[..., -1:] - g)[..., None]
    gl = jnp.exp(g[..., -1])

    def step(state, inp):
        qg_c, kd_c, u_c, w_c, qk_c, gl_c = inp
        v_new = u_c - jnp.einsum('bhck,bhkv->bhcv', w_c, state)
        o = (jnp.einsum('bhck,bhkv->bhcv', qg_c, state)
             + jnp.einsum('bhij,bhjv->bhiv', qk_c, v_new))
        state = state * gl_c[..., None, None] + jnp.einsum('bhck,bhcv->bhkv', kd_c, v_new)
        return state, o

    xs = tuple(jnp.moveaxis(t, 2, 0) for t in (qg, kd, u, w, qk, gl))
    state0 = jnp.zeros((B, H, dk, dv), jnp.float32)
    _, o = lax.scan(step, state0, xs)
    return jnp.transpose(o, (1, 0, 3, 2, 4)).reshape(B, S, H, dv)


def deltanet_branch(q, k, v, z, a, b, conv_w, a_log, dt_bias, norm_g):
    B, S, _ = q.shape
    qkv = jax.nn.silu(causal_depthwise_conv(jnp.concatenate([q, k, v], axis=-1), conv_w))
    q, k, v = jnp.split(qkv.astype(jnp.float32), 3, axis=-1)

    def heads(t):
        return t.reshape(B, S, DN_HEADS, DN_HEAD_DIM)

    q, k, v = heads(q), heads(k), heads(v)
    q = q * lax.rsqrt(jnp.sum(q * q, axis=-1, keepdims=True) + EPS) * (DN_HEAD_DIM ** -0.5)
    k = k * lax.rsqrt(jnp.sum(k * k, axis=-1, keepdims=True) + EPS)
    beta = jax.nn.sigmoid(b.astype(jnp.float32))
    g = -jnp.exp(a_log.astype(jnp.float32)) * jax.nn.softplus(
        a.astype(jnp.float32) + dt_bias.astype(jnp.float32))
    o = gated_delta_rule(q, k, v, g, beta)
    o = rmsnorm(o, norm_g) * jax.nn.silu(heads(z).astype(jnp.float32))
    return o.reshape(B, S, DN_WIDTH).astype(z.dtype)


def hybrid_mixer(h, cos, sin, w_in, gm_ln_g, gm_ln_b, gm_w_s, gm_b_s, da_lambda,
                 da_subln_g, dn_conv_w, dn_a_log, dn_dt_bias, dn_norm_g,
                 w_br_gm, w_br_da, w_br_dn, w_out, lambda_init):
    B, S, D = h.shape
    proj = jnp.einsum('bsd,de->bse', h, w_in)
    cuts = [int(i) for i in np.cumsum(IN_SPLITS)[:-1]]
    (gm_u, gm_v, da_q, da_k, da_v, dn_q, dn_k, dn_v, dn_z, dn_a, dn_b,
     gate_logits) = jnp.split(proj, cuts, axis=-1)
    y_gm = gmlp_branch(gm_u, gm_v, gm_ln_g, gm_ln_b, gm_w_s, gm_b_s)
    y_da = diff_attention(da_q, da_k, da_v, cos, sin, da_lambda, da_subln_g, lambda_init)
    y_dn = deltanet_branch(dn_q, dn_k, dn_v, dn_z, dn_a, dn_b, dn_conv_w,
                           dn_a_log, dn_dt_bias, dn_norm_g)
    gates = jax.nn.sigmoid(gate_logits.astype(jnp.float32)).astype(h.dtype)
    gates = gates.reshape(B, S, N_BRANCH, D)
    merged = (gates[:, :, 0] * jnp.einsum('bsk,kd->bsd', y_gm, w_br_gm)
              + gates[:, :, 1] * jnp.einsum('bsk,kd->bsd', y_da, w_br_da)
              + gates[:, :, 2] * jnp.einsum('bsk,kd->bsd', y_dn, w_br_dn))
    return jnp.einsum('bsd,de->bse', merged, w_out)


def swiglu(h, w_gate, w_up, w_down):
    a = jnp.einsum('bsd,df->bsf', h, w_gate)
    b = jnp.einsum('bsd,df->bsf', h, w_up)
    return jnp.einsum('bsf,fd->bsd', jax.nn.silu(a) * b, w_down)


def moe_swiglu(h, w_router, w_gate, w_up, w_down):
    B, S, D = h.shape
    T = B * S
    TK = T * TOP_K
    hf = h.reshape(T, D)
    logits = jnp.einsum('td,de->te', hf, w_router).astype(jnp.float32)
    top_logit, top_e = lax.top_k(logits, TOP_K)
    top_w = jax.nn.softmax(top_logit, axis=-1)
    flat_e = top_e.reshape(TK).astype(jnp.int32)
    flat_tok = jnp.arange(TK, dtype=jnp.int32) // TOP_K
    flat_w = top_w.reshape(TK)
    order = jnp.argsort(flat_e)
    sorted_e = flat_e[order]
    counts = jax.ops.segment_sum(jnp.ones((TK,), jnp.int32), flat_e, num_segments=N_EXPERTS)
    padded = (counts + MOE_BLOCK - 1) // MOE_BLOCK * MOE_BLOCK
    pad_end = jnp.cumsum(padded)
    pad_start = pad_end - padded
    start = jnp.cumsum(counts) - counts
    dest = pad_start[sorted_e] + jnp.arange(TK, dtype=jnp.int32) - start[sorted_e]
    P = TK + N_EXPERTS * MOE_BLOCK
    n_blk = P // MOE_BLOCK
    slot_tok = jnp.zeros((P,), jnp.int32).at[dest].set(flat_tok[order])
    slot_w = jnp.zeros((P,), jnp.float32).at[dest].set(flat_w[order])
    blk_e = jnp.minimum(
        jnp.searchsorted(pad_end, jnp.arange(n_blk, dtype=jnp.int32) * MOE_BLOCK, side='right'),
        N_EXPERTS - 1)
    xs = hf[slot_tok].reshape(n_blk, MOE_BLOCK, D)

    def expert_block(args):
        xb, e = args
        a = xb @ w_gate[e]
        b = xb @ w_up[e]
        return (jax.nn.silu(a) * b) @ w_down[e]

    ys = lax.map(expert_block, (xs, blk_e)).reshape(P, D)
    out = jnp.zeros((T, D), h.dtype).at[slot_tok].add(ys * slot_w[:, None].astype(h.dtype))
    return out.reshape(B, S, D)


def setup_inputs(seed: int = 0) -> dict:
    key = jax.random.key(seed)
    ks = jax.random.split(key, 32)
    D = D_MODEL

    def nrm(i, shape, scale):
        return jax.random.normal(ks[i], shape, jnp.float32) * scale

    x = nrm(0, (BATCH, SEQ, D), 1.0)
    c = nrm(1, (BATCH, D), 1.0)
    offsets = jax.random.randint(ks[2], (BATCH, 1), 0, 64, dtype=jnp.int32) * CHUNK
    positions = jnp.arange(SEQ, dtype=jnp.int32)[None, :] + offsets
    a_log = jnp.log(jax.random.uniform(ks[15], (DEPTH, DN_HEADS), jnp.float32, 1.0, 16.0))
    dt = jnp.exp(jax.random.uniform(ks[16], (DEPTH, DN_HEADS), jnp.float32,
                                    math.log(1e-3), math.log(1e-1)))
    dt_bias = dt + jnp.log(-jnp.expm1(-dt))
    return {
        'x': x,
        'c': c,
        'positions': positions,
        'norm1_g': 1.0 + nrm(3, (DEPTH, D), 0.05),
        'norm2_g': 1.0 + nrm(4, (DEPTH, D), 0.05),
        'w_mod': nrm(5, (DEPTH, D, 6 * D), 0.5 * D ** -0.5),
        'b_mod': nrm(6, (DEPTH, 6 * D), 0.02),
        'w_in': nrm(7, (DEPTH, D, IN_WIDTH), D ** -0.5),
        'gm_ln_g': 1.0 + nrm(8, (DEPTH, GM_WIDTH), 0.05),
        'gm_ln_b': nrm(9, (DEPTH, GM_WIDTH), 0.02),
        'gm_w_s': nrm(10, (DEPTH, GM_GROUPS, GM_BLOCK, GM_BLOCK), GM_BLOCK ** -0.5),
        'gm_b_s': 1.0 + nrm(11, (DEPTH, GM_GROUPS, GM_BLOCK), 0.1),
        'da_lambda': nrm(12, (DEPTH, 4, DA_HEAD_DIM), 0.1),
        'da_subln_g': 1.0 + nrm(13, (DEPTH, DA_V_DIM), 0.05),
        'dn_conv_w': nrm(14, (DEPTH, DN_CONV, 3 * DN_WIDTH), DN_CONV ** -0.5),
        'dn_a_log': a_log,
        'dn_dt_bias': dt_bias,
        'dn_norm_g': 1.0 + nrm(17, (DEPTH, DN_HEAD_DIM), 0.05),
        'w_br_gm': nrm(18, (DEPTH, GM_WIDTH, D), GM_WIDTH ** -0.5),
        'w_br_da': nrm(19, (DEPTH, DA_V_WIDTH, D), DA_V_WIDTH ** -0.5),
        'w_br_dn': nrm(20, (DEPTH, DN_WIDTH, D), DN_WIDTH ** -0.5),
        'w_out': nrm(21, (DEPTH, D, D), D ** -0.5),
        'ffn_w_gate': nrm(22, (N_DENSE, D, FF_DENSE), D ** -0.5),
        'ffn_w_up': nrm(23, (N_DENSE, D, FF_DENSE), D ** -0.5),
        'ffn_w_down': nrm(24, (N_DENSE, FF_DENSE, D), FF_DENSE ** -0.5),
        'moe_w_router': nrm(25, (N_MOE, D, N_EXPERTS), D ** -0.5),
        'moe_w_gate': nrm(26, (N_MOE, N_EXPERTS, D, FF_EXPERT), D ** -0.5),
        'moe_w_up': nrm(27, (N_MOE, N_EXPERTS, D, FF_EXPERT), D ** -0.5),
        'moe_w_down': nrm(28, (N_MOE, N_EXPERTS, FF_EXPERT, D), FF_EXPERT ** -0.5),
        'final_g': 1.0 + nrm(29, (D,), 0.05),
    }


def reference(x, c, positions, norm1_g, norm2_g, w_mod, b_mod, w_in, gm_ln_g, gm_ln_b,
              gm_w_s, gm_b_s, da_lambda, da_subln_g, dn_conv_w, dn_a_log, dn_dt_bias,
              dn_norm_g, w_br_gm, w_br_da, w_br_dn, w_out, ffn_w_gate, ffn_w_up,
              ffn_w_down, moe_w_router, moe_w_gate, moe_w_up, moe_w_down, final_g):
    inv_freq = ROPE_THETA ** (-jnp.arange(0, DA_HEAD_DIM, 2, dtype=jnp.float32) / DA_HEAD_DIM)
    ang = positions.astype(jnp.float32)[..., None] * inv_freq
    cos = jnp.cos(ang).astype(x.dtype)
    sin = jnp.sin(ang).astype(x.dtype)
    c_act = jax.nn.silu(c)
    for layer in range(DEPTH):
        mod = c_act @ w_mod[layer] + b_mod[layer]
        shift1, scale1, gate1, shift2, scale2, gate2 = jnp.split(mod, 6, axis=-1)
        h = rmsnorm(x, norm1_g[layer]) * (1.0 + scale1[:, None]) + shift1[:, None]
        y = hybrid_mixer(h, cos, sin, w_in[layer], gm_ln_g[layer], gm_ln_b[layer],
                         gm_w_s[layer], gm_b_s[layer], da_lambda[layer], da_subln_g[layer],
                         dn_conv_w[layer], dn_a_log[layer], dn_dt_bias[layer],
                         dn_norm_g[layer], w_br_gm[layer], w_br_da[layer], w_br_dn[layer],
                         w_out[layer], lambda_init_fn(layer))
        x = x + gate1[:, None] * y
        h = rmsnorm(x, norm2_g[layer]) * (1.0 + scale2[:, None]) + shift2[:, None]
        if layer % 2 == 0:
            f = swiglu(h, ffn_w_gate[layer // 2], ffn_w_up[layer // 2], ffn_w_down[layer // 2])
        else:
            f = moe_swiglu(h, moe_w_router[layer // 2], moe_w_gate[layer // 2],
                           moe_w_up[layer // 2], moe_w_down[layer // 2])
        x = x + gate2[:, None] * f
    return rmsnorm(x, final_g)
```

```python
import functools
import math

import jax
import jax.numpy as jnp
import numpy as np
from jax import lax
from jax.experimental import pallas as pl
from jax.experimental.pallas import tpu as pltpu

D_MODEL = 1024
SEQ = 2048
DEPTH = 2
CHUNK = 64
EPS = 1e-6
GM_WIDTH = D_MODEL // 2
GM_GROUPS = 4
GM_BLOCK = 128
DA_HEADS = 4
DA_HEAD_DIM = 64
DA_V_DIM = 2 * DA_HEAD_DIM
ROPE_THETA = 10000.0
DN_HEADS = 4
DN_HEAD_DIM = 128
DN_WIDTH = DN_HEADS * DN_HEAD_DIM
DN_CONV = 4
N_EXPERTS = 8
FF_EXPERT = 7 * D_MODEL // 2

LANES = 128
VMEM_LIMIT = 56 * 1024 * 1024

C_GM, C_QK, C_V, C_DN, C_GATE, C_AB = 0, 1024, 2048, 2560, 4608, 7680
W_ALL = C_AB + LANES
PROJ_CHUNK = 512

TM_PROJ = 512
TM_GM = 512
TQ = 256
TM_MERGE = 512
TM_FFN = 512
TM_ROUTE = 512
TM_MOE = 512
FC_MOE = 512
TM_DISP = 256
TM_COMB = 256

BF16 = jnp.bfloat16
F32 = jnp.float32


def _params(*sem):
    return pltpu.CompilerParams(dimension_semantics=sem, vmem_limit_bytes=VMEM_LIMIT)


def _dot(a, b):
    return jnp.dot(a, b, preferred_element_type=F32)


def _dot_nt(a, b):
    return lax.dot_general(a, b, (((1,), (1,)), ((), ())), preferred_element_type=F32)


def _dot_tn(a, b):
    return lax.dot_general(a, b, (((0,), (0,)), ((), ())), preferred_element_type=F32)


def _dot_hi(a, b):
    return jnp.dot(a, b, preferred_element_type=F32, precision=lax.Precision.HIGHEST)


def _rms(x, g):
    return x * lax.rsqrt(jnp.mean(x * x, axis=-1, keepdims=True) + EPS) * g


def _silu(x):
    return x * jax.nn.sigmoid(x)


def _mod_kernel(c_ref, w_ref, b_ref, o_ref):
    c = c_ref[...]
    o_ref[0] = _dot_hi(_silu(c), w_ref[0]) + b_ref[0]


def _modulation(c, w_mod, b_mod):
    B, D = c.shape
    L, _, N = w_mod.shape
    tn = 1536
    return pl.pallas_call(
        _mod_kernel,
        grid=(L, N // tn),
        in_specs=[pl.BlockSpec((B, D), lambda l, j: (0, 0)),
                  pl.BlockSpec((1, D, tn), lambda l, j: (l, 0, j)),
                  pl.BlockSpec((1, 1, tn), lambda l, j: (l, 0, j))],
        out_specs=pl.BlockSpec((1, B, tn), lambda l, j: (l, 0, j)),
        out_shape=jax.ShapeDtypeStruct((L, B, N), F32),
        compiler_params=_params("arbitrary", "arbitrary"),
        name="modulation",
    )(c, w_mod, b_mod.reshape(L, 1, N))


def _rope_tab_kernel(pos_ref, inv_ref, sgn_ref, cos_ref, sin_ref):
    ang = pos_ref[...].astype(F32) * inv_ref[...]
    cos_ref[...] = jnp.cos(ang)
    sin_ref[...] = jnp.sin(ang) * sgn_ref[...]


def _rope_tables(positions):
    T = positions.size
    inv_freq = ROPE_THETA ** (-jnp.arange(0, DA_HEAD_DIM, 2, dtype=F32) / DA_HEAD_DIM)
    inv = jnp.tile(inv_freq, LANES // (DA_HEAD_DIM // 2)).reshape(1, LANES)
    half = DA_HEAD_DIM // 2
    sgn = np.tile(np.concatenate([-np.ones(half), np.ones(half)]), LANES // DA_HEAD_DIM)
    sgn = jnp.asarray(sgn, F32).reshape(1, LANES)
    tm = 1024
    return pl.pallas_call(
        _rope_tab_kernel,
        grid=(T // tm,),
        in_specs=[pl.BlockSpec((tm, 1), lambda i: (i, 0)),
                  pl.BlockSpec((1, LANES), lambda i: (0, 0)),
                  pl.BlockSpec((1, LANES), lambda i: (0, 0))],
        out_specs=[pl.BlockSpec((tm, LANES), lambda i: (i, 0))] * 2,
        out_shape=[jax.ShapeDtypeStruct((T, LANES), F32)] * 2,
        compiler_params=_params("arbitrary"),
        name="rope_tables",
    )(positions.reshape(T, 1), inv, sgn)


def _rope_rows(x, cos, sin):
    lane = lax.broadcasted_iota(jnp.int32, x.shape, 1)
    first = (lane % DA_HEAD_DIM) < (DA_HEAD_DIM // 2)
    partner = jnp.where(first, pltpu.roll(x, LANES - DA_HEAD_DIM // 2, axis=1),
                        pltpu.roll(x, DA_HEAD_DIM // 2, axis=1))
    return x * cos + partner * sin


def _proj_kernel(x_ref, mod_ref, g_ref, w_ref, cos_ref, sin_ref,
                 gm_ref, qk_ref, v_ref, dn_ref, gate_ref, ab_ref):
    x = x_ref[...]
    shift = mod_ref[0, 0:1, :]
    scale = mod_ref[0, 1:2, :]
    h = (_rms(x, g_ref[...]) * (1.0 + scale) + shift).astype(BF16)
    cos = cos_ref[...]
    sin = sin_ref[...]
    q_scale = DA_HEAD_DIM ** -0.5

    def rope_chunk(y, mult):
        parts = [_rope_rows(y[:, j * LANES:(j + 1) * LANES], cos, sin) * mult
                 for j in range(PROJ_CHUNK // LANES)]
        return jnp.concatenate(parts, axis=1)

    for c0 in range(0, C_AB, PROJ_CHUNK):
        y = _dot(h, w_ref[:, c0:c0 + PROJ_CHUNK])
        if c0 < C_QK:
            gm_ref[:, c0:c0 + PROJ_CHUNK] = y.astype(BF16)
        elif c0 < C_V:
            mult = q_scale if c0 == C_QK else 1.0
            qk_ref[:, c0 - C_QK:c0 - C_QK + PROJ_CHUNK] = rope_chunk(y, mult).astype(BF16)
        elif c0 < C_DN:
            v_ref[...] = y.astype(BF16)
        elif c0 < C_GATE:
            dn_ref[:, c0 - C_DN:c0 - C_DN + PROJ_CHUNK] = y.astype(BF16)
        else:
            gate_ref[:, c0 - C_GATE:c0 - C_GATE + PROJ_CHUNK] = y.astype(BF16)
    ab_ref[...] = _dot(h, w_ref[:, C_AB:W_ALL])


def _in_projection(x, mod, g, w_all, cos_t, sin_t):
    T, D = x.shape
    tm = TM_PROJ
    per_b = SEQ // tm
    row = lambda i: (i, 0)
    widths = (C_QK - C_GM, C_V - C_QK, C_DN - C_V, C_GATE - C_DN, C_AB - C_GATE)
    return pl.pallas_call(
        _proj_kernel,
        grid=(T // tm,),
        in_specs=[pl.BlockSpec((tm, D), row),
                  pl.BlockSpec((1, 6, D), lambda i: (i // per_b, 0, 0)),
                  pl.BlockSpec((1, D), lambda i: (0, 0)),
                  pl.BlockSpec((D, W_ALL), lambda i: (0, 0), pipeline_mode=pl.Buffered(1)),
                  pl.BlockSpec((tm, LANES), row),
                  pl.BlockSpec((tm, LANES), row)],
        out_specs=[pl.BlockSpec((tm, w), row) for w in widths] + [pl.BlockSpec((tm, LANES), row)],
        out_shape=[jax.ShapeDtypeStruct((T, w), BF16) for w in widths]
                  + [jax.ShapeDtypeStruct((T, LANES), F32)],
        compiler_params=_params("arbitrary"),
        name="in_projection",
    )(x, mod, g, w_all, cos_t, sin_t)


def _gelu(x):
    return 0.5 * x * (1.0 + lax.erf(x * np.float32(math.sqrt(0.5))))


def _gmlp_kernel(uv_ref, lng_ref, lnb_ref, ws_ref, bs_ref, o_ref):
    u = _gelu(uv_ref[:, :GM_WIDTH].astype(F32))
    v = _gelu(uv_ref[:, GM_WIDTH:].astype(F32))
    mu = jnp.mean(v, axis=-1, keepdims=True)
    vc = v - mu
    var = jnp.mean(vc * vc, axis=-1, keepdims=True)
    v = (vc * lax.rsqrt(var + EPS) * lng_ref[...] + lnb_ref[...]).astype(BF16)
    ri = lax.broadcasted_iota(jnp.int32, (GM_BLOCK, GM_BLOCK), 0) // CHUNK
    ci = lax.broadcasted_iota(jnp.int32, (GM_BLOCK, GM_BLOCK), 1) // CHUNK
    allowed = ci <= ri
    gc = GM_WIDTH // GM_GROUPS
    for g in range(GM_GROUPS):
        w = jnp.where(allowed, ws_ref[g], 0.0).astype(BF16)
        bias = bs_ref[g]
        for r in range(uv_ref.shape[0] // GM_BLOCK):
            rows = slice(r * GM_BLOCK, (r + 1) * GM_BLOCK)
            cols = slice(g * gc, (g + 1) * gc)
            sv = _dot(w, v[rows, cols]) + bias
            o_ref[rows, cols] = (u[rows, cols] * sv).astype(BF16)


def _gmlp(uv, ln_g, ln_b, w_s, b_s):
    T = uv.shape[0]
    tm = TM_GM
    return pl.pallas_call(
        _gmlp_kernel,
        grid=(T // tm,),
        in_specs=[pl.BlockSpec((tm, 2 * GM_WIDTH), lambda i: (i, 0)),
                  pl.BlockSpec((1, GM_WIDTH), lambda i: (0, 0)),
                  pl.BlockSpec((1, GM_WIDTH), lambda i: (0, 0)),
                  pl.BlockSpec((GM_GROUPS, GM_BLOCK, GM_BLOCK), lambda i: (0, 0, 0)),
                  pl.BlockSpec((GM_GROUPS, GM_BLOCK, 1), lambda i: (0, 0, 0))],
        out_specs=pl.BlockSpec((tm, GM_WIDTH), lambda i: (i, 0)),
        out_shape=jax.ShapeDtypeStruct((T, GM_WIDTH), BF16),
        compiler_params=_params("arbitrary"),
        name="gmlp",
    )(uv, ln_g.reshape(1, -1), ln_b.reshape(1, -1), w_s, b_s.reshape(GM_GROUPS, GM_BLOCK, 1))


def _attn_kernel(lam_ref, g_ref, q_ref, k_ref, v_ref, o_ref, *, lambda_init):
    i = pl.program_id(2)
    lp = lam_ref[...]
    lam = (jnp.exp(jnp.sum(lp[0:1] * lp[1:2], axis=-1, keepdims=True))
           - jnp.exp(jnp.sum(lp[2:3] * lp[3:4], axis=-1, keepdims=True)) + lambda_init)
    q = q_ref[...]
    lane = lax.broadcasted_iota(jnp.int32, q.shape, 1)
    zero = jnp.zeros_like(q)
    qs = (jnp.where(lane < DA_HEAD_DIM, q, zero), jnp.where(lane >= DA_HEAD_DIM, q, zero))

    def block(kb, carry, diagonal):
        k = k_ref[pl.ds(pl.multiple_of(kb * TQ, TQ), TQ), :]
        v = v_ref[pl.ds(pl.multiple_of(kb * TQ, TQ), TQ), :]
        out = []
        for p in range(2):
            m, l, acc = carry[p]
            s = _dot_nt(qs[p], k)
            if diagonal:
                ri = lax.broadcasted_iota(jnp.int32, s.shape, 0) // CHUNK
                ci = lax.broadcasted_iota(jnp.int32, s.shape, 1) // CHUNK
                s = jnp.where(ci <= ri, s, -jnp.inf)
            m_new = jnp.maximum(m, jnp.max(s, axis=-1, keepdims=True))
            alpha = jnp.exp(m - m_new)
            e = jnp.exp(s - m_new)
            l = alpha * l + jnp.sum(e, axis=-1, keepdims=True)
            acc = alpha * acc + _dot(e.astype(BF16), v)
            out.append((m_new, l, acc))
        return tuple(out)

    init = tuple((jnp.full((TQ, 1), -jnp.inf, F32), jnp.zeros((TQ, 1), F32),
                  jnp.zeros((TQ, DA_V_DIM), F32)) for _ in range(2))
    carry = lax.fori_loop(0, i, lambda kb, c: block(kb, c, False), init)
    (_, l0, a0), (_, l1, a1) = block(i, carry, True)
    o = a0 / l0 - lam * (a1 / l1)
    o_ref[...] = (_rms(o, g_ref[...]) * (1.0 - lambda_init)).astype(BF16)


def _diff_attention(qk, v, lam_p, subln_g, lambda_init, batch):
    T = qk.shape[0]
    nq = SEQ // TQ
    kernel = functools.partial(_attn_kernel, lambda_init=lambda_init)
    return pl.pallas_call(
        kernel,
        grid=(batch, DA_HEADS, nq),
        in_specs=[pl.BlockSpec((4, DA_HEAD_DIM), lambda b, h, i: (0, 0)),
                  pl.BlockSpec((1, DA_V_DIM), lambda b, h, i: (0, 0)),
                  pl.BlockSpec((TQ, LANES), lambda b, h, i: (b * nq + i, h)),
                  pl.BlockSpec((SEQ, LANES), lambda b, h, i: (b, DA_HEADS + h)),
                  pl.BlockSpec((SEQ, LANES), lambda b, h, i: (b, h))],
        out_specs=pl.BlockSpec((TQ, LANES), lambda b, h, i: (b * nq + i, h)),
        out_shape=jax.ShapeDtypeStruct((T, DA_HEADS * DA_V_DIM), BF16),
        compiler_params=_params("arbitrary", "arbitrary", "arbitrary"),
        name="diff_attention",
    )(lam_p, subln_g.reshape(1, -1), qk, qk, v)


def _dn_kernel(dn_ref, a_ref, at_ref, cw_ref, alog_ref, dtb_ref, alogt_ref, dtbt_ref, ng_ref,
               o_ref, prev_ref, state_ref):
    c = pl.program_id(1)

    @pl.when(c == 0)
    def _():
        prev_ref[...] = jnp.zeros_like(prev_ref)
        state_ref[...] = jnp.zeros_like(state_ref)

    W3 = 3 * DN_WIDTH
    x = dn_ref[:, :W3].astype(F32)
    xcat = jnp.concatenate([prev_ref[...], x], axis=0)
    cw = cw_ref[...]
    y = x * cw[DN_CONV - 1:DN_CONV, :]
    for j in range(DN_CONV - 1):
        s = DN_CONV - 1 - j
        y = y + pltpu.roll(xcat, s, axis=0)[8:, :] * cw[j:j + 1, :]
    prev_ref[...] = x[CHUNK - 8:, :]
    y = _silu(y)

    ab = a_ref[...]
    g_col = -jnp.exp(alog_ref[...]) * jax.nn.softplus(ab + dtb_ref[...])
    beta_all = jax.nn.sigmoid(ab)
    abt = at_ref[0]
    g_row = -jnp.exp(alogt_ref[...]) * jax.nn.softplus(abt + dtbt_ref[...])
    ri = lax.broadcasted_iota(jnp.int32, (CHUNK, CHUNK), 0)
    ci = lax.broadcasted_iota(jnp.int32, (CHUNK, CHUNK), 1)
    tril = ri >= ci
    strict = ri > ci
    gc_col = _dot_hi(jnp.where(tril, 1.0, 0.0), g_col)
    gc_row = _dot_hi(g_row, jnp.where(ci >= ri, 1.0, 0.0))
    eye = jnp.where(ri == ci, 1.0, 0.0)

    for h in range(DN_HEADS):
        hs = slice(h * DN_HEAD_DIM, (h + 1) * DN_HEAD_DIM)
        q = y[:, hs]
        k = y[:, DN_WIDTH + h * DN_HEAD_DIM:DN_WIDTH + (h + 1) * DN_HEAD_DIM]
        v = y[:, 2 * DN_WIDTH + h * DN_HEAD_DIM:2 * DN_WIDTH + (h + 1) * DN_HEAD_DIM]
        q = q * lax.rsqrt(jnp.sum(q * q, axis=-1, keepdims=True) + EPS) * (DN_HEAD_DIM ** -0.5)
        k = k * lax.rsqrt(jnp.sum(k * k, axis=-1, keepdims=True) + EPS)
        gc = gc_col[:, h:h + 1]
        gr = gc_row[h:h + 1, :]
        beta = beta_all[:, DN_HEADS + h:DN_HEADS + h + 1]
        g_last = gc_col[CHUNK - 1:CHUNK, h:h + 1]
        decay = jnp.exp(jnp.where(tril, gc - gr, -jnp.inf))
        eg = jnp.exp(gc)
        kb = k * beta
        kb16 = kb.astype(BF16)
        k16 = k.astype(BF16)
        a_mat = jnp.where(strict, _dot_nt(kb16, k16) * decay, 0.0)
        p = -a_mat
        t_inv = eye + p
        for _ in range(5):
            p = _dot_hi(p, p)
            t_inv = t_inv + _dot_hi(t_inv, p)
        rhs = jnp.concatenate([v * beta, kb * eg], axis=1)
        sol = _dot_hi(t_inv, rhs)
        u = sol[:, :DN_HEAD_DIM]
        w = sol[:, DN_HEAD_DIM:]
        qk = _dot_nt(q.astype(BF16), k16) * decay
        qg = (q * eg).astype(BF16)
        kd = (k * jnp.exp(g_last - gc)).astype(BF16)
        state = state_ref[h]
        s16 = state.astype(BF16)
        v_new = u - _dot(w.astype(BF16), s16)
        vn16 = v_new.astype(BF16)
        o = _dot(qg, s16) + _dot(qk.astype(BF16), vn16)
        state_ref[h] = state * jnp.exp(g_last) + _dot_tn(kd, vn16)
        z = dn_ref[:, W3 + h * DN_HEAD_DIM:W3 + (h + 1) * DN_HEAD_DIM].astype(F32)
        o_ref[:, hs] = (_rms(o, ng_ref[...]) * _silu(z)).astype(BF16)


def _deltanet(dn, ab, conv_w, a_log, dt_bias, norm_g, batch):
    T = dn.shape[0]
    n = SEQ // CHUNK
    pad = lambda t: jnp.zeros((1, LANES), F32).at[0, :DN_HEADS].set(t)
    col8 = lambda t: jnp.zeros((8, 1), F32).at[:DN_HEADS, 0].set(t)
    ab_t = ab[:, :8].reshape(T // CHUNK, CHUNK, 8).transpose(0, 2, 1)
    return pl.pallas_call(
        _dn_kernel,
        grid=(batch, n),
        in_specs=[pl.BlockSpec((CHUNK, 4 * DN_WIDTH), lambda b, c: (b * n + c, 0)),
                  pl.BlockSpec((CHUNK, LANES), lambda b, c: (b * n + c, 0)),
                  pl.BlockSpec((1, 8, CHUNK), lambda b, c: (b * n + c, 0, 0)),
                  pl.BlockSpec((DN_CONV, 3 * DN_WIDTH), lambda b, c: (0, 0)),
                  pl.BlockSpec((1, LANES), lambda b, c: (0, 0)),
                  pl.BlockSpec((1, LANES), lambda b, c: (0, 0)),
                  pl.BlockSpec((8, 1), lambda b, c: (0, 0)),
                  pl.BlockSpec((8, 1), lambda b, c: (0, 0)),
                  pl.BlockSpec((1, DN_HEAD_DIM), lambda b, c: (0, 0))],
        out_specs=pl.BlockSpec((CHUNK, DN_WIDTH), lambda b, c: (b * n + c, 0)),
        out_shape=jax.ShapeDtypeStruct((T, DN_WIDTH), BF16),
        scratch_shapes=[pltpu.VMEM((8, 3 * DN_WIDTH), F32),
                        pltpu.VMEM((DN_HEADS, DN_HEAD_DIM, DN_HEAD_DIM), F32)],
        compiler_params=_params("arbitrary", "arbitrary"),
        name="deltanet",
    )(dn, ab, ab_t, conv_w, pad(a_log), pad(dt_bias), col8(a_log), col8(dt_bias),
      norm_g.reshape(1, -1))


def _merge_kernel(x_ref, mod_ref, g2_ref, ygm_ref, yda_ref, ydn_ref, gate_ref,
                  wgm_ref, wda_ref, wdn_ref, wout_ref, x1_ref, h2_ref):
    D = D_MODEL
    merged = (jax.nn.sigmoid(gate_ref[:, 0:D].astype(F32)) * _dot(ygm_ref[...], wgm_ref[...])
              + jax.nn.sigmoid(gate_ref[:, D:2 * D].astype(F32)) * _dot(yda_ref[...], wda_ref[...])
              + jax.nn.sigmoid(gate_ref[:, 2 * D:3 * D].astype(F32)) * _dot(ydn_ref[...], wdn_ref[...]))
    y = _dot(merged.astype(BF16), wout_ref[...])
    x1 = x_ref[...] + mod_ref[0, 2:3, :] * y
    x1_ref[...] = x1
    h2 = _rms(x1, g2_ref[...]) * (1.0 + mod_ref[0, 4:5, :]) + mod_ref[0, 3:4, :]
    h2_ref[...] = h2.astype(h2_ref.dtype)


def _merge(x, mod, g2, y_gm, y_da, y_dn, gates, w_gm, w_da, w_dn, w_out, h2_dtype):
    T, D = x.shape
    tm = TM_MERGE
    per_b = SEQ // tm
    row = lambda i: (i, 0)
    const = lambda i: (0, 0)
    return pl.pallas_call(
        _merge_kernel,
        grid=(T // tm,),
        in_specs=[pl.BlockSpec((tm, D), row),
                  pl.BlockSpec((1, 6, D), lambda i: (i // per_b, 0, 0)),
                  pl.BlockSpec((1, D), const),
                  pl.BlockSpec((tm, GM_WIDTH), row),
                  pl.BlockSpec((tm, GM_WIDTH), row),
                  pl.BlockSpec((tm, DN_WIDTH), row),
                  pl.BlockSpec((tm, 3 * D), row),
                  pl.BlockSpec((GM_WIDTH, D), const),
                  pl.BlockSpec((GM_WIDTH, D), const),
                  pl.BlockSpec((DN_WIDTH, D), const),
                  pl.BlockSpec((D, D), const)],
        out_specs=[pl.BlockSpec((tm, D), row), pl.BlockSpec((tm, D), row)],
        out_shape=[jax.ShapeDtypeStruct((T, D), F32), jax.ShapeDtypeStruct((T, D), h2_dtype)],
        compiler_params=_params("arbitrary"),
        name="merge",
    )(x, mod, g2, y_gm, y_da, y_dn, gates, w_gm, w_da, w_dn, w_out)


def _ffn_kernel(x_ref, h_ref, mod_ref, wg_ref, wu_ref, wd_ref, o_ref):
    h = h_ref[...]
    a = _dot(h, wg_ref[...])
    b = _dot(h, wu_ref[...])
    f = _dot((_silu(a) * b).astype(BF16), wd_ref[...])
    o_ref[...] = x_ref[...] + mod_ref[0, 5:6, :] * f


def _dense_ffn(x1, h2, mod, w_gate, w_up, w_down):
    T, D = x1.shape
    F = w_gate.shape[1]
    tm = TM_FFN
    per_b = SEQ // tm
    row = lambda i: (i, 0)
    const = lambda i: (0, 0)
    return pl.pallas_call(
        _ffn_kernel,
        grid=(T // tm,),
        in_specs=[pl.BlockSpec((tm, D), row),
                  pl.BlockSpec((tm, D), row),
                  pl.BlockSpec((1, 6, D), lambda i: (i // per_b, 0, 0)),
                  pl.BlockSpec((D, F), const, pipeline_mode=pl.Buffered(1)),
                  pl.BlockSpec((D, F), const, pipeline_mode=pl.Buffered(1)),
                  pl.BlockSpec((F, D), const, pipeline_mode=pl.Buffered(1))],
        out_specs=pl.BlockSpec((tm, D), row),
        out_shape=jax.ShapeDtypeStruct((T, D), F32),
        compiler_params=_params("arbitrary"),
        name="dense_ffn",
    )(x1, h2, mod, w_gate, w_up, w_down)


def _router_kernel(h_ref, wr_ref, e_ref, p_ref, r_ref, cnt_ref, base_ref):
    i = pl.program_id(0)
    tm = h_ref.shape[0]

    @pl.when(i == 0)
    def _():
        base_ref[...] = jnp.zeros_like(base_ref)

    logits = _dot_nt(wr_ref[...], h_ref[...].astype(BF16))
    row = lax.broadcasted_iota(jnp.int32, logits.shape, 0)
    m1 = jnp.max(logits, axis=0, keepdims=True)
    i1 = jnp.min(jnp.where(logits == m1, row, N_EXPERTS), axis=0, keepdims=True)
    rest = jnp.where(row == i1, -jnp.inf, logits)
    m2 = jnp.max(rest, axis=0, keepdims=True)
    i2 = jnp.min(jnp.where(rest == m2, row, N_EXPERTS), axis=0, keepdims=True)
    e2 = jnp.exp(m2 - m1)
    w1 = 1.0 / (1.0 + e2)
    w2 = e2 / (1.0 + e2)
    oh1 = jnp.where(row == i1, 1.0, 0.0)
    oh2 = jnp.where(row == i2, 1.0, 0.0)
    both = oh1 + oh2
    ti = lax.broadcasted_iota(jnp.int32, (tm, tm), 0)
    tj = lax.broadcasted_iota(jnp.int32, (tm, tm), 1)
    before = jnp.where(ti < tj, 1.0, 0.0).astype(BF16)
    pos = base_ref[...] + _dot(both.astype(BF16), before)
    r1 = jnp.sum(oh1 * pos, axis=0, keepdims=True)
    r2 = jnp.sum(oh2 * pos, axis=0, keepdims=True)
    base_ref[...] = base_ref[...] + jnp.sum(both, axis=1, keepdims=True)
    zi = jnp.zeros((N_EXPERTS - 2, tm), jnp.int32)
    zf = jnp.zeros((N_EXPERTS - 2, tm), F32)
    e_ref[...] = jnp.concatenate([i1, i2, zi], axis=0)
    p_ref[...] = jnp.concatenate([w1, w2, zf], axis=0)
    r_ref[...] = jnp.concatenate([r1.astype(jnp.int32), r2.astype(jnp.int32), zi], axis=0)
    cnt_ref[...] = jnp.broadcast_to(base_ref[...], cnt_ref.shape)


def _router(h2, w_router_t):
    T, D = h2.shape
    tm = TM_ROUTE
    col = lambda i: (0, i)
    return pl.pallas_call(
        _router_kernel,
        grid=(T // tm,),
        in_specs=[pl.BlockSpec((tm, D), lambda i: (i, 0)),
                  pl.BlockSpec((N_EXPERTS, D), lambda i: (0, 0))],
        out_specs=[pl.BlockSpec((N_EXPERTS, tm), col)] * 3 + [pl.BlockSpec((N_EXPERTS, LANES), lambda i: (0, 0))],
        out_shape=[jax.ShapeDtypeStruct((N_EXPERTS, T), jnp.int32),
                   jax.ShapeDtypeStruct((N_EXPERTS, T), F32),
                   jax.ShapeDtypeStruct((N_EXPERTS, T), jnp.int32),
                   jax.ShapeDtypeStruct((N_EXPERTS, LANES), F32)],
        scratch_shapes=[pltpu.VMEM((N_EXPERTS, 1), F32)],
        compiler_params=_params("arbitrary"),
        name="moe_router",
    )(h2, w_router_t)


def _dispatch_kernel(dest_ref, h_ref, xs_in_ref, xs_ref, sem):
    del xs_in_ref
    tm = h_ref.shape[0]

    def issue(r, _):
        for k in range(2):
            pltpu.make_async_copy(h_ref.at[pl.ds(r, 1)],
                                  xs_ref.at[pl.ds(dest_ref[0, 0, k * tm + r], 1)], sem).start()
        return 0

    lax.fori_loop(0, tm, issue, 0)
    for _ in range(2):
        pltpu.make_async_copy(h_ref, xs_ref.at[pl.ds(0, tm)], sem).wait()


def _dispatch(h2, dest, n_slots):
    T, D = h2.shape
    tm = TM_DISP
    dest_t = dest.reshape(2, T // tm, tm).transpose(1, 0, 2).reshape(T // tm, 1, 2 * tm)
    xs0 = jnp.zeros((n_slots, D), h2.dtype)
    return pl.pallas_call(
        _dispatch_kernel,
        grid=(T // tm,),
        in_specs=[pl.BlockSpec((1, 1, 2 * tm), lambda i: (i, 0, 0), memory_space=pltpu.SMEM),
                  pl.BlockSpec((tm, D), lambda i: (i, 0)),
                  pl.BlockSpec(memory_space=pl.ANY)],
        out_specs=pl.BlockSpec(memory_space=pl.ANY),
        out_shape=jax.ShapeDtypeStruct((n_slots, D), h2.dtype),
        scratch_shapes=[pltpu.SemaphoreType.DMA(())],
        input_output_aliases={2: 0},
        compiler_params=_params("arbitrary"),
        name="moe_dispatch",
    )(dest_t, h2, xs0)


def _expert_kernel(be_ref, nu_ref, x_ref, wg_ref, wu_ref, wd_ref, o_ref, xb_ref, acc_ref):
    i = pl.program_id(0)
    j = pl.program_id(1)
    last = pl.num_programs(1) - 1
    used = i < nu_ref[0]

    @pl.when(jnp.logical_and(used, j == 0))
    def _():
        xb_ref[...] = x_ref[...].astype(BF16)
        acc_ref[...] = jnp.zeros_like(acc_ref)

    @pl.when(used)
    def _():
        xb = xb_ref[...]
        a = _dot(xb, wg_ref[0])
        b = _dot(xb, wu_ref[0])
        acc_ref[...] += _dot((_silu(a) * b).astype(BF16), wd_ref[0])

    @pl.when(jnp.logical_and(used, j == last))
    def _():
        o_ref[...] = acc_ref[...]

    @pl.when(jnp.logical_and(jnp.logical_not(used), j == last))
    def _():
        o_ref[...] = jnp.zeros_like(o_ref)


def _experts(xs, blk_e, n_used, w_gate, w_up, w_down):
    P, D = xs.shape
    F = w_gate.shape[2]
    tm, fc = TM_MOE, FC_MOE
    nf = F // fc
    fsel = lambda i, j, nu: jnp.where(i < nu[0], j, nf - 1)
    grid_spec = pltpu.PrefetchScalarGridSpec(
        num_scalar_prefetch=2,
        grid=(P // tm, nf),
        in_specs=[pl.BlockSpec((tm, D), lambda i, j, be, nu: (i, 0)),
                  pl.BlockSpec((1, D, fc), lambda i, j, be, nu: (be[i], 0, fsel(i, j, nu))),
                  pl.BlockSpec((1, D, fc), lambda i, j, be, nu: (be[i], 0, fsel(i, j, nu))),
                  pl.BlockSpec((1, fc, D), lambda i, j, be, nu: (be[i], fsel(i, j, nu), 0))],
        out_specs=pl.BlockSpec((tm, D), lambda i, j, be, nu: (i, 0)),
        scratch_shapes=[pltpu.VMEM((tm, D), BF16), pltpu.VMEM((tm, D), F32)])
    return pl.pallas_call(
        _expert_kernel,
        grid_spec=grid_spec,
        out_shape=jax.ShapeDtypeStruct((P, D), F32),
        compiler_params=_params("arbitrary", "arbitrary"),
        name="moe_experts",
    )(blk_e, n_used, xs, w_gate, w_up, w_down)


def _combine_kernel(dest_ref, x_ref, mod_ref, p_ref, g_ref, ys_ref, o_ref, ybuf_ref, sem):
    tm = x_ref.shape[0]

    def issue(r, _):
        for k in range(2):
            pltpu.make_async_copy(ys_ref.at[pl.ds(dest_ref[0, 0, k * tm + r], 1)],
                                  ybuf_ref.at[k, pl.ds(r, 1)], sem).start()
        return 0

    lax.fori_loop(0, tm, issue, 0)
    for k in range(2):
        pltpu.make_async_copy(ys_ref.at[pl.ds(0, tm)], ybuf_ref.at[k], sem).wait()
    p = p_ref[...]
    f = ybuf_ref[0] * p[:, 0:1] + ybuf_ref[1] * p[:, 1:2]
    x2 = x_ref[...] + mod_ref[0, 5:6, :] * f
    o_ref[...] = _rms(x2, g_ref[...])


def _combine(x1, mod, probs, final_g, ys, dest):
    T, D = x1.shape
    tm = TM_COMB
    per_b = SEQ // tm
    dest_t = dest.reshape(2, T // tm, tm).transpose(1, 0, 2).reshape(T // tm, 1, 2 * tm)
    return pl.pallas_call(
        _combine_kernel,
        grid=(T // tm,),
        in_specs=[pl.BlockSpec((1, 1, 2 * tm), lambda i: (i, 0, 0), memory_space=pltpu.SMEM),
                  pl.BlockSpec((tm, D), lambda i: (i, 0)),
                  pl.BlockSpec((1, 6, D), lambda i: (i // per_b, 0, 0)),
                  pl.BlockSpec((tm, 2), lambda i: (i, 0)),
                  pl.BlockSpec((1, D), lambda i: (0, 0)),
                  pl.BlockSpec(memory_space=pl.ANY)],
        out_specs=pl.BlockSpec((tm, D), lambda i: (i, 0)),
        out_shape=jax.ShapeDtypeStruct((T, D), F32),
        scratch_shapes=[pltpu.VMEM((2, tm, D), F32), pltpu.SemaphoreType.DMA(())],
        compiler_params=_params("arbitrary"),
        name="moe_combine",
    )(dest_t, x1, mod, probs, final_g, ys)


def _moe_layer(x1, h2, mod, w_router, w_gate, w_up, w_down, final_g):
    T, D = x1.shape
    e_idx, probs, rank, counts = _router(h2, w_router.T.astype(BF16))
    counts = counts[:, 0].astype(jnp.int32)
    padded = (counts + TM_MOE - 1) // TM_MOE * TM_MOE
    pad_end = jnp.cumsum(padded)
    pad_start = pad_end - padded
    dest = pad_start[e_idx[:2]] + rank[:2]
    n_slots = 2 * T + N_EXPERTS * TM_MOE
    n_blk = n_slots // TM_MOE
    blk_e = jnp.minimum(jnp.searchsorted(pad_end, jnp.arange(n_blk, dtype=jnp.int32) * TM_MOE,
                                         side='right'), N_EXPERTS - 1).astype(jnp.int32)
    n_used = (pad_end[-1:] // TM_MOE).astype(jnp.int32)
    xs = _dispatch(h2, dest, n_slots)
    ys = _experts(xs, blk_e, n_used, w_gate.astype(BF16), w_up.astype(BF16), w_down.astype(BF16))
    return _combine(x1, mod, probs[:2].T, final_g.reshape(1, -1), ys, dest)


def _final_norm_kernel(x_ref, g_ref, o_ref):
    o_ref[...] = _rms(x_ref[...], g_ref[...])


def _final_norm(x, g):
    T, D = x.shape
    tm = 1024
    return pl.pallas_call(
        _final_norm_kernel,
        grid=(T // tm,),
        in_specs=[pl.BlockSpec((tm, D), lambda i: (i, 0)), pl.BlockSpec((1, D), lambda i: (0, 0))],
        out_specs=pl.BlockSpec((tm, D), lambda i: (i, 0)),
        out_shape=jax.ShapeDtypeStruct((T, D), F32),
        compiler_params=_params("arbitrary"),
        name="final_norm",
    )(x, g.reshape(1, -1))


def _regroup_w_in(w):
    ab = jnp.pad(w[:, 4608:4616], ((0, 0), (0, LANES - 8)))
    return jnp.concatenate([w[:, :4608], w[:, 4616:], ab], axis=1).astype(BF16)


def _lambda_init(layer):
    return 0.8 - 0.6 * math.exp(-0.3 * layer)


def kernel(x, c, positions, norm1_g, norm2_g, w_mod, b_mod, w_in, gm_ln_g, gm_ln_b, gm_w_s, gm_b_s, da_lambda, da_subln_g, dn_conv_w, dn_a_log, dn_dt_bias, dn_norm_g, w_br_gm, w_br_da, w_br_dn, w_out, ffn_w_gate, ffn_w_up, ffn_w_down, moe_w_router, moe_w_gate, moe_w_up, moe_w_down, final_g):
    B, S, D = x.shape
    T = B * S
    xt = x.reshape(T, D)
    mod_all = _modulation(c, w_mod, b_mod).reshape(DEPTH, B, 6, D)
    cos_t, sin_t = _rope_tables(positions)
    for layer in range(DEPTH):
        mod = mod_all[layer]
        gm, qk, v, dn, gates, ab = _in_projection(xt, mod, norm1_g[layer].reshape(1, D),
                                                  _regroup_w_in(w_in[layer]), cos_t, sin_t)
        y_gm = _gmlp(gm, gm_ln_g[layer], gm_ln_b[layer], gm_w_s[layer], gm_b_s[layer])
        y_da = _diff_attention(qk, v, da_lambda[layer], da_subln_g[layer], _lambda_init(layer), B)
        y_dn = _deltanet(dn, ab, dn_conv_w[layer], dn_a_log[layer], dn_dt_bias[layer],
                         dn_norm_g[layer], B)
        moe = layer % 2 == 1
        x1, h2 = _merge(xt, mod, norm2_g[layer].reshape(1, D), y_gm, y_da, y_dn, gates,
                        w_br_gm[layer].astype(BF16), w_br_da[layer].astype(BF16),
                        w_br_dn[layer].astype(BF16), w_out[layer].astype(BF16),
                        F32 if moe else BF16)
        if moe:
            xt = _moe_layer(x1, h2, mod, moe_w_router[layer // 2], moe_w_gate[layer // 2],
                            moe_w_up[layer // 2], moe_w_down[layer // 2], final_g)
        else:
            xt = _dense_ffn(x1, h2, mod, ffn_w_gate[layer // 2].astype(BF16),
                            ffn_w_up[layer // 2].astype(BF16), ffn_w_down[layer // 2].astype(BF16))
    if DEPTH % 2 == 1:
        xt = _final_norm(xt, final_g)
    return xt.reshape(B, S, D)
```

```python
import functools
import math

import jax
import jax.numpy as jnp
import numpy as np
from jax import lax
from jax.experimental import pallas as pl
from jax.experimental.pallas import tpu as pltpu

D_MODEL = 1024
SEQ = 2048
DEPTH = 2
CHUNK = 64
EPS = 1e-6
GM_WIDTH = D_MODEL // 2
GM_GROUPS = 4
GM_BLOCK = 128
DA_HEADS = 4
DA_HEAD_DIM = 64
DA_V_DIM = 2 * DA_HEAD_DIM
ROPE_THETA = 10000.0
DN_HEADS = 4
DN_HEAD_DIM = 128
DN_WIDTH = DN_HEADS * DN_HEAD_DIM
DN_CONV = 4
N_EXPERTS = 8
FF_EXPERT = 7 * D_MODEL // 2

LANES = 128
VMEM_LIMIT = 56 * 1024 * 1024

C_GM, C_QK, C_V, C_DN, C_GATE, C_AB = 0, 1024, 2048, 2560, 4608, 7680
W_ALL = C_AB + LANES
PROJ_CHUNK = 512

TM_PROJ = 512
TM_GM = 512
TQ = 256
TM_MERGE = 512
TM_FFN = 512
TM_ROUTE = 512
TM_MOE = 512
FC_MOE = 512
TM_DISP = 256
TM_COMB = 256
ATTN_LOOKAHEAD = 2
DN_BLOCK = 256

BF16 = jnp.bfloat16
F32 = jnp.float32


def _params(*sem):
    return pltpu.CompilerParams(dimension_semantics=sem, vmem_limit_bytes=VMEM_LIMIT)


def _dot(a, b):
    return jnp.dot(a, b, preferred_element_type=F32)


def _dot_nt(a, b):
    return lax.dot_general(a, b, (((1,), (1,)), ((), ())), preferred_element_type=F32)


def _dot_tn(a, b):
    return lax.dot_general(a, b, (((0,), (0,)), ((), ())), preferred_element_type=F32)


def _dot_hi(a, b):
    return jnp.dot(a, b, preferred_element_type=F32, precision=lax.Precision.HIGHEST)


def _rms(x, g):
    return x * lax.rsqrt(jnp.mean(x * x, axis=-1, keepdims=True) + EPS) * g


def _silu(x):
    return x * jax.nn.sigmoid(x)


def _mod_kernel(c_ref, w_ref, b_ref, o_ref):
    c = c_ref[...]
    o_ref[0] = _dot_hi(_silu(c), w_ref[0]) + b_ref[0]


def _modulation(c, w_mod, b_mod):
    B, D = c.shape
    L, _, N = w_mod.shape
    tn = 1536
    return pl.pallas_call(
        _mod_kernel,
        grid=(L, N // tn),
        in_specs=[pl.BlockSpec((B, D), lambda l, j: (0, 0)),
                  pl.BlockSpec((1, D, tn), lambda l, j: (l, 0, j)),
                  pl.BlockSpec((1, 1, tn), lambda l, j: (l, 0, j))],
        out_specs=pl.BlockSpec((1, B, tn), lambda l, j: (l, 0, j)),
        out_shape=jax.ShapeDtypeStruct((L, B, N), F32),
        compiler_params=_params("arbitrary", "arbitrary"),
        name="modulation",
    )(c, w_mod, b_mod.reshape(L, 1, N))


def _rope_tab_kernel(pos_ref, inv_ref, sgn_ref, cos_ref, sin_ref):
    ang = pos_ref[...].astype(F32) * inv_ref[...]
    cos_ref[...] = jnp.cos(ang)
    sin_ref[...] = jnp.sin(ang) * sgn_ref[...]


def _rope_tables(positions):
    T = positions.size
    inv_freq = ROPE_THETA ** (-jnp.arange(0, DA_HEAD_DIM, 2, dtype=F32) / DA_HEAD_DIM)
    inv = jnp.tile(inv_freq, LANES // (DA_HEAD_DIM // 2)).reshape(1, LANES)
    half = DA_HEAD_DIM // 2
    sgn = np.tile(np.concatenate([-np.ones(half), np.ones(half)]), LANES // DA_HEAD_DIM)
    sgn = jnp.asarray(sgn, F32).reshape(1, LANES)
    tm = 1024
    return pl.pallas_call(
        _rope_tab_kernel,
        grid=(T // tm,),
        in_specs=[pl.BlockSpec((tm, 1), lambda i: (i, 0)),
                  pl.BlockSpec((1, LANES), lambda i: (0, 0)),
                  pl.BlockSpec((1, LANES), lambda i: (0, 0))],
        out_specs=[pl.BlockSpec((tm, LANES), lambda i: (i, 0))] * 2,
        out_shape=[jax.ShapeDtypeStruct((T, LANES), F32)] * 2,
        compiler_params=_params("arbitrary"),
        name="rope_tables",
    )(positions.reshape(T, 1), inv, sgn)


def _rope_rows(x, cos, sin):
    lane = lax.broadcasted_iota(jnp.int32, x.shape, 1)
    first = (lane % DA_HEAD_DIM) < (DA_HEAD_DIM // 2)
    partner = jnp.where(first, pltpu.roll(x, LANES - DA_HEAD_DIM // 2, axis=1),
                        pltpu.roll(x, DA_HEAD_DIM // 2, axis=1))
    return x * cos + partner * sin


def _proj_kernel(x_ref, mod_ref, g_ref, w_ref, cos_ref, sin_ref,
                 gm_ref, qk_ref, v_ref, dn_ref, gate_ref, ab_ref):
    x = x_ref[...]
    shift = mod_ref[0, 0:1, :]
    scale = mod_ref[0, 1:2, :]
    h = (_rms(x, g_ref[...]) * (1.0 + scale) + shift).astype(BF16)
    cos = cos_ref[...]
    sin = sin_ref[...]
    q_scale = DA_HEAD_DIM ** -0.5 * math.log2(math.e)

    def rope_chunk(y, mult):
        parts = [_rope_rows(y[:, j * LANES:(j + 1) * LANES], cos, sin) * mult
                 for j in range(PROJ_CHUNK // LANES)]
        return jnp.concatenate(parts, axis=1)

    for c0 in range(0, C_AB, PROJ_CHUNK):
        y = _dot(h, w_ref[:, c0:c0 + PROJ_CHUNK])
        if c0 < C_QK:
            gm_ref[:, c0:c0 + PROJ_CHUNK] = y.astype(BF16)
        elif c0 < C_V:
            mult = q_scale if c0 == C_QK else 1.0
            qk_ref[:, c0 - C_QK:c0 - C_QK + PROJ_CHUNK] = rope_chunk(y, mult).astype(BF16)
        elif c0 < C_DN:
            v_ref[...] = y.astype(BF16)
        elif c0 < C_GATE:
            dn_ref[:, c0 - C_DN:c0 - C_DN + PROJ_CHUNK] = y.astype(BF16)
        else:
            gate_ref[:, c0 - C_GATE:c0 - C_GATE + PROJ_CHUNK] = y.astype(BF16)
    ab_ref[...] = _dot(h, w_ref[:, C_AB:W_ALL])


def _in_projection(x, mod, g, w_all, cos_t, sin_t):
    T, D = x.shape
    tm = TM_PROJ
    per_b = SEQ // tm
    row = lambda i: (i, 0)
    widths = (C_QK - C_GM, C_V - C_QK, C_DN - C_V, C_GATE - C_DN, C_AB - C_GATE)
    return pl.pallas_call(
        _proj_kernel,
        grid=(T // tm,),
        in_specs=[pl.BlockSpec((tm, D), row),
                  pl.BlockSpec((1, 6, D), lambda i: (i // per_b, 0, 0)),
                  pl.BlockSpec((1, D), lambda i: (0, 0)),
                  pl.BlockSpec((D, W_ALL), lambda i: (0, 0), pipeline_mode=pl.Buffered(1)),
                  pl.BlockSpec((tm, LANES), row),
                  pl.BlockSpec((tm, LANES), row)],
        out_specs=[pl.BlockSpec((tm, w), row) for w in widths] + [pl.BlockSpec((tm, LANES), row)],
        out_shape=[jax.ShapeDtypeStruct((T, w), BF16) for w in widths]
                  + [jax.ShapeDtypeStruct((T, LANES), F32)],
        compiler_params=_params("arbitrary"),
        name="in_projection",
    )(x, mod, g, w_all, cos_t, sin_t)


def _gelu(x):
    return 0.5 * x * (1.0 + lax.erf(x * np.float32(math.sqrt(0.5))))


def _gmlp_kernel(uv_ref, lng_ref, lnb_ref, ws_ref, bs_ref, o_ref):
    u = _gelu(uv_ref[:, :GM_WIDTH].astype(F32))
    v = _gelu(uv_ref[:, GM_WIDTH:].astype(F32))
    mu = jnp.mean(v, axis=-1, keepdims=True)
    vc = v - mu
    var = jnp.mean(vc * vc, axis=-1, keepdims=True)
    v = (vc * lax.rsqrt(var + EPS) * lng_ref[...] + lnb_ref[...]).astype(BF16)
    ri = lax.broadcasted_iota(jnp.int32, (GM_BLOCK, GM_BLOCK), 0) // CHUNK
    ci = lax.broadcasted_iota(jnp.int32, (GM_BLOCK, GM_BLOCK), 1) // CHUNK
    allowed = ci <= ri
    gc = GM_WIDTH // GM_GROUPS
    for g in range(GM_GROUPS):
        w = jnp.where(allowed, ws_ref[g], 0.0).astype(BF16)
        bias = bs_ref[g]
        for r in range(uv_ref.shape[0] // GM_BLOCK):
            rows = slice(r * GM_BLOCK, (r + 1) * GM_BLOCK)
            cols = slice(g * gc, (g + 1) * gc)
            sv = _dot(w, v[rows, cols]) + bias
            o_ref[rows, cols] = (u[rows, cols] * sv).astype(BF16)


def _gmlp(uv, ln_g, ln_b, w_s, b_s):
    T = uv.shape[0]
    tm = TM_GM
    return pl.pallas_call(
        _gmlp_kernel,
        grid=(T // tm,),
        in_specs=[pl.BlockSpec((tm, 2 * GM_WIDTH), lambda i: (i, 0)),
                  pl.BlockSpec((1, GM_WIDTH), lambda i: (0, 0)),
                  pl.BlockSpec((1, GM_WIDTH), lambda i: (0, 0)),
                  pl.BlockSpec((GM_GROUPS, GM_BLOCK, GM_BLOCK), lambda i: (0, 0, 0)),
                  pl.BlockSpec((GM_GROUPS, GM_BLOCK, 1), lambda i: (0, 0, 0))],
        out_specs=pl.BlockSpec((tm, GM_WIDTH), lambda i: (i, 0)),
        out_shape=jax.ShapeDtypeStruct((T, GM_WIDTH), BF16),
        compiler_params=_params("arbitrary"),
        name="gmlp",
    )(uv, ln_g.reshape(1, -1), ln_b.reshape(1, -1), w_s, b_s.reshape(GM_GROUPS, GM_BLOCK, 1))


def _attn_kernel(lam_ref, g_ref, q_ref, k_ref, v_ref, o_ref, qz_ref, vt_ref, *stats, lambda_init):
    nc = 2 * DA_HEADS
    m_ref, l_ref, acc_ref = stats[:nc], stats[nc:2 * nc], stats[2 * nc:]
    i = pl.program_id(1)
    lp = lam_ref[...]
    lam = (jnp.exp(jnp.sum(lp[0:1] * lp[1:2], axis=-1, keepdims=True))
           - jnp.exp(jnp.sum(lp[2:3] * lp[3:4], axis=-1, keepdims=True)) + lambda_init)
    @pl.when(i == 0)
    def _():
        for kb in range(SEQ // TQ):
            for h in range(DA_HEADS):
                vt_ref[h * LANES:(h + 1) * LANES, kb * TQ:(kb + 1) * TQ] = (
                    v_ref[kb * TQ:(kb + 1) * TQ, h * LANES:(h + 1) * LANES].T)

    row = lax.broadcasted_iota(jnp.int32, (LANES, TQ), 0)
    for h in range(DA_HEADS):
        qt = q_ref[:, h * LANES:(h + 1) * LANES].T
        zero = jnp.zeros_like(qt)
        qz_ref[2 * h] = jnp.where(row < DA_HEAD_DIM, qt, zero)
        qz_ref[2 * h + 1] = jnp.where(row >= DA_HEAD_DIM, qt, zero)
    for c in range(nc):
        m_ref[c][...] = jnp.full_like(m_ref[c], -jnp.inf)
        l_ref[c][...] = jnp.zeros_like(l_ref[c])
        acc_ref[c][...] = jnp.zeros_like(acc_ref[c])

    def block(kb, diagonal):
        rows = pl.ds(pl.multiple_of(kb * TQ, TQ), TQ)

        def scores(c):
            h = c // 2
            s = _dot(k_ref[rows, h * LANES:(h + 1) * LANES], qz_ref[c])
            if diagonal:
                ki = lax.broadcasted_iota(jnp.int32, s.shape, 0) // CHUNK
                qi = lax.broadcasted_iota(jnp.int32, s.shape, 1) // CHUNK
                s = jnp.where(ki <= qi, s, -jnp.inf)
            return s

        def accumulate(c, s):
            h = c // 2
            m_old = m_ref[c][...]
            m_new = jnp.maximum(m_old, jnp.max(s, axis=0, keepdims=True))
            alpha = jnp.exp2(m_old - m_new)
            e = jnp.exp2(s - m_new)
            l_ref[c][...] = alpha * l_ref[c][...] + jnp.sum(e, axis=0, keepdims=True)
            vt = vt_ref[h * LANES:(h + 1) * LANES, rows]
            acc_ref[c][...] = alpha * acc_ref[c][...] + _dot(vt, e.astype(BF16))
            m_ref[c][...] = m_new

        pending = {}
        for c in range(nc + ATTN_LOOKAHEAD):
            if c < nc:
                pending[c] = scores(c)
            if c >= ATTN_LOOKAHEAD:
                accumulate(c - ATTN_LOOKAHEAD, pending.pop(c - ATTN_LOOKAHEAD))

    def body(kb, carry):
        block(kb, False)
        return carry

    lax.fori_loop(0, i, body, 0)
    block(i, True)
    for h in range(DA_HEADS):
        o = (acc_ref[2 * h][...] / l_ref[2 * h][...]
             - lam * (acc_ref[2 * h + 1][...] / l_ref[2 * h + 1][...]))
        ms = jnp.mean(o * o, axis=0, keepdims=True)
        o = o * lax.rsqrt(ms + EPS) * (g_ref[...] * (1.0 - lambda_init))
        o_ref[:, h * LANES:(h + 1) * LANES] = o.T.astype(BF16)


def _diff_attention(qk, v, lam_p, subln_g, lambda_init, batch):
    T = qk.shape[0]
    nq = SEQ // TQ
    W = DA_HEADS * DA_V_DIM
    kernel = functools.partial(_attn_kernel, lambda_init=lambda_init)
    return pl.pallas_call(
        kernel,
        grid=(batch, nq),
        in_specs=[pl.BlockSpec((4, DA_HEAD_DIM), lambda b, i: (0, 0)),
                  pl.BlockSpec((DA_V_DIM, 1), lambda b, i: (0, 0)),
                  pl.BlockSpec((TQ, W), lambda b, i: (b * nq + i, 0)),
                  pl.BlockSpec((SEQ, W), lambda b, i: (b, 1)),
                  pl.BlockSpec((SEQ, W), lambda b, i: (b, 0))],
        out_specs=pl.BlockSpec((TQ, W), lambda b, i: (b * nq + i, 0)),
        out_shape=jax.ShapeDtypeStruct((T, W), BF16),
        scratch_shapes=[pltpu.VMEM((2 * DA_HEADS, LANES, TQ), BF16), pltpu.VMEM((W, SEQ), BF16)]
                       + [pltpu.VMEM((1, TQ), F32)] * (4 * DA_HEADS)
                       + [pltpu.VMEM((DA_V_DIM, TQ), F32)] * (2 * DA_HEADS),
        compiler_params=_params("arbitrary", "arbitrary"),
        name="diff_attention",
    )(lam_p, subln_g.reshape(-1, 1), qk, qk, v)


def _split3(x):
    hi = x.astype(BF16)
    r = x - hi.astype(F32)
    mid = r.astype(BF16)
    lo = (r - mid.astype(F32)).astype(BF16)
    return hi, mid, lo


def _dn_kernel(dn_ref, a_ref, at_ref, cw_ref, alog_ref, dtb_ref, alogt_ref, dtbt_ref, ng_ref,
               o_ref, prev_ref, *state_ref):
    c = pl.program_id(1)
    R = DN_BLOCK

    @pl.when(c == 0)
    def _():
        prev_ref[...] = jnp.zeros_like(prev_ref)
        for ref in state_ref:
            ref[...] = jnp.zeros_like(ref)

    W3 = 3 * DN_WIDTH
    x = dn_ref[:, :W3].astype(F32)
    xcat = jnp.concatenate([prev_ref[...], x], axis=0)
    cw = cw_ref[...]
    y = x * cw[DN_CONV - 1:DN_CONV, :]
    for j in range(DN_CONV - 1):
        s = DN_CONV - 1 - j
        y = y + pltpu.roll(xcat, s, axis=0)[8:, :] * cw[j:j + 1, :]
    prev_ref[...] = x[R - 8:, :]
    y = _silu(y)

    ab = a_ref[...]
    g_col = -jnp.exp(alog_ref[...]) * jax.nn.softplus(ab + dtb_ref[...])
    beta_all = jax.nn.sigmoid(ab)
    abt = at_ref[0]
    g_row = -jnp.exp(alogt_ref[...]) * jax.nn.softplus(abt + dtbt_ref[...])
    ri = lax.broadcasted_iota(jnp.int32, (R, R), 0)
    ci = lax.broadcasted_iota(jnp.int32, (R, R), 1)
    tril = ri >= ci
    strict = ri > ci
    ones_tril = jnp.where(tril, 1.0, 0.0).astype(BF16)
    ones_triu = jnp.where(ci >= ri, 1.0, 0.0).astype(BF16)
    gc_col = sum(_dot(ones_tril, part) for part in _split3(g_col))
    gc_row = sum(_dot(part, ones_triu) for part in _split3(g_row))
    eye = jnp.where(ri == ci, 1.0, 0.0)
    same = {n: (ri // n) == (ci // n) for n in (16, 32, 64, 128)}

    H = range(DN_HEADS)
    hsl = [slice(h * DN_HEAD_DIM, (h + 1) * DN_HEAD_DIM) for h in H]
    q = [y[:, hsl[h]] for h in H]
    k = [y[:, DN_WIDTH + h * DN_HEAD_DIM:DN_WIDTH + (h + 1) * DN_HEAD_DIM] for h in H]
    v = [y[:, 2 * DN_WIDTH + h * DN_HEAD_DIM:2 * DN_WIDTH + (h + 1) * DN_HEAD_DIM] for h in H]
    q = [t * lax.rsqrt(jnp.sum(t * t, axis=-1, keepdims=True) + EPS) * (DN_HEAD_DIM ** -0.5) for t in q]
    k = [t * lax.rsqrt(jnp.sum(t * t, axis=-1, keepdims=True) + EPS) for t in k]
    gc = [gc_col[:, h:h + 1] for h in H]
    beta = [beta_all[:, DN_HEADS + h:DN_HEADS + h + 1] for h in H]
    g_last = [gc_col[R - 1:R, h:h + 1] for h in H]
    decay = [jnp.exp(jnp.where(tril, gc[h] - gc_row[h:h + 1, :], -jnp.inf)) for h in H]
    eg = [jnp.exp(t) for t in gc]
    kb = [k[h] * beta[h] for h in H]
    s1 = [_dot_nt(jnp.concatenate([kb[h], q[h]], axis=0).astype(BF16), k[h].astype(BF16)) for h in H]
    a_mat = [jnp.where(strict, s1[h][:R] * decay[h], 0.0) for h in H]
    qk = [(s1[h][R:] * decay[h]).astype(BF16) for h in H]
    p = [jnp.where(same[16], -t, 0.0) for t in a_mat]
    t_inv = [eye + t for t in p]
    for _ in range(3):
        p16 = [t.astype(BF16) for t in p]
        p = [_dot(t, t) for t in p16]
        t_inv = [t_inv[h] + _dot(t_inv[h].astype(BF16), p[h].astype(BF16)) for h in H]
    for n in (16, 32, 64, 128):
        off = jnp.logical_not(same[n])
        if 2 * n < R:
            off = jnp.logical_and(same[2 * n], off)
        t16 = [t.astype(BF16) for t in t_inv]
        tl = [_dot(t16[h], jnp.where(off, a_mat[h], 0.0).astype(BF16)).astype(BF16) for h in H]
        t_inv = [t_inv[h] - _dot(tl[h], t16[h]) for h in H]
    sol = [_dot(t_inv[h].astype(BF16),
                jnp.concatenate([v[h] * beta[h], kb[h] * eg[h]], axis=1).astype(BF16)) for h in H]
    state = [state_ref[h][...] for h in H]
    m1 = [_dot(jnp.concatenate([sol[h][:, DN_HEAD_DIM:], q[h] * eg[h]], axis=0).astype(BF16),
               state[h].astype(BF16)) for h in H]
    vn16 = [(sol[h][:, :DN_HEAD_DIM] - m1[h][:R]).astype(BF16) for h in H]
    o = [m1[h][R:] + _dot(qk[h], vn16[h]) for h in H]
    kd = [(k[h] * jnp.exp(g_last[h] - gc[h])).astype(BF16) for h in H]
    for h in H:
        state_ref[h][...] = state[h] * jnp.exp(g_last[h]) + _dot_tn(kd[h], vn16[h])
    for h in H:
        z = dn_ref[:, W3 + h * DN_HEAD_DIM:W3 + (h + 1) * DN_HEAD_DIM].astype(F32)
        o_ref[:, hsl[h]] = (_rms(o[h], ng_ref[...]) * _silu(z)).astype(BF16)


def _deltanet(dn, ab, conv_w, a_log, dt_bias, norm_g, batch):
    T = dn.shape[0]
    R = DN_BLOCK
    n = SEQ // R
    pad = lambda t: jnp.zeros((1, LANES), F32).at[0, :DN_HEADS].set(t)
    col8 = lambda t: jnp.zeros((8, 1), F32).at[:DN_HEADS, 0].set(t)
    ab_t = ab[:, :8].reshape(T // R, R, 8).transpose(0, 2, 1)
    return pl.pallas_call(
        _dn_kernel,
        grid=(batch, n),
        in_specs=[pl.BlockSpec((R, 4 * DN_WIDTH), lambda b, c: (b * n + c, 0)),
                  pl.BlockSpec((R, LANES), lambda b, c: (b * n + c, 0)),
                  pl.BlockSpec((1, 8, R), lambda b, c: (b * n + c, 0, 0)),
                  pl.BlockSpec((DN_CONV, 3 * DN_WIDTH), lambda b, c: (0, 0)),
                  pl.BlockSpec((1, LANES), lambda b, c: (0, 0)),
                  pl.BlockSpec((1, LANES), lambda b, c: (0, 0)),
                  pl.BlockSpec((8, 1), lambda b, c: (0, 0)),
                  pl.BlockSpec((8, 1), lambda b, c: (0, 0)),
                  pl.BlockSpec((1, DN_HEAD_DIM), lambda b, c: (0, 0))],
        out_specs=pl.BlockSpec((R, DN_WIDTH), lambda b, c: (b * n + c, 0)),
        out_shape=jax.ShapeDtypeStruct((T, DN_WIDTH), BF16),
        scratch_shapes=[pltpu.VMEM((8, 3 * DN_WIDTH), F32)]
                       + [pltpu.VMEM((DN_HEAD_DIM, DN_HEAD_DIM), F32)] * DN_HEADS,
        compiler_params=_params("arbitrary", "arbitrary"),
        name="deltanet",
    )(dn, ab, ab_t, conv_w, pad(a_log), pad(dt_bias), col8(a_log), col8(dt_bias),
      norm_g.reshape(1, -1))


def _merge_kernel(x_ref, mod_ref, g2_ref, ygm_ref, yda_ref, ydn_ref, gate_ref,
                  wgm_ref, wda_ref, wdn_ref, wout_ref, x1_ref, h2_ref):
    D = D_MODEL
    merged = (jax.nn.sigmoid(gate_ref[:, 0:D].astype(F32)) * _dot(ygm_ref[...], wgm_ref[...])
              + jax.nn.sigmoid(gate_ref[:, D:2 * D].astype(F32)) * _dot(yda_ref[...], wda_ref[...])
              + jax.nn.sigmoid(gate_ref[:, 2 * D:3 * D].astype(F32)) * _dot(ydn_ref[...], wdn_ref[...]))
    y = _dot(merged.astype(BF16), wout_ref[...])
    x1 = x_ref[...] + mod_ref[0, 2:3, :] * y
    x1_ref[...] = x1
    h2 = _rms(x1, g2_ref[...]) * (1.0 + mod_ref[0, 4:5, :]) + mod_ref[0, 3:4, :]
    h2_ref[...] = h2.astype(h2_ref.dtype)


def _merge(x, mod, g2, y_gm, y_da, y_dn, gates, w_gm, w_da, w_dn, w_out, h2_dtype):
    T, D = x.shape
    tm = TM_MERGE
    per_b = SEQ // tm
    row = lambda i: (i, 0)
    const = lambda i: (0, 0)
    return pl.pallas_call(
        _merge_kernel,
        grid=(T // tm,),
        in_specs=[pl.BlockSpec((tm, D), row),
                  pl.BlockSpec((1, 6, D), lambda i: (i // per_b, 0, 0)),
                  pl.BlockSpec((1, D), const),
                  pl.BlockSpec((tm, GM_WIDTH), row),
                  pl.BlockSpec((tm, GM_WIDTH), row),
                  pl.BlockSpec((tm, DN_WIDTH), row),
                  pl.BlockSpec((tm, 3 * D), row),
                  pl.BlockSpec((GM_WIDTH, D), const),
                  pl.BlockSpec((GM_WIDTH, D), const),
                  pl.BlockSpec((DN_WIDTH, D), const),
                  pl.BlockSpec((D, D), const)],
        out_specs=[pl.BlockSpec((tm, D), row), pl.BlockSpec((tm, D), row)],
        out_shape=[jax.ShapeDtypeStruct((T, D), F32), jax.ShapeDtypeStruct((T, D), h2_dtype)],
        compiler_params=_params("arbitrary"),
        name="merge",
    )(x, mod, g2, y_gm, y_da, y_dn, gates, w_gm, w_da, w_dn, w_out)


def _ffn_kernel(x_ref, h_ref, mod_ref, wg_ref, wu_ref, wd_ref, o_ref):
    h = h_ref[...]
    a = _dot(h, wg_ref[...])
    b = _dot(h, wu_ref[...])
    f = _dot((_silu(a) * b).astype(BF16), wd_ref[...])
    o_ref[...] = x_ref[...] + mod_ref[0, 5:6, :] * f


def _dense_ffn(x1, h2, mod, w_gate, w_up, w_down):
    T, D = x1.shape
    F = w_gate.shape[1]
    tm = TM_FFN
    per_b = SEQ // tm
    row = lambda i: (i, 0)
    const = lambda i: (0, 0)
    return pl.pallas_call(
        _ffn_kernel,
        grid=(T // tm,),
        in_specs=[pl.BlockSpec((tm, D), row),
                  pl.BlockSpec((tm, D), row),
                  pl.BlockSpec((1, 6, D), lambda i: (i // per_b, 0, 0)),
                  pl.BlockSpec((D, F), const, pipeline_mode=pl.Buffered(1)),
                  pl.BlockSpec((D, F), const, pipeline_mode=pl.Buffered(1)),
                  pl.BlockSpec((F, D), const, pipeline_mode=pl.Buffered(1))],
        out_specs=pl.BlockSpec((tm, D), row),
        out_shape=jax.ShapeDtypeStruct((T, D), F32),
        compiler_params=_params("arbitrary"),
        name="dense_ffn",
    )(x1, h2, mod, w_gate, w_up, w_down)


def _router_kernel(h_ref, wr_ref, e_ref, p_ref, r_ref, cnt_ref, base_ref):
    i = pl.program_id(0)
    tm = h_ref.shape[0]

    @pl.when(i == 0)
    def _():
        base_ref[...] = jnp.zeros_like(base_ref)

    logits = _dot_nt(wr_ref[...], h_ref[...].astype(BF16))
    row = lax.broadcasted_iota(jnp.int32, logits.shape, 0)
    m1 = jnp.max(logits, axis=0, keepdims=True)
    i1 = jnp.min(jnp.where(logits == m1, row, N_EXPERTS), axis=0, keepdims=True)
    rest = jnp.where(row == i1, -jnp.inf, logits)
    m2 = jnp.max(rest, axis=0, keepdims=True)
    i2 = jnp.min(jnp.where(rest == m2, row, N_EXPERTS), axis=0, keepdims=True)
    e2 = jnp.exp(m2 - m1)
    w1 = 1.0 / (1.0 + e2)
    w2 = e2 / (1.0 + e2)
    oh1 = jnp.where(row == i1, 1.0, 0.0)
    oh2 = jnp.where(row == i2, 1.0, 0.0)
    both = oh1 + oh2
    ti = lax.broadcasted_iota(jnp.int32, (tm, tm), 0)
    tj = lax.broadcasted_iota(jnp.int32, (tm, tm), 1)
    before = jnp.where(ti < tj, 1.0, 0.0).astype(BF16)
    pos = base_ref[...] + _dot(both.astype(BF16), before)
    r1 = jnp.sum(oh1 * pos, axis=0, keepdims=True)
    r2 = jnp.sum(oh2 * pos, axis=0, keepdims=True)
    base_ref[...] = base_ref[...] + jnp.sum(both, axis=1, keepdims=True)
    zi = jnp.zeros((N_EXPERTS - 2, tm), jnp.int32)
    zf = jnp.zeros((N_EXPERTS - 2, tm), F32)
    e_ref[...] = jnp.concatenate([i1, i2, zi], axis=0)
    p_ref[...] = jnp.concatenate([w1, w2, zf], axis=0)
    r_ref[...] = jnp.concatenate([r1.astype(jnp.int32), r2.astype(jnp.int32), zi], axis=0)
    cnt_ref[...] = jnp.broadcast_to(base_ref[...], cnt_ref.shape)


def _router(h2, w_router_t):
    T, D = h2.shape
    tm = TM_ROUTE
    col = lambda i: (0, i)
    return pl.pallas_call(
        _router_kernel,
        grid=(T // tm,),
        in_specs=[pl.BlockSpec((tm, D), lambda i: (i, 0)),
                  pl.BlockSpec((N_EXPERTS, D), lambda i: (0, 0))],
        out_specs=[pl.BlockSpec((N_EXPERTS, tm), col)] * 3 + [pl.BlockSpec((N_EXPERTS, LANES), lambda i: (0, 0))],
        out_shape=[jax.ShapeDtypeStruct((N_EXPERTS, T), jnp.int32),
                   jax.ShapeDtypeStruct((N_EXPERTS, T), F32),
                   jax.ShapeDtypeStruct((N_EXPERTS, T), jnp.int32),
                   jax.ShapeDtypeStruct((N_EXPERTS, LANES), F32)],
        scratch_shapes=[pltpu.VMEM((N_EXPERTS, 1), F32)],
        compiler_params=_params("arbitrary"),
        name="moe_router",
    )(h2, w_router_t)


def _dispatch_kernel(dest_ref, h_ref, xs_in_ref, xs_ref, sem):
    del xs_in_ref
    tm = h_ref.shape[0]

    def issue(r, _):
        for k in range(2):
            pltpu.make_async_copy(h_ref.at[pl.ds(r, 1)],
                                  xs_ref.at[pl.ds(dest_ref[0, 0, k * tm + r], 1)], sem).start()
        return 0

    lax.fori_loop(0, tm, issue, 0)
    for _ in range(2):
        pltpu.make_async_copy(h_ref, xs_ref.at[pl.ds(0, tm)], sem).wait()


def _dispatch(h2, dest, n_slots):
    T, D = h2.shape
    tm = TM_DISP
    dest_t = dest.reshape(2, T // tm, tm).transpose(1, 0, 2).reshape(T // tm, 1, 2 * tm)
    xs0 = jnp.zeros((n_slots, D), h2.dtype)
    return pl.pallas_call(
        _dispatch_kernel,
        grid=(T // tm,),
        in_specs=[pl.BlockSpec((1, 1, 2 * tm), lambda i: (i, 0, 0), memory_space=pltpu.SMEM),
                  pl.BlockSpec((tm, D), lambda i: (i, 0)),
                  pl.BlockSpec(memory_space=pl.ANY)],
        out_specs=pl.BlockSpec(memory_space=pl.ANY),
        out_shape=jax.ShapeDtypeStruct((n_slots, D), h2.dtype),
        scratch_shapes=[pltpu.SemaphoreType.DMA(())],
        input_output_aliases={2: 0},
        compiler_params=_params("arbitrary"),
        name="moe_dispatch",
    )(dest_t, h2, xs0)


def _expert_kernel(be_ref, nu_ref, x_ref, wg_ref, wu_ref, wd_ref, o_ref, xb_ref, acc_ref):
    i = pl.program_id(0)
    j = pl.program_id(1)
    last = pl.num_programs(1) - 1
    used = i < nu_ref[0]

    @pl.when(jnp.logical_and(used, j == 0))
    def _():
        xb_ref[...] = x_ref[...].astype(BF16)
        acc_ref[...] = jnp.zeros_like(acc_ref)

    @pl.when(used)
    def _():
        xb = xb_ref[...]
        a = _dot(xb, wg_ref[0])
        b = _dot(xb, wu_ref[0])
        acc_ref[...] += _dot((_silu(a) * b).astype(BF16), wd_ref[0])

    @pl.when(jnp.logical_and(used, j == last))
    def _():
        o_ref[...] = acc_ref[...]

    @pl.when(jnp.logical_and(jnp.logical_not(used), j == last))
    def _():
        o_ref[...] = jnp.zeros_like(o_ref)


def _experts(xs, blk_e, n_used, w_gate, w_up, w_down):
    P, D = xs.shape
    F = w_gate.shape[2]
    tm, fc = TM_MOE, FC_MOE
    nf = F // fc
    fsel = lambda i, j, nu: jnp.where(i < nu[0], j, nf - 1)
    grid_spec = pltpu.PrefetchScalarGridSpec(
        num_scalar_prefetch=2,
        grid=(P // tm, nf),
        in_specs=[pl.BlockSpec((tm, D), lambda i, j, be, nu: (i, 0)),
                  pl.BlockSpec((1, D, fc), lambda i, j, be, nu: (be[i], 0, fsel(i, j, nu))),
                  pl.BlockSpec((1, D, fc), lambda i, j, be, nu: (be[i], 0, fsel(i, j, nu))),
                  pl.BlockSpec((1, fc, D), lambda i, j, be, nu: (be[i], fsel(i, j, nu), 0))],
        out_specs=pl.BlockSpec((tm, D), lambda i, j, be, nu: (i, 0)),
        scratch_shapes=[pltpu.VMEM((tm, D), BF16), pltpu.VMEM((tm, D), F32)])
    return pl.pallas_call(
        _expert_kernel,
        grid_spec=grid_spec,
        out_shape=jax.ShapeDtypeStruct((P, D), F32),
        compiler_params=_params("arbitrary", "arbitrary"),
        name="moe_experts",
    )(blk_e, n_used, xs, w_gate, w_up, w_down)


def _combine_kernel(dest_ref, x_ref, mod_ref, p_ref, g_ref, ys_ref, o_ref, ybuf_ref, sem):
    tm = x_ref.shape[0]

    def issue(r, _):
        for k in range(2):
            pltpu.make_async_copy(ys_ref.at[pl.ds(dest_ref[0, 0, k * tm + r], 1)],
                                  ybuf_ref.at[k, pl.ds(r, 1)], sem).start()
        return 0

    lax.fori_loop(0, tm, issue, 0)
    for k in range(2):
        pltpu.make_async_copy(ys_ref.at[pl.ds(0, tm)], ybuf_ref.at[k], sem).wait()
    p = p_ref[...]
    f = ybuf_ref[0] * p[:, 0:1] + ybuf_ref[1] * p[:, 1:2]
    x2 = x_ref[...] + mod_ref[0, 5:6, :] * f
    o_ref[...] = _rms(x2, g_ref[...])


def _combine(x1, mod, probs, final_g, ys, dest):
    T, D = x1.shape
    tm = TM_COMB
    per_b = SEQ // tm
    dest_t = dest.reshape(2, T // tm, tm).transpose(1, 0, 2).reshape(T // tm, 1, 2 * tm)
    return pl.pallas_call(
        _combine_kernel,
        grid=(T // tm,),
        in_specs=[pl.BlockSpec((1, 1, 2 * tm), lambda i: (i, 0, 0), memory_space=pltpu.SMEM),
                  pl.BlockSpec((tm, D), lambda i: (i, 0)),
                  pl.BlockSpec((1, 6, D), lambda i: (i // per_b, 0, 0)),
                  pl.BlockSpec((tm, 2), lambda i: (i, 0)),
                  pl.BlockSpec((1, D), lambda i: (0, 0)),
                  pl.BlockSpec(memory_space=pl.ANY)],
        out_specs=pl.BlockSpec((tm, D), lambda i: (i, 0)),
        out_shape=jax.ShapeDtypeStruct((T, D), F32),
        scratch_shapes=[pltpu.VMEM((2, tm, D), F32), pltpu.SemaphoreType.DMA(())],
        compiler_params=_params("arbitrary"),
        name="moe_combine",
    )(dest_t, x1, mod, probs, final_g, ys)


def _moe_layer(x1, h2, mod, w_router, w_gate, w_up, w_down, final_g):
    T, D = x1.shape
    e_idx, probs, rank, counts = _router(h2, w_router.T.astype(BF16))
    counts = counts[:, 0].astype(jnp.int32)
    padded = (counts + TM_MOE - 1) // TM_MOE * TM_MOE
    pad_end = jnp.cumsum(padded)
    pad_start = pad_end - padded
    eid = jnp.arange(N_EXPERTS, dtype=jnp.int32)[:, None, None]
    dest = jnp.sum(jnp.where(e_idx[None, :2] == eid, pad_start[:, None, None], 0), axis=0) + rank[:2]
    n_slots = 2 * T + N_EXPERTS * TM_MOE
    n_blk = n_slots // TM_MOE
    blk_e = jnp.minimum(jnp.searchsorted(pad_end, jnp.arange(n_blk, dtype=jnp.int32) * TM_MOE,
                                         side='right'), N_EXPERTS - 1).astype(jnp.int32)
    n_used = (pad_end[-1:] // TM_MOE).astype(jnp.int32)
    xs = _dispatch(h2, dest, n_slots)
    ys = _experts(xs, blk_e, n_used, w_gate.astype(BF16), w_up.astype(BF16), w_down.astype(BF16))
    return _combine(x1, mod, probs[:2].T, final_g.reshape(1, -1), ys, dest)


def _final_norm_kernel(x_ref, g_ref, o_ref):
    o_ref[...] = _rms(x_ref[...], g_ref[...])


def _final_norm(x, g):
    T, D = x.shape
    tm = 1024
    return pl.pallas_call(
        _final_norm_kernel,
        grid=(T // tm,),
        in_specs=[pl.BlockSpec((tm, D), lambda i: (i, 0)), pl.BlockSpec((1, D), lambda i: (0, 0))],
        out_specs=pl.BlockSpec((tm, D), lambda i: (i, 0)),
        out_shape=jax.ShapeDtypeStruct((T, D), F32),
        compiler_params=_params("arbitrary"),
        name="final_norm",
    )(x, g.reshape(1, -1))


def _regroup_w_in(w):
    ab = jnp.pad(w[:, 4608:4616], ((0, 0), (0, LANES - 8)))
    return jnp.concatenate([w[:, :4608], w[:, 4616:], ab], axis=1).astype(BF16)


def _lambda_init(layer):
    return 0.8 - 0.6 * math.exp(-0.3 * layer)


def kernel(x, c, positions, norm1_g, norm2_g, w_mod, b_mod, w_in, gm_ln_g, gm_ln_b, gm_w_s, gm_b_s, da_lambda, da_subln_g, dn_conv_w, dn_a_log, dn_dt_bias, dn_norm_g, w_br_gm, w_br_da, w_br_dn, w_out, ffn_w_gate, ffn_w_up, ffn_w_down, moe_w_router, moe_w_gate, moe_w_up, moe_w_down, final_g):
    B, S, D = x.shape
    T = B * S
    xt = x.reshape(T, D)
    mod_all = _modulation(c, w_mod, b_mod).reshape(DEPTH, B, 6, D)
    cos_t, sin_t = _rope_tables(positions)
    for layer in range(DEPTH):
        mod = mod_all[layer]
        gm, qk, v, dn, gates, ab = _in_projection(xt, mod, norm1_g[layer].reshape(1, D),
                                                  _regroup_w_in(w_in[layer]), cos_t, sin_t)
        y_gm = _gmlp(gm, gm_ln_g[layer], gm_ln_b[layer], gm_w_s[layer], gm_b_s[layer])
        y_da = _diff_attention(qk, v, da_lambda[layer], da_subln_g[layer], _lambda_init(layer), B)
        y_dn = _deltanet(dn, ab, dn_conv_w[layer], dn_a_log[layer], dn_dt_bias[layer],
                         dn_norm_g[layer], B)
        moe = layer % 2 == 1
        x1, h2 = _merge(xt, mod, norm2_g[layer].reshape(1, D), y_gm, y_da, y_dn, gates,
                        w_br_gm[layer].astype(BF16), w_br_da[layer].astype(BF16),
                        w_br_dn[layer].astype(BF16), w_out[layer].astype(BF16),
                        F32 if moe else BF16)
        if moe:
            xt = _moe_layer(x1, h2, mod, moe_w_router[layer // 2], moe_w_gate[layer // 2],
                            moe_w_up[layer // 2], moe_w_down[layer // 2], final_g)
        else:
            xt = _dense_ffn(x1, h2, mod, ffn_w_gate[layer // 2].astype(BF16),
                            ffn_w_up[layer // 2].astype(BF16), ffn_w_down[layer // 2].astype(BF16))
    if DEPTH % 2 == 1:
        xt = _final_norm(xt, final_g)
    return xt.reshape(B, S, D)
```

```python
import functools
import math

import jax
import jax.numpy as jnp
import numpy as np
from jax import lax
from jax.experimental import pallas as pl
from jax.experimental.pallas import tpu as pltpu

D_MODEL = 1024
SEQ = 2048
DEPTH = 2
CHUNK = 64
EPS = 1e-6
GM_WIDTH = D_MODEL // 2
GM_GROUPS = 4
GM_BLOCK = 128
DA_HEADS = 4
DA_HEAD_DIM = 64
DA_V_DIM = 2 * DA_HEAD_DIM
ROPE_THETA = 10000.0
DN_HEADS = 4
DN_HEAD_DIM = 128
DN_WIDTH = DN_HEADS * DN_HEAD_DIM
DN_CONV = 4
N_EXPERTS = 8
FF_EXPERT = 7 * D_MODEL // 2

LANES = 128
VMEM_LIMIT = 56 * 1024 * 1024

C_GM, C_QK, C_V, C_DN, C_GATE, C_AB = 0, 1024, 2048, 2560, 4608, 7680
W_ALL = C_AB + LANES
PROJ_CHUNK = 512

TM_PROJ = 512
TM_GM = 512
TQ = 256
TM_MERGE = 512
TM_FFN = 512
TM_ROUTE = 512
TM_MOE = 512
TM_DISP = 256
TM_COMB = 256
ATTN_LOOKAHEAD = 3
DN_BLOCK = 256

BF16 = jnp.bfloat16
F32 = jnp.float32


def _params(*sem):
    return pltpu.CompilerParams(dimension_semantics=sem, vmem_limit_bytes=VMEM_LIMIT)


def _dot(a, b):
    return jnp.dot(a, b, preferred_element_type=F32)


def _dot_nt(a, b):
    return lax.dot_general(a, b, (((1,), (1,)), ((), ())), preferred_element_type=F32)


def _dot_tn(a, b):
    return lax.dot_general(a, b, (((0,), (0,)), ((), ())), preferred_element_type=F32)


def _dot_hi(a, b):
    return jnp.dot(a, b, preferred_element_type=F32, precision=lax.Precision.HIGHEST)


def _rms(x, g):
    return x * lax.rsqrt(jnp.mean(x * x, axis=-1, keepdims=True) + EPS) * g


def _silu(x):
    return x * jax.nn.sigmoid(x)


def _mod_kernel(c_ref, w_ref, b_ref, o_ref):
    c = c_ref[...]
    o_ref[0] = _dot_hi(_silu(c), w_ref[0]) + b_ref[0]


def _modulation(c, w_mod, b_mod):
    B, D = c.shape
    L, _, N = w_mod.shape
    tn = 1536
    return pl.pallas_call(
        _mod_kernel,
        grid=(L, N // tn),
        in_specs=[pl.BlockSpec((B, D), lambda l, j: (0, 0)),
                  pl.BlockSpec((1, D, tn), lambda l, j: (l, 0, j)),
                  pl.BlockSpec((1, 1, tn), lambda l, j: (l, 0, j))],
        out_specs=pl.BlockSpec((1, B, tn), lambda l, j: (l, 0, j)),
        out_shape=jax.ShapeDtypeStruct((L, B, N), F32),
        compiler_params=_params("arbitrary", "arbitrary"),
        name="modulation",
    )(c, w_mod, b_mod.reshape(L, 1, N))


def _rope_tab_kernel(pos_ref, inv_ref, sgn_ref, cos_ref, sin_ref):
    ang = pos_ref[...].astype(F32) * inv_ref[...]
    cos_ref[...] = jnp.cos(ang)
    sin_ref[...] = jnp.sin(ang) * sgn_ref[...]


def _rope_tables(positions):
    T = positions.size
    inv_freq = ROPE_THETA ** (-jnp.arange(0, DA_HEAD_DIM, 2, dtype=F32) / DA_HEAD_DIM)
    inv = jnp.tile(inv_freq, LANES // (DA_HEAD_DIM // 2)).reshape(1, LANES)
    half = DA_HEAD_DIM // 2
    sgn = np.tile(np.concatenate([-np.ones(half), np.ones(half)]), LANES // DA_HEAD_DIM)
    sgn = jnp.asarray(sgn, F32).reshape(1, LANES)
    tm = 1024
    return pl.pallas_call(
        _rope_tab_kernel,
        grid=(T // tm,),
        in_specs=[pl.BlockSpec((tm, 1), lambda i: (i, 0)),
                  pl.BlockSpec((1, LANES), lambda i: (0, 0)),
                  pl.BlockSpec((1, LANES), lambda i: (0, 0))],
        out_specs=[pl.BlockSpec((tm, LANES), lambda i: (i, 0))] * 2,
        out_shape=[jax.ShapeDtypeStruct((T, LANES), F32)] * 2,
        compiler_params=_params("arbitrary"),
        name="rope_tables",
    )(positions.reshape(T, 1), inv, sgn)


def _rope_rows(x, cos, sin):
    lane = lax.broadcasted_iota(jnp.int32, x.shape, 1)
    first = (lane % DA_HEAD_DIM) < (DA_HEAD_DIM // 2)
    partner = jnp.where(first, pltpu.roll(x, LANES - DA_HEAD_DIM // 2, axis=1),
                        pltpu.roll(x, DA_HEAD_DIM // 2, axis=1))
    return x * cos + partner * sin


def _proj_kernel(x_ref, mod_ref, g_ref, w_ref, cos_ref, sin_ref,
                 gm_ref, qk_ref, v_ref, dn_ref, gate_ref, ab_ref):
    x = x_ref[...]
    shift = mod_ref[0, 0:1, :]
    scale = mod_ref[0, 1:2, :]
    h = (_rms(x, g_ref[...]) * (1.0 + scale) + shift).astype(BF16)
    cos = cos_ref[...]
    sin = sin_ref[...]
    q_scale = DA_HEAD_DIM ** -0.5 * math.log2(math.e)

    def rope_chunk(y, mult):
        parts = [_rope_rows(y[:, j * LANES:(j + 1) * LANES], cos, sin) * mult
                 for j in range(PROJ_CHUNK // LANES)]
        return jnp.concatenate(parts, axis=1)

    for c0 in range(0, C_AB, PROJ_CHUNK):
        y = _dot(h, w_ref[:, c0:c0 + PROJ_CHUNK])
        if c0 < C_QK:
            gm_ref[:, c0:c0 + PROJ_CHUNK] = y.astype(BF16)
        elif c0 < C_V:
            mult = q_scale if c0 == C_QK else 1.0
            qk_ref[:, c0 - C_QK:c0 - C_QK + PROJ_CHUNK] = rope_chunk(y, mult).astype(BF16)
        elif c0 < C_DN:
            v_ref[...] = y.astype(BF16)
        elif c0 < C_GATE:
            dn_ref[:, c0 - C_DN:c0 - C_DN + PROJ_CHUNK] = y.astype(BF16)
        else:
            gate_ref[:, c0 - C_GATE:c0 - C_GATE + PROJ_CHUNK] = y.astype(BF16)
    ab_ref[...] = _dot(h, w_ref[:, C_AB:W_ALL])


def _in_projection(x, mod, g, w_all, cos_t, sin_t):
    T, D = x.shape
    tm = TM_PROJ
    per_b = SEQ // tm
    row = lambda i: (i, 0)
    widths = (C_QK - C_GM, C_V - C_QK, C_DN - C_V, C_GATE - C_DN, C_AB - C_GATE)
    return pl.pallas_call(
        _proj_kernel,
        grid=(T // tm,),
        in_specs=[pl.BlockSpec((tm, D), row),
                  pl.BlockSpec((1, 6, D), lambda i: (i // per_b, 0, 0)),
                  pl.BlockSpec((1, D), lambda i: (0, 0)),
                  pl.BlockSpec((D, W_ALL), lambda i: (0, 0), pipeline_mode=pl.Buffered(1)),
                  pl.BlockSpec((tm, LANES), row),
                  pl.BlockSpec((tm, LANES), row)],
        out_specs=[pl.BlockSpec((tm, w), row) for w in widths] + [pl.BlockSpec((tm, LANES), row)],
        out_shape=[jax.ShapeDtypeStruct((T, w), BF16) for w in widths]
                  + [jax.ShapeDtypeStruct((T, LANES), F32)],
        compiler_params=_params("arbitrary"),
        name="in_projection",
    )(x, mod, g, w_all, cos_t, sin_t)


def _gelu(x):
    return 0.5 * x * (1.0 + lax.erf(x * np.float32(math.sqrt(0.5))))


def _gmlp_kernel(uv_ref, lng_ref, lnb_ref, ws_ref, bs_ref, o_ref):
    u = _gelu(uv_ref[:, :GM_WIDTH].astype(F32))
    v = _gelu(uv_ref[:, GM_WIDTH:].astype(F32))
    mu = jnp.mean(v, axis=-1, keepdims=True)
    vc = v - mu
    var = jnp.mean(vc * vc, axis=-1, keepdims=True)
    v = (vc * lax.rsqrt(var + EPS) * lng_ref[...] + lnb_ref[...]).astype(BF16)
    ri = lax.broadcasted_iota(jnp.int32, (GM_BLOCK, GM_BLOCK), 0) // CHUNK
    ci = lax.broadcasted_iota(jnp.int32, (GM_BLOCK, GM_BLOCK), 1) // CHUNK
    allowed = ci <= ri
    gc = GM_WIDTH // GM_GROUPS
    for g in range(GM_GROUPS):
        w = jnp.where(allowed, ws_ref[g], 0.0).astype(BF16)
        bias = bs_ref[g]
        for r in range(uv_ref.shape[0] // GM_BLOCK):
            rows = slice(r * GM_BLOCK, (r + 1) * GM_BLOCK)
            cols = slice(g * gc, (g + 1) * gc)
            sv = _dot(w, v[rows, cols]) + bias
            o_ref[rows, cols] = (u[rows, cols] * sv).astype(BF16)


def _gmlp(uv, ln_g, ln_b, w_s, b_s):
    T = uv.shape[0]
    tm = TM_GM
    return pl.pallas_call(
        _gmlp_kernel,
        grid=(T // tm,),
        in_specs=[pl.BlockSpec((tm, 2 * GM_WIDTH), lambda i: (i, 0)),
                  pl.BlockSpec((1, GM_WIDTH), lambda i: (0, 0)),
                  pl.BlockSpec((1, GM_WIDTH), lambda i: (0, 0)),
                  pl.BlockSpec((GM_GROUPS, GM_BLOCK, GM_BLOCK), lambda i: (0, 0, 0)),
                  pl.BlockSpec((GM_GROUPS, GM_BLOCK, 1), lambda i: (0, 0, 0))],
        out_specs=pl.BlockSpec((tm, GM_WIDTH), lambda i: (i, 0)),
        out_shape=jax.ShapeDtypeStruct((T, GM_WIDTH), BF16),
        compiler_params=_params("arbitrary"),
        name="gmlp",
    )(uv, ln_g.reshape(1, -1), ln_b.reshape(1, -1), w_s, b_s.reshape(GM_GROUPS, GM_BLOCK, 1))


def _attn_kernel(lam_ref, g_ref, q_ref, k_ref, v_ref, o_ref, qz_ref, vt_ref, *stats, lambda_init):
    nc = 2 * DA_HEADS
    m_ref, l_ref, acc_ref = stats[:nc], stats[nc:2 * nc], stats[2 * nc:]
    i = pl.program_id(1)
    lp = lam_ref[...]
    lam = (jnp.exp(jnp.sum(lp[0:1] * lp[1:2], axis=-1, keepdims=True))
           - jnp.exp(jnp.sum(lp[2:3] * lp[3:4], axis=-1, keepdims=True)) + lambda_init)
    @pl.when(i == 0)
    def _():
        for kb in range(SEQ // TQ):
            for h in range(DA_HEADS):
                vt_ref[h * LANES:(h + 1) * LANES, kb * TQ:(kb + 1) * TQ] = (
                    v_ref[kb * TQ:(kb + 1) * TQ, h * LANES:(h + 1) * LANES].T)

    row = lax.broadcasted_iota(jnp.int32, (LANES, TQ), 0)
    for h in range(DA_HEADS):
        qt = q_ref[:, h * LANES:(h + 1) * LANES].T
        zero = jnp.zeros_like(qt)
        qz_ref[2 * h] = jnp.where(row < DA_HEAD_DIM, qt, zero)
        qz_ref[2 * h + 1] = jnp.where(row >= DA_HEAD_DIM, qt, zero)
    for c in range(nc):
        m_ref[c][...] = jnp.full_like(m_ref[c], -jnp.inf)
        l_ref[c][...] = jnp.zeros_like(l_ref[c])
        acc_ref[c][...] = jnp.zeros_like(acc_ref[c])

    def block(kb, diagonal):
        rows = pl.ds(pl.multiple_of(kb * TQ, TQ), TQ)

        def scores(c):
            h = c // 2
            s = _dot(k_ref[rows, h * LANES:(h + 1) * LANES], qz_ref[c])
            if diagonal:
                ki = lax.broadcasted_iota(jnp.int32, s.shape, 0) // CHUNK
                qi = lax.broadcasted_iota(jnp.int32, s.shape, 1) // CHUNK
                s = jnp.where(ki <= qi, s, -jnp.inf)
            return s

        def accumulate(c, s):
            h = c // 2
            m_old = m_ref[c][...]
            m_new = jnp.maximum(m_old, jnp.max(s, axis=0, keepdims=True))
            alpha = jnp.exp2(m_old - m_new)
            e = jnp.exp2(s - m_new)
            l_ref[c][...] = alpha * l_ref[c][...] + jnp.sum(e, axis=0, keepdims=True)
            vt = vt_ref[h * LANES:(h + 1) * LANES, rows]
            acc_ref[c][...] = alpha * acc_ref[c][...] + _dot(vt, e.astype(BF16))
            m_ref[c][...] = m_new

        pending = {}
        for c in range(nc + ATTN_LOOKAHEAD):
            if c < nc:
                pending[c] = scores(c)
            if c >= ATTN_LOOKAHEAD:
                accumulate(c - ATTN_LOOKAHEAD, pending.pop(c - ATTN_LOOKAHEAD))

    def body(kb, carry):
        block(kb, False)
        return carry

    lax.fori_loop(0, i, body, 0)
    block(i, True)
    for h in range(DA_HEADS):
        o = (acc_ref[2 * h][...] / l_ref[2 * h][...]
             - lam * (acc_ref[2 * h + 1][...] / l_ref[2 * h + 1][...]))
        ms = jnp.mean(o * o, axis=0, keepdims=True)
        o = o * lax.rsqrt(ms + EPS) * (g_ref[...] * (1.0 - lambda_init))
        o_ref[:, h * LANES:(h + 1) * LANES] = o.T.astype(BF16)


def _diff_attention(qk, v, lam_p, subln_g, lambda_init, batch):
    T = qk.shape[0]
    nq = SEQ // TQ
    W = DA_HEADS * DA_V_DIM
    kernel = functools.partial(_attn_kernel, lambda_init=lambda_init)
    return pl.pallas_call(
        kernel,
        grid=(batch, nq),
        in_specs=[pl.BlockSpec((4, DA_HEAD_DIM), lambda b, i: (0, 0)),
                  pl.BlockSpec((DA_V_DIM, 1), lambda b, i: (0, 0)),
                  pl.BlockSpec((TQ, W), lambda b, i: (b * nq + i, 0)),
                  pl.BlockSpec((SEQ, W), lambda b, i: (b, 1)),
                  pl.BlockSpec((SEQ, W), lambda b, i: (b, 0))],
        out_specs=pl.BlockSpec((TQ, W), lambda b, i: (b * nq + i, 0)),
        out_shape=jax.ShapeDtypeStruct((T, W), BF16),
        scratch_shapes=[pltpu.VMEM((2 * DA_HEADS, LANES, TQ), BF16), pltpu.VMEM((W, SEQ), BF16)]
                       + [pltpu.VMEM((1, TQ), F32)] * (4 * DA_HEADS)
                       + [pltpu.VMEM((DA_V_DIM, TQ), F32)] * (2 * DA_HEADS),
        compiler_params=_params("arbitrary", "arbitrary"),
        name="diff_attention",
    )(lam_p, subln_g.reshape(-1, 1), qk, qk, v)


def _split3(x):
    hi = x.astype(BF16)
    r = x - hi.astype(F32)
    mid = r.astype(BF16)
    lo = (r - mid.astype(F32)).astype(BF16)
    return hi, mid, lo


def _dn_kernel(dn_ref, a_ref, at_ref, cw_ref, alog_ref, dtb_ref, alogt_ref, dtbt_ref, ng_ref,
               o_ref, prev_ref, *state_ref):
    c = pl.program_id(1)
    R = DN_BLOCK

    @pl.when(c == 0)
    def _():
        prev_ref[...] = jnp.zeros_like(prev_ref)
        for ref in state_ref:
            ref[...] = jnp.zeros_like(ref)

    W3 = 3 * DN_WIDTH
    x = dn_ref[:, :W3].astype(F32)
    xcat = jnp.concatenate([prev_ref[...], x], axis=0)
    cw = cw_ref[...]
    y = x * cw[DN_CONV - 1:DN_CONV, :]
    for j in range(DN_CONV - 1):
        s = DN_CONV - 1 - j
        y = y + pltpu.roll(xcat, s, axis=0)[8:, :] * cw[j:j + 1, :]
    prev_ref[...] = x[R - 8:, :]
    y = _silu(y)

    ab = a_ref[...]
    g_col = -jnp.exp(alog_ref[...]) * jax.nn.softplus(ab + dtb_ref[...])
    beta_all = jax.nn.sigmoid(ab)
    abt = at_ref[0]
    g_row = -jnp.exp(alogt_ref[...]) * jax.nn.softplus(abt + dtbt_ref[...])
    ri = lax.broadcasted_iota(jnp.int32, (R, R), 0)
    ci = lax.broadcasted_iota(jnp.int32, (R, R), 1)
    tril = ri >= ci
    strict = ri > ci
    ones_tril = jnp.where(tril, 1.0, 0.0).astype(BF16)
    ones_triu = jnp.where(ci >= ri, 1.0, 0.0).astype(BF16)
    gc_col = sum(_dot(ones_tril, part) for part in _split3(g_col))
    gc_row = sum(_dot(part, ones_triu) for part in _split3(g_row))
    eye = jnp.where(ri == ci, 1.0, 0.0)
    same = {n: (ri // n) == (ci // n) for n in (16, 32, 64, 128)}

    H = range(DN_HEADS)
    hsl = [slice(h * DN_HEAD_DIM, (h + 1) * DN_HEAD_DIM) for h in H]
    q = [y[:, hsl[h]] for h in H]
    k = [y[:, DN_WIDTH + h * DN_HEAD_DIM:DN_WIDTH + (h + 1) * DN_HEAD_DIM] for h in H]
    v = [y[:, 2 * DN_WIDTH + h * DN_HEAD_DIM:2 * DN_WIDTH + (h + 1) * DN_HEAD_DIM] for h in H]
    q = [t * lax.rsqrt(jnp.sum(t * t, axis=-1, keepdims=True) + EPS) * (DN_HEAD_DIM ** -0.5) for t in q]
    k = [t * lax.rsqrt(jnp.sum(t * t, axis=-1, keepdims=True) + EPS) for t in k]
    gc = [gc_col[:, h:h + 1] for h in H]
    beta = [beta_all[:, DN_HEADS + h:DN_HEADS + h + 1] for h in H]
    g_last = [gc_col[R - 1:R, h:h + 1] for h in H]
    decay = [jnp.exp(jnp.where(tril, gc[h] - gc_row[h:h + 1, :], -jnp.inf)) for h in H]
    eg = [jnp.exp(t) for t in gc]
    kb = [k[h] * beta[h] for h in H]
    s1 = [_dot_nt(jnp.concatenate([kb[h], q[h]], axis=0).astype(BF16), k[h].astype(BF16)) for h in H]
    a_mat = [jnp.where(strict, s1[h][:R] * decay[h], 0.0) for h in H]
    qk = [(s1[h][R:] * decay[h]).astype(BF16) for h in H]
    p = [jnp.where(same[16], -t, 0.0) for t in a_mat]
    t_inv = [eye + t for t in p]
    for _ in range(3):
        p16 = [t.astype(BF16) for t in p]
        p = [_dot(t, t) for t in p16]
        t_inv = [t_inv[h] + _dot(t_inv[h].astype(BF16), p[h].astype(BF16)) for h in H]
    a16 = [t.astype(BF16) for t in a_mat]
    t16 = [t.astype(BF16) for t in t_inv]
    for n in (16, 32, 64, 128):
        off = jnp.logical_not(same[n])
        if 2 * n < R:
            off = jnp.logical_and(same[2 * n], off)
        off16 = jnp.where(off, 1.0, 0.0).astype(BF16)
        tl = [_dot(t16[h], a16[h] * off16).astype(BF16) for h in H]
        t16 = [t16[h] - _dot(tl[h], t16[h]).astype(BF16) * off16 for h in H]
    sol = [_dot(t16[h], jnp.concatenate([v[h] * beta[h], kb[h] * eg[h]], axis=1).astype(BF16))
           for h in H]
    state = [state_ref[h][...] for h in H]
    m1 = [_dot(jnp.concatenate([sol[h][:, DN_HEAD_DIM:], q[h] * eg[h]], axis=0).astype(BF16),
               state[h].astype(BF16)) for h in H]
    vn16 = [(sol[h][:, :DN_HEAD_DIM] - m1[h][:R]).astype(BF16) for h in H]
    o = [m1[h][R:] + _dot(qk[h], vn16[h]) for h in H]
    kd = [(k[h] * jnp.exp(g_last[h] - gc[h])).astype(BF16) for h in H]
    for h in H:
        state_ref[h][...] = state[h] * jnp.exp(g_last[h]) + _dot_tn(kd[h], vn16[h])
    for h in H:
        z = dn_ref[:, W3 + h * DN_HEAD_DIM:W3 + (h + 1) * DN_HEAD_DIM].astype(F32)
        o_ref[:, hsl[h]] = (_rms(o[h], ng_ref[...]) * _silu(z)).astype(BF16)


def _deltanet(dn, ab, conv_w, a_log, dt_bias, norm_g, batch):
    T = dn.shape[0]
    R = DN_BLOCK
    n = SEQ // R
    pad = lambda t: jnp.zeros((1, LANES), F32).at[0, :DN_HEADS].set(t)
    col8 = lambda t: jnp.zeros((8, 1), F32).at[:DN_HEADS, 0].set(t)
    ab_t = ab[:, :8].reshape(T // R, R, 8).transpose(0, 2, 1)
    return pl.pallas_call(
        _dn_kernel,
        grid=(batch, n),
        in_specs=[pl.BlockSpec((R, 4 * DN_WIDTH), lambda b, c: (b * n + c, 0)),
                  pl.BlockSpec((R, LANES), lambda b, c: (b * n + c, 0)),
                  pl.BlockSpec((1, 8, R), lambda b, c: (b * n + c, 0, 0)),
                  pl.BlockSpec((DN_CONV, 3 * DN_WIDTH), lambda b, c: (0, 0)),
                  pl.BlockSpec((1, LANES), lambda b, c: (0, 0)),
                  pl.BlockSpec((1, LANES), lambda b, c: (0, 0)),
                  pl.BlockSpec((8, 1), lambda b, c: (0, 0)),
                  pl.BlockSpec((8, 1), lambda b, c: (0, 0)),
                  pl.BlockSpec((1, DN_HEAD_DIM), lambda b, c: (0, 0))],
        out_specs=pl.BlockSpec((R, DN_WIDTH), lambda b, c: (b * n + c, 0)),
        out_shape=jax.ShapeDtypeStruct((T, DN_WIDTH), BF16),
        scratch_shapes=[pltpu.VMEM((8, 3 * DN_WIDTH), F32)]
                       + [pltpu.VMEM((DN_HEAD_DIM, DN_HEAD_DIM), F32)] * DN_HEADS,
        compiler_params=_params("arbitrary", "arbitrary"),
        name="deltanet",
    )(dn, ab, ab_t, conv_w, pad(a_log), pad(dt_bias), col8(a_log), col8(dt_bias),
      norm_g.reshape(1, -1))


def _pack_bf16_pair(lo, hi):
    lo_bits = lax.bitcast_convert_type(lo.astype(BF16).astype(F32), jnp.uint32)
    hi_bits = lax.bitcast_convert_type(hi.astype(BF16).astype(F32), jnp.uint32)
    return (lo_bits >> 16) | (hi_bits & jnp.uint32(0xFFFF0000))


def _unpack_bf16_pair(words):
    lo = lax.bitcast_convert_type(words << 16, F32)
    hi = lax.bitcast_convert_type(words & jnp.uint32(0xFFFF0000), F32)
    return lo.astype(BF16), hi.astype(BF16)


def _merge_kernel(x_ref, mod_ref, g2_ref, ygm_ref, yda_ref, ydn_ref, gate_ref,
                  wgm_ref, wda_ref, wdn_ref, wout_ref, x1_ref, h2_ref, *maybe_packed_ref):
    D = D_MODEL
    merged = (jax.nn.sigmoid(gate_ref[:, 0:D].astype(F32)) * _dot(ygm_ref[...], wgm_ref[...])
              + jax.nn.sigmoid(gate_ref[:, D:2 * D].astype(F32)) * _dot(yda_ref[...], wda_ref[...])
              + jax.nn.sigmoid(gate_ref[:, 2 * D:3 * D].astype(F32)) * _dot(ydn_ref[...], wdn_ref[...]))
    y = _dot(merged.astype(BF16), wout_ref[...])
    x1 = x_ref[...] + mod_ref[0, 2:3, :] * y
    x1_ref[...] = x1
    h2 = _rms(x1, g2_ref[...]) * (1.0 + mod_ref[0, 4:5, :]) + mod_ref[0, 3:4, :]
    h2_ref[...] = h2.astype(BF16)
    for packed_ref in maybe_packed_ref:
        packed_ref[...] = _pack_bf16_pair(h2[:, :D // 2], h2[:, D // 2:])


def _merge(x, mod, g2, y_gm, y_da, y_dn, gates, w_gm, w_da, w_dn, w_out, with_packed):
    T, D = x.shape
    tm = TM_MERGE
    per_b = SEQ // tm
    row = lambda i: (i, 0)
    const = lambda i: (0, 0)
    out_specs = [pl.BlockSpec((tm, D), row), pl.BlockSpec((tm, D), row)]
    out_shape = [jax.ShapeDtypeStruct((T, D), F32), jax.ShapeDtypeStruct((T, D), BF16)]
    if with_packed:
        out_specs.append(pl.BlockSpec((tm, D // 2), row))
        out_shape.append(jax.ShapeDtypeStruct((T, D // 2), jnp.uint32))
    return pl.pallas_call(
        _merge_kernel,
        grid=(T // tm,),
        in_specs=[pl.BlockSpec((tm, D), row),
                  pl.BlockSpec((1, 6, D), lambda i: (i // per_b, 0, 0)),
                  pl.BlockSpec((1, D), const),
                  pl.BlockSpec((tm, GM_WIDTH), row),
                  pl.BlockSpec((tm, GM_WIDTH), row),
                  pl.BlockSpec((tm, DN_WIDTH), row),
                  pl.BlockSpec((tm, 3 * D), row),
                  pl.BlockSpec((GM_WIDTH, D), const),
                  pl.BlockSpec((GM_WIDTH, D), const),
                  pl.BlockSpec((DN_WIDTH, D), const),
                  pl.BlockSpec((D, D), const)],
        out_specs=out_specs,
        out_shape=out_shape,
        compiler_params=_params("arbitrary"),
        name="merge",
    )(x, mod, g2, y_gm, y_da, y_dn, gates, w_gm, w_da, w_dn, w_out)


def _ffn_kernel(x_ref, h_ref, mod_ref, wg_ref, wu_ref, wd_ref, o_ref):
    h = h_ref[...]
    a = _dot(h, wg_ref[...])
    b = _dot(h, wu_ref[...])
    f = _dot((_silu(a) * b).astype(BF16), wd_ref[...])
    o_ref[...] = x_ref[...] + mod_ref[0, 5:6, :] * f


def _dense_ffn(x1, h2, mod, w_gate, w_up, w_down):
    T, D = x1.shape
    F = w_gate.shape[1]
    tm = TM_FFN
    per_b = SEQ // tm
    row = lambda i: (i, 0)
    const = lambda i: (0, 0)
    return pl.pallas_call(
        _ffn_kernel,
        grid=(T // tm,),
        in_specs=[pl.BlockSpec((tm, D), row),
                  pl.BlockSpec((tm, D), row),
                  pl.BlockSpec((1, 6, D), lambda i: (i // per_b, 0, 0)),
                  pl.BlockSpec((D, F), const, pipeline_mode=pl.Buffered(1)),
                  pl.BlockSpec((D, F), const, pipeline_mode=pl.Buffered(1)),
                  pl.BlockSpec((F, D), const, pipeline_mode=pl.Buffered(1))],
        out_specs=pl.BlockSpec((tm, D), row),
        out_shape=jax.ShapeDtypeStruct((T, D), F32),
        compiler_params=_params("arbitrary"),
        name="dense_ffn",
    )(x1, h2, mod, w_gate, w_up, w_down)


def _router_kernel(h_ref, wr_ref, e_ref, p_ref, r_ref, cnt_ref, base_ref):
    i = pl.program_id(0)
    tm = h_ref.shape[0]

    @pl.when(i == 0)
    def _():
        base_ref[...] = jnp.zeros_like(base_ref)

    logits = _dot_nt(wr_ref[...], h_ref[...])
    row = lax.broadcasted_iota(jnp.int32, logits.shape, 0)
    m1 = jnp.max(logits, axis=0, keepdims=True)
    i1 = jnp.min(jnp.where(logits == m1, row, N_EXPERTS), axis=0, keepdims=True)
    rest = jnp.where(row == i1, -jnp.inf, logits)
    m2 = jnp.max(rest, axis=0, keepdims=True)
    i2 = jnp.min(jnp.where(rest == m2, row, N_EXPERTS), axis=0, keepdims=True)
    e2 = jnp.exp(m2 - m1)
    w1 = 1.0 / (1.0 + e2)
    w2 = e2 / (1.0 + e2)
    oh1 = jnp.where(row == i1, 1.0, 0.0)
    oh2 = jnp.where(row == i2, 1.0, 0.0)
    both = oh1 + oh2
    ti = lax.broadcasted_iota(jnp.int32, (tm, tm), 0)
    tj = lax.broadcasted_iota(jnp.int32, (tm, tm), 1)
    before = jnp.where(ti < tj, 1.0, 0.0).astype(BF16)
    pos = base_ref[...] + _dot(both.astype(BF16), before)
    r1 = jnp.sum(oh1 * pos, axis=0, keepdims=True)
    r2 = jnp.sum(oh2 * pos, axis=0, keepdims=True)
    base_ref[...] = base_ref[...] + jnp.sum(both, axis=1, keepdims=True)
    zi = jnp.zeros((N_EXPERTS - 2, tm), jnp.int32)
    zf = jnp.zeros((N_EXPERTS - 2, tm), F32)
    e_ref[...] = jnp.concatenate([i1, i2, zi], axis=0)
    p_ref[...] = jnp.concatenate([w1, w2, zf], axis=0)
    r_ref[...] = jnp.concatenate([r1.astype(jnp.int32), r2.astype(jnp.int32), zi], axis=0)
    cnt_ref[...] = jnp.broadcast_to(base_ref[...], cnt_ref.shape)


def _router(h2, w_router_t):
    T, D = h2.shape
    tm = TM_ROUTE
    col = lambda i: (0, i)
    return pl.pallas_call(
        _router_kernel,
        grid=(T // tm,),
        in_specs=[pl.BlockSpec((tm, D), lambda i: (i, 0)),
                  pl.BlockSpec((N_EXPERTS, D), lambda i: (0, 0))],
        out_specs=[pl.BlockSpec((N_EXPERTS, tm), col)] * 3 + [pl.BlockSpec((N_EXPERTS, LANES), lambda i: (0, 0))],
        out_shape=[jax.ShapeDtypeStruct((N_EXPERTS, T), jnp.int32),
                   jax.ShapeDtypeStruct((N_EXPERTS, T), F32),
                   jax.ShapeDtypeStruct((N_EXPERTS, T), jnp.int32),
                   jax.ShapeDtypeStruct((N_EXPERTS, LANES), F32)],
        scratch_shapes=[pltpu.VMEM((N_EXPERTS, 1), F32)],
        compiler_params=_params("arbitrary"),
        name="moe_router",
    )(h2, w_router_t)


def _dispatch_kernel(dest_ref, h_ref, xs_in_ref, xs_ref, sem):
    del xs_in_ref
    i = pl.program_id(0)
    tm = dest_ref.shape[2] // 2
    base = i * tm

    def issue(r, _):
        for k in range(2):
            pltpu.make_async_copy(h_ref.at[pl.ds(base + r, 1)],
                                  xs_ref.at[pl.ds(dest_ref[0, 0, k * tm + r], 1)], sem).start()
        return 0

    lax.fori_loop(0, tm, issue, 0)

    def wait_one_step():
        for _ in range(2):
            pltpu.make_async_copy(h_ref.at[pl.ds(0, tm)], xs_ref.at[pl.ds(0, tm)], sem).wait()

    pl.when(i > 0)(wait_one_step)
    pl.when(i == pl.num_programs(0) - 1)(wait_one_step)


def _dispatch(h2, dest, n_slots):
    T, D = h2.shape
    tm = TM_DISP
    dest_t = dest.reshape(2, T // tm, tm).transpose(1, 0, 2).reshape(T // tm, 1, 2 * tm)
    xs0 = jnp.zeros((n_slots, D), h2.dtype)
    return pl.pallas_call(
        _dispatch_kernel,
        grid=(T // tm,),
        in_specs=[pl.BlockSpec((1, 1, 2 * tm), lambda i: (i, 0, 0), memory_space=pltpu.SMEM),
                  pl.BlockSpec(memory_space=pl.ANY),
                  pl.BlockSpec(memory_space=pl.ANY)],
        out_specs=pl.BlockSpec(memory_space=pl.ANY),
        out_shape=jax.ShapeDtypeStruct((n_slots, D), h2.dtype),
        scratch_shapes=[pltpu.SemaphoreType.DMA(())],
        input_output_aliases={2: 0},
        compiler_params=_params("arbitrary"),
        name="moe_dispatch",
    )(dest_t, h2, xs0)


def _expert_kernel(be_ref, nu_ref, x_ref, wg_ref, wu_ref, wd_ref, o_ref):
    del be_ref
    used = pl.program_id(0) < nu_ref[0]
    half = D_MODEL // 2

    @pl.when(used)
    def _():
        lo, hi = _unpack_bf16_pair(x_ref[...])
        a = _dot(lo, wg_ref[0, :half, :]) + _dot(hi, wg_ref[0, half:, :])
        b = _dot(lo, wu_ref[0, :half, :]) + _dot(hi, wu_ref[0, half:, :])
        o_ref[...] = _dot((_silu(a) * b).astype(BF16), wd_ref[0])

    @pl.when(jnp.logical_not(used))
    def _():
        o_ref[...] = jnp.zeros_like(o_ref)


def _experts(xs, blk_e, n_used, w_gate, w_up, w_down):
    P = xs.shape[0]
    _, D, F = w_gate.shape
    tm = TM_MOE
    grid_spec = pltpu.PrefetchScalarGridSpec(
        num_scalar_prefetch=2,
        grid=(P // tm,),
        in_specs=[pl.BlockSpec((tm, D // 2), lambda i, be, nu: (i, 0)),
                  pl.BlockSpec((1, D, F), lambda i, be, nu: (be[i], 0, 0), pipeline_mode=pl.Buffered(1)),
                  pl.BlockSpec((1, D, F), lambda i, be, nu: (be[i], 0, 0), pipeline_mode=pl.Buffered(1)),
                  pl.BlockSpec((1, F, D), lambda i, be, nu: (be[i], 0, 0), pipeline_mode=pl.Buffered(1))],
        out_specs=pl.BlockSpec((tm, D), lambda i, be, nu: (i, 0)))
    return pl.pallas_call(
        _expert_kernel,
        grid_spec=grid_spec,
        out_shape=jax.ShapeDtypeStruct((P, D), F32),
        compiler_params=_params("arbitrary"),
        name="moe_experts",
    )(blk_e, n_used, xs, w_gate, w_up, w_down)


def _combine_kernel(dest_ref, dest_next_ref, x_ref, mod_ref, p_ref, g_ref, ys_ref, o_ref, ybuf_ref, sems):
    i = pl.program_id(0)
    tm = x_ref.shape[0]
    slot = i % 2

    def gather(d_ref, s):
        def issue(r, _):
            for k in range(2):
                pltpu.make_async_copy(ys_ref.at[pl.ds(d_ref[0, 0, k * tm + r], 1)],
                                      ybuf_ref.at[s, k, pl.ds(r, 1)], sems.at[s]).start()
            return 0

        lax.fori_loop(0, tm, issue, 0)

    pl.when(i == 0)(lambda: gather(dest_ref, 0))
    pl.when(i + 1 < pl.num_programs(0))(lambda: gather(dest_next_ref, 1 - slot))
    for k in range(2):
        pltpu.make_async_copy(ys_ref.at[pl.ds(0, tm)], ybuf_ref.at[slot, k], sems.at[slot]).wait()
    p = p_ref[...]
    f = ybuf_ref[slot, 0] * p[:, 0:1] + ybuf_ref[slot, 1] * p[:, 1:2]
    x2 = x_ref[...] + mod_ref[0, 5:6, :] * f
    o_ref[...] = _rms(x2, g_ref[...])


def _combine(x1, mod, probs, final_g, ys, dest):
    T, D = x1.shape
    tm = TM_COMB
    per_b = SEQ // tm
    n = T // tm
    dest_t = dest.reshape(2, n, tm).transpose(1, 0, 2).reshape(n, 1, 2 * tm)
    return pl.pallas_call(
        _combine_kernel,
        grid=(n,),
        in_specs=[pl.BlockSpec((1, 1, 2 * tm), lambda i: (i, 0, 0), memory_space=pltpu.SMEM),
                  pl.BlockSpec((1, 1, 2 * tm), lambda i: (jnp.minimum(i + 1, n - 1), 0, 0),
                               memory_space=pltpu.SMEM),
                  pl.BlockSpec((tm, D), lambda i: (i, 0)),
                  pl.BlockSpec((1, 6, D), lambda i: (i // per_b, 0, 0)),
                  pl.BlockSpec((tm, 2), lambda i: (i, 0)),
                  pl.BlockSpec((1, D), lambda i: (0, 0)),
                  pl.BlockSpec(memory_space=pl.ANY)],
        out_specs=pl.BlockSpec((tm, D), lambda i: (i, 0)),
        out_shape=jax.ShapeDtypeStruct((T, D), F32),
        scratch_shapes=[pltpu.VMEM((2, 2, tm, D), F32), pltpu.SemaphoreType.DMA((2,))],
        compiler_params=_params("arbitrary"),
        name="moe_combine",
    )(dest_t, dest_t, x1, mod, probs, final_g, ys)


def _moe_layer(x1, h2, h2_packed, mod, w_router, w_gate, w_up, w_down, final_g):
    T, D = x1.shape
    e_idx, probs, rank, counts = _router(h2, w_router.T.astype(BF16))
    counts = counts[:, 0].astype(jnp.int32)
    padded = (counts + TM_MOE - 1) // TM_MOE * TM_MOE
    pad_end = jnp.cumsum(padded)
    pad_start = pad_end - padded
    eid = jnp.arange(N_EXPERTS, dtype=jnp.int32)[:, None, None]
    dest = jnp.sum(jnp.where(e_idx[None, :2] == eid, pad_start[:, None, None], 0), axis=0) + rank[:2]
    n_slots = 2 * T + N_EXPERTS * TM_MOE
    n_blk = n_slots // TM_MOE
    blk_e = jnp.minimum(jnp.searchsorted(pad_end, jnp.arange(n_blk, dtype=jnp.int32) * TM_MOE,
                                         side='right'), N_EXPERTS - 1).astype(jnp.int32)
    n_used = (pad_end[-1:] // TM_MOE).astype(jnp.int32)
    xs = _dispatch(h2_packed, dest, n_slots)
    ys = _experts(xs, blk_e, n_used, w_gate.astype(BF16), w_up.astype(BF16), w_down.astype(BF16))
    return _combine(x1, mod, probs[:2].T, final_g.reshape(1, -1), ys, dest)


def _final_norm_kernel(x_ref, g_ref, o_ref):
    o_ref[...] = _rms(x_ref[...], g_ref[...])


def _final_norm(x, g):
    T, D = x.shape
    tm = 1024
    return pl.pallas_call(
        _final_norm_kernel,
        grid=(T // tm,),
        in_specs=[pl.BlockSpec((tm, D), lambda i: (i, 0)), pl.BlockSpec((1, D), lambda i: (0, 0))],
        out_specs=pl.BlockSpec((tm, D), lambda i: (i, 0)),
        out_shape=jax.ShapeDtypeStruct((T, D), F32),
        compiler_params=_params("arbitrary"),
        name="final_norm",
    )(x, g.reshape(1, -1))


def _regroup_w_in(w):
    ab = jnp.pad(w[:, 4608:4616], ((0, 0), (0, LANES - 8)))
    return jnp.concatenate([w[:, :4608], w[:, 4616:], ab], axis=1).astype(BF16)


def _lambda_init(layer):
    return 0.8 - 0.6 * math.exp(-0.3 * layer)


def kernel(x, c, positions, norm1_g, norm2_g, w_mod, b_mod, w_in, gm_ln_g, gm_ln_b, gm_w_s, gm_b_s, da_lambda, da_subln_g, dn_conv_w, dn_a_log, dn_dt_bias, dn_norm_g, w_br_gm, w_br_da, w_br_dn, w_out, ffn_w_gate, ffn_w_up, ffn_w_down, moe_w_router, moe_w_gate, moe_w_up, moe_w_down, final_g):
    B, S, D = x.shape
    T = B * S
    xt = x.reshape(T, D)
    mod_all = _modulation(c, w_mod, b_mod).reshape(DEPTH, B, 6, D)
    cos_t, sin_t = _rope_tables(positions)
    for layer in range(DEPTH):
        mod = mod_all[layer]
        gm, qk, v, dn, gates, ab = _in_projection(xt, mod, norm1_g[layer].reshape(1, D),
                                                  _regroup_w_in(w_in[layer]), cos_t, sin_t)
        y_gm = _gmlp(gm, gm_ln_g[layer], gm_ln_b[layer], gm_w_s[layer], gm_b_s[layer])
        y_da = _diff_attention(qk, v, da_lambda[layer], da_subln_g[layer], _lambda_init(layer), B)
        y_dn = _deltanet(dn, ab, dn_conv_w[layer], dn_a_log[layer], dn_dt_bias[layer],
                         dn_norm_g[layer], B)
        moe = layer % 2 == 1
        x1, h2, *packed = _merge(xt, mod, norm2_g[layer].reshape(1, D), y_gm, y_da, y_dn, gates,
                                 w_br_gm[layer].astype(BF16), w_br_da[layer].astype(BF16),
                                 w_br_dn[layer].astype(BF16), w_out[layer].astype(BF16), moe)
        if moe:
            xt = _moe_layer(x1, h2, packed[0], mod, moe_w_router[layer // 2], moe_w_gate[layer // 2],
                            moe_w_up[layer // 2], moe_w_down[layer // 2], final_g)
        else:
            xt = _dense_ffn(x1, h2, mod, ffn_w_gate[layer // 2].astype(BF16),
                            ffn_w_up[layer // 2].astype(BF16), ffn_w_down[layer // 2].astype(BF16))
    if DEPTH % 2 == 1:
        xt = _final_norm(xt, final_g)
    return xt.reshape(B, S, D)
```

```python
import functools
import math

import jax
import jax.numpy as jnp
import numpy as np
from jax import lax
from jax.experimental import pallas as pl
from jax.experimental.pallas import tpu as pltpu

D_MODEL = 1024
SEQ = 2048
DEPTH = 2
CHUNK = 64
EPS = 1e-6
GM_WIDTH = D_MODEL // 2
GM_GROUPS = 4
GM_BLOCK = 128
DA_HEADS = 4
DA_HEAD_DIM = 64
DA_V_DIM = 2 * DA_HEAD_DIM
ROPE_THETA = 10000.0
DN_HEADS = 4
DN_HEAD_DIM = 128
DN_WIDTH = DN_HEADS * DN_HEAD_DIM
DN_CONV = 4
N_EXPERTS = 8
FF_EXPERT = 7 * D_MODEL // 2

LANES = 128
VMEM_LIMIT = 56 * 1024 * 1024

C_GM, C_QK, C_V, C_DN, C_GATE, C_AB = 0, 1024, 2048, 2560, 4608, 7680
W_ALL = C_AB + LANES
PROJ_CHUNK = 512

TM_PROJ = 512
TM_GM = 512
TQ = 256
TM_MERGE = 512
TM_FFN = 512
TM_ROUTE = 512
TM_MOE = 512
TM_DISP = 256
DISP_SLOTS = 3
TM_COMB = 256
ATTN_LOOKAHEAD = 3
DN_BLOCK = 256

BF16 = jnp.bfloat16
F32 = jnp.float32


def _params(*sem):
    return pltpu.CompilerParams(dimension_semantics=sem, vmem_limit_bytes=VMEM_LIMIT)


def _dot(a, b):
    return jnp.dot(a, b, preferred_element_type=F32)


def _dot_nt(a, b):
    return lax.dot_general(a, b, (((1,), (1,)), ((), ())), preferred_element_type=F32)


def _dot_tn(a, b):
    return lax.dot_general(a, b, (((0,), (0,)), ((), ())), preferred_element_type=F32)


def _dot_hi(a, b):
    return jnp.dot(a, b, preferred_element_type=F32, precision=lax.Precision.HIGHEST)


def _rms(x, g):
    return x * lax.rsqrt(jnp.mean(x * x, axis=-1, keepdims=True) + EPS) * g


def _silu(x):
    return x * jax.nn.sigmoid(x)


def _mod_kernel(c_ref, w_ref, b_ref, o_ref):
    c = c_ref[...]
    o_ref[0] = _dot_hi(_silu(c), w_ref[0]) + b_ref[0]


def _modulation(c, w_mod, b_mod):
    B, D = c.shape
    L, _, N = w_mod.shape
    tn = 1536
    return pl.pallas_call(
        _mod_kernel,
        grid=(L, N // tn),
        in_specs=[pl.BlockSpec((B, D), lambda l, j: (0, 0)),
                  pl.BlockSpec((1, D, tn), lambda l, j: (l, 0, j)),
                  pl.BlockSpec((1, 1, tn), lambda l, j: (l, 0, j))],
        out_specs=pl.BlockSpec((1, B, tn), lambda l, j: (l, 0, j)),
        out_shape=jax.ShapeDtypeStruct((L, B, N), F32),
        compiler_params=_params("arbitrary", "arbitrary"),
        name="modulation",
    )(c, w_mod, b_mod.reshape(L, 1, N))


def _rope_tab_kernel(pos_ref, inv_ref, sgn_ref, cos_ref, sin_ref):
    ang = pos_ref[...].astype(F32) * inv_ref[...]
    cos_ref[...] = jnp.cos(ang)
    sin_ref[...] = jnp.sin(ang) * sgn_ref[...]


def _rope_tables(positions):
    T = positions.size
    inv_freq = ROPE_THETA ** (-jnp.arange(0, DA_HEAD_DIM, 2, dtype=F32) / DA_HEAD_DIM)
    inv = jnp.tile(inv_freq, LANES // (DA_HEAD_DIM // 2)).reshape(1, LANES)
    half = DA_HEAD_DIM // 2
    sgn = np.tile(np.concatenate([-np.ones(half), np.ones(half)]), LANES // DA_HEAD_DIM)
    sgn = jnp.asarray(sgn, F32).reshape(1, LANES)
    tm = 1024
    return pl.pallas_call(
        _rope_tab_kernel,
        grid=(T // tm,),
        in_specs=[pl.BlockSpec((tm, 1), lambda i: (i, 0)),
                  pl.BlockSpec((1, LANES), lambda i: (0, 0)),
                  pl.BlockSpec((1, LANES), lambda i: (0, 0))],
        out_specs=[pl.BlockSpec((tm, LANES), lambda i: (i, 0))] * 2,
        out_shape=[jax.ShapeDtypeStruct((T, LANES), F32)] * 2,
        compiler_params=_params("arbitrary"),
        name="rope_tables",
    )(positions.reshape(T, 1), inv, sgn)


def _rope_rows(x, cos, sin):
    lane = lax.broadcasted_iota(jnp.int32, x.shape, 1)
    first = (lane % DA_HEAD_DIM) < (DA_HEAD_DIM // 2)
    partner = jnp.where(first, pltpu.roll(x, LANES - DA_HEAD_DIM // 2, axis=1),
                        pltpu.roll(x, DA_HEAD_DIM // 2, axis=1))
    return x * cos + partner * sin


def _proj_kernel(x_ref, mod_ref, g_ref, w_ref, cos_ref, sin_ref, cw_ref,
                 gm_ref, qk_ref, v_ref, dn_ref, gate_ref, ab_ref, prev_ref):
    tm = x_ref.shape[0]

    @pl.when(pl.program_id(0) % (SEQ // tm) == 0)
    def _():
        prev_ref[...] = jnp.zeros_like(prev_ref)

    def conv_silu(y, idx):
        cw = cw_ref[:, idx * PROJ_CHUNK:(idx + 1) * PROJ_CHUNK]
        ycat = jnp.concatenate([prev_ref[idx], y], axis=0)
        out = y * cw[DN_CONV - 1:DN_CONV, :]
        for j in range(DN_CONV - 1):
            out = out + pltpu.roll(ycat, DN_CONV - 1 - j, axis=0)[8:, :] * cw[j:j + 1, :]
        prev_ref[idx] = y[tm - 8:, :]
        return _silu(out)

    def l2norm_heads(t, mult):
        parts = []
        for j in range(PROJ_CHUNK // DN_HEAD_DIM):
            seg = t[:, j * DN_HEAD_DIM:(j + 1) * DN_HEAD_DIM]
            parts.append(seg * (lax.rsqrt(jnp.sum(seg * seg, axis=-1, keepdims=True) + EPS) * mult))
        return jnp.concatenate(parts, axis=1)

    x = x_ref[...]
    shift = mod_ref[0, 0:1, :]
    scale = mod_ref[0, 1:2, :]
    h = (_rms(x, g_ref[...]) * (1.0 + scale) + shift).astype(BF16)
    cos = cos_ref[...]
    sin = sin_ref[...]
    q_scale = DA_HEAD_DIM ** -0.5 * math.log2(math.e)

    def rope_chunk(y, mult):
        parts = [_rope_rows(y[:, j * LANES:(j + 1) * LANES], cos, sin) * mult
                 for j in range(PROJ_CHUNK // LANES)]
        return jnp.concatenate(parts, axis=1)

    for c0 in range(0, C_AB, PROJ_CHUNK):
        y = _dot(h, w_ref[:, c0:c0 + PROJ_CHUNK])
        if c0 < C_QK:
            gm_ref[:, c0:c0 + PROJ_CHUNK] = y.astype(BF16)
        elif c0 < C_V:
            mult = q_scale if c0 == C_QK else 1.0
            qk_ref[:, c0 - C_QK:c0 - C_QK + PROJ_CHUNK] = rope_chunk(y, mult).astype(BF16)
        elif c0 < C_DN:
            v_ref[...] = y.astype(BF16)
        elif c0 < C_GATE:
            idx = (c0 - C_DN) // PROJ_CHUNK
            if idx < 3:
                y = conv_silu(y, idx)
            if idx < 2:
                y = l2norm_heads(y, DN_HEAD_DIM ** -0.5 if idx == 0 else 1.0)
            dn_ref[:, c0 - C_DN:c0 - C_DN + PROJ_CHUNK] = y.astype(BF16)
        else:
            gate_ref[:, c0 - C_GATE:c0 - C_GATE + PROJ_CHUNK] = y.astype(BF16)
    ab_ref[...] = _dot(h, w_ref[:, C_AB:W_ALL])


def _in_projection(x, mod, g, w_all, cos_t, sin_t, conv_w):
    T, D = x.shape
    tm = TM_PROJ
    per_b = SEQ // tm
    row = lambda i: (i, 0)
    widths = (C_QK - C_GM, C_V - C_QK, C_DN - C_V, C_GATE - C_DN, C_AB - C_GATE)
    return pl.pallas_call(
        _proj_kernel,
        grid=(T // tm,),
        in_specs=[pl.BlockSpec((tm, D), row),
                  pl.BlockSpec((1, 6, D), lambda i: (i // per_b, 0, 0)),
                  pl.BlockSpec((1, D), lambda i: (0, 0)),
                  pl.BlockSpec((D, W_ALL), lambda i: (0, 0), pipeline_mode=pl.Buffered(1)),
                  pl.BlockSpec((tm, LANES), row),
                  pl.BlockSpec((tm, LANES), row),
                  pl.BlockSpec((DN_CONV, 3 * DN_WIDTH), lambda i: (0, 0))],
        out_specs=[pl.BlockSpec((tm, w), row) for w in widths] + [pl.BlockSpec((tm, LANES), row)],
        out_shape=[jax.ShapeDtypeStruct((T, w), BF16) for w in widths]
                  + [jax.ShapeDtypeStruct((T, LANES), F32)],
        scratch_shapes=[pltpu.VMEM((3, 8, PROJ_CHUNK), F32)],
        compiler_params=_params("arbitrary"),
        name="in_projection",
    )(x, mod, g, w_all, cos_t, sin_t, conv_w)


def _gelu(x):
    return 0.5 * x * (1.0 + lax.erf(x * np.float32(math.sqrt(0.5))))


def _gmlp_kernel(uv_ref, lng_ref, lnb_ref, ws_ref, bs_ref, o_ref):
    u = _gelu(uv_ref[:, :GM_WIDTH].astype(F32))
    v = _gelu(uv_ref[:, GM_WIDTH:].astype(F32))
    mu = jnp.mean(v, axis=-1, keepdims=True)
    vc = v - mu
    var = jnp.mean(vc * vc, axis=-1, keepdims=True)
    v = (vc * lax.rsqrt(var + EPS) * lng_ref[...] + lnb_ref[...]).astype(BF16)
    ri = lax.broadcasted_iota(jnp.int32, (GM_BLOCK, GM_BLOCK), 0) // CHUNK
    ci = lax.broadcasted_iota(jnp.int32, (GM_BLOCK, GM_BLOCK), 1) // CHUNK
    allowed = ci <= ri
    gc = GM_WIDTH // GM_GROUPS
    for g in range(GM_GROUPS):
        w = jnp.where(allowed, ws_ref[g], 0.0).astype(BF16)
        bias = bs_ref[g]
        for r in range(uv_ref.shape[0] // GM_BLOCK):
            rows = slice(r * GM_BLOCK, (r + 1) * GM_BLOCK)
            cols = slice(g * gc, (g + 1) * gc)
            sv = _dot(w, v[rows, cols]) + bias
            o_ref[rows, cols] = (u[rows, cols] * sv).astype(BF16)


def _gmlp(uv, ln_g, ln_b, w_s, b_s):
    T = uv.shape[0]
    tm = TM_GM
    return pl.pallas_call(
        _gmlp_kernel,
        grid=(T // tm,),
        in_specs=[pl.BlockSpec((tm, 2 * GM_WIDTH), lambda i: (i, 0)),
                  pl.BlockSpec((1, GM_WIDTH), lambda i: (0, 0)),
                  pl.BlockSpec((1, GM_WIDTH), lambda i: (0, 0)),
                  pl.BlockSpec((GM_GROUPS, GM_BLOCK, GM_BLOCK), lambda i: (0, 0, 0)),
                  pl.BlockSpec((GM_GROUPS, GM_BLOCK, 1), lambda i: (0, 0, 0))],
        out_specs=pl.BlockSpec((tm, GM_WIDTH), lambda i: (i, 0)),
        out_shape=jax.ShapeDtypeStruct((T, GM_WIDTH), BF16),
        compiler_params=_params("arbitrary"),
        name="gmlp",
    )(uv, ln_g.reshape(1, -1), ln_b.reshape(1, -1), w_s, b_s.reshape(GM_GROUPS, GM_BLOCK, 1))


def _attn_kernel(lam_ref, g_ref, q_ref, k_ref, v_ref, o_ref, qz_ref, vt_ref, *stats, lambda_init):
    nc = 2 * DA_HEADS
    m_ref, l_ref, acc_ref = stats[:nc], stats[nc:2 * nc], stats[2 * nc:]
    i = pl.program_id(1)
    lp = lam_ref[...]
    lam = (jnp.exp(jnp.sum(lp[0:1] * lp[1:2], axis=-1, keepdims=True))
           - jnp.exp(jnp.sum(lp[2:3] * lp[3:4], axis=-1, keepdims=True)) + lambda_init)
    @pl.when(i == 0)
    def _():
        for kb in range(SEQ // TQ):
            for h in range(DA_HEADS):
                vt_ref[h * LANES:(h + 1) * LANES, kb * TQ:(kb + 1) * TQ] = (
                    v_ref[kb * TQ:(kb + 1) * TQ, h * LANES:(h + 1) * LANES].T)

    row = lax.broadcasted_iota(jnp.int32, (LANES, TQ), 0)
    for h in range(DA_HEADS):
        qt = q_ref[:, h * LANES:(h + 1) * LANES].T
        zero = jnp.zeros_like(qt)
        qz_ref[2 * h] = jnp.where(row < DA_HEAD_DIM, qt, zero)
        qz_ref[2 * h + 1] = jnp.where(row >= DA_HEAD_DIM, qt, zero)
    for c in range(nc):
        m_ref[c][...] = jnp.full_like(m_ref[c], -jnp.inf)
        l_ref[c][...] = jnp.zeros_like(l_ref[c])
        acc_ref[c][...] = jnp.zeros_like(acc_ref[c])

    def block(kb, diagonal):
        rows = pl.ds(pl.multiple_of(kb * TQ, TQ), TQ)

        def scores(c):
            h = c // 2
            s = _dot(k_ref[rows, h * LANES:(h + 1) * LANES], qz_ref[c])
            if diagonal:
                ki = lax.broadcasted_iota(jnp.int32, s.shape, 0) // CHUNK
                qi = lax.broadcasted_iota(jnp.int32, s.shape, 1) // CHUNK
                s = jnp.where(ki <= qi, s, -jnp.inf)
            return s

        def accumulate(c, s):
            h = c // 2
            m_old = m_ref[c][...]
            m_new = jnp.maximum(m_old, jnp.max(s, axis=0, keepdims=True))
            alpha = jnp.exp2(m_old - m_new)
            e = jnp.exp2(s - m_new)
            l_ref[c][...] = alpha * l_ref[c][...] + jnp.sum(e, axis=0, keepdims=True)
            vt = vt_ref[h * LANES:(h + 1) * LANES, rows]
            acc_ref[c][...] = alpha * acc_ref[c][...] + _dot(vt, e.astype(BF16))
            m_ref[c][...] = m_new

        pending = {}
        for c in range(nc + ATTN_LOOKAHEAD):
            if c < nc:
                pending[c] = scores(c)
            if c >= ATTN_LOOKAHEAD:
                accumulate(c - ATTN_LOOKAHEAD, pending.pop(c - ATTN_LOOKAHEAD))

    def body(kb, carry):
        block(kb, False)
        return carry

    lax.fori_loop(0, i, body, 0)
    block(i, True)
    for h in range(DA_HEADS):
        o = (acc_ref[2 * h][...] / l_ref[2 * h][...]
             - lam * (acc_ref[2 * h + 1][...] / l_ref[2 * h + 1][...]))
        ms = jnp.mean(o * o, axis=0, keepdims=True)
        o = o * lax.rsqrt(ms + EPS) * (g_ref[...] * (1.0 - lambda_init))
        o_ref[:, h * LANES:(h + 1) * LANES] = o.T.astype(BF16)


def _diff_attention(qk, v, lam_p, subln_g, lambda_init, batch):
    T = qk.shape[0]
    nq = SEQ // TQ
    W = DA_HEADS * DA_V_DIM
    kernel = functools.partial(_attn_kernel, lambda_init=lambda_init)
    return pl.pallas_call(
        kernel,
        grid=(batch, nq),
        in_specs=[pl.BlockSpec((4, DA_HEAD_DIM), lambda b, i: (0, 0)),
                  pl.BlockSpec((DA_V_DIM, 1), lambda b, i: (0, 0)),
                  pl.BlockSpec((TQ, W), lambda b, i: (b * nq + i, 0)),
                  pl.BlockSpec((SEQ, W), lambda b, i: (b, 1)),
                  pl.BlockSpec((SEQ, W), lambda b, i: (b, 0))],
        out_specs=pl.BlockSpec((TQ, W), lambda b, i: (b * nq + i, 0)),
        out_shape=jax.ShapeDtypeStruct((T, W), BF16),
        scratch_shapes=[pltpu.VMEM((2 * DA_HEADS, LANES, TQ), BF16), pltpu.VMEM((W, SEQ), BF16)]
                       + [pltpu.VMEM((1, TQ), F32)] * (4 * DA_HEADS)
                       + [pltpu.VMEM((DA_V_DIM, TQ), F32)] * (2 * DA_HEADS),
        compiler_params=_params("arbitrary", "arbitrary"),
        name="diff_attention",
    )(lam_p, subln_g.reshape(-1, 1), qk, qk, v)


def _split3(x):
    hi = x.astype(BF16)
    r = x - hi.astype(F32)
    mid = r.astype(BF16)
    lo = (r - mid.astype(F32)).astype(BF16)
    return hi, mid, lo


def _dn_kernel(dn_ref, a_ref, at_ref, alog_ref, dtb_ref, alogt_ref, dtbt_ref, ng_ref,
               o_ref, *state_ref):
    c = pl.program_id(1)
    R = DN_BLOCK

    @pl.when(c == 0)
    def _():
        for ref in state_ref:
            ref[...] = jnp.zeros_like(ref)

    W3 = 3 * DN_WIDTH
    ab = a_ref[...]
    g_col = -jnp.exp(alog_ref[...]) * jax.nn.softplus(ab + dtb_ref[...])
    beta_all = jax.nn.sigmoid(ab)
    abt = at_ref[0]
    g_row = -jnp.exp(alogt_ref[...]) * jax.nn.softplus(abt + dtbt_ref[...])
    ri = lax.broadcasted_iota(jnp.int32, (R, R), 0)
    ci = lax.broadcasted_iota(jnp.int32, (R, R), 1)
    tril = ri >= ci
    strict = ri > ci
    ones_tril = jnp.where(tril, 1.0, 0.0).astype(BF16)
    ones_triu = jnp.where(ci >= ri, 1.0, 0.0).astype(BF16)
    gc_col = sum(_dot(ones_tril, part) for part in _split3(g_col))
    gc_row = sum(_dot(part, ones_triu) for part in _split3(g_row))
    eye = jnp.where(ri == ci, 1.0, 0.0)
    same = {n: (ri // n) == (ci // n) for n in (16, 32, 64, 128)}

    H = range(DN_HEADS)
    hsl = [slice(h * DN_HEAD_DIM, (h + 1) * DN_HEAD_DIM) for h in H]
    q16 = [dn_ref[:, hsl[h]] for h in H]
    k16 = [dn_ref[:, DN_WIDTH + h * DN_HEAD_DIM:DN_WIDTH + (h + 1) * DN_HEAD_DIM] for h in H]
    q = [t.astype(F32) for t in q16]
    k = [t.astype(F32) for t in k16]
    v = [dn_ref[:, 2 * DN_WIDTH + h * DN_HEAD_DIM:2 * DN_WIDTH + (h + 1) * DN_HEAD_DIM].astype(F32) for h in H]
    gc =[gc_col[:, h:h + 1] for h in H]
    beta = [beta_all[:, DN_HEADS + h:DN_HEADS + h + 1] for h in H]
    g_last = [gc_col[R - 1:R, h:h + 1] for h in H]
    decay = [jnp.exp(jnp.where(tril, gc[h] - gc_row[h:h + 1, :], -jnp.inf)) for h in H]
    eg = [jnp.exp(t) for t in gc]
    kb = [k[h] * beta[h] for h in H]
    s1 = [_dot_nt(jnp.concatenate([kb[h].astype(BF16), q16[h]], axis=0), k16[h]) for h in H]
    a_mat = [jnp.where(strict, s1[h][:R] * decay[h], 0.0) for h in H]
    qk = [(s1[h][R:] * decay[h]).astype(BF16) for h in H]
    p = [jnp.where(same[16], -t, 0.0) for t in a_mat]
    t_inv = [eye + t for t in p]
    for _ in range(3):
        p16 = [t.astype(BF16) for t in p]
        p = [_dot(t, t) for t in p16]
        t_inv = [t_inv[h] + _dot(t_inv[h].astype(BF16), p[h].astype(BF16)) for h in H]
    a16 = [t.astype(BF16) for t in a_mat]
    t16 = [t.astype(BF16) for t in t_inv]
    for n in (16, 32, 64, 128):
        off = jnp.logical_not(same[n])
        if 2 * n < R:
            off = jnp.logical_and(same[2 * n], off)
        off16 = jnp.where(off, 1.0, 0.0).astype(BF16)
        tl = [_dot(t16[h], a16[h] * off16).astype(BF16) for h in H]
        t16 = [t16[h] - _dot(tl[h], t16[h]).astype(BF16) * off16 for h in H]
    sol = [_dot(t16[h], jnp.concatenate([v[h] * beta[h], kb[h] * eg[h]], axis=1).astype(BF16))
           for h in H]
    state = [state_ref[h][...] for h in H]
    m1 = [_dot(jnp.concatenate([sol[h][:, DN_HEAD_DIM:], q[h] * eg[h]], axis=0).astype(BF16),
               state[h].astype(BF16)) for h in H]
    vn16 = [(sol[h][:, :DN_HEAD_DIM] - m1[h][:R]).astype(BF16) for h in H]
    o = [m1[h][R:] + _dot(qk[h], vn16[h]) for h in H]
    kd = [(k[h] * jnp.exp(g_last[h] - gc[h])).astype(BF16) for h in H]
    for h in H:
        state_ref[h][...] = state[h] * jnp.exp(g_last[h]) + _dot_tn(kd[h], vn16[h])
    for h in H:
        z = dn_ref[:, W3 + h * DN_HEAD_DIM:W3 + (h + 1) * DN_HEAD_DIM].astype(F32)
        o_ref[:, hsl[h]] = (_rms(o[h], ng_ref[...]) * _silu(z)).astype(BF16)


def _deltanet(dn, ab, a_log, dt_bias, norm_g, batch):
    T = dn.shape[0]
    R = DN_BLOCK
    n = SEQ // R
    pad = lambda t: jnp.zeros((1, LANES), F32).at[0, :DN_HEADS].set(t)
    col8 = lambda t: jnp.zeros((8, 1), F32).at[:DN_HEADS, 0].set(t)
    ab_t = ab[:, :8].reshape(T // R, R, 8).transpose(0, 2, 1)
    return pl.pallas_call(
        _dn_kernel,
        grid=(batch, n),
        in_specs=[pl.BlockSpec((R, 4 * DN_WIDTH), lambda b, c: (b * n + c, 0)),
                  pl.BlockSpec((R, LANES), lambda b, c: (b * n + c, 0)),
                  pl.BlockSpec((1, 8, R), lambda b, c: (b * n + c, 0, 0)),
                  pl.BlockSpec((1, LANES), lambda b, c: (0, 0)),
                  pl.BlockSpec((1, LANES), lambda b, c: (0, 0)),
                  pl.BlockSpec((8, 1), lambda b, c: (0, 0)),
                  pl.BlockSpec((8, 1), lambda b, c: (0, 0)),
                  pl.BlockSpec((1, DN_HEAD_DIM), lambda b, c: (0, 0))],
        out_specs=pl.BlockSpec((R, DN_WIDTH), lambda b, c: (b * n + c, 0)),
        out_shape=jax.ShapeDtypeStruct((T, DN_WIDTH), BF16),
        scratch_shapes=[pltpu.VMEM((DN_HEAD_DIM, DN_HEAD_DIM), F32)] * DN_HEADS,
        compiler_params=_params("arbitrary", "arbitrary"),
        name="deltanet",
    )(dn, ab, ab_t, pad(a_log), pad(dt_bias), col8(a_log), col8(dt_bias), norm_g.reshape(1, -1))


def _pack_bf16_pair(lo, hi):
    lo_bits = lax.bitcast_convert_type(lo.astype(BF16).astype(F32), jnp.uint32)
    hi_bits = lax.bitcast_convert_type(hi.astype(BF16).astype(F32), jnp.uint32)
    return (lo_bits >> 16) | (hi_bits & jnp.uint32(0xFFFF0000))


def _unpack_bf16_pair(words):
    lo = lax.bitcast_convert_type(words << 16, F32)
    hi = lax.bitcast_convert_type(words & jnp.uint32(0xFFFF0000), F32)
    return lo.astype(BF16), hi.astype(BF16)


def _merge_kernel(x_ref, mod_ref, g2_ref, ygm_ref, yda_ref, ydn_ref, gate_ref,
                  wgm_ref, wda_ref, wdn_ref, wout_ref, x1_ref, h2_ref, *maybe_packed_ref):
    D = D_MODEL
    merged = (jax.nn.sigmoid(gate_ref[:, 0:D].astype(F32)) * _dot(ygm_ref[...], wgm_ref[...])
              + jax.nn.sigmoid(gate_ref[:, D:2 * D].astype(F32)) * _dot(yda_ref[...], wda_ref[...])
              + jax.nn.sigmoid(gate_ref[:, 2 * D:3 * D].astype(F32)) * _dot(ydn_ref[...], wdn_ref[...]))
    y = _dot(merged.astype(BF16), wout_ref[...])
    x1 = x_ref[...] + mod_ref[0, 2:3, :] * y
    x1_ref[...] = x1
    h2 = _rms(x1, g2_ref[...]) * (1.0 + mod_ref[0, 4:5, :]) + mod_ref[0, 3:4, :]
    h2_ref[...] = h2.astype(BF16)
    for packed_ref in maybe_packed_ref:
        packed_ref[...] = _pack_bf16_pair(h2[:, :D // 2], h2[:, D // 2:])


def _merge(x, mod, g2, y_gm, y_da, y_dn, gates, w_gm, w_da, w_dn, w_out, with_packed):
    T, D = x.shape
    tm = TM_MERGE
    per_b = SEQ // tm
    row = lambda i: (i, 0)
    const = lambda i: (0, 0)
    out_specs = [pl.BlockSpec((tm, D), row), pl.BlockSpec((tm, D), row)]
    out_shape = [jax.ShapeDtypeStruct((T, D), F32), jax.ShapeDtypeStruct((T, D), BF16)]
    if with_packed:
        out_specs.append(pl.BlockSpec((tm, D // 2), row))
        out_shape.append(jax.ShapeDtypeStruct((T, D // 2), jnp.uint32))
    return pl.pallas_call(
        _merge_kernel,
        grid=(T // tm,),
        in_specs=[pl.BlockSpec((tm, D), row),
                  pl.BlockSpec((1, 6, D), lambda i: (i // per_b, 0, 0)),
                  pl.BlockSpec((1, D), const),
                  pl.BlockSpec((tm, GM_WIDTH), row),
                  pl.BlockSpec((tm, GM_WIDTH), row),
                  pl.BlockSpec((tm, DN_WIDTH), row),
                  pl.BlockSpec((tm, 3 * D), row),
                  pl.BlockSpec((GM_WIDTH, D), const),
                  pl.BlockSpec((GM_WIDTH, D), const),
                  pl.BlockSpec((DN_WIDTH, D), const),
                  pl.BlockSpec((D, D), const)],
        out_specs=out_specs,
        out_shape=out_shape,
        compiler_params=_params("arbitrary"),
        name="merge",
    )(x, mod, g2, y_gm, y_da, y_dn, gates, w_gm, w_da, w_dn, w_out)


def _ffn_kernel(x_ref, h_ref, mod_ref, wg_ref, wu_ref, wd_ref, o_ref):
    h = h_ref[...]
    a = _dot(h, wg_ref[...])
    b = _dot(h, wu_ref[...])
    f = _dot((_silu(a) * b).astype(BF16), wd_ref[...])
    o_ref[...] = x_ref[...] + mod_ref[0, 5:6, :] * f


def _dense_ffn(x1, h2, mod, w_gate, w_up, w_down):
    T, D = x1.shape
    F = w_gate.shape[1]
    tm = TM_FFN
    per_b = SEQ // tm
    row = lambda i: (i, 0)
    const = lambda i: (0, 0)
    return pl.pallas_call(
        _ffn_kernel,
        grid=(T // tm,),
        in_specs=[pl.BlockSpec((tm, D), row),
                  pl.BlockSpec((tm, D), row),
                  pl.BlockSpec((1, 6, D), lambda i: (i // per_b, 0, 0)),
                  pl.BlockSpec((D, F), const, pipeline_mode=pl.Buffered(1)),
                  pl.BlockSpec((D, F), const, pipeline_mode=pl.Buffered(1)),
                  pl.BlockSpec((F, D), const, pipeline_mode=pl.Buffered(1))],
        out_specs=pl.BlockSpec((tm, D), row),
        out_shape=jax.ShapeDtypeStruct((T, D), F32),
        compiler_params=_params("arbitrary"),
        name="dense_ffn",
    )(x1, h2, mod, w_gate, w_up, w_down)


def _router_kernel(h_ref, wr_ref, e_ref, p_ref, r_ref, cnt_ref, base_ref):
    i = pl.program_id(0)
    tm = h_ref.shape[0]

    @pl.when(i == 0)
    def _():
        base_ref[...] = jnp.zeros_like(base_ref)

    logits = _dot_nt(wr_ref[...], h_ref[...])
    row = lax.broadcasted_iota(jnp.int32, logits.shape, 0)
    m1 = jnp.max(logits, axis=0, keepdims=True)
    i1 = jnp.min(jnp.where(logits == m1, row, N_EXPERTS), axis=0, keepdims=True)
    rest = jnp.where(row == i1, -jnp.inf, logits)
    m2 = jnp.max(rest, axis=0, keepdims=True)
    i2 = jnp.min(jnp.where(rest == m2, row, N_EXPERTS), axis=0, keepdims=True)
    e2 = jnp.exp(m2 - m1)
    w1 = 1.0 / (1.0 + e2)
    w2 = e2 / (1.0 + e2)
    oh1 = jnp.where(row == i1, 1.0, 0.0)
    oh2 = jnp.where(row == i2, 1.0, 0.0)
    both = oh1 + oh2
    ti = lax.broadcasted_iota(jnp.int32, (tm, tm), 0)
    tj = lax.broadcasted_iota(jnp.int32, (tm, tm), 1)
    before = jnp.where(ti < tj, 1.0, 0.0).astype(BF16)
    pos = base_ref[...] + _dot(both.astype(BF16), before)
    r1 = jnp.sum(oh1 * pos, axis=0, keepdims=True)
    r2 = jnp.sum(oh2 * pos, axis=0, keepdims=True)
    base_ref[...] = base_ref[...] + jnp.sum(both, axis=1, keepdims=True)
    zi = jnp.zeros((N_EXPERTS - 2, tm), jnp.int32)
    zf = jnp.zeros((N_EXPERTS - 2, tm), F32)
    e_ref[...] = jnp.concatenate([i1, i2, zi], axis=0)
    p_ref[...] = jnp.concatenate([w1, w2, zf], axis=0)
    r_ref[...] = jnp.concatenate([r1.astype(jnp.int32), r2.astype(jnp.int32), zi], axis=0)
    cnt_ref[...] = jnp.broadcast_to(base_ref[...], cnt_ref.shape)


def _router(h2, w_router_t):
    T, D = h2.shape
    tm = TM_ROUTE
    col = lambda i: (0, i)
    return pl.pallas_call(
        _router_kernel,
        grid=(T // tm,),
        in_specs=[pl.BlockSpec((tm, D), lambda i: (i, 0)),
                  pl.BlockSpec((N_EXPERTS, D), lambda i: (0, 0))],
        out_specs=[pl.BlockSpec((N_EXPERTS, tm), col)] * 3 + [pl.BlockSpec((N_EXPERTS, LANES), lambda i: (0, 0))],
        out_shape=[jax.ShapeDtypeStruct((N_EXPERTS, T), jnp.int32),
                   jax.ShapeDtypeStruct((N_EXPERTS, T), F32),
                   jax.ShapeDtypeStruct((N_EXPERTS, T), jnp.int32),
                   jax.ShapeDtypeStruct((N_EXPERTS, LANES), F32)],
        scratch_shapes=[pltpu.VMEM((N_EXPERTS, 1), F32)],
        compiler_params=_params("arbitrary"),
        name="moe_router",
    )(h2, w_router_t)


def _dispatch_kernel(dest_ref, h_ref, xs_in_ref, xs_ref, buf_ref, sem_in, sem_out):
    del xs_in_ref
    i = pl.program_id(0)
    n = pl.num_programs(0)
    tm = buf_ref.shape[1]

    def load(t):
        s = t % DISP_SLOTS
        return pltpu.make_async_copy(h_ref.at[pl.ds(pl.multiple_of(t * tm, tm), tm)], buf_ref.at[s],
                                     sem_in.at[s])

    def wait_rows(t):
        s = t % DISP_SLOTS
        for _ in range(2):
            pltpu.make_async_copy(buf_ref.at[s], xs_ref.at[pl.ds(0, tm)], sem_out.at[s]).wait()

    @pl.when(i == 0)
    def _():
        load(0).start()
        pl.when(n > 1)(lambda: load(1).start())

    load(i).wait()
    slot = i % DISP_SLOTS

    def issue(r, _):
        for k in range(2):
            pltpu.make_async_copy(buf_ref.at[slot, pl.ds(r, 1)],
                                  xs_ref.at[pl.ds(dest_ref[0, 0, k * tm + r], 1)], sem_out.at[slot]).start()
        return 0

    lax.fori_loop(0, tm, issue, 0, unroll=8)
    pl.when(i > 0)(lambda: wait_rows(i - 1))
    pl.when(i + 2 < n)(lambda: load(i + 2).start())
    pl.when(i == n - 1)(lambda: wait_rows(i))


def _dispatch(h2, dest, n_slots):
    T, D = h2.shape
    tm = TM_DISP
    dest_t = dest.reshape(2, T // tm, tm).transpose(1, 0, 2).reshape(T // tm, 1, 2 * tm)
    xs0 = jnp.zeros((n_slots, D), h2.dtype)
    return pl.pallas_call(
        _dispatch_kernel,
        grid=(T // tm,),
        in_specs=[pl.BlockSpec((1, 1, 2 * tm), lambda i: (i, 0, 0), memory_space=pltpu.SMEM),
                  pl.BlockSpec(memory_space=pl.ANY),
                  pl.BlockSpec(memory_space=pl.ANY)],
        out_specs=pl.BlockSpec(memory_space=pl.ANY),
        out_shape=jax.ShapeDtypeStruct((n_slots, D), h2.dtype),
        scratch_shapes=[pltpu.VMEM((DISP_SLOTS, tm, D), h2.dtype),
                        pltpu.SemaphoreType.DMA((DISP_SLOTS,)), pltpu.SemaphoreType.DMA((DISP_SLOTS,))],
        input_output_aliases={2: 0},
        compiler_params=_params("arbitrary"),
        name="moe_dispatch",
    )(dest_t, h2, xs0)


def _expert_kernel(be_ref, nu_ref, x_ref, wg_ref, wu_ref, wd_ref, o_ref):
    del be_ref
    used = pl.program_id(0) < nu_ref[0]
    half = D_MODEL // 2

    @pl.when(used)
    def _():
        lo, hi = _unpack_bf16_pair(x_ref[...])
        a = _dot(lo, wg_ref[0, :half, :]) + _dot(hi, wg_ref[0, half:, :])
        b = _dot(lo, wu_ref[0, :half, :]) + _dot(hi, wu_ref[0, half:, :])
        o_ref[...] = _dot((_silu(a) * b).astype(BF16), wd_ref[0])

    @pl.when(jnp.logical_not(used))
    def _():
        o_ref[...] = jnp.zeros_like(o_ref)


def _experts(xs, blk_e, n_used, w_gate, w_up, w_down):
    P = xs.shape[0]
    _, D, F = w_gate.shape
    tm = TM_MOE
    grid_spec = pltpu.PrefetchScalarGridSpec(
        num_scalar_prefetch=2,
        grid=(P // tm,),
        in_specs=[pl.BlockSpec((tm, D // 2), lambda i, be, nu: (i, 0)),
                  pl.BlockSpec((1, D, F), lambda i, be, nu: (be[i], 0, 0), pipeline_mode=pl.Buffered(1)),
                  pl.BlockSpec((1, D, F), lambda i, be, nu: (be[i], 0, 0), pipeline_mode=pl.Buffered(1)),
                  pl.BlockSpec((1, F, D), lambda i, be, nu: (be[i], 0, 0), pipeline_mode=pl.Buffered(1))],
        out_specs=pl.BlockSpec((tm, D), lambda i, be, nu: (i, 0)))
    return pl.pallas_call(
        _expert_kernel,
        grid_spec=grid_spec,
        out_shape=jax.ShapeDtypeStruct((P, D), F32),
        compiler_params=_params("arbitrary"),
        name="moe_experts",
    )(blk_e, n_used, xs, w_gate, w_up, w_down)


def _combine_kernel(dest_ref, dest_next_ref, x_ref, mod_ref, p_ref, g_ref, ys_ref, o_ref, ybuf_ref, sems):
    i = pl.program_id(0)
    tm = x_ref.shape[0]
    slot = i % 2

    def gather(d_ref, s):
        def issue(r, _):
            for k in range(2):
                pltpu.make_async_copy(ys_ref.at[pl.ds(d_ref[0, 0, k * tm + r], 1)],
                                      ybuf_ref.at[s, k, pl.ds(r, 1)], sems.at[s]).start()
            return 0

        lax.fori_loop(0, tm, issue, 0, unroll=8)

    pl.when(i == 0)(lambda: gather(dest_ref, 0))
    pl.when(i + 1 < pl.num_programs(0))(lambda: gather(dest_next_ref, 1 - slot))
    for k in range(2):
        pltpu.make_async_copy(ys_ref.at[pl.ds(0, tm)], ybuf_ref.at[slot, k], sems.at[slot]).wait()
    p = p_ref[...]
    f = ybuf_ref[slot, 0] * p[:, 0:1] + ybuf_ref[slot, 1] * p[:, 1:2]
    x2 = x_ref[...] + mod_ref[0, 5:6, :] * f
    o_ref[...] = _rms(x2, g_ref[...])


def _combine(x1, mod, probs, final_g, ys, dest):
    T, D = x1.shape
    tm = TM_COMB
    per_b = SEQ // tm
    n = T // tm
    dest_t = dest.reshape(2, n, tm).transpose(1, 0, 2).reshape(n, 1, 2 * tm)
    return pl.pallas_call(
        _combine_kernel,
        grid=(n,),
        in_specs=[pl.BlockSpec((1, 1, 2 * tm), lambda i: (i, 0, 0), memory_space=pltpu.SMEM),
                  pl.BlockSpec((1, 1, 2 * tm), lambda i: (jnp.minimum(i + 1, n - 1), 0, 0),
                               memory_space=pltpu.SMEM),
                  pl.BlockSpec((tm, D), lambda i: (i, 0)),
                  pl.BlockSpec((1, 6, D), lambda i: (i // per_b, 0, 0)),
                  pl.BlockSpec((tm, 2), lambda i: (i, 0)),
                  pl.BlockSpec((1, D), lambda i: (0, 0)),
                  pl.BlockSpec(memory_space=pl.ANY)],
        out_specs=pl.BlockSpec((tm, D), lambda i: (i, 0)),
        out_shape=jax.ShapeDtypeStruct((T, D), F32),
        scratch_shapes=[pltpu.VMEM((2, 2, tm, D), F32), pltpu.SemaphoreType.DMA((2,))],
        compiler_params=_params("arbitrary"),
        name="moe_combine",
    )(dest_t, dest_t, x1, mod, probs, final_g, ys)


def _moe_layer(x1, h2, h2_packed, mod, w_router, w_gate, w_up, w_down, final_g):
    T, D = x1.shape
    e_idx, probs, rank, counts = _router(h2, w_router.T.astype(BF16))
    counts = counts[:, 0].astype(jnp.int32)
    padded = (counts + TM_MOE - 1) // TM_MOE * TM_MOE
    pad_end = jnp.cumsum(padded)
    pad_start = pad_end - padded
    eid = jnp.arange(N_EXPERTS, dtype=jnp.int32)[:, None, None]
    dest = jnp.sum(jnp.where(e_idx[None, :2] == eid, pad_start[:, None, None], 0), axis=0) + rank[:2]
    n_slots = 2 * T + N_EXPERTS * TM_MOE
    n_blk = n_slots // TM_MOE
    blk_e = jnp.minimum(jnp.searchsorted(pad_end, jnp.arange(n_blk, dtype=jnp.int32) * TM_MOE,
                                         side='right'), N_EXPERTS - 1).astype(jnp.int32)
    n_used = (pad_end[-1:] // TM_MOE).astype(jnp.int32)
    xs = _dispatch(h2_packed, dest, n_slots)
    ys = _experts(xs, blk_e, n_used, w_gate.astype(BF16), w_up.astype(BF16), w_down.astype(BF16))
    return _combine(x1, mod, probs[:2].T, final_g.reshape(1, -1), ys, dest)


def _final_norm_kernel(x_ref, g_ref, o_ref):
    o_ref[...] = _rms(x_ref[...], g_ref[...])


def _final_norm(x, g):
    T, D = x.shape
    tm = 1024
    return pl.pallas_call(
        _final_norm_kernel,
        grid=(T // tm,),
        in_specs=[pl.BlockSpec((tm, D), lambda i: (i, 0)), pl.BlockSpec((1, D), lambda i: (0, 0))],
        out_specs=pl.BlockSpec((tm, D), lambda i: (i, 0)),
        out_shape=jax.ShapeDtypeStruct((T, D), F32),
        compiler_params=_params("arbitrary"),
        name="final_norm",
    )(x, g.reshape(1, -1))


def _regroup_w_in(w):
    ab = jnp.pad(w[:, 4608:4616], ((0, 0), (0, LANES - 8)))
    return jnp.concatenate([w[:, :4608], w[:, 4616:], ab], axis=1).astype(BF16)


def _lambda_init(layer):
    return 0.8 - 0.6 * math.exp(-0.3 * layer)


def kernel(x, c, positions, norm1_g, norm2_g, w_mod, b_mod, w_in, gm_ln_g, gm_ln_b, gm_w_s, gm_b_s, da_lambda, da_subln_g, dn_conv_w, dn_a_log, dn_dt_bias, dn_norm_g, w_br_gm, w_br_da, w_br_dn, w_out, ffn_w_gate, ffn_w_up, ffn_w_down, moe_w_router, moe_w_gate, moe_w_up, moe_w_down, final_g):
    B, S, D = x.shape
    T = B * S
    xt = x.reshape(T, D)
    mod_all = _modulation(c, w_mod, b_mod).reshape(DEPTH, B, 6, D)
    cos_t, sin_t = _rope_tables(positions)
    for layer in range(DEPTH):
        mod = mod_all[layer]
        gm, qk, v, dn, gates, ab = _in_projection(xt, mod, norm1_g[layer].reshape(1, D),
                                                  _regroup_w_in(w_in[layer]), cos_t, sin_t, dn_conv_w[layer])
        y_gm = _gmlp(gm, gm_ln_g[layer], gm_ln_b[layer], gm_w_s[layer], gm_b_s[layer])
        y_da = _diff_attention(qk, v, da_lambda[layer], da_subln_g[layer], _lambda_init(layer), B)
        y_dn = _deltanet(dn, ab, dn_a_log[layer], dn_dt_bias[layer], dn_norm_g[layer], B)
        moe = layer % 2 == 1
        x1, h2, *packed = _merge(xt, mod, norm2_g[layer].reshape(1, D), y_gm, y_da, y_dn, gates,
                                 w_br_gm[layer].astype(BF16), w_br_da[layer].astype(BF16),
                                 w_br_dn[layer].astype(BF16), w_out[layer].astype(BF16), moe)
        if moe:
            xt = _moe_layer(x1, h2, packed[0], mod, moe_w_router[layer // 2], moe_w_gate[layer // 2],
                            moe_w_up[layer // 2], moe_w_down[layer // 2], final_g)
        else:
            xt = _dense_ffn(x1, h2, mod, ffn_w_gate[layer // 2].astype(BF16),
                            ffn_w_up[layer // 2].astype(BF16), ffn_w_down[layer // 2].astype(BF16))
    if DEPTH % 2 == 1:
        xt = _final_norm(xt, final_g)
    return xt.reshape(B, S, D)
```

```python
import functools
import math

import jax
import jax.numpy as jnp
import numpy as np
from jax import lax
from jax.experimental import pallas as pl
from jax.experimental.pallas import tpu as pltpu

D_MODEL = 1024
SEQ = 2048
DEPTH = 2
CHUNK = 64
EPS = 1e-6
GM_WIDTH = D_MODEL // 2
GM_GROUPS = 4
GM_BLOCK = 128
DA_HEADS = 4
DA_HEAD_DIM = 64
DA_V_DIM = 2 * DA_HEAD_DIM
ROPE_THETA = 10000.0
DN_HEADS = 4
DN_HEAD_DIM = 128
DN_WIDTH = DN_HEADS * DN_HEAD_DIM
DN_CONV = 4
N_EXPERTS = 8
FF_EXPERT = 7 * D_MODEL // 2

LANES = 128
VMEM_LIMIT = 56 * 1024 * 1024

C_GM, C_QK, C_V, C_DN, C_GATE, C_AB = 0, 1024, 2048, 2560, 4608, 7680
W_ALL = C_AB + LANES
PROJ_CHUNK = 512

TM_PROJ = 512
TM_GM = 512
TQ = 256
TM_MERGE = 512
TM_FFN = 512
TM_ROUTE = 512
TM_MOE = 512
TM_DISP = 256
DISP_SLOTS = 3
TM_COMB = 256
ATTN_LOOKAHEAD = 2
VT_ROWS = DA_V_DIM + 16
DN_BLOCK = 256

BF16 = jnp.bfloat16
F32 = jnp.float32


def _params(*sem):
    return pltpu.CompilerParams(dimension_semantics=sem, vmem_limit_bytes=VMEM_LIMIT)


def _dot(a, b):
    return jnp.dot(a, b, preferred_element_type=F32)


def _dot_nt(a, b):
    return lax.dot_general(a, b, (((1,), (1,)), ((), ())), preferred_element_type=F32)


def _dot_tn(a, b):
    return lax.dot_general(a, b, (((0,), (0,)), ((), ())), preferred_element_type=F32)


def _dot_hi(a, b):
    return jnp.dot(a, b, preferred_element_type=F32, precision=lax.Precision.HIGHEST)


def _rms(x, g):
    return x * lax.rsqrt(jnp.mean(x * x, axis=-1, keepdims=True) + EPS) * g


def _silu(x):
    return x * jax.nn.sigmoid(x)


def _mod_kernel(c_ref, w_ref, b_ref, o_ref):
    c = c_ref[...]
    o_ref[0] = _dot_hi(_silu(c), w_ref[0]) + b_ref[0]


def _modulation(c, w_mod, b_mod):
    B, D = c.shape
    L, _, N = w_mod.shape
    tn = 1536
    return pl.pallas_call(
        _mod_kernel,
        grid=(L, N // tn),
        in_specs=[pl.BlockSpec((B, D), lambda l, j: (0, 0)),
                  pl.BlockSpec((1, D, tn), lambda l, j: (l, 0, j)),
                  pl.BlockSpec((1, 1, tn), lambda l, j: (l, 0, j))],
        out_specs=pl.BlockSpec((1, B, tn), lambda l, j: (l, 0, j)),
        out_shape=jax.ShapeDtypeStruct((L, B, N), F32),
        compiler_params=_params("arbitrary", "arbitrary"),
        name="modulation",
    )(c, w_mod, b_mod.reshape(L, 1, N))


def _rope_tab_kernel(pos_ref, inv_ref, sgn_ref, cos_ref, sin_ref):
    ang = pos_ref[...].astype(F32) * inv_ref[...]
    cos_ref[...] = jnp.cos(ang)
    sin_ref[...] = jnp.sin(ang) * sgn_ref[...]


def _rope_tables(positions):
    T = positions.size
    inv_freq = ROPE_THETA ** (-jnp.arange(0, DA_HEAD_DIM, 2, dtype=F32) / DA_HEAD_DIM)
    inv = jnp.tile(inv_freq, LANES // (DA_HEAD_DIM // 2)).reshape(1, LANES)
    half = DA_HEAD_DIM // 2
    sgn = np.tile(np.concatenate([-np.ones(half), np.ones(half)]), LANES // DA_HEAD_DIM)
    sgn = jnp.asarray(sgn, F32).reshape(1, LANES)
    tm = 1024
    return pl.pallas_call(
        _rope_tab_kernel,
        grid=(T // tm,),
        in_specs=[pl.BlockSpec((tm, 1), lambda i: (i, 0)),
                  pl.BlockSpec((1, LANES), lambda i: (0, 0)),
                  pl.BlockSpec((1, LANES), lambda i: (0, 0))],
        out_specs=[pl.BlockSpec((tm, LANES), lambda i: (i, 0))] * 2,
        out_shape=[jax.ShapeDtypeStruct((T, LANES), F32)] * 2,
        compiler_params=_params("arbitrary"),
        name="rope_tables",
    )(positions.reshape(T, 1), inv, sgn)


def _rope_rows(x, cos, sin):
    lane = lax.broadcasted_iota(jnp.int32, x.shape, 1)
    first = (lane % DA_HEAD_DIM) < (DA_HEAD_DIM // 2)
    partner = jnp.where(first, pltpu.roll(x, LANES - DA_HEAD_DIM // 2, axis=1),
                        pltpu.roll(x, DA_HEAD_DIM // 2, axis=1))
    return x * cos + partner * sin


def _proj_kernel(x_ref, mod_ref, g_ref, w_ref, cos_ref, sin_ref, cw_ref,
                 gm_ref, qk_ref, v_ref, dn_ref, gate_ref, ab_ref, prev_ref):
    tm = x_ref.shape[0]

    @pl.when(pl.program_id(0) % (SEQ // tm) == 0)
    def _():
        prev_ref[...] = jnp.zeros_like(prev_ref)

    def conv_silu(y, idx):
        cw = cw_ref[:, idx * PROJ_CHUNK:(idx + 1) * PROJ_CHUNK]
        ycat = jnp.concatenate([prev_ref[idx], y], axis=0)
        out = y * cw[DN_CONV - 1:DN_CONV, :]
        for j in range(DN_CONV - 1):
            out = out + pltpu.roll(ycat, DN_CONV - 1 - j, axis=0)[8:, :] * cw[j:j + 1, :]
        prev_ref[idx] = y[tm - 8:, :]
        return _silu(out)

    def l2norm_heads(t, mult):
        parts = []
        for j in range(PROJ_CHUNK // DN_HEAD_DIM):
            seg = t[:, j * DN_HEAD_DIM:(j + 1) * DN_HEAD_DIM]
            parts.append(seg * (lax.rsqrt(jnp.sum(seg * seg, axis=-1, keepdims=True) + EPS) * mult))
        return jnp.concatenate(parts, axis=1)

    x = x_ref[...]
    shift = mod_ref[0, 0:1, :]
    scale = mod_ref[0, 1:2, :]
    h = (_rms(x, g_ref[...]) * (1.0 + scale) + shift).astype(BF16)
    cos = cos_ref[...]
    sin = sin_ref[...]
    q_scale = DA_HEAD_DIM ** -0.5 * math.log2(math.e)

    def rope_chunk(y, mult):
        parts = [_rope_rows(y[:, j * LANES:(j + 1) * LANES], cos, sin) * mult
                 for j in range(PROJ_CHUNK // LANES)]
        return jnp.concatenate(parts, axis=1)

    for c0 in range(0, C_AB, PROJ_CHUNK):
        y = _dot(h, w_ref[:, c0:c0 + PROJ_CHUNK])
        if c0 < C_QK:
            gm_ref[:, c0:c0 + PROJ_CHUNK] = y.astype(BF16)
        elif c0 < C_V:
            mult = q_scale if c0 == C_QK else 1.0
            qk_ref[:, c0 - C_QK:c0 - C_QK + PROJ_CHUNK] = rope_chunk(y, mult).astype(BF16)
        elif c0 < C_DN:
            v_ref[...] = y.astype(BF16)
        elif c0 < C_GATE:
            idx = (c0 - C_DN) // PROJ_CHUNK
            if idx < 3:
                y = conv_silu(y, idx)
            if idx < 2:
                y = l2norm_heads(y, DN_HEAD_DIM ** -0.5 if idx == 0 else 1.0)
            dn_ref[:, c0 - C_DN:c0 - C_DN + PROJ_CHUNK] = y.astype(BF16)
        else:
            gate_ref[:, c0 - C_GATE:c0 - C_GATE + PROJ_CHUNK] = y.astype(BF16)
    ab_ref[...] = _dot(h, w_ref[:, C_AB:W_ALL])


def _in_projection(x, mod, g, w_all, cos_t, sin_t, conv_w):
    T, D = x.shape
    tm = TM_PROJ
    per_b = SEQ // tm
    row = lambda i: (i, 0)
    widths = (C_QK - C_GM, C_V - C_QK, C_DN - C_V, C_GATE - C_DN, C_AB - C_GATE)
    return pl.pallas_call(
        _proj_kernel,
        grid=(T // tm,),
        in_specs=[pl.BlockSpec((tm, D), row),
                  pl.BlockSpec((1, 6, D), lambda i: (i // per_b, 0, 0)),
                  pl.BlockSpec((1, D), lambda i: (0, 0)),
                  pl.BlockSpec((D, W_ALL), lambda i: (0, 0), pipeline_mode=pl.Buffered(1)),
                  pl.BlockSpec((tm, LANES), row),
                  pl.BlockSpec((tm, LANES), row),
                  pl.BlockSpec((DN_CONV, 3 * DN_WIDTH), lambda i: (0, 0))],
        out_specs=[pl.BlockSpec((tm, w), row) for w in widths] + [pl.BlockSpec((tm, LANES), row)],
        out_shape=[jax.ShapeDtypeStruct((T, w), BF16) for w in widths]
                  + [jax.ShapeDtypeStruct((T, LANES), F32)],
        scratch_shapes=[pltpu.VMEM((3, 8, PROJ_CHUNK), F32)],
        compiler_params=_params("arbitrary"),
        name="in_projection",
    )(x, mod, g, w_all, cos_t, sin_t, conv_w)


def _gelu(x):
    return 0.5 * x * (1.0 + lax.erf(x * np.float32(math.sqrt(0.5))))


def _gmlp_kernel(uv_ref, lng_ref, lnb_ref, ws_ref, bs_ref, o_ref):
    u = _gelu(uv_ref[:, :GM_WIDTH].astype(F32))
    v = _gelu(uv_ref[:, GM_WIDTH:].astype(F32))
    mu = jnp.mean(v, axis=-1, keepdims=True)
    vc = v - mu
    var = jnp.mean(vc * vc, axis=-1, keepdims=True)
    v = (vc * lax.rsqrt(var + EPS) * lng_ref[...] + lnb_ref[...]).astype(BF16)
    ri = lax.broadcasted_iota(jnp.int32, (GM_BLOCK, GM_BLOCK), 0) // CHUNK
    ci = lax.broadcasted_iota(jnp.int32, (GM_BLOCK, GM_BLOCK), 1) // CHUNK
    allowed = ci <= ri
    gc = GM_WIDTH // GM_GROUPS
    for g in range(GM_GROUPS):
        w = jnp.where(allowed, ws_ref[g], 0.0).astype(BF16)
        bias = bs_ref[g]
        for r in range(uv_ref.shape[0] // GM_BLOCK):
            rows = slice(r * GM_BLOCK, (r + 1) * GM_BLOCK)
            cols = slice(g * gc, (g + 1) * gc)
            sv = _dot(w, v[rows, cols]) + bias
            o_ref[rows, cols] = (u[rows, cols] * sv).astype(BF16)


def _gmlp(uv, ln_g, ln_b, w_s, b_s):
    T = uv.shape[0]
    tm = TM_GM
    return pl.pallas_call(
        _gmlp_kernel,
        grid=(T // tm,),
        in_specs=[pl.BlockSpec((tm, 2 * GM_WIDTH), lambda i: (i, 0)),
                  pl.BlockSpec((1, GM_WIDTH), lambda i: (0, 0)),
                  pl.BlockSpec((1, GM_WIDTH), lambda i: (0, 0)),
                  pl.BlockSpec((GM_GROUPS, GM_BLOCK, GM_BLOCK), lambda i: (0, 0, 0)),
                  pl.BlockSpec((GM_GROUPS, GM_BLOCK, 1), lambda i: (0, 0, 0))],
        out_specs=pl.BlockSpec((tm, GM_WIDTH), lambda i: (i, 0)),
        out_shape=jax.ShapeDtypeStruct((T, GM_WIDTH), BF16),
        compiler_params=_params("arbitrary"),
        name="gmlp",
    )(uv, ln_g.reshape(1, -1), ln_b.reshape(1, -1), w_s, b_s.reshape(GM_GROUPS, GM_BLOCK, 1))


def _attn_kernel(lam_ref, g_ref, q_ref, k_ref, v_ref, o_ref, qz_ref, vt_ref, *stats, lambda_init):
    nc = 2 * DA_HEADS
    m_ref, acc_ref, s_ref = (stats[j * nc:(j + 1) * nc] for j in range(3))
    i = pl.program_id(1)
    lp = lam_ref[...]
    lam = (jnp.exp(jnp.sum(lp[0:1] * lp[1:2], axis=-1, keepdims=True))
           - jnp.exp(jnp.sum(lp[2:3] * lp[3:4], axis=-1, keepdims=True)) + lambda_init)
    @pl.when(i == 0)
    def _():
        for kb in range(SEQ // TQ):
            for h in range(DA_HEADS):
                vt_ref[h * VT_ROWS:h * VT_ROWS + DA_V_DIM, kb * TQ:(kb + 1) * TQ] = (
                    v_ref[kb * TQ:(kb + 1) * TQ, h * LANES:(h + 1) * LANES].T)
        for h in range(DA_HEADS):
            vt_ref[h * VT_ROWS + DA_V_DIM:(h + 1) * VT_ROWS, :] = jnp.ones((VT_ROWS - DA_V_DIM, SEQ), BF16)

    row = lax.broadcasted_iota(jnp.int32, (LANES, TQ), 0)
    for h in range(DA_HEADS):
        qt = q_ref[:, h * LANES:(h + 1) * LANES].T
        zero = jnp.zeros_like(qt)
        qz_ref[2 * h] = jnp.where(row < DA_HEAD_DIM, qt, zero)
        qz_ref[2 * h + 1] = jnp.where(row >= DA_HEAD_DIM, qt, zero)
    for c in range(nc):
        m_ref[c][...] = jnp.full_like(m_ref[c], -jnp.inf)
        acc_ref[c][...] = jnp.zeros_like(acc_ref[c])

    def key_rows(kb):
        return pl.ds(pl.multiple_of(kb * TQ, TQ), TQ)

    def scores(kb, c, diagonal):
        h = c // 2
        s = _dot(k_ref[key_rows(kb), h * LANES:(h + 1) * LANES], qz_ref[c])
        if diagonal:
            ki = lax.broadcasted_iota(jnp.int32, s.shape, 0) // CHUNK
            qi = lax.broadcasted_iota(jnp.int32, s.shape, 1) // CHUNK
            s = jnp.where(ki <= qi, s, -jnp.inf)
        return s

    def accumulate(kb, c, s):
        h = c // 2
        m_old = m_ref[c][...]
        m_new = jnp.maximum(m_old, jnp.max(s, axis=0, keepdims=True))
        alpha = jnp.exp2(m_old - m_new)
        e = jnp.exp2((s - m_new).astype(BF16))
        vt = vt_ref[h * VT_ROWS:(h + 1) * VT_ROWS, key_rows(kb)]
        acc_ref[c][...] = alpha * acc_ref[c][...] + _dot(vt, e)
        m_ref[c][...] = m_new

    def fold_and_prefetch(kb, diagonal_next):
        fresh = {}
        for c in range(nc + ATTN_LOOKAHEAD):
            if c < nc:
                fresh[c] = scores(kb + 1, c, diagonal_next)
            if c >= ATTN_LOOKAHEAD:
                cc = c - ATTN_LOOKAHEAD
                accumulate(kb, cc, s_ref[cc][...])
                s_ref[cc][...] = fresh.pop(cc)

    @pl.when(i == 0)
    def _():
        for c in range(nc):
            s_ref[c][...] = scores(0, c, True)

    @pl.when(i > 0)
    def _():
        for c in range(nc):
            s_ref[c][...] = scores(0, c, False)

        def body(kb, carry):
            fold_and_prefetch(kb, False)
            return carry

        lax.fori_loop(0, i - 1, body, 0)
        fold_and_prefetch(i - 1, True)

    for c in range(nc):
        accumulate(i, c, s_ref[c][...])
    for h in range(DA_HEADS):
        num0, den0 = acc_ref[2 * h][:DA_V_DIM, :], acc_ref[2 * h][DA_V_DIM:DA_V_DIM + 1, :]
        num1, den1 = acc_ref[2 * h + 1][:DA_V_DIM, :], acc_ref[2 * h + 1][DA_V_DIM:DA_V_DIM + 1, :]
        o = num0 / den0 - lam * (num1 / den1)
        ms = jnp.mean(o * o, axis=0, keepdims=True)
        o = o * lax.rsqrt(ms + EPS) * (g_ref[...] * (1.0 - lambda_init))
        o_ref[:, h * LANES:(h + 1) * LANES] = o.T.astype(BF16)


def _diff_attention(qk, v, lam_p, subln_g, lambda_init, batch):
    T = qk.shape[0]
    nq = SEQ // TQ
    W = DA_HEADS * DA_V_DIM
    kernel = functools.partial(_attn_kernel, lambda_init=lambda_init)
    return pl.pallas_call(
        kernel,
        grid=(batch, nq),
        in_specs=[pl.BlockSpec((4, DA_HEAD_DIM), lambda b, i: (0, 0)),
                  pl.BlockSpec((DA_V_DIM, 1), lambda b, i: (0, 0)),
                  pl.BlockSpec((TQ, W), lambda b, i: (b * nq + i, 0)),
                  pl.BlockSpec((SEQ, W), lambda b, i: (b, 1)),
                  pl.BlockSpec((SEQ, W), lambda b, i: (b, 0))],
        out_specs=pl.BlockSpec((TQ, W), lambda b, i: (b * nq + i, 0)),
        out_shape=jax.ShapeDtypeStruct((T, W), BF16),
        scratch_shapes=[pltpu.VMEM((2 * DA_HEADS, LANES, TQ), BF16),
                        pltpu.VMEM((DA_HEADS * VT_ROWS, SEQ), BF16)]
                       + [pltpu.VMEM((1, TQ), F32)] * (2 * DA_HEADS)
                       + [pltpu.VMEM((VT_ROWS, TQ), F32)] * (2 * DA_HEADS)
                       + [pltpu.VMEM((TQ, TQ), F32)] * (2 * DA_HEADS),
        compiler_params=_params("arbitrary", "arbitrary"),
        name="diff_attention",
    )(lam_p, subln_g.reshape(-1, 1), qk, qk, v)


def _split3(x):
    hi = x.astype(BF16)
    r = x - hi.astype(F32)
    mid = r.astype(BF16)
    lo = (r - mid.astype(F32)).astype(BF16)
    return hi, mid, lo


def _dn_kernel(dn_ref, a_ref, at_ref, alog_ref, dtb_ref, alogt_ref, dtbt_ref, ng_ref,
               o_ref, *state_ref):
    c = pl.program_id(1)
    R = DN_BLOCK

    @pl.when(c == 0)
    def _():
        for ref in state_ref:
            ref[...] = jnp.zeros_like(ref)

    W3 = 3 * DN_WIDTH
    ab = a_ref[...]
    g_col = -jnp.exp(alog_ref[...]) * jax.nn.softplus(ab + dtb_ref[...])
    beta_all = jax.nn.sigmoid(ab)
    abt = at_ref[0]
    g_row = -jnp.exp(alogt_ref[...]) * jax.nn.softplus(abt + dtbt_ref[...])
    ri = lax.broadcasted_iota(jnp.int32, (R, R), 0)
    ci = lax.broadcasted_iota(jnp.int32, (R, R), 1)
    tril = ri >= ci
    strict = ri > ci
    ones_tril = jnp.where(tril, 1.0, 0.0).astype(BF16)
    ones_triu = jnp.where(ci >= ri, 1.0, 0.0).astype(BF16)
    gc_col = sum(_dot(ones_tril, part) for part in _split3(g_col))
    gc_row = sum(_dot(part, ones_triu) for part in _split3(g_row))
    eye = jnp.where(ri == ci, 1.0, 0.0)
    same = {n: (ri // n) == (ci // n) for n in (16, 32, 64, 128)}

    H = range(DN_HEADS)
    hsl = [slice(h * DN_HEAD_DIM, (h + 1) * DN_HEAD_DIM) for h in H]
    q16 = [dn_ref[:, hsl[h]] for h in H]
    k16 = [dn_ref[:, DN_WIDTH + h * DN_HEAD_DIM:DN_WIDTH + (h + 1) * DN_HEAD_DIM] for h in H]
    q = [t.astype(F32) for t in q16]
    k = [t.astype(F32) for t in k16]
    v = [dn_ref[:, 2 * DN_WIDTH + h * DN_HEAD_DIM:2 * DN_WIDTH + (h + 1) * DN_HEAD_DIM].astype(F32) for h in H]
    gc =[gc_col[:, h:h + 1] for h in H]
    beta = [beta_all[:, DN_HEADS + h:DN_HEADS + h + 1] for h in H]
    g_last = [gc_col[R - 1:R, h:h + 1] for h in H]
    decay = [jnp.exp(jnp.where(tril, gc[h] - gc_row[h:h + 1, :], -jnp.inf)) for h in H]
    eg = [jnp.exp(t) for t in gc]
    kb = [k[h] * beta[h] for h in H]
    s1 = [_dot_nt(jnp.concatenate([kb[h].astype(BF16), q16[h]], axis=0), k16[h]) for h in H]
    a_mat = [jnp.where(strict, s1[h][:R] * decay[h], 0.0) for h in H]
    qk = [(s1[h][R:] * decay[h]).astype(BF16) for h in H]
    p = [jnp.where(same[16], -t, 0.0) for t in a_mat]
    t_inv = [eye + t for t in p]
    for _ in range(3):
        p16 = [t.astype(BF16) for t in p]
        p = [_dot(t, t) for t in p16]
        t_inv = [t_inv[h] + _dot(t_inv[h].astype(BF16), p[h].astype(BF16)) for h in H]
    a16 = [t.astype(BF16) for t in a_mat]
    t16 = [t.astype(BF16) for t in t_inv]
    for n in (16, 32, 64, 128):
        off = jnp.logical_not(same[n])
        if 2 * n < R:
            off = jnp.logical_and(same[2 * n], off)
        off16 = jnp.where(off, 1.0, 0.0).astype(BF16)
        tl = [_dot(t16[h], a16[h] * off16).astype(BF16) for h in H]
        t16 = [t16[h] - _dot(tl[h], t16[h]).astype(BF16) * off16 for h in H]
    sol = [_dot(t16[h], jnp.concatenate([v[h] * beta[h], kb[h] * eg[h]], axis=1).astype(BF16))
           for h in H]
    state = [state_ref[h][...] for h in H]
    m1 = [_dot(jnp.concatenate([sol[h][:, DN_HEAD_DIM:], q[h] * eg[h]], axis=0).astype(BF16),
               state[h].astype(BF16)) for h in H]
    vn16 = [(sol[h][:, :DN_HEAD_DIM] - m1[h][:R]).astype(BF16) for h in H]
    o = [m1[h][R:] + _dot(qk[h], vn16[h]) for h in H]
    kd = [(k[h] * jnp.exp(g_last[h] - gc[h])).astype(BF16) for h in H]
    for h in H:
        state_ref[h][...] = state[h] * jnp.exp(g_last[h]) + _dot_tn(kd[h], vn16[h])
    for h in H:
        z = dn_ref[:, W3 + h * DN_HEAD_DIM:W3 + (h + 1) * DN_HEAD_DIM].astype(F32)
        o_ref[:, hsl[h]] = (_rms(o[h], ng_ref[...]) * _silu(z)).astype(BF16)


def _deltanet(dn, ab, a_log, dt_bias, norm_g, batch):
    T = dn.shape[0]
    R = DN_BLOCK
    n = SEQ // R
    pad = lambda t: jnp.zeros((1, LANES), F32).at[0, :DN_HEADS].set(t)
    col8 = lambda t: jnp.zeros((8, 1), F32).at[:DN_HEADS, 0].set(t)
    ab_t = ab[:, :8].reshape(T // R, R, 8).transpose(0, 2, 1)
    return pl.pallas_call(
        _dn_kernel,
        grid=(batch, n),
        in_specs=[pl.BlockSpec((R, 4 * DN_WIDTH), lambda b, c: (b * n + c, 0)),
                  pl.BlockSpec((R, LANES), lambda b, c: (b * n + c, 0)),
                  pl.BlockSpec((1, 8, R), lambda b, c: (b * n + c, 0, 0)),
                  pl.BlockSpec((1, LANES), lambda b, c: (0, 0)),
                  pl.BlockSpec((1, LANES), lambda b, c: (0, 0)),
                  pl.BlockSpec((8, 1), lambda b, c: (0, 0)),
                  pl.BlockSpec((8, 1), lambda b, c: (0, 0)),
                  pl.BlockSpec((1, DN_HEAD_DIM), lambda b, c: (0, 0))],
        out_specs=pl.BlockSpec((R, DN_WIDTH), lambda b, c: (b * n + c, 0)),
        out_shape=jax.ShapeDtypeStruct((T, DN_WIDTH), BF16),
        scratch_shapes=[pltpu.VMEM((DN_HEAD_DIM, DN_HEAD_DIM), F32)] * DN_HEADS,
        compiler_params=_params("arbitrary", "arbitrary"),
        name="deltanet",
    )(dn, ab, ab_t, pad(a_log), pad(dt_bias), col8(a_log), col8(dt_bias), norm_g.reshape(1, -1))


def _pack_bf16_pair(lo, hi):
    lo_bits = lax.bitcast_convert_type(lo.astype(BF16).astype(F32), jnp.uint32)
    hi_bits = lax.bitcast_convert_type(hi.astype(BF16).astype(F32), jnp.uint32)
    return (lo_bits >> 16) | (hi_bits & jnp.uint32(0xFFFF0000))


def _unpack_bf16_pair(words):
    lo = lax.bitcast_convert_type(words << 16, F32)
    hi = lax.bitcast_convert_type(words & jnp.uint32(0xFFFF0000), F32)
    return lo.astype(BF16), hi.astype(BF16)


def _merge_kernel(x_ref, mod_ref, g2_ref, ygm_ref, yda_ref, ydn_ref, gate_ref,
                  wgm_ref, wda_ref, wdn_ref, wout_ref, x1_ref, h2_ref, *maybe_packed_ref):
    D = D_MODEL
    merged = (jax.nn.sigmoid(gate_ref[:, 0:D].astype(F32)) * _dot(ygm_ref[...], wgm_ref[...])
              + jax.nn.sigmoid(gate_ref[:, D:2 * D].astype(F32)) * _dot(yda_ref[...], wda_ref[...])
              + jax.nn.sigmoid(gate_ref[:, 2 * D:3 * D].astype(F32)) * _dot(ydn_ref[...], wdn_ref[...]))
    y = _dot(merged.astype(BF16), wout_ref[...])
    x1 = x_ref[...] + mod_ref[0, 2:3, :] * y
    x1_ref[...] = x1
    h2 = _rms(x1, g2_ref[...]) * (1.0 + mod_ref[0, 4:5, :]) + mod_ref[0, 3:4, :]
    h2_ref[...] = h2.astype(BF16)
    for packed_ref in maybe_packed_ref:
        packed_ref[...] = _pack_bf16_pair(h2[:, :D // 2], h2[:, D // 2:])


def _merge(x, mod, g2, y_gm, y_da, y_dn, gates, w_gm, w_da, w_dn, w_out, with_packed):
    T, D = x.shape
    tm = TM_MERGE
    per_b = SEQ // tm
    row = lambda i: (i, 0)
    const = lambda i: (0, 0)
    out_specs = [pl.BlockSpec((tm, D), row), pl.BlockSpec((tm, D), row)]
    out_shape = [jax.ShapeDtypeStruct((T, D), F32), jax.ShapeDtypeStruct((T, D), BF16)]
    if with_packed:
        out_specs.append(pl.BlockSpec((tm, D // 2), row))
        out_shape.append(jax.ShapeDtypeStruct((T, D // 2), jnp.uint32))
    return pl.pallas_call(
        _merge_kernel,
        grid=(T // tm,),
        in_specs=[pl.BlockSpec((tm, D), row),
                  pl.BlockSpec((1, 6, D), lambda i: (i // per_b, 0, 0)),
                  pl.BlockSpec((1, D), const),
                  pl.BlockSpec((tm, GM_WIDTH), row),
                  pl.BlockSpec((tm, GM_WIDTH), row),
                  pl.BlockSpec((tm, DN_WIDTH), row),
                  pl.BlockSpec((tm, 3 * D), row),
                  pl.BlockSpec((GM_WIDTH, D), const),
                  pl.BlockSpec((GM_WIDTH, D), const),
                  pl.BlockSpec((DN_WIDTH, D), const),
                  pl.BlockSpec((D, D), const)],
        out_specs=out_specs,
        out_shape=out_shape,
        compiler_params=_params("arbitrary"),
        name="merge",
    )(x, mod, g2, y_gm, y_da, y_dn, gates, w_gm, w_da, w_dn, w_out)


def _ffn_kernel(x_ref, h_ref, mod_ref, wg_ref, wu_ref, wd_ref, o_ref):
    h = h_ref[...]
    a = _dot(h, wg_ref[...])
    b = _dot(h, wu_ref[...])
    f = _dot((_silu(a) * b).astype(BF16), wd_ref[...])
    o_ref[...] = x_ref[...] + mod_ref[0, 5:6, :] * f


def _dense_ffn(x1, h2, mod, w_gate, w_up, w_down):
    T, D = x1.shape
    F = w_gate.shape[1]
    tm = TM_FFN
    per_b = SEQ // tm
    row = lambda i: (i, 0)
    const = lambda i: (0, 0)
    return pl.pallas_call(
        _ffn_kernel,
        grid=(T // tm,),
        in_specs=[pl.BlockSpec((tm, D), row),
                  pl.BlockSpec((tm, D), row),
                  pl.BlockSpec((1, 6, D), lambda i: (i // per_b, 0, 0)),
                  pl.BlockSpec((D, F), const, pipeline_mode=pl.Buffered(1)),
                  pl.BlockSpec((D, F), const, pipeline_mode=pl.Buffered(1)),
                  pl.BlockSpec((F, D), const, pipeline_mode=pl.Buffered(1))],
        out_specs=pl.BlockSpec((tm, D), row),
        out_shape=jax.ShapeDtypeStruct((T, D), F32),
        compiler_params=_params("arbitrary"),
        name="dense_ffn",
    )(x1, h2, mod, w_gate, w_up, w_down)


def _router_kernel(h_ref, wr_ref, e_ref, p_ref, r_ref, cnt_ref, base_ref):
    i = pl.program_id(0)
    tm = h_ref.shape[0]

    @pl.when(i == 0)
    def _():
        base_ref[...] = jnp.zeros_like(base_ref)

    logits = _dot_nt(wr_ref[...], h_ref[...])
    row = lax.broadcasted_iota(jnp.int32, logits.shape, 0)
    m1 = jnp.max(logits, axis=0, keepdims=True)
    i1 = jnp.min(jnp.where(logits == m1, row, N_EXPERTS), axis=0, keepdims=True)
    rest = jnp.where(row == i1, -jnp.inf, logits)
    m2 = jnp.max(rest, axis=0, keepdims=True)
    i2 = jnp.min(jnp.where(rest == m2, row, N_EXPERTS), axis=0, keepdims=True)
    e2 = jnp.exp(m2 - m1)
    w1 = 1.0 / (1.0 + e2)
    w2 = e2 / (1.0 + e2)
    oh1 = jnp.where(row == i1, 1.0, 0.0)
    oh2 = jnp.where(row == i2, 1.0, 0.0)
    both = oh1 + oh2
    ti = lax.broadcasted_iota(jnp.int32, (tm, tm), 0)
    tj = lax.broadcasted_iota(jnp.int32, (tm, tm), 1)
    before = jnp.where(ti < tj, 1.0, 0.0).astype(BF16)
    pos = base_ref[...] + _dot(both.astype(BF16), before)
    r1 = jnp.sum(oh1 * pos, axis=0, keepdims=True)
    r2 = jnp.sum(oh2 * pos, axis=0, keepdims=True)
    base_ref[...] = base_ref[...] + jnp.sum(both, axis=1, keepdims=True)
    zi = jnp.zeros((N_EXPERTS - 2, tm), jnp.int32)
    zf = jnp.zeros((N_EXPERTS - 2, tm), F32)
    e_ref[...] = jnp.concatenate([i1, i2, zi], axis=0)
    p_ref[...] = jnp.concatenate([w1, w2, zf], axis=0)
    r_ref[...] = jnp.concatenate([r1.astype(jnp.int32), r2.astype(jnp.int32), zi], axis=0)
    cnt_ref[...] = jnp.broadcast_to(base_ref[...], cnt_ref.shape)


def _router(h2, w_router_t):
    T, D = h2.shape
    tm = TM_ROUTE
    col = lambda i: (0, i)
    return pl.pallas_call(
        _router_kernel,
        grid=(T // tm,),
        in_specs=[pl.BlockSpec((tm, D), lambda i: (i, 0)),
                  pl.BlockSpec((N_EXPERTS, D), lambda i: (0, 0))],
        out_specs=[pl.BlockSpec((N_EXPERTS, tm), col)] * 3 + [pl.BlockSpec((N_EXPERTS, LANES), lambda i: (0, 0))],
        out_shape=[jax.ShapeDtypeStruct((N_EXPERTS, T), jnp.int32),
                   jax.ShapeDtypeStruct((N_EXPERTS, T), F32),
                   jax.ShapeDtypeStruct((N_EXPERTS, T), jnp.int32),
                   jax.ShapeDtypeStruct((N_EXPERTS, LANES), F32)],
        scratch_shapes=[pltpu.VMEM((N_EXPERTS, 1), F32)],
        compiler_params=_params("arbitrary"),
        name="moe_router",
    )(h2, w_router_t)


def _dispatch_kernel(dest_ref, h_ref, xs_in_ref, xs_ref, buf_ref, sem_in, sem_out):
    del xs_in_ref
    i = pl.program_id(0)
    n = pl.num_programs(0)
    tm = buf_ref.shape[1]

    def load(t):
        s = t % DISP_SLOTS
        return pltpu.make_async_copy(h_ref.at[pl.ds(pl.multiple_of(t * tm, tm), tm)], buf_ref.at[s],
                                     sem_in.at[s])

    def wait_rows(t):
        s = t % DISP_SLOTS
        for _ in range(2):
            pltpu.make_async_copy(buf_ref.at[s], xs_ref.at[pl.ds(0, tm)], sem_out.at[s]).wait()

    @pl.when(i == 0)
    def _():
        load(0).start()
        pl.when(n > 1)(lambda: load(1).start())

    load(i).wait()
    slot = i % DISP_SLOTS

    def issue(r, _):
        for k in range(2):
            pltpu.make_async_copy(buf_ref.at[slot, pl.ds(r, 1)],
                                  xs_ref.at[pl.ds(dest_ref[0, 0, k * tm + r], 1)], sem_out.at[slot]).start()
        return 0

    lax.fori_loop(0, tm, issue, 0, unroll=8)
    pl.when(i > 0)(lambda: wait_rows(i - 1))
    pl.when(i + 2 < n)(lambda: load(i + 2).start())
    pl.when(i == n - 1)(lambda: wait_rows(i))


def _dispatch(h2, dest, n_slots):
    T, D = h2.shape
    tm = TM_DISP
    dest_t = dest.reshape(2, T // tm, tm).transpose(1, 0, 2).reshape(T // tm, 1, 2 * tm)
    xs0 = jnp.zeros((n_slots, D), h2.dtype)
    return pl.pallas_call(
        _dispatch_kernel,
        grid=(T // tm,),
        in_specs=[pl.BlockSpec((1, 1, 2 * tm), lambda i: (i, 0, 0), memory_space=pltpu.SMEM),
                  pl.BlockSpec(memory_space=pl.ANY),
                  pl.BlockSpec(memory_space=pl.ANY)],
        out_specs=pl.BlockSpec(memory_space=pl.ANY),
        out_shape=jax.ShapeDtypeStruct((n_slots, D), h2.dtype),
        scratch_shapes=[pltpu.VMEM((DISP_SLOTS, tm, D), h2.dtype),
                        pltpu.SemaphoreType.DMA((DISP_SLOTS,)), pltpu.SemaphoreType.DMA((DISP_SLOTS,))],
        input_output_aliases={2: 0},
        compiler_params=_params("arbitrary"),
        name="moe_dispatch",
    )(dest_t, h2, xs0)


def _expert_kernel(be_ref, nu_ref, x_ref, wg_ref, wu_ref, wd_ref, o_ref):
    del be_ref
    used = pl.program_id(0) < nu_ref[0]
    half = D_MODEL // 2

    @pl.when(used)
    def _():
        lo, hi = _unpack_bf16_pair(x_ref[...])
        a = _dot(lo, wg_ref[0, :half, :]) + _dot(hi, wg_ref[0, half:, :])
        b = _dot(lo, wu_ref[0, :half, :]) + _dot(hi, wu_ref[0, half:, :])
        o_ref[...] = _dot((_silu(a) * b).astype(BF16), wd_ref[0])

    @pl.when(jnp.logical_not(used))
    def _():
        o_ref[...] = jnp.zeros_like(o_ref)


def _experts(xs, blk_e, n_used, w_gate, w_up, w_down):
    P = xs.shape[0]
    _, D, F = w_gate.shape
    tm = TM_MOE
    grid_spec = pltpu.PrefetchScalarGridSpec(
        num_scalar_prefetch=2,
        grid=(P // tm,),
        in_specs=[pl.BlockSpec((tm, D // 2), lambda i, be, nu: (i, 0)),
                  pl.BlockSpec((1, D, F), lambda i, be, nu: (be[i], 0, 0), pipeline_mode=pl.Buffered(1)),
                  pl.BlockSpec((1, D, F), lambda i, be, nu: (be[i], 0, 0), pipeline_mode=pl.Buffered(1)),
                  pl.BlockSpec((1, F, D), lambda i, be, nu: (be[i], 0, 0), pipeline_mode=pl.Buffered(1))],
        out_specs=pl.BlockSpec((tm, D), lambda i, be, nu: (i, 0)))
    return pl.pallas_call(
        _expert_kernel,
        grid_spec=grid_spec,
        out_shape=jax.ShapeDtypeStruct((P, D), F32),
        compiler_params=_params("arbitrary"),
        name="moe_experts",
    )(blk_e, n_used, xs, w_gate, w_up, w_down)


def _combine_kernel(dest_ref, dest_next_ref, x_ref, mod_ref, p_ref, g_ref, ys_ref, o_ref, ybuf_ref, sems):
    i = pl.program_id(0)
    tm = x_ref.shape[0]
    slot = i % 2

    def gather(d_ref, s):
        def issue(r, _):
            for k in range(2):
                pltpu.make_async_copy(ys_ref.at[pl.ds(d_ref[0, 0, k * tm + r], 1)],
                                      ybuf_ref.at[s, k, pl.ds(r, 1)], sems.at[s]).start()
            return 0

        lax.fori_loop(0, tm, issue, 0, unroll=8)

    pl.when(i == 0)(lambda: gather(dest_ref, 0))
    pl.when(i + 1 < pl.num_programs(0))(lambda: gather(dest_next_ref, 1 - slot))
    for k in range(2):
        pltpu.make_async_copy(ys_ref.at[pl.ds(0, tm)], ybuf_ref.at[slot, k], sems.at[slot]).wait()
    p = p_ref[...]
    f = ybuf_ref[slot, 0] * p[:, 0:1] + ybuf_ref[slot, 1] * p[:, 1:2]
    x2 = x_ref[...] + mod_ref[0, 5:6, :] * f
    o_ref[...] = _rms(x2, g_ref[...])


def _combine(x1, mod, probs, final_g, ys, dest):
    T, D = x1.shape
    tm = TM_COMB
    per_b = SEQ // tm
    n = T // tm
    dest_t = dest.reshape(2, n, tm).transpose(1, 0, 2).reshape(n, 1, 2 * tm)
    return pl.pallas_call(
        _combine_kernel,
        grid=(n,),
        in_specs=[pl.BlockSpec((1, 1, 2 * tm), lambda i: (i, 0, 0), memory_space=pltpu.SMEM),
                  pl.BlockSpec((1, 1, 2 * tm), lambda i: (jnp.minimum(i + 1, n - 1), 0, 0),
                               memory_space=pltpu.SMEM),
                  pl.BlockSpec((tm, D), lambda i: (i, 0)),
                  pl.BlockSpec((1, 6, D), lambda i: (i // per_b, 0, 0)),
                  pl.BlockSpec((tm, 2), lambda i: (i, 0)),
                  pl.BlockSpec((1, D), lambda i: (0, 0)),
                  pl.BlockSpec(memory_space=pl.ANY)],
        out_specs=pl.BlockSpec((tm, D), lambda i: (i, 0)),
        out_shape=jax.ShapeDtypeStruct((T, D), F32),
        scratch_shapes=[pltpu.VMEM((2, 2, tm, D), F32), pltpu.SemaphoreType.DMA((2,))],
        compiler_params=_params("arbitrary"),
        name="moe_combine",
    )(dest_t, dest_t, x1, mod, probs, final_g, ys)


def _moe_layer(x1, h2, h2_packed, mod, w_router, w_gate, w_up, w_down, final_g):
    T, D = x1.shape
    e_idx, probs, rank, counts = _router(h2, w_router.T.astype(BF16))
    counts = counts[:, 0].astype(jnp.int32)
    padded = (counts + TM_MOE - 1) // TM_MOE * TM_MOE
    pad_end = jnp.cumsum(padded)
    pad_start = pad_end - padded
    eid = jnp.arange(N_EXPERTS, dtype=jnp.int32)[:, None, None]
    dest = jnp.sum(jnp.where(e_idx[None, :2] == eid, pad_start[:, None, None], 0), axis=0) + rank[:2]
    n_slots = 2 * T + N_EXPERTS * TM_MOE
    n_blk = n_slots // TM_MOE
    blk_e = jnp.minimum(jnp.searchsorted(pad_end, jnp.arange(n_blk, dtype=jnp.int32) * TM_MOE,
                                         side='right'), N_EXPERTS - 1).astype(jnp.int32)
    n_used = (pad_end[-1:] // TM_MOE).astype(jnp.int32)
    xs = _dispatch(h2_packed, dest, n_slots)
    ys = _experts(xs, blk_e, n_used, w_gate.astype(BF16), w_up.astype(BF16), w_down.astype(BF16))
    return _combine(x1, mod, probs[:2].T, final_g.reshape(1, -1), ys, dest)


def _final_norm_kernel(x_ref, g_ref, o_ref):
    o_ref[...] = _rms(x_ref[...], g_ref[...])


def _final_norm(x, g):
    T, D = x.shape
    tm = 1024
    return pl.pallas_call(
        _final_norm_kernel,
        grid=(T // tm,),
        in_specs=[pl.BlockSpec((tm, D), lambda i: (i, 0)), pl.BlockSpec((1, D), lambda i: (0, 0))],
        out_specs=pl.BlockSpec((tm, D), lambda i: (i, 0)),
        out_shape=jax.ShapeDtypeStruct((T, D), F32),
        compiler_params=_params("arbitrary"),
        name="final_norm",
    )(x, g.reshape(1, -1))


def _regroup_w_in(w):
    ab = jnp.pad(w[:, 4608:4616], ((0, 0), (0, LANES - 8)))
    return jnp.concatenate([w[:, :4608], w[:, 4616:], ab], axis=1).astype(BF16)


def _lambda_init(layer):
    return 0.8 - 0.6 * math.exp(-0.3 * layer)


def kernel(x, c, positions, norm1_g, norm2_g, w_mod, b_mod, w_in, gm_ln_g, gm_ln_b, gm_w_s, gm_b_s, da_lambda, da_subln_g, dn_conv_w, dn_a_log, dn_dt_bias, dn_norm_g, w_br_gm, w_br_da, w_br_dn, w_out, ffn_w_gate, ffn_w_up, ffn_w_down, moe_w_router, moe_w_gate, moe_w_up, moe_w_down, final_g):
    B, S, D = x.shape
    T = B * S
    xt = x.reshape(T, D)
    mod_all = _modulation(c, w_mod, b_mod).reshape(DEPTH, B, 6, D)
    cos_t, sin_t = _rope_tables(positions)
    for layer in range(DEPTH):
        mod = mod_all[layer]
        gm, qk, v, dn, gates, ab = _in_projection(xt, mod, norm1_g[layer].reshape(1, D),
                                                  _regroup_w_in(w_in[layer]), cos_t, sin_t, dn_conv_w[layer])
        y_gm = _gmlp(gm, gm_ln_g[layer], gm_ln_b[layer], gm_w_s[layer], gm_b_s[layer])
        y_da = _diff_attention(qk, v, da_lambda[layer], da_subln_g[layer], _lambda_init(layer), B)
        y_dn = _deltanet(dn, ab, dn_a_log[layer], dn_dt_bias[layer], dn_norm_g[layer], B)
        moe = layer % 2 == 1
        x1, h2, *packed = _merge(xt, mod, norm2_g[layer].reshape(1, D), y_gm, y_da, y_dn, gates,
                                 w_br_gm[layer].astype(BF16), w_br_da[layer].astype(BF16),
                                 w_br_dn[layer].astype(BF16), w_out[layer].astype(BF16), moe)
        if moe:
            xt = _moe_layer(x1, h2, packed[0], mod, moe_w_router[layer // 2], moe_w_gate[layer // 2],
                            moe_w_up[layer // 2], moe_w_down[layer // 2], final_g)
        else:
            xt = _dense_ffn(x1, h2, mod, ffn_w_gate[layer // 2].astype(BF16),
                            ffn_w_up[layer // 2].astype(BF16), ffn_w_down[layer // 2].astype(BF16))
    if DEPTH % 2 == 1:
        xt = _final_norm(xt, final_g)
    return xt.reshape(B, S, D)
```

```python
import functools
import math

import jax
import jax.numpy as jnp
import numpy as np
from jax import lax
from jax.experimental import pallas as pl
from jax.experimental.pallas import tpu as pltpu

D_MODEL = 1024
SEQ = 2048
DEPTH = 2
CHUNK = 64
EPS = 1e-6
GM_WIDTH = D_MODEL // 2
GM_GROUPS = 4
GM_BLOCK = 128
DA_HEADS = 4
DA_HEAD_DIM = 64
DA_V_DIM = 2 * DA_HEAD_DIM
ROPE_THETA = 10000.0
DN_HEADS = 4
DN_HEAD_DIM = 128
DN_WIDTH = DN_HEADS * DN_HEAD_DIM
DN_CONV = 4
N_EXPERTS = 8
FF_EXPERT = 7 * D_MODEL // 2

LANES = 128
VMEM_LIMIT = 56 * 1024 * 1024

C_GM, C_QK, C_V, C_DN, C_GATE, C_AB = 0, 1024, 2048, 2560, 4608, 7680
W_ALL = C_AB + LANES
PROJ_CHUNK = 512

TM_PROJ = 512
TM_GM = 512
TQ = 512
TK = 256
TM_MERGE = 512
TM_FFN = 512
TM_ROUTE = 512
TM_MOE = 512
TM_DISP = 256
DISP_SLOTS = 3
TM_COMB = 256
ATTN_LOOKAHEAD = 2
VT_ROWS = DA_V_DIM + 16
DN_BLOCK = 256

BF16 = jnp.bfloat16
F32 = jnp.float32


def _params(*sem):
    return pltpu.CompilerParams(dimension_semantics=sem, vmem_limit_bytes=VMEM_LIMIT)


def _dot(a, b):
    return jnp.dot(a, b, preferred_element_type=F32)


def _dot_nt(a, b):
    return lax.dot_general(a, b, (((1,), (1,)), ((), ())), preferred_element_type=F32)


def _dot_tn(a, b):
    return lax.dot_general(a, b, (((0,), (0,)), ((), ())), preferred_element_type=F32)


def _rms(x, g):
    return x * lax.rsqrt(jnp.mean(x * x, axis=-1, keepdims=True) + EPS) * g


def _silu(x):
    return x * jax.nn.sigmoid(x)


def _mod_kernel(c_ref, w_ref, b_ref, o_ref):
    c = c_ref[...]
    o_ref[0] = _dot(_silu(c).astype(BF16), w_ref[0].astype(BF16)) + b_ref[0]


def _modulation(c, w_mod, b_mod):
    B, D = c.shape
    L, _, N = w_mod.shape
    tn = 1536
    return pl.pallas_call(
        _mod_kernel,
        grid=(L, N // tn),
        in_specs=[pl.BlockSpec((B, D), lambda l, j: (0, 0)),
                  pl.BlockSpec((1, D, tn), lambda l, j: (l, 0, j)),
                  pl.BlockSpec((1, 1, tn), lambda l, j: (l, 0, j))],
        out_specs=pl.BlockSpec((1, B, tn), lambda l, j: (l, 0, j)),
        out_shape=jax.ShapeDtypeStruct((L, B, N), F32),
        compiler_params=_params("arbitrary", "arbitrary"),
        name="modulation",
    )(c, w_mod, b_mod.reshape(L, 1, N))


def _rope_tab_kernel(pos_ref, inv_ref, sgn_ref, cos_ref, sin_ref):
    ang = pos_ref[...].astype(F32) * inv_ref[...]
    cos_ref[...] = jnp.cos(ang)
    sin_ref[...] = jnp.sin(ang) * sgn_ref[...]


def _rope_tables(positions):
    T = positions.size
    inv_freq = ROPE_THETA ** (-jnp.arange(0, DA_HEAD_DIM, 2, dtype=F32) / DA_HEAD_DIM)
    inv = jnp.tile(inv_freq, LANES // (DA_HEAD_DIM // 2)).reshape(1, LANES)
    half = DA_HEAD_DIM // 2
    sgn = np.tile(np.concatenate([-np.ones(half), np.ones(half)]), LANES // DA_HEAD_DIM)
    sgn = jnp.asarray(sgn, F32).reshape(1, LANES)
    tm = 1024
    return pl.pallas_call(
        _rope_tab_kernel,
        grid=(T // tm,),
        in_specs=[pl.BlockSpec((tm, 1), lambda i: (i, 0)),
                  pl.BlockSpec((1, LANES), lambda i: (0, 0)),
                  pl.BlockSpec((1, LANES), lambda i: (0, 0))],
        out_specs=[pl.BlockSpec((tm, LANES), lambda i: (i, 0))] * 2,
        out_shape=[jax.ShapeDtypeStruct((T, LANES), F32)] * 2,
        compiler_params=_params("arbitrary"),
        name="rope_tables",
    )(positions.reshape(T, 1), inv, sgn)


def _rope_rows(x, cos, sin):
    lane = lax.broadcasted_iota(jnp.int32, x.shape, 1)
    first = (lane % DA_HEAD_DIM) < (DA_HEAD_DIM // 2)
    partner = jnp.where(first, pltpu.roll(x, LANES - DA_HEAD_DIM // 2, axis=1),
                        pltpu.roll(x, DA_HEAD_DIM // 2, axis=1))
    return x * cos + partner * sin


def _proj_kernel(x_ref, mod_ref, g_ref, w_ref, cos_ref, sin_ref, cw_ref,
                 gm_ref, qk_ref, v_ref, dn_ref, gate_ref, ab_ref, prev_ref):
    tm = x_ref.shape[0]

    @pl.when(pl.program_id(0) % (SEQ // tm) == 0)
    def _():
        prev_ref[...] = jnp.zeros_like(prev_ref)

    def conv_silu(y, idx):
        cw = cw_ref[:, idx * PROJ_CHUNK:(idx + 1) * PROJ_CHUNK]
        ycat = jnp.concatenate([prev_ref[idx], y], axis=0)
        out = y * cw[DN_CONV - 1:DN_CONV, :]
        for j in range(DN_CONV - 1):
            out = out + pltpu.roll(ycat, DN_CONV - 1 - j, axis=0)[8:, :] * cw[j:j + 1, :]
        prev_ref[idx] = y[tm - 8:, :]
        return _silu(out)

    def l2norm_heads(t, mult):
        parts = []
        for j in range(PROJ_CHUNK // DN_HEAD_DIM):
            seg = t[:, j * DN_HEAD_DIM:(j + 1) * DN_HEAD_DIM]
            parts.append(seg * (lax.rsqrt(jnp.sum(seg * seg, axis=-1, keepdims=True) + EPS) * mult))
        return jnp.concatenate(parts, axis=1)

    x = x_ref[...]
    shift = mod_ref[0, 0:1, :]
    scale = mod_ref[0, 1:2, :]
    h = (_rms(x, g_ref[...]) * (1.0 + scale) + shift).astype(BF16)
    cos = cos_ref[...]
    sin = sin_ref[...]
    q_scale = DA_HEAD_DIM ** -0.5 * math.log2(math.e)

    def rope_chunk(y, mult):
        parts = [_rope_rows(y[:, j * LANES:(j + 1) * LANES], cos, sin) * mult
                 for j in range(PROJ_CHUNK // LANES)]
        return jnp.concatenate(parts, axis=1)

    for c0 in range(0, C_AB, PROJ_CHUNK):
        y = _dot(h, w_ref[:, c0:c0 + PROJ_CHUNK])
        if c0 < C_QK:
            gm_ref[:, c0:c0 + PROJ_CHUNK] = y.astype(BF16)
        elif c0 < C_V:
            mult = q_scale if c0 == C_QK else 1.0
            qk_ref[:, c0 - C_QK:c0 - C_QK + PROJ_CHUNK] = rope_chunk(y, mult).astype(BF16)
        elif c0 < C_DN:
            v_ref[...] = y.astype(BF16)
        elif c0 < C_GATE:
            idx = (c0 - C_DN) // PROJ_CHUNK
            if idx < 3:
                y = conv_silu(y, idx)
            if idx < 2:
                y = l2norm_heads(y, DN_HEAD_DIM ** -0.5 if idx == 0 else 1.0)
            dn_ref[:, c0 - C_DN:c0 - C_DN + PROJ_CHUNK] = y.astype(BF16)
        else:
            gate_ref[:, c0 - C_GATE:c0 - C_GATE + PROJ_CHUNK] = y.astype(BF16)
    ab_ref[...] = _dot(h, w_ref[:, C_AB:W_ALL])


def _in_projection(x, mod, g, w_all, cos_t, sin_t, conv_w):
    T, D = x.shape
    tm = TM_PROJ
    per_b = SEQ // tm
    row = lambda i: (i, 0)
    widths = (C_QK - C_GM, C_V - C_QK, C_DN - C_V, C_GATE - C_DN, C_AB - C_GATE)
    return pl.pallas_call(
        _proj_kernel,
        grid=(T // tm,),
        in_specs=[pl.BlockSpec((tm, D), row),
                  pl.BlockSpec((1, 6, D), lambda i: (i // per_b, 0, 0)),
                  pl.BlockSpec((1, D), lambda i: (0, 0)),
                  pl.BlockSpec((D, W_ALL), lambda i: (0, 0), pipeline_mode=pl.Buffered(1)),
                  pl.BlockSpec((tm, LANES), row),
                  pl.BlockSpec((tm, LANES), row),
                  pl.BlockSpec((DN_CONV, 3 * DN_WIDTH), lambda i: (0, 0))],
        out_specs=[pl.BlockSpec((tm, w), row) for w in widths] + [pl.BlockSpec((tm, LANES), row)],
        out_shape=[jax.ShapeDtypeStruct((T, w), BF16) for w in widths]
                  + [jax.ShapeDtypeStruct((T, LANES), F32)],
        scratch_shapes=[pltpu.VMEM((3, 8, PROJ_CHUNK), F32)],
        compiler_params=_params("arbitrary"),
        name="in_projection",
    )(x, mod, g, w_all, cos_t, sin_t, conv_w)


def _gelu(x):
    return 0.5 * x * (1.0 + lax.erf(x * np.float32(math.sqrt(0.5))))


def _gmlp_kernel(uv_ref, lng_ref, lnb_ref, ws_ref, bs_ref, o_ref):
    u = _gelu(uv_ref[:, :GM_WIDTH].astype(F32))
    v = _gelu(uv_ref[:, GM_WIDTH:].astype(F32))
    mu = jnp.mean(v, axis=-1, keepdims=True)
    vc = v - mu
    var = jnp.mean(vc * vc, axis=-1, keepdims=True)
    v = (vc * lax.rsqrt(var + EPS) * lng_ref[...] + lnb_ref[...]).astype(BF16)
    ri = lax.broadcasted_iota(jnp.int32, (GM_BLOCK, GM_BLOCK), 0) // CHUNK
    ci = lax.broadcasted_iota(jnp.int32, (GM_BLOCK, GM_BLOCK), 1) // CHUNK
    allowed = ci <= ri
    gc = GM_WIDTH // GM_GROUPS
    for g in range(GM_GROUPS):
        w = jnp.where(allowed, ws_ref[g], 0.0).astype(BF16)
        bias = bs_ref[g]
        for r in range(uv_ref.shape[0] // GM_BLOCK):
            rows = slice(r * GM_BLOCK, (r + 1) * GM_BLOCK)
            cols = slice(g * gc, (g + 1) * gc)
            sv = _dot(w, v[rows, cols]) + bias
            o_ref[rows, cols] = (u[rows, cols] * sv).astype(BF16)


def _gmlp(uv, ln_g, ln_b, w_s, b_s):
    T = uv.shape[0]
    tm = TM_GM
    return pl.pallas_call(
        _gmlp_kernel,
        grid=(T // tm,),
        in_specs=[pl.BlockSpec((tm, 2 * GM_WIDTH), lambda i: (i, 0)),
                  pl.BlockSpec((1, GM_WIDTH), lambda i: (0, 0)),
                  pl.BlockSpec((1, GM_WIDTH), lambda i: (0, 0)),
                  pl.BlockSpec((GM_GROUPS, GM_BLOCK, GM_BLOCK), lambda i: (0, 0, 0)),
                  pl.BlockSpec((GM_GROUPS, GM_BLOCK, 1), lambda i: (0, 0, 0))],
        out_specs=pl.BlockSpec((tm, GM_WIDTH), lambda i: (i, 0)),
        out_shape=jax.ShapeDtypeStruct((T, GM_WIDTH), BF16),
        compiler_params=_params("arbitrary"),
        name="gmlp",
    )(uv, ln_g.reshape(1, -1), ln_b.reshape(1, -1), w_s, b_s.reshape(GM_GROUPS, GM_BLOCK, 1))


def _attn_kernel(lam_ref, g_ref, q_ref, k_ref, v_ref, o_ref, qz_ref, vt_ref, *stats, lambda_init):
    nc = 2 * DA_HEADS
    m_ref, acc_ref, s_ref = (stats[j * nc:(j + 1) * nc] for j in range(3))
    i = pl.program_id(1)
    lp = lam_ref[...]
    lam = (jnp.exp(jnp.sum(lp[0:1] * lp[1:2], axis=-1, keepdims=True))
           - jnp.exp(jnp.sum(lp[2:3] * lp[3:4], axis=-1, keepdims=True)) + lambda_init)
    @pl.when(i == 0)
    def _():
        for kb in range(SEQ // TK):
            for h in range(DA_HEADS):
                vt_ref[h * VT_ROWS:h * VT_ROWS + DA_V_DIM, kb * TK:(kb + 1) * TK] = (
                    v_ref[kb * TK:(kb + 1) * TK, h * LANES:(h + 1) * LANES].T)
        for h in range(DA_HEADS):
            vt_ref[h * VT_ROWS + DA_V_DIM:(h + 1) * VT_ROWS, :] = jnp.ones((VT_ROWS - DA_V_DIM, SEQ), BF16)

    row = lax.broadcasted_iota(jnp.int32, (LANES, TQ), 0)
    for h in range(DA_HEADS):
        qt = q_ref[:, h * LANES:(h + 1) * LANES].T
        zero = jnp.zeros_like(qt)
        qz_ref[2 * h] = jnp.where(row < DA_HEAD_DIM, qt, zero)
        qz_ref[2 * h + 1] = jnp.where(row >= DA_HEAD_DIM, qt, zero)
    for c in range(nc):
        m_ref[c][...] = jnp.full_like(m_ref[c], -jnp.inf)
        acc_ref[c][...] = jnp.zeros_like(acc_ref[c])

    def key_rows(kb):
        return pl.ds(pl.multiple_of(kb * TK, TK), TK)

    def scores(kb, c, masked):
        h = c // 2
        s = _dot(k_ref[key_rows(kb), h * LANES:(h + 1) * LANES], qz_ref[c])
        if masked:
            ki = lax.broadcasted_iota(jnp.int32, s.shape, 0) // CHUNK + kb * (TK // CHUNK)
            qi = lax.broadcasted_iota(jnp.int32, s.shape, 1) // CHUNK + i * (TQ // CHUNK)
            s = jnp.where(ki <= qi, s, -jnp.inf)
        return s

    def accumulate(kb, c, s):
        h = c // 2
        m_old = m_ref[c][...]
        m_new = jnp.maximum(m_old, jnp.max(s, axis=0, keepdims=True))
        alpha = jnp.exp2(m_old - m_new)
        e = jnp.exp2((s - m_new).astype(BF16))
        vt = vt_ref[h * VT_ROWS:(h + 1) * VT_ROWS, key_rows(kb)]
        acc_ref[c][...] = alpha * acc_ref[c][...] + _dot(vt, e)
        m_ref[c][...] = m_new

    def fold_and_prefetch(kb, masked_next):
        fresh = {}
        for c in range(nc + ATTN_LOOKAHEAD):
            if c < nc:
                fresh[c] = scores(kb + 1, c, masked_next)
            if c >= ATTN_LOOKAHEAD:
                cc = c - ATTN_LOOKAHEAD
                accumulate(kb, cc, s_ref[cc][...])
                s_ref[cc][...] = fresh.pop(cc)

    first_masked = i * (TQ // TK)

    @pl.when(i == 0)
    def _():
        for c in range(nc):
            s_ref[c][...] = scores(0, c, True)

    @pl.when(i > 0)
    def _():
        for c in range(nc):
            s_ref[c][...] = scores(0, c, False)

        def body(kb, carry):
            fold_and_prefetch(kb, False)
            return carry

        lax.fori_loop(0, first_masked - 1, body, 0)
        fold_and_prefetch(first_masked - 1, True)

    for r in range(TQ // TK - 1):
        fold_and_prefetch(first_masked + r, True)
    for c in range(nc):
        accumulate(first_masked + TQ // TK - 1, c, s_ref[c][...])
    for h in range(DA_HEADS):
        num0, den0 = acc_ref[2 * h][:DA_V_DIM, :], acc_ref[2 * h][DA_V_DIM:DA_V_DIM + 1, :]
        num1, den1 = acc_ref[2 * h + 1][:DA_V_DIM, :], acc_ref[2 * h + 1][DA_V_DIM:DA_V_DIM + 1, :]
        o = num0 / den0 - lam * (num1 / den1)
        ms = jnp.mean(o * o, axis=0, keepdims=True)
        o = o * lax.rsqrt(ms + EPS) * (g_ref[...] * (1.0 - lambda_init))
        o_ref[:, h * LANES:(h + 1) * LANES] = o.T.astype(BF16)


def _diff_attention(qk, v, lam_p, subln_g, lambda_init, batch):
    T = qk.shape[0]
    nq = SEQ // TQ
    W = DA_HEADS * DA_V_DIM
    kernel = functools.partial(_attn_kernel, lambda_init=lambda_init)
    return pl.pallas_call(
        kernel,
        grid=(batch, nq),
        in_specs=[pl.BlockSpec((4, DA_HEAD_DIM), lambda b, i: (0, 0)),
                  pl.BlockSpec((DA_V_DIM, 1), lambda b, i: (0, 0)),
                  pl.BlockSpec((TQ, W), lambda b, i: (b * nq + i, 0)),
                  pl.BlockSpec((SEQ, W), lambda b, i: (b, 1)),
                  pl.BlockSpec((SEQ, W), lambda b, i: (b, 0))],
        out_specs=pl.BlockSpec((TQ, W), lambda b, i: (b * nq + i, 0)),
        out_shape=jax.ShapeDtypeStruct((T, W), BF16),
        scratch_shapes=[pltpu.VMEM((2 * DA_HEADS, LANES, TQ), BF16),
                        pltpu.VMEM((DA_HEADS * VT_ROWS, SEQ), BF16)]
                       + [pltpu.VMEM((1, TQ), F32)] * (2 * DA_HEADS)
                       + [pltpu.VMEM((VT_ROWS, TQ), F32)] * (2 * DA_HEADS)
                       + [pltpu.VMEM((TK, TQ), F32)] * (2 * DA_HEADS),
        compiler_params=_params("arbitrary", "arbitrary"),
        name="diff_attention",
    )(lam_p, subln_g.reshape(-1, 1), qk, qk, v)


def _split3(x):
    hi = x.astype(BF16)
    r = x - hi.astype(F32)
    mid = r.astype(BF16)
    lo = (r - mid.astype(F32)).astype(BF16)
    return hi, mid, lo


def _dn_kernel(dn_ref, a_ref, at_ref, alog_ref, dtb_ref, alogt_ref, dtbt_ref, ng_ref,
               o_ref, *state_ref):
    c = pl.program_id(1)
    R = DN_BLOCK

    @pl.when(c == 0)
    def _():
        for ref in state_ref:
            ref[...] = jnp.zeros_like(ref)

    W3 = 3 * DN_WIDTH
    ab = a_ref[...]
    g_col = -jnp.exp(alog_ref[...]) * jax.nn.softplus(ab + dtb_ref[...])
    beta_all = jax.nn.sigmoid(ab)
    abt = at_ref[0]
    g_row = -jnp.exp(alogt_ref[...]) * jax.nn.softplus(abt + dtbt_ref[...])
    ri = lax.broadcasted_iota(jnp.int32, (R, R), 0)
    ci = lax.broadcasted_iota(jnp.int32, (R, R), 1)
    tril = ri >= ci
    strict = ri > ci
    ones_tril = jnp.where(tril, 1.0, 0.0).astype(BF16)
    ones_triu = jnp.where(ci >= ri, 1.0, 0.0).astype(BF16)
    gc_col = sum(_dot(ones_tril, part) for part in _split3(g_col))
    gc_row = sum(_dot(part, ones_triu) for part in _split3(g_row))
    eye = jnp.where(ri == ci, 1.0, 0.0)
    same = {n: (ri // n) == (ci // n) for n in (16, 32, 64, 128)}

    H = range(DN_HEADS)
    hsl = [slice(h * DN_HEAD_DIM, (h + 1) * DN_HEAD_DIM) for h in H]
    q16 = [dn_ref[:, hsl[h]] for h in H]
    k16 = [dn_ref[:, DN_WIDTH + h * DN_HEAD_DIM:DN_WIDTH + (h + 1) * DN_HEAD_DIM] for h in H]
    q = [t.astype(F32) for t in q16]
    k = [t.astype(F32) for t in k16]
    v = [dn_ref[:, 2 * DN_WIDTH + h * DN_HEAD_DIM:2 * DN_WIDTH + (h + 1) * DN_HEAD_DIM].astype(F32) for h in H]
    gc =[gc_col[:, h:h + 1] for h in H]
    beta = [beta_all[:, DN_HEADS + h:DN_HEADS + h + 1] for h in H]
    g_last = [gc_col[R - 1:R, h:h + 1] for h in H]
    decay = [jnp.exp(jnp.where(tril, gc[h] - gc_row[h:h + 1, :], -jnp.inf)) for h in H]
    eg = [jnp.exp(t) for t in gc]
    kb = [k[h] * beta[h] for h in H]
    s1 = [_dot_nt(jnp.concatenate([kb[h].astype(BF16), q16[h]], axis=0), k16[h]) for h in H]
    a_mat = [jnp.where(strict, s1[h][:R] * decay[h], 0.0) for h in H]
    qk = [(s1[h][R:] * decay[h]).astype(BF16) for h in H]
    p = [jnp.where(same[16], -t, 0.0) for t in a_mat]
    t_inv = [eye + t for t in p]
    for _ in range(3):
        p16 = [t.astype(BF16) for t in p]
        p = [_dot(t, t) for t in p16]
        t_inv = [t_inv[h] + _dot(t_inv[h].astype(BF16), p[h].astype(BF16)) for h in H]
    a16 = [t.astype(BF16) for t in a_mat]
    t16 = [t.astype(BF16) for t in t_inv]
    for n in (16, 32, 64, 128):
        off = jnp.logical_not(same[n])
        if 2 * n < R:
            off = jnp.logical_and(same[2 * n], off)
        off16 = jnp.where(off, 1.0, 0.0).astype(BF16)
        tl = [_dot(t16[h], a16[h] * off16).astype(BF16) for h in H]
        t16 = [t16[h] - _dot(tl[h], t16[h]).astype(BF16) * off16 for h in H]
    sol = [_dot(t16[h], jnp.concatenate([v[h] * beta[h], kb[h] * eg[h]], axis=1).astype(BF16))
           for h in H]
    state = [state_ref[h][...] for h in H]
    m1 = [_dot(jnp.concatenate([sol[h][:, DN_HEAD_DIM:], q[h] * eg[h]], axis=0).astype(BF16),
               state[h].astype(BF16)) for h in H]
    vn16 = [(sol[h][:, :DN_HEAD_DIM] - m1[h][:R]).astype(BF16) for h in H]
    o = [m1[h][R:] + _dot(qk[h], vn16[h]) for h in H]
    kd = [(k[h] * jnp.exp(g_last[h] - gc[h])).astype(BF16) for h in H]
    for h in H:
        state_ref[h][...] = state[h] * jnp.exp(g_last[h]) + _dot_tn(kd[h], vn16[h])
    for h in H:
        z = dn_ref[:, W3 + h * DN_HEAD_DIM:W3 + (h + 1) * DN_HEAD_DIM].astype(F32)
        o_ref[:, hsl[h]] = (_rms(o[h], ng_ref[...]) * _silu(z)).astype(BF16)


def _deltanet(dn, ab, a_log, dt_bias, norm_g, batch):
    T = dn.shape[0]
    R = DN_BLOCK
    n = SEQ // R
    pad = lambda t: jnp.zeros((1, LANES), F32).at[0, :DN_HEADS].set(t)
    col8 = lambda t: jnp.zeros((8, 1), F32).at[:DN_HEADS, 0].set(t)
    ab_t = ab[:, :8].reshape(T // R, R, 8).transpose(0, 2, 1)
    return pl.pallas_call(
        _dn_kernel,
        grid=(batch, n),
        in_specs=[pl.BlockSpec((R, 4 * DN_WIDTH), lambda b, c: (b * n + c, 0)),
                  pl.BlockSpec((R, LANES), lambda b, c: (b * n + c, 0)),
                  pl.BlockSpec((1, 8, R), lambda b, c: (b * n + c, 0, 0)),
                  pl.BlockSpec((1, LANES), lambda b, c: (0, 0)),
                  pl.BlockSpec((1, LANES), lambda b, c: (0, 0)),
                  pl.BlockSpec((8, 1), lambda b, c: (0, 0)),
                  pl.BlockSpec((8, 1), lambda b, c: (0, 0)),
                  pl.BlockSpec((1, DN_HEAD_DIM), lambda b, c: (0, 0))],
        out_specs=pl.BlockSpec((R, DN_WIDTH), lambda b, c: (b * n + c, 0)),
        out_shape=jax.ShapeDtypeStruct((T, DN_WIDTH), BF16),
        scratch_shapes=[pltpu.VMEM((DN_HEAD_DIM, DN_HEAD_DIM), F32)] * DN_HEADS,
        compiler_params=_params("arbitrary", "arbitrary"),
        name="deltanet",
    )(dn, ab, ab_t, pad(a_log), pad(dt_bias), col8(a_log), col8(dt_bias), norm_g.reshape(1, -1))


def _pack_bf16_pair(lo, hi):
    lo_bits = lax.bitcast_convert_type(lo.astype(BF16).astype(F32), jnp.uint32)
    hi_bits = lax.bitcast_convert_type(hi.astype(BF16).astype(F32), jnp.uint32)
    return (lo_bits >> 16) | (hi_bits & jnp.uint32(0xFFFF0000))


def _unpack_bf16_pair(words):
    lo = lax.bitcast_convert_type(words << 16, F32)
    hi = lax.bitcast_convert_type(words & jnp.uint32(0xFFFF0000), F32)
    return lo.astype(BF16), hi.astype(BF16)


def _merge_kernel(x_ref, mod_ref, g2_ref, ygm_ref, yda_ref, ydn_ref, gate_ref,
                  wgm_ref, wda_ref, wdn_ref, wout_ref, x1_ref, h2_ref, *maybe_packed_ref):
    D = D_MODEL
    merged = (jax.nn.sigmoid(gate_ref[:, 0:D].astype(F32)) * _dot(ygm_ref[...], wgm_ref[...])
              + jax.nn.sigmoid(gate_ref[:, D:2 * D].astype(F32)) * _dot(yda_ref[...], wda_ref[...])
              + jax.nn.sigmoid(gate_ref[:, 2 * D:3 * D].astype(F32)) * _dot(ydn_ref[...], wdn_ref[...]))
    y = _dot(merged.astype(BF16), wout_ref[...])
    x1 = x_ref[...] + mod_ref[0, 2:3, :] * y
    x1_ref[...] = x1
    h2 = _rms(x1, g2_ref[...]) * (1.0 + mod_ref[0, 4:5, :]) + mod_ref[0, 3:4, :]
    h2_ref[...] = h2.astype(BF16)
    for packed_ref in maybe_packed_ref:
        packed_ref[...] = _pack_bf16_pair(h2[:, :D // 2], h2[:, D // 2:])


def _merge(x, mod, g2, y_gm, y_da, y_dn, gates, w_gm, w_da, w_dn, w_out, with_packed):
    T, D = x.shape
    tm = TM_MERGE
    per_b = SEQ // tm
    row = lambda i: (i, 0)
    const = lambda i: (0, 0)
    out_specs = [pl.BlockSpec((tm, D), row), pl.BlockSpec((tm, D), row)]
    out_shape = [jax.ShapeDtypeStruct((T, D), F32), jax.ShapeDtypeStruct((T, D), BF16)]
    if with_packed:
        out_specs.append(pl.BlockSpec((tm, D // 2), row))
        out_shape.append(jax.ShapeDtypeStruct((T, D // 2), jnp.uint32))
    return pl.pallas_call(
        _merge_kernel,
        grid=(T // tm,),
        in_specs=[pl.BlockSpec((tm, D), row),
                  pl.BlockSpec((1, 6, D), lambda i: (i // per_b, 0, 0)),
                  pl.BlockSpec((1, D), const),
                  pl.BlockSpec((tm, GM_WIDTH), row),
                  pl.BlockSpec((tm, GM_WIDTH), row),
                  pl.BlockSpec((tm, DN_WIDTH), row),
                  pl.BlockSpec((tm, 3 * D), row),
                  pl.BlockSpec((GM_WIDTH, D), const),
                  pl.BlockSpec((GM_WIDTH, D), const),
                  pl.BlockSpec((DN_WIDTH, D), const),
                  pl.BlockSpec((D, D), const)],
        out_specs=out_specs,
        out_shape=out_shape,
        compiler_params=_params("arbitrary"),
        name="merge",
    )(x, mod, g2, y_gm, y_da, y_dn, gates, w_gm, w_da, w_dn, w_out)


def _ffn_kernel(x_ref, h_ref, mod_ref, wg_ref, wu_ref, wd_ref, o_ref):
    h = h_ref[...]
    a = _dot(h, wg_ref[...])
    b = _dot(h, wu_ref[...])
    f = _dot((_silu(a) * b).astype(BF16), wd_ref[...])
    o_ref[...] = x_ref[...] + mod_ref[0, 5:6, :] * f


def _dense_ffn(x1, h2, mod, w_gate, w_up, w_down):
    T, D = x1.shape
    F = w_gate.shape[1]
    tm = TM_FFN
    per_b = SEQ // tm
    row = lambda i: (i, 0)
    const = lambda i: (0, 0)
    return pl.pallas_call(
        _ffn_kernel,
        grid=(T // tm,),
        in_specs=[pl.BlockSpec((tm, D), row),
                  pl.BlockSpec((tm, D), row),
                  pl.BlockSpec((1, 6, D), lambda i: (i // per_b, 0, 0)),
                  pl.BlockSpec((D, F), const, pipeline_mode=pl.Buffered(1)),
                  pl.BlockSpec((D, F), const, pipeline_mode=pl.Buffered(1)),
                  pl.BlockSpec((F, D), const, pipeline_mode=pl.Buffered(1))],
        out_specs=pl.BlockSpec((tm, D), row),
        out_shape=jax.ShapeDtypeStruct((T, D), F32),
        compiler_params=_params("arbitrary"),
        name="dense_ffn",
    )(x1, h2, mod, w_gate, w_up, w_down)


def _router_kernel(h_ref, wr_ref, e_ref, p_ref, r_ref, cnt_ref, base_ref):
    i = pl.program_id(0)
    tm = h_ref.shape[0]

    @pl.when(i == 0)
    def _():
        base_ref[...] = jnp.zeros_like(base_ref)

    logits = _dot_nt(wr_ref[...], h_ref[...])
    row = lax.broadcasted_iota(jnp.int32, logits.shape, 0)
    m1 = jnp.max(logits, axis=0, keepdims=True)
    i1 = jnp.min(jnp.where(logits == m1, row, N_EXPERTS), axis=0, keepdims=True)
    rest = jnp.where(row == i1, -jnp.inf, logits)
    m2 = jnp.max(rest, axis=0, keepdims=True)
    i2 = jnp.min(jnp.where(rest == m2, row, N_EXPERTS), axis=0, keepdims=True)
    e2 = jnp.exp(m2 - m1)
    w1 = 1.0 / (1.0 + e2)
    w2 = e2 / (1.0 + e2)
    oh1 = jnp.where(row == i1, 1.0, 0.0)
    oh2 = jnp.where(row == i2, 1.0, 0.0)
    both = oh1 + oh2
    ti = lax.broadcasted_iota(jnp.int32, (tm, tm), 0)
    tj = lax.broadcasted_iota(jnp.int32, (tm, tm), 1)
    before = jnp.where(ti < tj, 1.0, 0.0).astype(BF16)
    pos = base_ref[...] + _dot(both.astype(BF16), before)
    r1 = jnp.sum(oh1 * pos, axis=0, keepdims=True)
    r2 = jnp.sum(oh2 * pos, axis=0, keepdims=True)
    base_ref[...] = base_ref[...] + jnp.sum(both, axis=1, keepdims=True)
    zi = jnp.zeros((N_EXPERTS - 2, tm), jnp.int32)
    zf = jnp.zeros((N_EXPERTS - 2, tm), F32)
    e_ref[...] = jnp.concatenate([i1, i2, zi], axis=0)
    p_ref[...] = jnp.concatenate([w1, w2, zf], axis=0)
    r_ref[...] = jnp.concatenate([r1.astype(jnp.int32), r2.astype(jnp.int32), zi], axis=0)
    cnt_ref[...] = jnp.broadcast_to(base_ref[...], cnt_ref.shape)


def _router(h2, w_router_t):
    T, D = h2.shape
    tm = TM_ROUTE
    col = lambda i: (0, i)
    return pl.pallas_call(
        _router_kernel,
        grid=(T // tm,),
        in_specs=[pl.BlockSpec((tm, D), lambda i: (i, 0)),
                  pl.BlockSpec((N_EXPERTS, D), lambda i: (0, 0))],
        out_specs=[pl.BlockSpec((N_EXPERTS, tm), col)] * 3 + [pl.BlockSpec((N_EXPERTS, LANES), lambda i: (0, 0))],
        out_shape=[jax.ShapeDtypeStruct((N_EXPERTS, T), jnp.int32),
                   jax.ShapeDtypeStruct((N_EXPERTS, T), F32),
                   jax.ShapeDtypeStruct((N_EXPERTS, T), jnp.int32),
                   jax.ShapeDtypeStruct((N_EXPERTS, LANES), F32)],
        scratch_shapes=[pltpu.VMEM((N_EXPERTS, 1), F32)],
        compiler_params=_params("arbitrary"),
        name="moe_router",
    )(h2, w_router_t)


def _dispatch_kernel(dest_ref, h_ref, xs_in_ref, xs_ref, buf_ref, sem_in, sem_out):
    del xs_in_ref
    i = pl.program_id(0)
    n = pl.num_programs(0)
    tm = buf_ref.shape[1]

    def load(t):
        s = t % DISP_SLOTS
        return pltpu.make_async_copy(h_ref.at[pl.ds(pl.multiple_of(t * tm, tm), tm)], buf_ref.at[s],
                                     sem_in.at[s])

    def wait_rows(t):
        s = t % DISP_SLOTS
        for _ in range(2):
            pltpu.make_async_copy(buf_ref.at[s], xs_ref.at[pl.ds(0, tm)], sem_out.at[s]).wait()

    @pl.when(i == 0)
    def _():
        load(0).start()
        pl.when(n > 1)(lambda: load(1).start())

    load(i).wait()
    slot = i % DISP_SLOTS

    def issue(r, _):
        for k in range(2):
            pltpu.make_async_copy(buf_ref.at[slot, pl.ds(r, 1)],
                                  xs_ref.at[pl.ds(dest_ref[0, 0, k * tm + r], 1)], sem_out.at[slot]).start()
        return 0

    lax.fori_loop(0, tm, issue, 0, unroll=8)
    pl.when(i > 0)(lambda: wait_rows(i - 1))
    pl.when(i + 2 < n)(lambda: load(i + 2).start())
    pl.when(i == n - 1)(lambda: wait_rows(i))


def _dispatch(h2, dest, n_slots):
    T, D = h2.shape
    tm = TM_DISP
    dest_t = dest.reshape(2, T // tm, tm).transpose(1, 0, 2).reshape(T // tm, 1, 2 * tm)
    xs0 = jnp.zeros((n_slots, D), h2.dtype)
    return pl.pallas_call(
        _dispatch_kernel,
        grid=(T // tm,),
        in_specs=[pl.BlockSpec((1, 1, 2 * tm), lambda i: (i, 0, 0), memory_space=pltpu.SMEM),
                  pl.BlockSpec(memory_space=pl.ANY),
                  pl.BlockSpec(memory_space=pl.ANY)],
        out_specs=pl.BlockSpec(memory_space=pl.ANY),
        out_shape=jax.ShapeDtypeStruct((n_slots, D), h2.dtype),
        scratch_shapes=[pltpu.VMEM((DISP_SLOTS, tm, D), h2.dtype),
                        pltpu.SemaphoreType.DMA((DISP_SLOTS,)), pltpu.SemaphoreType.DMA((DISP_SLOTS,))],
        input_output_aliases={2: 0},
        compiler_params=_params("arbitrary"),
        name="moe_dispatch",
    )(dest_t, h2, xs0)


def _expert_kernel(be_ref, nu_ref, x_ref, wg_ref, wu_ref, wd_ref, o_ref):
    del be_ref
    used = pl.program_id(0) < nu_ref[0]
    half = D_MODEL // 2

    @pl.when(used)
    def _():
        lo, hi = _unpack_bf16_pair(x_ref[...])
        a = _dot(lo, wg_ref[0, :half, :]) + _dot(hi, wg_ref[0, half:, :])
        b = _dot(lo, wu_ref[0, :half, :]) + _dot(hi, wu_ref[0, half:, :])
        o_ref[...] = _dot((_silu(a) * b).astype(BF16), wd_ref[0])

    @pl.when(jnp.logical_not(used))
    def _():
        o_ref[...] = jnp.zeros_like(o_ref)


def _experts(xs, blk_e, n_used, w_gate, w_up, w_down):
    P = xs.shape[0]
    _, D, F = w_gate.shape
    tm = TM_MOE
    grid_spec = pltpu.PrefetchScalarGridSpec(
        num_scalar_prefetch=2,
        grid=(P // tm,),
        in_specs=[pl.BlockSpec((tm, D // 2), lambda i, be, nu: (i, 0)),
                  pl.BlockSpec((1, D, F), lambda i, be, nu: (be[i], 0, 0), pipeline_mode=pl.Buffered(1)),
                  pl.BlockSpec((1, D, F), lambda i, be, nu: (be[i], 0, 0), pipeline_mode=pl.Buffered(1)),
                  pl.BlockSpec((1, F, D), lambda i, be, nu: (be[i], 0, 0), pipeline_mode=pl.Buffered(1))],
        out_specs=pl.BlockSpec((tm, D), lambda i, be, nu: (i, 0)))
    return pl.pallas_call(
        _expert_kernel,
        grid_spec=grid_spec,
        out_shape=jax.ShapeDtypeStruct((P, D), F32),
        compiler_params=_params("arbitrary"),
        name="moe_experts",
    )(blk_e, n_used, xs, w_gate, w_up, w_down)


def _combine_kernel(dest_ref, dest_next_ref, x_ref, mod_ref, p_ref, g_ref, ys_ref, o_ref, ybuf_ref, sems):
    i = pl.program_id(0)
    tm = x_ref.shape[0]
    slot = i % 2

    def gather(d_ref, s):
        def issue(r, _):
            for k in range(2):
                pltpu.make_async_copy(ys_ref.at[pl.ds(d_ref[0, 0, k * tm + r], 1)],
                                      ybuf_ref.at[s, k, pl.ds(r, 1)], sems.at[s]).start()
            return 0

        lax.fori_loop(0, tm, issue, 0, unroll=8)

    pl.when(i == 0)(lambda: gather(dest_ref, 0))
    pl.when(i + 1 < pl.num_programs(0))(lambda: gather(dest_next_ref, 1 - slot))
    for k in range(2):
        pltpu.make_async_copy(ys_ref.at[pl.ds(0, tm)], ybuf_ref.at[slot, k], sems.at[slot]).wait()
    p = p_ref[...]
    f = ybuf_ref[slot, 0] * p[:, 0:1] + ybuf_ref[slot, 1] * p[:, 1:2]
    x2 = x_ref[...] + mod_ref[0, 5:6, :] * f
    o_ref[...] = _rms(x2, g_ref[...])


def _combine(x1, mod, probs, final_g, ys, dest):
    T, D = x1.shape
    tm = TM_COMB
    per_b = SEQ // tm
    n = T // tm
    dest_t = dest.reshape(2, n, tm).transpose(1, 0, 2).reshape(n, 1, 2 * tm)
    return pl.pallas_call(
        _combine_kernel,
        grid=(n,),
        in_specs=[pl.BlockSpec((1, 1, 2 * tm), lambda i: (i, 0, 0), memory_space=pltpu.SMEM),
                  pl.BlockSpec((1, 1, 2 * tm), lambda i: (jnp.minimum(i + 1, n - 1), 0, 0),
                               memory_space=pltpu.SMEM),
                  pl.BlockSpec((tm, D), lambda i: (i, 0)),
                  pl.BlockSpec((1, 6, D), lambda i: (i // per_b, 0, 0)),
                  pl.BlockSpec((tm, 2), lambda i: (i, 0)),
                  pl.BlockSpec((1, D), lambda i: (0, 0)),
                  pl.BlockSpec(memory_space=pl.ANY)],
        out_specs=pl.BlockSpec((tm, D), lambda i: (i, 0)),
        out_shape=jax.ShapeDtypeStruct((T, D), F32),
        scratch_shapes=[pltpu.VMEM((2, 2, tm, D), F32), pltpu.SemaphoreType.DMA((2,))],
        compiler_params=_params("arbitrary"),
        name="moe_combine",
    )(dest_t, dest_t, x1, mod, probs, final_g, ys)


def _moe_layer(x1, h2, h2_packed, mod, w_router, w_gate, w_up, w_down, final_g):
    T, D = x1.shape
    e_idx, probs, rank, counts = _router(h2, w_router.T.astype(BF16))
    counts = counts[:, 0].astype(jnp.int32)
    padded = (counts + TM_MOE - 1) // TM_MOE * TM_MOE
    pad_end = jnp.cumsum(padded)
    pad_start = pad_end - padded
    eid = jnp.arange(N_EXPERTS, dtype=jnp.int32)[:, None, None]
    dest = jnp.sum(jnp.where(e_idx[None, :2] == eid, pad_start[:, None, None], 0), axis=0) + rank[:2]
    n_slots = 2 * T + N_EXPERTS * TM_MOE
    n_blk = n_slots // TM_MOE
    blk_e = jnp.minimum(jnp.searchsorted(pad_end, jnp.arange(n_blk, dtype=jnp.int32) * TM_MOE,
                                         side='right'), N_EXPERTS - 1).astype(jnp.int32)
    n_used = (pad_end[-1:] // TM_MOE).astype(jnp.int32)
    xs = _dispatch(h2_packed, dest, n_slots)
    ys = _experts(xs, blk_e, n_used, w_gate.astype(BF16), w_up.astype(BF16), w_down.astype(BF16))
    return _combine(x1, mod, probs[:2].T, final_g.reshape(1, -1), ys, dest)


def _final_norm_kernel(x_ref, g_ref, o_ref):
    o_ref[...] = _rms(x_ref[...], g_ref[...])


def _final_norm(x, g):
    T, D = x.shape
    tm = 1024
    return pl.pallas_call(
        _final_norm_kernel,
        grid=(T // tm,),
        in_specs=[pl.BlockSpec((tm, D), lambda i: (i, 0)), pl.BlockSpec((1, D), lambda i: (0, 0))],
        out_specs=pl.BlockSpec((tm, D), lambda i: (i, 0)),
        out_shape=jax.ShapeDtypeStruct((T, D), F32),
        compiler_params=_params("arbitrary"),
        name="final_norm",
    )(x, g.reshape(1, -1))


def _regroup_w_in(w):
    ab = jnp.pad(w[:, 4608:4616], ((0, 0), (0, LANES - 8)))
    return jnp.concatenate([w[:, :4608], w[:, 4616:], ab], axis=1).astype(BF16)


def _lambda_init(layer):
    return 0.8 - 0.6 * math.exp(-0.3 * layer)


def kernel(x, c, positions, norm1_g, norm2_g, w_mod, b_mod, w_in, gm_ln_g, gm_ln_b, gm_w_s, gm_b_s, da_lambda, da_subln_g, dn_conv_w, dn_a_log, dn_dt_bias, dn_norm_g, w_br_gm, w_br_da, w_br_dn, w_out, ffn_w_gate, ffn_w_up, ffn_w_down, moe_w_router, moe_w_gate, moe_w_up, moe_w_down, final_g):
    B, S, D = x.shape
    T = B * S
    xt = x.reshape(T, D)
    mod_all = _modulation(c, w_mod, b_mod).reshape(DEPTH, B, 6, D)
    cos_t, sin_t = _rope_tables(positions)
    for layer in range(DEPTH):
        mod = mod_all[layer]
        gm, qk, v, dn, gates, ab = _in_projection(xt, mod, norm1_g[layer].reshape(1, D),
                                                  _regroup_w_in(w_in[layer]), cos_t, sin_t, dn_conv_w[layer])
        y_gm = _gmlp(gm, gm_ln_g[layer], gm_ln_b[layer], gm_w_s[layer], gm_b_s[layer])
        y_da = _diff_attention(qk, v, da_lambda[layer], da_subln_g[layer], _lambda_init(layer), B)
        y_dn = _deltanet(dn, ab, dn_a_log[layer], dn_dt_bias[layer], dn_norm_g[layer], B)
        moe = layer % 2 == 1
        x1, h2, *packed = _merge(xt, mod, norm2_g[layer].reshape(1, D), y_gm, y_da, y_dn, gates,
                                 w_br_gm[layer].astype(BF16), w_br_da[layer].astype(BF16),
                                 w_br_dn[layer].astype(BF16), w_out[layer].astype(BF16), moe)
        if moe:
            xt = _moe_layer(x1, h2, packed[0], mod, moe_w_router[layer // 2], moe_w_gate[layer // 2],
                            moe_w_up[layer // 2], moe_w_down[layer // 2], final_g)
        else:
            xt = _dense_ffn(x1, h2, mod, ffn_w_gate[layer // 2].astype(BF16),
                            ffn_w_up[layer // 2].astype(BF16), ffn_w_down[layer // 2].astype(BF16))
    if DEPTH % 2 == 1:
        xt = _final_norm(xt, final_g)
    return xt.reshape(B, S, D)
```

```python
import functools
import math

import jax
import jax.numpy as jnp
import numpy as np
from jax import lax
from jax.experimental import pallas as pl
from jax.experimental.pallas import tpu as pltpu

D_MODEL = 1024
SEQ = 2048
DEPTH = 2
CHUNK = 64
EPS = 1e-6
GM_WIDTH = D_MODEL // 2
GM_GROUPS = 4
GM_BLOCK = 128
DA_HEADS = 4
DA_HEAD_DIM = 64
DA_V_DIM = 2 * DA_HEAD_DIM
ROPE_THETA = 10000.0
DN_HEADS = 4
DN_HEAD_DIM = 128
DN_WIDTH = DN_HEADS * DN_HEAD_DIM
DN_CONV = 4
N_EXPERTS = 8
FF_EXPERT = 7 * D_MODEL // 2

LANES = 128
VMEM_LIMIT = 56 * 1024 * 1024

C_GM, C_QK, C_V, C_DN, C_GATE, C_AB = 0, 1024, 2048, 2560, 4608, 7680
W_ALL = C_AB + LANES
PROJ_CHUNK = 512

TM_PROJ = 512
TM_GM = 512
TQ = 256
TK = 256
TM_MERGE = 512
TM_FFN = 512
TM_ROUTE = 512
TM_MOE = 512
TM_DISP = 512
DISP_SLOTS = 3
TM_COMB = 512
ATTN_LOOKAHEAD = 2
VT_ROWS = DA_V_DIM + 16
DN_BLOCK = 256

BF16 = jnp.bfloat16
F32 = jnp.float32


def _params(*sem):
    return pltpu.CompilerParams(dimension_semantics=sem, vmem_limit_bytes=VMEM_LIMIT)


def _dot(a, b):
    return jnp.dot(a, b, preferred_element_type=F32)


def _dot_nt(a, b):
    return lax.dot_general(a, b, (((1,), (1,)), ((), ())), preferred_element_type=F32)


def _dot_tn(a, b):
    return lax.dot_general(a, b, (((0,), (0,)), ((), ())), preferred_element_type=F32)


def _rms(x, g):
    return x * lax.rsqrt(jnp.mean(x * x, axis=-1, keepdims=True) + EPS) * g


def _silu(x):
    return x * jax.nn.sigmoid(x)


def _mod_kernel(c_ref, w_ref, b_ref, o_ref):
    c = c_ref[...]
    o_ref[0] = _dot(_silu(c).astype(BF16), w_ref[0].astype(BF16)) + b_ref[0]


def _modulation(c, w_mod, b_mod):
    B, D = c.shape
    L, _, N = w_mod.shape
    tn = 1536
    return pl.pallas_call(
        _mod_kernel,
        grid=(L, N // tn),
        in_specs=[pl.BlockSpec((B, D), lambda l, j: (0, 0)),
                  pl.BlockSpec((1, D, tn), lambda l, j: (l, 0, j)),
                  pl.BlockSpec((1, 1, tn), lambda l, j: (l, 0, j))],
        out_specs=pl.BlockSpec((1, B, tn), lambda l, j: (l, 0, j)),
        out_shape=jax.ShapeDtypeStruct((L, B, N), F32),
        compiler_params=_params("arbitrary", "arbitrary"),
        name="modulation",
    )(c, w_mod, b_mod.reshape(L, 1, N))


def _rope_tab_kernel(pos_ref, inv_ref, sgn_ref, cos_ref, sin_ref):
    ang = pos_ref[...].astype(F32) * inv_ref[...]
    cos_ref[...] = jnp.cos(ang)
    sin_ref[...] = jnp.sin(ang) * sgn_ref[...]


def _rope_tables(positions):
    T = positions.size
    inv_freq = ROPE_THETA ** (-jnp.arange(0, DA_HEAD_DIM, 2, dtype=F32) / DA_HEAD_DIM)
    inv = jnp.tile(inv_freq, LANES // (DA_HEAD_DIM // 2)).reshape(1, LANES)
    half = DA_HEAD_DIM // 2
    sgn = np.tile(np.concatenate([-np.ones(half), np.ones(half)]), LANES // DA_HEAD_DIM)
    sgn = jnp.asarray(sgn, F32).reshape(1, LANES)
    tm = 1024
    return pl.pallas_call(
        _rope_tab_kernel,
        grid=(T // tm,),
        in_specs=[pl.BlockSpec((tm, 1), lambda i: (i, 0)),
                  pl.BlockSpec((1, LANES), lambda i: (0, 0)),
                  pl.BlockSpec((1, LANES), lambda i: (0, 0))],
        out_specs=[pl.BlockSpec((tm, LANES), lambda i: (i, 0))] * 2,
        out_shape=[jax.ShapeDtypeStruct((T, LANES), F32)] * 2,
        compiler_params=_params("arbitrary"),
        name="rope_tables",
    )(positions.reshape(T, 1), inv, sgn)


def _rope_rows(x, cos, sin):
    lane = lax.broadcasted_iota(jnp.int32, x.shape, 1)
    first = (lane % DA_HEAD_DIM) < (DA_HEAD_DIM // 2)
    partner = jnp.where(first, pltpu.roll(x, LANES - DA_HEAD_DIM // 2, axis=1),
                        pltpu.roll(x, DA_HEAD_DIM // 2, axis=1))
    return x * cos + partner * sin


def _proj_kernel(x_ref, mod_ref, g_ref, w_ref, cos_ref, sin_ref, cw_ref,
                 gm_ref, qk_ref, v_ref, dn_ref, gate_ref, ab_ref, prev_ref):
    tm = x_ref.shape[0]

    @pl.when(pl.program_id(0) % (SEQ // tm) == 0)
    def _():
        prev_ref[...] = jnp.zeros_like(prev_ref)

    def conv_silu(y, idx):
        cw = cw_ref[:, idx * PROJ_CHUNK:(idx + 1) * PROJ_CHUNK]
        ycat = jnp.concatenate([prev_ref[idx], y], axis=0)
        out = y * cw[DN_CONV - 1:DN_CONV, :]
        for j in range(DN_CONV - 1):
            out = out + pltpu.roll(ycat, DN_CONV - 1 - j, axis=0)[8:, :] * cw[j:j + 1, :]
        prev_ref[idx] = y[tm - 8:, :]
        return _silu(out)

    def l2norm_heads(t, mult):
        parts = []
        for j in range(PROJ_CHUNK // DN_HEAD_DIM):
            seg = t[:, j * DN_HEAD_DIM:(j + 1) * DN_HEAD_DIM]
            parts.append(seg * (lax.rsqrt(jnp.sum(seg * seg, axis=-1, keepdims=True) + EPS) * mult))
        return jnp.concatenate(parts, axis=1)

    x = x_ref[...]
    shift = mod_ref[0, 0:1, :]
    scale = mod_ref[0, 1:2, :]
    h = (_rms(x, g_ref[...]) * (1.0 + scale) + shift).astype(BF16)
    cos = cos_ref[...]
    sin = sin_ref[...]
    q_scale = DA_HEAD_DIM ** -0.5 * math.log2(math.e)

    def rope_chunk(y, mult):
        parts = [_rope_rows(y[:, j * LANES:(j + 1) * LANES], cos, sin) * mult
                 for j in range(PROJ_CHUNK // LANES)]
        return jnp.concatenate(parts, axis=1)

    for c0 in range(0, C_AB, PROJ_CHUNK):
        y = _dot(h, w_ref[:, c0:c0 + PROJ_CHUNK])
        if c0 < C_QK:
            gm_ref[:, c0:c0 + PROJ_CHUNK] = y.astype(BF16)
        elif c0 < C_V:
            mult = q_scale if c0 == C_QK else 1.0
            qk_ref[:, c0 - C_QK:c0 - C_QK + PROJ_CHUNK] = rope_chunk(y, mult).astype(BF16)
        elif c0 < C_DN:
            v_ref[...] = y.astype(BF16)
        elif c0 < C_GATE:
            idx = (c0 - C_DN) // PROJ_CHUNK
            if idx < 3:
                y = conv_silu(y, idx)
            if idx < 2:
                y = l2norm_heads(y, DN_HEAD_DIM ** -0.5 if idx == 0 else 1.0)
            dn_ref[:, c0 - C_DN:c0 - C_DN + PROJ_CHUNK] = y.astype(BF16)
        else:
            gate_ref[:, c0 - C_GATE:c0 - C_GATE + PROJ_CHUNK] = y.astype(BF16)
    ab_ref[...] = _dot(h, w_ref[:, C_AB:W_ALL])


def _in_projection(x, mod, g, w_all, cos_t, sin_t, conv_w):
    T, D = x.shape
    tm = TM_PROJ
    per_b = SEQ // tm
    row = lambda i: (i, 0)
    widths = (C_QK - C_GM, C_V - C_QK, C_DN - C_V, C_GATE - C_DN, C_AB - C_GATE)
    return pl.pallas_call(
        _proj_kernel,
        grid=(T // tm,),
        in_specs=[pl.BlockSpec((tm, D), row),
                  pl.BlockSpec((1, 6, D), lambda i: (i // per_b, 0, 0)),
                  pl.BlockSpec((1, D), lambda i: (0, 0)),
                  pl.BlockSpec((D, W_ALL), lambda i: (0, 0), pipeline_mode=pl.Buffered(1)),
                  pl.BlockSpec((tm, LANES), row),
                  pl.BlockSpec((tm, LANES), row),
                  pl.BlockSpec((DN_CONV, 3 * DN_WIDTH), lambda i: (0, 0))],
        out_specs=[pl.BlockSpec((tm, w), row) for w in widths] + [pl.BlockSpec((tm, LANES), row)],
        out_shape=[jax.ShapeDtypeStruct((T, w), BF16) for w in widths]
                  + [jax.ShapeDtypeStruct((T, LANES), F32)],
        scratch_shapes=[pltpu.VMEM((3, 8, PROJ_CHUNK), F32)],
        compiler_params=_params("arbitrary"),
        name="in_projection",
    )(x, mod, g, w_all, cos_t, sin_t, conv_w)


def _gelu(x):
    return 0.5 * x * (1.0 + lax.erf(x * np.float32(math.sqrt(0.5))))


def _gmlp_kernel(uv_ref, lng_ref, lnb_ref, ws_ref, bs_ref, o_ref):
    u = _gelu(uv_ref[:, :GM_WIDTH].astype(F32))
    v = _gelu(uv_ref[:, GM_WIDTH:].astype(F32))
    mu = jnp.mean(v, axis=-1, keepdims=True)
    vc = v - mu
    var = jnp.mean(vc * vc, axis=-1, keepdims=True)
    v = (vc * lax.rsqrt(var + EPS) * lng_ref[...] + lnb_ref[...]).astype(BF16)
    ri = lax.broadcasted_iota(jnp.int32, (GM_BLOCK, GM_BLOCK), 0) // CHUNK
    ci = lax.broadcasted_iota(jnp.int32, (GM_BLOCK, GM_BLOCK), 1) // CHUNK
    allowed = ci <= ri
    gc = GM_WIDTH // GM_GROUPS
    for g in range(GM_GROUPS):
        w = jnp.where(allowed, ws_ref[g], 0.0).astype(BF16)
        bias = bs_ref[g]
        for r in range(uv_ref.shape[0] // GM_BLOCK):
            rows = slice(r * GM_BLOCK, (r + 1) * GM_BLOCK)
            cols = slice(g * gc, (g + 1) * gc)
            sv = _dot(w, v[rows, cols]) + bias
            o_ref[rows, cols] = (u[rows, cols] * sv).astype(BF16)


def _gmlp(uv, ln_g, ln_b, w_s, b_s):
    T = uv.shape[0]
    tm = TM_GM
    return pl.pallas_call(
        _gmlp_kernel,
        grid=(T // tm,),
        in_specs=[pl.BlockSpec((tm, 2 * GM_WIDTH), lambda i: (i, 0)),
                  pl.BlockSpec((1, GM_WIDTH), lambda i: (0, 0)),
                  pl.BlockSpec((1, GM_WIDTH), lambda i: (0, 0)),
                  pl.BlockSpec((GM_GROUPS, GM_BLOCK, GM_BLOCK), lambda i: (0, 0, 0)),
                  pl.BlockSpec((GM_GROUPS, GM_BLOCK, 1), lambda i: (0, 0, 0))],
        out_specs=pl.BlockSpec((tm, GM_WIDTH), lambda i: (i, 0)),
        out_shape=jax.ShapeDtypeStruct((T, GM_WIDTH), BF16),
        compiler_params=_params("arbitrary"),
        name="gmlp",
    )(uv, ln_g.reshape(1, -1), ln_b.reshape(1, -1), w_s, b_s.reshape(GM_GROUPS, GM_BLOCK, 1))


def _attn_kernel(lam_ref, g_ref, q_ref, k_ref, v_ref, o_ref, qz_ref, vt_ref, *stats, lambda_init):
    nc = 2 * DA_HEADS
    m_ref, acc_ref, s_ref = (stats[j * nc:(j + 1) * nc] for j in range(3))
    i = pl.program_id(1)
    lp = lam_ref[...]
    lam = (jnp.exp(jnp.sum(lp[0:1] * lp[1:2], axis=-1, keepdims=True))
           - jnp.exp(jnp.sum(lp[2:3] * lp[3:4], axis=-1, keepdims=True)) + lambda_init)
    @pl.when(i == 0)
    def _():
        for kb in range(SEQ // TK):
            for h in range(DA_HEADS):
                vt_ref[h * VT_ROWS:h * VT_ROWS + DA_V_DIM, kb * TK:(kb + 1) * TK] = (
                    v_ref[kb * TK:(kb + 1) * TK, h * LANES:(h + 1) * LANES].T)
        for h in range(DA_HEADS):
            vt_ref[h * VT_ROWS + DA_V_DIM:(h + 1) * VT_ROWS, :] = jnp.ones((VT_ROWS - DA_V_DIM, SEQ), BF16)

    row = lax.broadcasted_iota(jnp.int32, (LANES, TQ), 0)
    for h in range(DA_HEADS):
        qt = q_ref[:, h * LANES:(h + 1) * LANES].T
        zero = jnp.zeros_like(qt)
        qz_ref[2 * h] = jnp.where(row < DA_HEAD_DIM, qt, zero)
        qz_ref[2 * h + 1] = jnp.where(row >= DA_HEAD_DIM, qt, zero)
    for c in range(nc):
        m_ref[c][...] = jnp.full_like(m_ref[c], -jnp.inf)
        acc_ref[c][...] = jnp.zeros_like(acc_ref[c])

    def key_rows(kb):
        return pl.ds(pl.multiple_of(kb * TK, TK), TK)

    def scores(kb, c, masked):
        h = c // 2
        s = _dot(k_ref[key_rows(kb), h * LANES:(h + 1) * LANES], qz_ref[c])
        if masked:
            ki = lax.broadcasted_iota(jnp.int32, s.shape, 0) // CHUNK + kb * (TK // CHUNK)
            qi = lax.broadcasted_iota(jnp.int32, s.shape, 1) // CHUNK + i * (TQ // CHUNK)
            s = jnp.where(ki <= qi, s, -jnp.inf)
        return s

    def accumulate(kb, c, s):
        h = c // 2
        m_old = m_ref[c][...]
        m_new = jnp.maximum(m_old, jnp.max(s, axis=0, keepdims=True))
        alpha = jnp.exp2(m_old - m_new)
        e = jnp.exp2((s - m_new).astype(BF16))
        vt = vt_ref[h * VT_ROWS:(h + 1) * VT_ROWS, key_rows(kb)]
        acc_ref[c][...] = alpha * acc_ref[c][...] + _dot(vt, e)
        m_ref[c][...] = m_new

    def fold_and_prefetch(kb, masked_next):
        fresh = {}
        for c in range(nc + ATTN_LOOKAHEAD):
            if c < nc:
                fresh[c] = scores(kb + 1, c, masked_next)
            if c >= ATTN_LOOKAHEAD:
                cc = c - ATTN_LOOKAHEAD
                accumulate(kb, cc, s_ref[cc][...])
                s_ref[cc][...] = fresh.pop(cc)

    first_masked = i * (TQ // TK)

    @pl.when(i == 0)
    def _():
        for c in range(nc):
            s_ref[c][...] = scores(0, c, True)

    @pl.when(i > 0)
    def _():
        for c in range(nc):
            s_ref[c][...] = scores(0, c, False)

        def body(kb, carry):
            fold_and_prefetch(kb, False)
            return carry

        lax.fori_loop(0, first_masked - 1, body, 0)
        fold_and_prefetch(first_masked - 1, True)

    for r in range(TQ // TK - 1):
        fold_and_prefetch(first_masked + r, True)
    for c in range(nc):
        accumulate(first_masked + TQ // TK - 1, c, s_ref[c][...])
    for h in range(DA_HEADS):
        num0, den0 = acc_ref[2 * h][:DA_V_DIM, :], acc_ref[2 * h][DA_V_DIM:DA_V_DIM + 1, :]
        num1, den1 = acc_ref[2 * h + 1][:DA_V_DIM, :], acc_ref[2 * h + 1][DA_V_DIM:DA_V_DIM + 1, :]
        o = num0 / den0 - lam * (num1 / den1)
        ms = jnp.mean(o * o, axis=0, keepdims=True)
        o = o * lax.rsqrt(ms + EPS) * (g_ref[...] * (1.0 - lambda_init))
        o_ref[:, h * LANES:(h + 1) * LANES] = o.T.astype(BF16)


def _diff_attention(qk, v, lam_p, subln_g, lambda_init, batch):
    T = qk.shape[0]
    nq = SEQ // TQ
    W = DA_HEADS * DA_V_DIM
    kernel = functools.partial(_attn_kernel, lambda_init=lambda_init)
    return pl.pallas_call(
        kernel,
        grid=(batch, nq),
        in_specs=[pl.BlockSpec((4, DA_HEAD_DIM), lambda b, i: (0, 0)),
                  pl.BlockSpec((DA_V_DIM, 1), lambda b, i: (0, 0)),
                  pl.BlockSpec((TQ, W), lambda b, i: (b * nq + i, 0)),
                  pl.BlockSpec((SEQ, W), lambda b, i: (b, 1)),
                  pl.BlockSpec((SEQ, W), lambda b, i: (b, 0))],
        out_specs=pl.BlockSpec((TQ, W), lambda b, i: (b * nq + i, 0)),
        out_shape=jax.ShapeDtypeStruct((T, W), BF16),
        scratch_shapes=[pltpu.VMEM((2 * DA_HEADS, LANES, TQ), BF16),
                        pltpu.VMEM((DA_HEADS * VT_ROWS, SEQ), BF16)]
                       + [pltpu.VMEM((1, TQ), F32)] * (2 * DA_HEADS)
                       + [pltpu.VMEM((VT_ROWS, TQ), F32)] * (2 * DA_HEADS)
                       + [pltpu.VMEM((TK, TQ), F32)] * (2 * DA_HEADS),
        compiler_params=_params("arbitrary", "arbitrary"),
        name="diff_attention",
    )(lam_p, subln_g.reshape(-1, 1), qk, qk, v)


def _split3(x):
    hi = x.astype(BF16)
    r = x - hi.astype(F32)
    mid = r.astype(BF16)
    lo = (r - mid.astype(F32)).astype(BF16)
    return hi, mid, lo


def _dn_kernel(dn_ref, a_ref, at_ref, alog_ref, dtb_ref, alogt_ref, dtbt_ref, ng_ref,
               o_ref, *state_ref):
    c = pl.program_id(1)
    R = DN_BLOCK

    @pl.when(c == 0)
    def _():
        for ref in state_ref:
            ref[...] = jnp.zeros_like(ref)

    W3 = 3 * DN_WIDTH
    ab = a_ref[...]
    g_col = -jnp.exp(alog_ref[...]) * jax.nn.softplus(ab + dtb_ref[...])
    beta_all = jax.nn.sigmoid(ab)
    abt = at_ref[0]
    g_row = -jnp.exp(alogt_ref[...]) * jax.nn.softplus(abt + dtbt_ref[...])
    ri = lax.broadcasted_iota(jnp.int32, (R, R), 0)
    ci = lax.broadcasted_iota(jnp.int32, (R, R), 1)
    tril = ri >= ci
    strict = ri > ci
    ones_tril = jnp.where(tril, 1.0, 0.0).astype(BF16)
    ones_triu = jnp.where(ci >= ri, 1.0, 0.0).astype(BF16)
    gc_col = sum(_dot(ones_tril, part) for part in _split3(g_col))
    gc_row = sum(_dot(part, ones_triu) for part in _split3(g_row))
    eye = jnp.where(ri == ci, 1.0, 0.0)
    same = {n: (ri // n) == (ci // n) for n in (16, 32, 64, 128)}

    H = range(DN_HEADS)
    hsl = [slice(h * DN_HEAD_DIM, (h + 1) * DN_HEAD_DIM) for h in H]
    q16 = [dn_ref[:, hsl[h]] for h in H]
    k16 = [dn_ref[:, DN_WIDTH + h * DN_HEAD_DIM:DN_WIDTH + (h + 1) * DN_HEAD_DIM] for h in H]
    q = [t.astype(F32) for t in q16]
    k = [t.astype(F32) for t in k16]
    v = [dn_ref[:, 2 * DN_WIDTH + h * DN_HEAD_DIM:2 * DN_WIDTH + (h + 1) * DN_HEAD_DIM].astype(F32) for h in H]
    gc =[gc_col[:, h:h + 1] for h in H]
    beta = [beta_all[:, DN_HEADS + h:DN_HEADS + h + 1] for h in H]
    g_last = [gc_col[R - 1:R, h:h + 1] for h in H]
    decay = [jnp.exp(jnp.where(tril, gc[h] - gc_row[h:h + 1, :], -jnp.inf)) for h in H]
    eg = [jnp.exp(t) for t in gc]
    kb = [k[h] * beta[h] for h in H]
    s1 = [_dot_nt(jnp.concatenate([kb[h].astype(BF16), q16[h]], axis=0), k16[h]) for h in H]
    a_mat = [jnp.where(strict, s1[h][:R] * decay[h], 0.0) for h in H]
    qk = [(s1[h][R:] * decay[h]).astype(BF16) for h in H]
    p = [jnp.where(same[16], -t, 0.0) for t in a_mat]
    t_inv = [eye + t for t in p]
    for _ in range(3):
        p16 = [t.astype(BF16) for t in p]
        p = [_dot(t, t) for t in p16]
        t_inv = [t_inv[h] + _dot(t_inv[h].astype(BF16), p[h].astype(BF16)) for h in H]
    a16 = [t.astype(BF16) for t in a_mat]
    t16 = [t.astype(BF16) for t in t_inv]
    for n in (16, 32, 64, 128):
        off = jnp.logical_not(same[n])
        if 2 * n < R:
            off = jnp.logical_and(same[2 * n], off)
        off16 = jnp.where(off, 1.0, 0.0).astype(BF16)
        tl = [_dot(t16[h], a16[h] * off16).astype(BF16) for h in H]
        t16 = [t16[h] - _dot(tl[h], t16[h]).astype(BF16) * off16 for h in H]
    sol = [_dot(t16[h], jnp.concatenate([v[h] * beta[h], kb[h] * eg[h]], axis=1).astype(BF16))
           for h in H]
    state = [state_ref[h][...] for h in H]
    m1 = [_dot(jnp.concatenate([sol[h][:, DN_HEAD_DIM:], q[h] * eg[h]], axis=0).astype(BF16),
               state[h].astype(BF16)) for h in H]
    vn16 = [(sol[h][:, :DN_HEAD_DIM] - m1[h][:R]).astype(BF16) for h in H]
    o = [m1[h][R:] + _dot(qk[h], vn16[h]) for h in H]
    kd = [(k[h] * jnp.exp(g_last[h] - gc[h])).astype(BF16) for h in H]
    for h in H:
        state_ref[h][...] = state[h] * jnp.exp(g_last[h]) + _dot_tn(kd[h], vn16[h])
    for h in H:
        z = dn_ref[:, W3 + h * DN_HEAD_DIM:W3 + (h + 1) * DN_HEAD_DIM].astype(F32)
        o_ref[:, hsl[h]] = (_rms(o[h], ng_ref[...]) * _silu(z)).astype(BF16)


def _deltanet(dn, ab, a_log, dt_bias, norm_g, batch):
    T = dn.shape[0]
    R = DN_BLOCK
    n = SEQ // R
    pad = lambda t: jnp.zeros((1, LANES), F32).at[0, :DN_HEADS].set(t)
    col8 = lambda t: jnp.zeros((8, 1), F32).at[:DN_HEADS, 0].set(t)
    ab_t = ab[:, :8].reshape(T // R, R, 8).transpose(0, 2, 1)
    return pl.pallas_call(
        _dn_kernel,
        grid=(batch, n),
        in_specs=[pl.BlockSpec((R, 4 * DN_WIDTH), lambda b, c: (b * n + c, 0)),
                  pl.BlockSpec((R, LANES), lambda b, c: (b * n + c, 0)),
                  pl.BlockSpec((1, 8, R), lambda b, c: (b * n + c, 0, 0)),
                  pl.BlockSpec((1, LANES), lambda b, c: (0, 0)),
                  pl.BlockSpec((1, LANES), lambda b, c: (0, 0)),
                  pl.BlockSpec((8, 1), lambda b, c: (0, 0)),
                  pl.BlockSpec((8, 1), lambda b, c: (0, 0)),
                  pl.BlockSpec((1, DN_HEAD_DIM), lambda b, c: (0, 0))],
        out_specs=pl.BlockSpec((R, DN_WIDTH), lambda b, c: (b * n + c, 0)),
        out_shape=jax.ShapeDtypeStruct((T, DN_WIDTH), BF16),
        scratch_shapes=[pltpu.VMEM((DN_HEAD_DIM, DN_HEAD_DIM), F32)] * DN_HEADS,
        compiler_params=_params("arbitrary", "arbitrary"),
        name="deltanet",
    )(dn, ab, ab_t, pad(a_log), pad(dt_bias), col8(a_log), col8(dt_bias), norm_g.reshape(1, -1))


def _pack_bf16_pair(lo, hi):
    lo_bits = lax.bitcast_convert_type(lo.astype(BF16).astype(F32), jnp.uint32)
    hi_bits = lax.bitcast_convert_type(hi.astype(BF16).astype(F32), jnp.uint32)
    return (lo_bits >> 16) | (hi_bits & jnp.uint32(0xFFFF0000))


def _unpack_bf16_pair(words):
    lo = lax.bitcast_convert_type(words << 16, F32)
    hi = lax.bitcast_convert_type(words & jnp.uint32(0xFFFF0000), F32)
    return lo.astype(BF16), hi.astype(BF16)


def _merge_kernel(x_ref, mod_ref, g2_ref, ygm_ref, yda_ref, ydn_ref, gate_ref,
                  wgm_ref, wda_ref, wdn_ref, wout_ref, x1_ref, h2_ref, *maybe_packed_ref):
    D = D_MODEL
    merged = (jax.nn.sigmoid(gate_ref[:, 0:D].astype(F32)) * _dot(ygm_ref[...], wgm_ref[...])
              + jax.nn.sigmoid(gate_ref[:, D:2 * D].astype(F32)) * _dot(yda_ref[...], wda_ref[...])
              + jax.nn.sigmoid(gate_ref[:, 2 * D:3 * D].astype(F32)) * _dot(ydn_ref[...], wdn_ref[...]))
    y = _dot(merged.astype(BF16), wout_ref[...])
    x1 = x_ref[...] + mod_ref[0, 2:3, :] * y
    x1_ref[...] = x1
    h2 = _rms(x1, g2_ref[...]) * (1.0 + mod_ref[0, 4:5, :]) + mod_ref[0, 3:4, :]
    h2_ref[...] = h2.astype(BF16)
    for packed_ref in maybe_packed_ref:
        packed_ref[...] = _pack_bf16_pair(h2[:, :D // 2], h2[:, D // 2:])


def _merge(x, mod, g2, y_gm, y_da, y_dn, gates, w_gm, w_da, w_dn, w_out, with_packed):
    T, D = x.shape
    tm = TM_MERGE
    per_b = SEQ // tm
    row = lambda i: (i, 0)
    const = lambda i: (0, 0)
    out_specs = [pl.BlockSpec((tm, D), row), pl.BlockSpec((tm, D), row)]
    out_shape = [jax.ShapeDtypeStruct((T, D), F32), jax.ShapeDtypeStruct((T, D), BF16)]
    if with_packed:
        out_specs.append(pl.BlockSpec((tm, D // 2), row))
        out_shape.append(jax.ShapeDtypeStruct((T, D // 2), jnp.uint32))
    return pl.pallas_call(
        _merge_kernel,
        grid=(T // tm,),
        in_specs=[pl.BlockSpec((tm, D), row),
                  pl.BlockSpec((1, 6, D), lambda i: (i // per_b, 0, 0)),
                  pl.BlockSpec((1, D), const),
                  pl.BlockSpec((tm, GM_WIDTH), row),
                  pl.BlockSpec((tm, GM_WIDTH), row),
                  pl.BlockSpec((tm, DN_WIDTH), row),
                  pl.BlockSpec((tm, 3 * D), row),
                  pl.BlockSpec((GM_WIDTH, D), const),
                  pl.BlockSpec((GM_WIDTH, D), const),
                  pl.BlockSpec((DN_WIDTH, D), const),
                  pl.BlockSpec((D, D), const)],
        out_specs=out_specs,
        out_shape=out_shape,
        compiler_params=_params("arbitrary"),
        name="merge",
    )(x, mod, g2, y_gm, y_da, y_dn, gates, w_gm, w_da, w_dn, w_out)


def _ffn_kernel(x_ref, h_ref, mod_ref, wg_ref, wu_ref, wd_ref, o_ref):
    h = h_ref[...]
    a = _dot(h, wg_ref[...])
    b = _dot(h, wu_ref[...])
    f = _dot((_silu(a) * b).astype(BF16), wd_ref[...])
    o_ref[...] = x_ref[...] + mod_ref[0, 5:6, :] * f


def _dense_ffn(x1, h2, mod, w_gate, w_up, w_down):
    T, D = x1.shape
    F = w_gate.shape[1]
    tm = TM_FFN
    per_b = SEQ // tm
    row = lambda i: (i, 0)
    const = lambda i: (0, 0)
    return pl.pallas_call(
        _ffn_kernel,
        grid=(T // tm,),
        in_specs=[pl.BlockSpec((tm, D), row),
                  pl.BlockSpec((tm, D), row),
                  pl.BlockSpec((1, 6, D), lambda i: (i // per_b, 0, 0)),
                  pl.BlockSpec((D, F), const, pipeline_mode=pl.Buffered(1)),
                  pl.BlockSpec((D, F), const, pipeline_mode=pl.Buffered(1)),
                  pl.BlockSpec((F, D), const, pipeline_mode=pl.Buffered(1))],
        out_specs=pl.BlockSpec((tm, D), row),
        out_shape=jax.ShapeDtypeStruct((T, D), F32),
        compiler_params=_params("arbitrary"),
        name="dense_ffn",
    )(x1, h2, mod, w_gate, w_up, w_down)


def _router_kernel(h_ref, wr_ref, e_ref, p_ref, r_ref, cnt_ref, base_ref):
    i = pl.program_id(0)
    tm = h_ref.shape[0]

    @pl.when(i == 0)
    def _():
        base_ref[...] = jnp.zeros_like(base_ref)

    logits = _dot_nt(wr_ref[...], h_ref[...])
    row = lax.broadcasted_iota(jnp.int32, logits.shape, 0)
    m1 = jnp.max(logits, axis=0, keepdims=True)
    i1 = jnp.min(jnp.where(logits == m1, row, N_EXPERTS), axis=0, keepdims=True)
    rest = jnp.where(row == i1, -jnp.inf, logits)
    m2 = jnp.max(rest, axis=0, keepdims=True)
    i2 = jnp.min(jnp.where(rest == m2, row, N_EXPERTS), axis=0, keepdims=True)
    e2 = jnp.exp(m2 - m1)
    w1 = 1.0 / (1.0 + e2)
    w2 = e2 / (1.0 + e2)
    oh1 = jnp.where(row == i1, 1.0, 0.0)
    oh2 = jnp.where(row == i2, 1.0, 0.0)
    both = oh1 + oh2
    ti = lax.broadcasted_iota(jnp.int32, (tm, tm), 0)
    tj = lax.broadcasted_iota(jnp.int32, (tm, tm), 1)
    before = jnp.where(ti < tj, 1.0, 0.0).astype(BF16)
    pos = base_ref[...] + _dot(both.astype(BF16), before)
    r1 = jnp.sum(oh1 * pos, axis=0, keepdims=True)
    r2 = jnp.sum(oh2 * pos, axis=0, keepdims=True)
    base_ref[...] = base_ref[...] + jnp.sum(both, axis=1, keepdims=True)
    zi = jnp.zeros((N_EXPERTS - 2, tm), jnp.int32)
    zf = jnp.zeros((N_EXPERTS - 2, tm), F32)
    e_ref[...] = jnp.concatenate([i1, i2, zi], axis=0)
    p_ref[...] = jnp.concatenate([w1, w2, zf], axis=0)
    r_ref[...] = jnp.concatenate([r1.astype(jnp.int32), r2.astype(jnp.int32), zi], axis=0)
    cnt_ref[...] = jnp.broadcast_to(base_ref[...], cnt_ref.shape)


def _router(h2, w_router_t):
    T, D = h2.shape
    tm = TM_ROUTE
    col = lambda i: (0, i)
    return pl.pallas_call(
        _router_kernel,
        grid=(T // tm,),
        in_specs=[pl.BlockSpec((tm, D), lambda i: (i, 0)),
                  pl.BlockSpec((N_EXPERTS, D), lambda i: (0, 0))],
        out_specs=[pl.BlockSpec((N_EXPERTS, tm), col)] * 3 + [pl.BlockSpec((N_EXPERTS, LANES), lambda i: (0, 0))],
        out_shape=[jax.ShapeDtypeStruct((N_EXPERTS, T), jnp.int32),
                   jax.ShapeDtypeStruct((N_EXPERTS, T), F32),
                   jax.ShapeDtypeStruct((N_EXPERTS, T), jnp.int32),
                   jax.ShapeDtypeStruct((N_EXPERTS, LANES), F32)],
        scratch_shapes=[pltpu.VMEM((N_EXPERTS, 1), F32)],
        compiler_params=_params("arbitrary"),
        name="moe_router",
    )(h2, w_router_t)


def _dispatch_kernel(dest_ref, h_ref, xs_in_ref, xs_ref, buf_ref, sem_in, sem_out):
    del xs_in_ref
    i = pl.program_id(0)
    n = pl.num_programs(0)
    tm = buf_ref.shape[1]

    def load(t):
        s = t % DISP_SLOTS
        return pltpu.make_async_copy(h_ref.at[pl.ds(pl.multiple_of(t * tm, tm), tm)], buf_ref.at[s],
                                     sem_in.at[s])

    def wait_rows(t):
        s = t % DISP_SLOTS
        for _ in range(2):
            pltpu.make_async_copy(buf_ref.at[s], xs_ref.at[pl.ds(0, tm)], sem_out.at[s]).wait()

    @pl.when(i == 0)
    def _():
        load(0).start()
        pl.when(n > 1)(lambda: load(1).start())

    load(i).wait()
    slot = i % DISP_SLOTS

    def issue(r, _):
        for k in range(2):
            pltpu.make_async_copy(buf_ref.at[slot, pl.ds(r, 1)],
                                  xs_ref.at[pl.ds(dest_ref[0, 0, k * tm + r], 1)], sem_out.at[slot]).start()
        return 0

    lax.fori_loop(0, tm, issue, 0, unroll=8)
    pl.when(i > 0)(lambda: wait_rows(i - 1))
    pl.when(i + 2 < n)(lambda: load(i + 2).start())
    pl.when(i == n - 1)(lambda: wait_rows(i))


def _dispatch(h2, dest, n_slots):
    T, D = h2.shape
    tm = TM_DISP
    dest_t = dest.reshape(2, T // tm, tm).transpose(1, 0, 2).reshape(T // tm, 1, 2 * tm)
    xs0 = jnp.zeros((n_slots, D), h2.dtype)
    return pl.pallas_call(
        _dispatch_kernel,
        grid=(T // tm,),
        in_specs=[pl.BlockSpec((1, 1, 2 * tm), lambda i: (i, 0, 0), memory_space=pltpu.SMEM),
                  pl.BlockSpec(memory_space=pl.ANY),
                  pl.BlockSpec(memory_space=pl.ANY)],
        out_specs=pl.BlockSpec(memory_space=pl.ANY),
        out_shape=jax.ShapeDtypeStruct((n_slots, D), h2.dtype),
        scratch_shapes=[pltpu.VMEM((DISP_SLOTS, tm, D), h2.dtype),
                        pltpu.SemaphoreType.DMA((DISP_SLOTS,)), pltpu.SemaphoreType.DMA((DISP_SLOTS,))],
        input_output_aliases={2: 0},
        compiler_params=_params("arbitrary"),
        name="moe_dispatch",
    )(dest_t, h2, xs0)


def _expert_kernel(be_ref, nu_ref, x_ref, wg_ref, wu_ref, wd_ref, o_ref):
    del be_ref
    used = pl.program_id(0) < nu_ref[0]
    half = D_MODEL // 2

    @pl.when(used)
    def _():
        lo, hi = _unpack_bf16_pair(x_ref[...])
        a = _dot(lo, wg_ref[0, :half, :]) + _dot(hi, wg_ref[0, half:, :])
        b = _dot(lo, wu_ref[0, :half, :]) + _dot(hi, wu_ref[0, half:, :])
        o_ref[...] = _dot((_silu(a) * b).astype(BF16), wd_ref[0])

    @pl.when(jnp.logical_not(used))
    def _():
        o_ref[...] = jnp.zeros_like(o_ref)


def _experts(xs, blk_e, n_used, w_gate, w_up, w_down):
    P = xs.shape[0]
    _, D, F = w_gate.shape
    tm = TM_MOE
    grid_spec = pltpu.PrefetchScalarGridSpec(
        num_scalar_prefetch=2,
        grid=(P // tm,),
        in_specs=[pl.BlockSpec((tm, D // 2), lambda i, be, nu: (i, 0)),
                  pl.BlockSpec((1, D, F), lambda i, be, nu: (be[i], 0, 0), pipeline_mode=pl.Buffered(1)),
                  pl.BlockSpec((1, D, F), lambda i, be, nu: (be[i], 0, 0), pipeline_mode=pl.Buffered(1)),
                  pl.BlockSpec((1, F, D), lambda i, be, nu: (be[i], 0, 0), pipeline_mode=pl.Buffered(1))],
        out_specs=pl.BlockSpec((tm, D), lambda i, be, nu: (i, 0)))
    return pl.pallas_call(
        _expert_kernel,
        grid_spec=grid_spec,
        out_shape=jax.ShapeDtypeStruct((P, D), F32),
        compiler_params=_params("arbitrary"),
        name="moe_experts",
    )(blk_e, n_used, xs, w_gate, w_up, w_down)


def _combine_kernel(dest_ref, dest_next_ref, x_ref, mod_ref, p_ref, g_ref, ys_ref, o_ref, ybuf_ref, sems):
    i = pl.program_id(0)
    tm = x_ref.shape[0]
    slot = i % 2

    def gather(d_ref, s):
        def issue(r, _):
            for k in range(2):
                pltpu.make_async_copy(ys_ref.at[pl.ds(d_ref[0, 0, k * tm + r], 1)],
                                      ybuf_ref.at[s, k, pl.ds(r, 1)], sems.at[s]).start()
            return 0

        lax.fori_loop(0, tm, issue, 0, unroll=8)

    pl.when(i == 0)(lambda: gather(dest_ref, 0))
    pl.when(i + 1 < pl.num_programs(0))(lambda: gather(dest_next_ref, 1 - slot))
    for k in range(2):
        pltpu.make_async_copy(ys_ref.at[pl.ds(0, tm)], ybuf_ref.at[slot, k], sems.at[slot]).wait()
    p = p_ref[...]
    f = ybuf_ref[slot, 0] * p[:, 0:1] + ybuf_ref[slot, 1] * p[:, 1:2]
    x2 = x_ref[...] + mod_ref[0, 5:6, :] * f
    o_ref[...] = _rms(x2, g_ref[...])


def _combine(x1, mod, probs, final_g, ys, dest):
    T, D = x1.shape
    tm = TM_COMB
    per_b = SEQ // tm
    n = T // tm
    dest_t = dest.reshape(2, n, tm).transpose(1, 0, 2).reshape(n, 1, 2 * tm)
    return pl.pallas_call(
        _combine_kernel,
        grid=(n,),
        in_specs=[pl.BlockSpec((1, 1, 2 * tm), lambda i: (i, 0, 0), memory_space=pltpu.SMEM),
                  pl.BlockSpec((1, 1, 2 * tm), lambda i: (jnp.minimum(i + 1, n - 1), 0, 0),
                               memory_space=pltpu.SMEM),
                  pl.BlockSpec((tm, D), lambda i: (i, 0)),
                  pl.BlockSpec((1, 6, D), lambda i: (i // per_b, 0, 0)),
                  pl.BlockSpec((tm, 2), lambda i: (i, 0)),
                  pl.BlockSpec((1, D), lambda i: (0, 0)),
                  pl.BlockSpec(memory_space=pl.ANY)],
        out_specs=pl.BlockSpec((tm, D), lambda i: (i, 0)),
        out_shape=jax.ShapeDtypeStruct((T, D), F32),
        scratch_shapes=[pltpu.VMEM((2, 2, tm, D), F32), pltpu.SemaphoreType.DMA((2,))],
        compiler_params=_params("arbitrary"),
        name="moe_combine",
    )(dest_t, dest_t, x1, mod, probs, final_g, ys)


def _moe_layer(x1, h2, h2_packed, mod, w_router, w_gate, w_up, w_down, final_g):
    T, D = x1.shape
    e_idx, probs, rank, counts = _router(h2, w_router.T.astype(BF16))
    counts = counts[:, 0].astype(jnp.int32)
    padded = (counts + TM_MOE - 1) // TM_MOE * TM_MOE
    pad_end = jnp.cumsum(padded)
    pad_start = pad_end - padded
    eid = jnp.arange(N_EXPERTS, dtype=jnp.int32)[:, None, None]
    dest = jnp.sum(jnp.where(e_idx[None, :2] == eid, pad_start[:, None, None], 0), axis=0) + rank[:2]
    n_slots = 2 * T + N_EXPERTS * TM_MOE
    n_blk = n_slots // TM_MOE
    blk_e = jnp.minimum(jnp.searchsorted(pad_end, jnp.arange(n_blk, dtype=jnp.int32) * TM_MOE,
                                         side='right'), N_EXPERTS - 1).astype(jnp.int32)
    n_used = (pad_end[-1:] // TM_MOE).astype(jnp.int32)
    xs = _dispatch(h2_packed, dest, n_slots)
    ys = _experts(xs, blk_e, n_used, w_gate.astype(BF16), w_up.astype(BF16), w_down.astype(BF16))
    return _combine(x1, mod, probs[:2].T, final_g.reshape(1, -1), ys, dest)


def _final_norm_kernel(x_ref, g_ref, o_ref):
    o_ref[...] = _rms(x_ref[...], g_ref[...])


def _final_norm(x, g):
    T, D = x.shape
    tm = 1024
    return pl.pallas_call(
        _final_norm_kernel,
        grid=(T // tm,),
        in_specs=[pl.BlockSpec((tm, D), lambda i: (i, 0)), pl.BlockSpec((1, D), lambda i: (0, 0))],
        out_specs=pl.BlockSpec((tm, D), lambda i: (i, 0)),
        out_shape=jax.ShapeDtypeStruct((T, D), F32),
        compiler_params=_params("arbitrary"),
        name="final_norm",
    )(x, g.reshape(1, -1))


def _regroup_w_in(w):
    ab = jnp.pad(w[:, 4608:4616], ((0, 0), (0, LANES - 8)))
    return jnp.concatenate([w[:, :4608], w[:, 4616:], ab], axis=1).astype(BF16)


def _lambda_init(layer):
    return 0.8 - 0.6 * math.exp(-0.3 * layer)


def kernel(x, c, positions, norm1_g, norm2_g, w_mod, b_mod, w_in, gm_ln_g, gm_ln_b, gm_w_s, gm_b_s, da_lambda, da_subln_g, dn_conv_w, dn_a_log, dn_dt_bias, dn_norm_g, w_br_gm, w_br_da, w_br_dn, w_out, ffn_w_gate, ffn_w_up, ffn_w_down, moe_w_router, moe_w_gate, moe_w_up, moe_w_down, final_g):
    B, S, D = x.shape
    T = B * S
    xt = x.reshape(T, D)
    mod_all = _modulation(c, w_mod, b_mod).reshape(DEPTH, B, 6, D)
    cos_t, sin_t = _rope_tables(positions)
    for layer in range(DEPTH):
        mod = mod_all[layer]
        gm, qk, v, dn, gates, ab = _in_projection(xt, mod, norm1_g[layer].reshape(1, D),
                                                  _regroup_w_in(w_in[layer]), cos_t, sin_t, dn_conv_w[layer])
        y_gm = _gmlp(gm, gm_ln_g[layer], gm_ln_b[layer], gm_w_s[layer], gm_b_s[layer])
        y_da = _diff_attention(qk, v, da_lambda[layer], da_subln_g[layer], _lambda_init(layer), B)
        y_dn = _deltanet(dn, ab, dn_a_log[layer], dn_dt_bias[layer], dn_norm_g[layer], B)
        moe = layer % 2 == 1
        x1, h2, *packed = _merge(xt, mod, norm2_g[layer].reshape(1, D), y_gm, y_da, y_dn, gates,
                                 w_br_gm[layer].astype(BF16), w_br_da[layer].astype(BF16),
                                 w_br_dn[layer].astype(BF16), w_out[layer].astype(BF16), moe)
        if moe:
            xt = _moe_layer(x1, h2, packed[0], mod, moe_w_router[layer // 2], moe_w_gate[layer // 2],
                            moe_w_up[layer // 2], moe_w_down[layer // 2], final_g)
        else:
            xt = _dense_ffn(x1, h2, mod, ffn_w_gate[layer // 2].astype(BF16),
                            ffn_w_up[layer // 2].astype(BF16), ffn_w_down[layer // 2].astype(BF16))
    if DEPTH % 2 == 1:
        xt = _final_norm(xt, final_g)
    return xt.reshape(B, S, D)
```

```python
import functools
import math

import jax
import jax.numpy as jnp
import numpy as np
from jax import lax
from jax.experimental import pallas as pl
from jax.experimental.pallas import tpu as pltpu

D_MODEL = 1024
SEQ = 2048
DEPTH = 2
CHUNK = 64
EPS = 1e-6
GM_WIDTH = D_MODEL // 2
GM_GROUPS = 4
GM_BLOCK = 128
DA_HEADS = 4
DA_HEAD_DIM = 64
DA_V_DIM = 2 * DA_HEAD_DIM
ROPE_THETA = 10000.0
DN_HEADS = 4
DN_HEAD_DIM = 128
DN_WIDTH = DN_HEADS * DN_HEAD_DIM
DN_CONV = 4
N_EXPERTS = 8
FF_EXPERT = 7 * D_MODEL // 2

LANES = 128
VMEM_LIMIT = 56 * 1024 * 1024

C_GM, C_QK, C_V, C_DN, C_GATE, C_AB = 0, 1024, 2048, 2560, 4608, 7680
W_ALL = C_AB + LANES
PROJ_CHUNK = 512

TM_PROJ = 512
TM_GM = 512
TQ = 256
TK = 256
TM_MERGE = 512
TM_FFN = 512
TM_ROUTE = 512
TM_MOE = 512
TM_DISP = 512
DISP_SLOTS = 3
TM_COMB = 512
ATTN_LOOKAHEAD = 2
VT_ROWS = DA_V_DIM + 16
DN_SEQS = 2
DN_BLOCK = 256

BF16 = jnp.bfloat16
F32 = jnp.float32


def _params(*sem):
    return pltpu.CompilerParams(dimension_semantics=sem, vmem_limit_bytes=VMEM_LIMIT)


def _dot(a, b):
    return jnp.dot(a, b, preferred_element_type=F32)


def _dot_nt(a, b):
    return lax.dot_general(a, b, (((1,), (1,)), ((), ())), preferred_element_type=F32)


def _dot_tn(a, b):
    return lax.dot_general(a, b, (((0,), (0,)), ((), ())), preferred_element_type=F32)


def _rms(x, g):
    return x * lax.rsqrt(jnp.mean(x * x, axis=-1, keepdims=True) + EPS) * g


def _silu(x):
    return x * jax.nn.sigmoid(x)


def _mod_kernel(c_ref, w_ref, b_ref, o_ref):
    c = c_ref[...]
    o_ref[0] = _dot(_silu(c).astype(BF16), w_ref[0].astype(BF16)) + b_ref[0]


def _modulation(c, w_mod, b_mod):
    B, D = c.shape
    L, _, N = w_mod.shape
    tn = 1536
    return pl.pallas_call(
        _mod_kernel,
        grid=(L, N // tn),
        in_specs=[pl.BlockSpec((B, D), lambda l, j: (0, 0)),
                  pl.BlockSpec((1, D, tn), lambda l, j: (l, 0, j)),
                  pl.BlockSpec((1, 1, tn), lambda l, j: (l, 0, j))],
        out_specs=pl.BlockSpec((1, B, tn), lambda l, j: (l, 0, j)),
        out_shape=jax.ShapeDtypeStruct((L, B, N), F32),
        compiler_params=_params("arbitrary", "arbitrary"),
        name="modulation",
    )(c, w_mod, b_mod.reshape(L, 1, N))


def _rope_tab_kernel(pos_ref, inv_ref, sgn_ref, cos_ref, sin_ref):
    ang = pos_ref[...].astype(F32) * inv_ref[...]
    cos_ref[...] = jnp.cos(ang)
    sin_ref[...] = jnp.sin(ang) * sgn_ref[...]


def _rope_tables(positions):
    T = positions.size
    inv_freq = ROPE_THETA ** (-jnp.arange(0, DA_HEAD_DIM, 2, dtype=F32) / DA_HEAD_DIM)
    inv = jnp.tile(inv_freq, LANES // (DA_HEAD_DIM // 2)).reshape(1, LANES)
    half = DA_HEAD_DIM // 2
    sgn = np.tile(np.concatenate([-np.ones(half), np.ones(half)]), LANES // DA_HEAD_DIM)
    sgn = jnp.asarray(sgn, F32).reshape(1, LANES)
    tm = 1024
    return pl.pallas_call(
        _rope_tab_kernel,
        grid=(T // tm,),
        in_specs=[pl.BlockSpec((tm, 1), lambda i: (i, 0)),
                  pl.BlockSpec((1, LANES), lambda i: (0, 0)),
                  pl.BlockSpec((1, LANES), lambda i: (0, 0))],
        out_specs=[pl.BlockSpec((tm, LANES), lambda i: (i, 0))] * 2,
        out_shape=[jax.ShapeDtypeStruct((T, LANES), F32)] * 2,
        compiler_params=_params("arbitrary"),
        name="rope_tables",
    )(positions.reshape(T, 1), inv, sgn)


def _rope_rows(x, cos, sin):
    lane = lax.broadcasted_iota(jnp.int32, x.shape, 1)
    first = (lane % DA_HEAD_DIM) < (DA_HEAD_DIM // 2)
    partner = jnp.where(first, pltpu.roll(x, LANES - DA_HEAD_DIM // 2, axis=1),
                        pltpu.roll(x, DA_HEAD_DIM // 2, axis=1))
    return x * cos + partner * sin


def _proj_kernel(x_ref, mod_ref, g_ref, w_ref, cos_ref, sin_ref, cw_ref,
                 gm_ref, qk_ref, v_ref, dn_ref, gate_ref, ab_ref, prev_ref):
    tm = x_ref.shape[0]

    @pl.when(pl.program_id(0) % (SEQ // tm) == 0)
    def _():
        prev_ref[...] = jnp.zeros_like(prev_ref)

    def conv_silu(y, idx):
        cw = cw_ref[:, idx * PROJ_CHUNK:(idx + 1) * PROJ_CHUNK]
        ycat = jnp.concatenate([prev_ref[idx], y], axis=0)
        out = y * cw[DN_CONV - 1:DN_CONV, :]
        for j in range(DN_CONV - 1):
            out = out + pltpu.roll(ycat, DN_CONV - 1 - j, axis=0)[8:, :] * cw[j:j + 1, :]
        prev_ref[idx] = y[tm - 8:, :]
        return _silu(out)

    def l2norm_heads(t, mult):
        parts = []
        for j in range(PROJ_CHUNK // DN_HEAD_DIM):
            seg = t[:, j * DN_HEAD_DIM:(j + 1) * DN_HEAD_DIM]
            parts.append(seg * (lax.rsqrt(jnp.sum(seg * seg, axis=-1, keepdims=True) + EPS) * mult))
        return jnp.concatenate(parts, axis=1)

    x = x_ref[...]
    shift = mod_ref[0, 0:1, :]
    scale = mod_ref[0, 1:2, :]
    h = (_rms(x, g_ref[...]) * (1.0 + scale) + shift).astype(BF16)
    cos = cos_ref[...]
    sin = sin_ref[...]
    q_scale = DA_HEAD_DIM ** -0.5 * math.log2(math.e)

    def rope_chunk(y, mult):
        parts = [_rope_rows(y[:, j * LANES:(j + 1) * LANES], cos, sin) * mult
                 for j in range(PROJ_CHUNK // LANES)]
        return jnp.concatenate(parts, axis=1)

    dn_chunks = list(range(C_DN, C_GATE, PROJ_CHUNK))
    gate_chunks = list(range(C_GATE, C_AB, PROJ_CHUNK))
    order = list(range(0, C_DN, PROJ_CHUNK))
    for j, c in enumerate(dn_chunks):
        order += [c] + gate_chunks[2 * j:2 * j + 2]
    assert sorted(order) == list(range(0, C_AB, PROJ_CHUNK))
    for c0 in order:
        y = _dot(h, w_ref[:, c0:c0 + PROJ_CHUNK])
        if c0 < C_QK:
            gm_ref[:, c0:c0 + PROJ_CHUNK] = y.astype(BF16)
        elif c0 < C_V:
            mult = q_scale if c0 == C_QK else 1.0
            qk_ref[:, c0 - C_QK:c0 - C_QK + PROJ_CHUNK] = rope_chunk(y, mult).astype(BF16)
        elif c0 < C_DN:
            v_ref[...] = y.astype(BF16)
        elif c0 < C_GATE:
            idx = (c0 - C_DN) // PROJ_CHUNK
            if idx < 3:
                y = conv_silu(y, idx)
            if idx < 2:
                y = l2norm_heads(y, DN_HEAD_DIM ** -0.5 if idx == 0 else 1.0)
            dn_ref[:, c0 - C_DN:c0 - C_DN + PROJ_CHUNK] = y.astype(BF16)
        else:
            gate_ref[:, c0 - C_GATE:c0 - C_GATE + PROJ_CHUNK] = y.astype(BF16)
    ab_ref[...] = _dot(h, w_ref[:, C_AB:W_ALL])


def _in_projection(x, mod, g, w_all, cos_t, sin_t, conv_w):
    T, D = x.shape
    tm = TM_PROJ
    per_b = SEQ // tm
    row = lambda i: (i, 0)
    widths = (C_QK - C_GM, C_V - C_QK, C_DN - C_V, C_GATE - C_DN, C_AB - C_GATE)
    return pl.pallas_call(
        _proj_kernel,
        grid=(T // tm,),
        in_specs=[pl.BlockSpec((tm, D), row),
                  pl.BlockSpec((1, 6, D), lambda i: (i // per_b, 0, 0)),
                  pl.BlockSpec((1, D), lambda i: (0, 0)),
                  pl.BlockSpec((D, W_ALL), lambda i: (0, 0), pipeline_mode=pl.Buffered(1)),
                  pl.BlockSpec((tm, LANES), row),
                  pl.BlockSpec((tm, LANES), row),
                  pl.BlockSpec((DN_CONV, 3 * DN_WIDTH), lambda i: (0, 0))],
        out_specs=[pl.BlockSpec((tm, w), row) for w in widths] + [pl.BlockSpec((tm, LANES), row)],
        out_shape=[jax.ShapeDtypeStruct((T, w), BF16) for w in widths]
                  + [jax.ShapeDtypeStruct((T, LANES), F32)],
        scratch_shapes=[pltpu.VMEM((3, 8, PROJ_CHUNK), F32)],
        compiler_params=_params("arbitrary"),
        name="in_projection",
    )(x, mod, g, w_all, cos_t, sin_t, conv_w)


def _gelu(x):
    return 0.5 * x * (1.0 + lax.erf(x * np.float32(math.sqrt(0.5))))


def _gmlp_kernel(uv_ref, lng_ref, lnb_ref, ws_ref, bs_ref, o_ref):
    u = _gelu(uv_ref[:, :GM_WIDTH].astype(F32))
    v = _gelu(uv_ref[:, GM_WIDTH:].astype(F32))
    mu = jnp.mean(v, axis=-1, keepdims=True)
    vc = v - mu
    var = jnp.mean(vc * vc, axis=-1, keepdims=True)
    v = (vc * lax.rsqrt(var + EPS) * lng_ref[...] + lnb_ref[...]).astype(BF16)
    ri = lax.broadcasted_iota(jnp.int32, (GM_BLOCK, GM_BLOCK), 0) // CHUNK
    ci = lax.broadcasted_iota(jnp.int32, (GM_BLOCK, GM_BLOCK), 1) // CHUNK
    allowed = ci <= ri
    gc = GM_WIDTH // GM_GROUPS
    for g in range(GM_GROUPS):
        w = jnp.where(allowed, ws_ref[g], 0.0).astype(BF16)
        bias = bs_ref[g]
        for r in range(uv_ref.shape[0] // GM_BLOCK):
            rows = slice(r * GM_BLOCK, (r + 1) * GM_BLOCK)
            cols = slice(g * gc, (g + 1) * gc)
            sv = _dot(w, v[rows, cols]) + bias
            o_ref[rows, cols] = (u[rows, cols] * sv).astype(BF16)


def _gmlp(uv, ln_g, ln_b, w_s, b_s):
    T = uv.shape[0]
    tm = TM_GM
    return pl.pallas_call(
        _gmlp_kernel,
        grid=(T // tm,),
        in_specs=[pl.BlockSpec((tm, 2 * GM_WIDTH), lambda i: (i, 0)),
                  pl.BlockSpec((1, GM_WIDTH), lambda i: (0, 0)),
                  pl.BlockSpec((1, GM_WIDTH), lambda i: (0, 0)),
                  pl.BlockSpec((GM_GROUPS, GM_BLOCK, GM_BLOCK), lambda i: (0, 0, 0)),
                  pl.BlockSpec((GM_GROUPS, GM_BLOCK, 1), lambda i: (0, 0, 0))],
        out_specs=pl.BlockSpec((tm, GM_WIDTH), lambda i: (i, 0)),
        out_shape=jax.ShapeDtypeStruct((T, GM_WIDTH), BF16),
        compiler_params=_params("arbitrary"),
        name="gmlp",
    )(uv, ln_g.reshape(1, -1), ln_b.reshape(1, -1), w_s, b_s.reshape(GM_GROUPS, GM_BLOCK, 1))


def _attn_kernel(lam_ref, g_ref, q_ref, k_ref, v_ref, o_ref, qz_ref, vt_ref, *stats, lambda_init):
    nc = 2 * DA_HEADS
    m_ref, acc_ref, s_ref = (stats[j * nc:(j + 1) * nc] for j in range(3))
    i = pl.program_id(1)
    lp = lam_ref[...]
    lam = (jnp.exp(jnp.sum(lp[0:1] * lp[1:2], axis=-1, keepdims=True))
           - jnp.exp(jnp.sum(lp[2:3] * lp[3:4], axis=-1, keepdims=True)) + lambda_init)
    @pl.when(i == 0)
    def _():
        for kb in range(SEQ // TK):
            for h in range(DA_HEADS):
                vt_ref[h * VT_ROWS:h * VT_ROWS + DA_V_DIM, kb * TK:(kb + 1) * TK] = (
                    v_ref[kb * TK:(kb + 1) * TK, h * LANES:(h + 1) * LANES].T)
        for h in range(DA_HEADS):
            vt_ref[h * VT_ROWS + DA_V_DIM:(h + 1) * VT_ROWS, :] = jnp.ones((VT_ROWS - DA_V_DIM, SEQ), BF16)

    row = lax.broadcasted_iota(jnp.int32, (LANES, TQ), 0)
    for h in range(DA_HEADS):
        qt = q_ref[:, h * LANES:(h + 1) * LANES].T
        zero = jnp.zeros_like(qt)
        qz_ref[2 * h] = jnp.where(row < DA_HEAD_DIM, qt, zero)
        qz_ref[2 * h + 1] = jnp.where(row >= DA_HEAD_DIM, qt, zero)
    for c in range(nc):
        m_ref[c][...] = jnp.full_like(m_ref[c], -jnp.inf)
        acc_ref[c][...] = jnp.zeros_like(acc_ref[c])

    def key_rows(kb):
        return pl.ds(pl.multiple_of(kb * TK, TK), TK)

    def scores(kb, c, masked):
        h = c // 2
        s = _dot(k_ref[key_rows(kb), h * LANES:(h + 1) * LANES], qz_ref[c])
        if masked:
            ki = lax.broadcasted_iota(jnp.int32, s.shape, 0) // CHUNK + kb * (TK // CHUNK)
            qi = lax.broadcasted_iota(jnp.int32, s.shape, 1) // CHUNK + i * (TQ // CHUNK)
            s = jnp.where(ki <= qi, s, -jnp.inf)
        return s

    def accumulate(kb, c, s):
        h = c // 2
        m_old = m_ref[c][...]
        m_new = jnp.maximum(m_old, jnp.max(s, axis=0, keepdims=True))
        alpha = jnp.exp2(m_old - m_new)
        e = jnp.exp2((s - m_new).astype(BF16))
        vt = vt_ref[h * VT_ROWS:(h + 1) * VT_ROWS, key_rows(kb)]
        acc_ref[c][...] = alpha * acc_ref[c][...] + _dot(vt, e)
        m_ref[c][...] = m_new

    def fold_and_prefetch(kb, masked_next):
        fresh = {}
        for c in range(nc + ATTN_LOOKAHEAD):
            if c < nc:
                fresh[c] = scores(kb + 1, c, masked_next)
            if c >= ATTN_LOOKAHEAD:
                cc = c - ATTN_LOOKAHEAD
                accumulate(kb, cc, s_ref[cc][...])
                s_ref[cc][...] = fresh.pop(cc)

    first_masked = i * (TQ // TK)

    @pl.when(i == 0)
    def _():
        for c in range(nc):
            s_ref[c][...] = scores(0, c, True)

    @pl.when(i > 0)
    def _():
        for c in range(nc):
            s_ref[c][...] = scores(0, c, False)

        def body(kb, carry):
            fold_and_prefetch(kb, False)
            return carry

        lax.fori_loop(0, first_masked - 1, body, 0)
        fold_and_prefetch(first_masked - 1, True)

    for r in range(TQ // TK - 1):
        fold_and_prefetch(first_masked + r, True)
    for c in range(nc):
        accumulate(first_masked + TQ // TK - 1, c, s_ref[c][...])
    for h in range(DA_HEADS):
        num0, den0 = acc_ref[2 * h][:DA_V_DIM, :], acc_ref[2 * h][DA_V_DIM:DA_V_DIM + 1, :]
        num1, den1 = acc_ref[2 * h + 1][:DA_V_DIM, :], acc_ref[2 * h + 1][DA_V_DIM:DA_V_DIM + 1, :]
        o = num0 / den0 - lam * (num1 / den1)
        ms = jnp.mean(o * o, axis=0, keepdims=True)
        o = o * lax.rsqrt(ms + EPS) * (g_ref[...] * (1.0 - lambda_init))
        o_ref[:, h * LANES:(h + 1) * LANES] = o.T.astype(BF16)


def _diff_attention(qk, v, lam_p, subln_g, lambda_init, batch):
    T = qk.shape[0]
    nq = SEQ // TQ
    W = DA_HEADS * DA_V_DIM
    kernel = functools.partial(_attn_kernel, lambda_init=lambda_init)
    return pl.pallas_call(
        kernel,
        grid=(batch, nq),
        in_specs=[pl.BlockSpec((4, DA_HEAD_DIM), lambda b, i: (0, 0)),
                  pl.BlockSpec((DA_V_DIM, 1), lambda b, i: (0, 0)),
                  pl.BlockSpec((TQ, W), lambda b, i: (b * nq + i, 0)),
                  pl.BlockSpec((SEQ, W), lambda b, i: (b, 1)),
                  pl.BlockSpec((SEQ, W), lambda b, i: (b, 0))],
        out_specs=pl.BlockSpec((TQ, W), lambda b, i: (b * nq + i, 0)),
        out_shape=jax.ShapeDtypeStruct((T, W), BF16),
        scratch_shapes=[pltpu.VMEM((2 * DA_HEADS, LANES, TQ), BF16),
                        pltpu.VMEM((DA_HEADS * VT_ROWS, SEQ), BF16)]
                       + [pltpu.VMEM((1, TQ), F32)] * (2 * DA_HEADS)
                       + [pltpu.VMEM((VT_ROWS, TQ), F32)] * (2 * DA_HEADS)
                       + [pltpu.VMEM((TK, TQ), F32)] * (2 * DA_HEADS),
        compiler_params=_params("arbitrary", "arbitrary"),
        name="diff_attention",
    )(lam_p, subln_g.reshape(-1, 1), qk, qk, v)


def _split3(x):
    hi = x.astype(BF16)
    r = x - hi.astype(F32)
    mid = r.astype(BF16)
    lo = (r - mid.astype(F32)).astype(BF16)
    return hi, mid, lo


def _dn_kernel(dn_ref, a_ref, at_ref, alog_ref, dtb_ref, alogt_ref, dtbt_ref, ng_ref,
               o_ref, *state_ref):
    c = pl.program_id(1)
    R = DN_BLOCK
    G = dn_ref.shape[0]

    @pl.when(c == 0)
    def _():
        for ref in state_ref:
            ref[...] = jnp.zeros_like(ref)

    W3 = 3 * DN_WIDTH
    ri = lax.broadcasted_iota(jnp.int32, (R, R), 0)
    ci = lax.broadcasted_iota(jnp.int32, (R, R), 1)
    tril = ri >= ci
    strict = ri > ci
    ones_tril = jnp.where(tril, 1.0, 0.0).astype(BF16)
    ones_triu = jnp.where(ci >= ri, 1.0, 0.0).astype(BF16)
    eye = jnp.where(ri == ci, 1.0, 0.0)
    same = {n: (ri // n) == (ci // n) for n in (16, 32, 64, 128)}
    gc_col, gc_row, beta_all = [], [], []
    for j in range(G):
        ab = a_ref[j]
        g_col = -jnp.exp(alog_ref[...]) * jax.nn.softplus(ab + dtb_ref[...])
        beta_all.append(jax.nn.sigmoid(ab))
        abt = at_ref[j, 0]
        g_row = -jnp.exp(alogt_ref[...]) * jax.nn.softplus(abt + dtbt_ref[...])
        gc_col.append(sum(_dot(ones_tril, part) for part in _split3(g_col)))
        gc_row.append(sum(_dot(part, ones_triu) for part in _split3(g_row)))

    H = range(G * DN_HEADS)
    seq = [n // DN_HEADS for n in H]
    hd = [n % DN_HEADS for n in H]
    hsl = [slice(hd[n] * DN_HEAD_DIM, (hd[n] + 1) * DN_HEAD_DIM) for n in H]
    q16 = [dn_ref[seq[n], :, hsl[n]] for n in H]
    k16 = [dn_ref[seq[n], :, DN_WIDTH + hd[n] * DN_HEAD_DIM:DN_WIDTH + (hd[n] + 1) * DN_HEAD_DIM] for n in H]
    q = [t.astype(F32) for t in q16]
    k = [t.astype(F32) for t in k16]
    v = [dn_ref[seq[n], :, 2 * DN_WIDTH + hd[n] * DN_HEAD_DIM:2 * DN_WIDTH + (hd[n] + 1) * DN_HEAD_DIM]
         .astype(F32) for n in H]
    gc = [gc_col[seq[n]][:, hd[n]:hd[n] + 1] for n in H]
    beta = [beta_all[seq[n]][:, DN_HEADS + hd[n]:DN_HEADS + hd[n] + 1] for n in H]
    g_last = [gc_col[seq[n]][R - 1:R, hd[n]:hd[n] + 1] for n in H]
    decay = [jnp.exp(jnp.where(tril, gc[n] - gc_row[seq[n]][hd[n]:hd[n] + 1, :], -jnp.inf)) for n in H]
    eg = [jnp.exp(t) for t in gc]
    kb = [k[h] * beta[h] for h in H]
    s1 = [_dot_nt(jnp.concatenate([kb[h].astype(BF16), q16[h]], axis=0), k16[h]) for h in H]
    a_mat = [jnp.where(strict, s1[h][:R] * decay[h], 0.0) for h in H]
    qk = [(s1[h][R:] * decay[h]).astype(BF16) for h in H]
    p = [jnp.where(same[16], -t, 0.0) for t in a_mat]
    t_inv = [eye + t for t in p]
    for _ in range(3):
        p16 = [t.astype(BF16) for t in p]
        p = [_dot(t, t) for t in p16]
        t_inv = [t_inv[h] + _dot(t_inv[h].astype(BF16), p[h].astype(BF16)) for h in H]
    a16 = [t.astype(BF16) for t in a_mat]
    t16 = [t.astype(BF16) for t in t_inv]
    for n in (16, 32, 64, 128):
        off = jnp.logical_not(same[n])
        if 2 * n < R:
            off = jnp.logical_and(same[2 * n], off)
        off16 = jnp.where(off, 1.0, 0.0).astype(BF16)
        tl = [_dot(t16[h], a16[h] * off16).astype(BF16) for h in H]
        t16 = [t16[h] - _dot(tl[h], t16[h]).astype(BF16) * off16 for h in H]
    sol = [_dot(t16[h], jnp.concatenate([v[h] * beta[h], kb[h] * eg[h]], axis=1).astype(BF16))
           for h in H]
    state = [state_ref[h][...] for h in H]
    m1 = [_dot(jnp.concatenate([sol[h][:, DN_HEAD_DIM:], q[h] * eg[h]], axis=0).astype(BF16),
               state[h].astype(BF16)) for h in H]
    vn16 = [(sol[h][:, :DN_HEAD_DIM] - m1[h][:R]).astype(BF16) for h in H]
    o = [m1[h][R:] + _dot(qk[h], vn16[h]) for h in H]
    kd = [(k[h] * jnp.exp(g_last[h] - gc[h])).astype(BF16) for h in H]
    for h in H:
        state_ref[h][...] = state[h] * jnp.exp(g_last[h]) + _dot_tn(kd[h], vn16[h])
    for n in H:
        z = dn_ref[seq[n], :, W3 + hd[n] * DN_HEAD_DIM:W3 + (hd[n] + 1) * DN_HEAD_DIM].astype(F32)
        o_ref[seq[n], :, hsl[n]] = (_rms(o[n], ng_ref[...]) * _silu(z)).astype(BF16)


def _deltanet(dn, ab, a_log, dt_bias, norm_g, batch):
    T = dn.shape[0]
    R = DN_BLOCK
    n = SEQ // R
    G = DN_SEQS if batch % DN_SEQS == 0 else 1
    pad = lambda t: jnp.zeros((1, LANES), F32).at[0, :DN_HEADS].set(t)
    col8 = lambda t: jnp.zeros((8, 1), F32).at[:DN_HEADS, 0].set(t)
    ab_t = ab[:, :8].reshape(batch, n, R, 8).transpose(0, 1, 3, 2)
    const = lambda b, c: (0, 0)
    out = pl.pallas_call(
        _dn_kernel,
        grid=(batch // G, n),
        in_specs=[pl.BlockSpec((G, R, 4 * DN_WIDTH), lambda b, c: (b, c, 0)),
                  pl.BlockSpec((G, R, LANES), lambda b, c: (b, c, 0)),
                  pl.BlockSpec((G, 1, 8, R), lambda b, c: (b, c, 0, 0)),
                  pl.BlockSpec((1, LANES), const),
                  pl.BlockSpec((1, LANES), const),
                  pl.BlockSpec((8, 1), const),
                  pl.BlockSpec((8, 1), const),
                  pl.BlockSpec((1, DN_HEAD_DIM), const)],
        out_specs=pl.BlockSpec((G, R, DN_WIDTH), lambda b, c: (b, c, 0)),
        out_shape=jax.ShapeDtypeStruct((batch, SEQ, DN_WIDTH), BF16),
        scratch_shapes=[pltpu.VMEM((DN_HEAD_DIM, DN_HEAD_DIM), F32)] * (G * DN_HEADS),
        compiler_params=_params("arbitrary", "arbitrary"),
        name="deltanet",
    )(dn.reshape(batch, SEQ, -1), ab.reshape(batch, SEQ, -1), ab_t, pad(a_log), pad(dt_bias),
      col8(a_log), col8(dt_bias), norm_g.reshape(1, -1))
    return out.reshape(T, DN_WIDTH)


def _pack_bf16_pair(lo, hi):
    lo_bits = lax.bitcast_convert_type(lo.astype(BF16).astype(F32), jnp.uint32)
    hi_bits = lax.bitcast_convert_type(hi.astype(BF16).astype(F32), jnp.uint32)
    return (lo_bits >> 16) | (hi_bits & jnp.uint32(0xFFFF0000))


def _unpack_bf16_pair(words):
    lo = lax.bitcast_convert_type(words << 16, F32)
    hi = lax.bitcast_convert_type(words & jnp.uint32(0xFFFF0000), F32)
    return lo.astype(BF16), hi.astype(BF16)


def _merge_kernel(x_ref, mod_ref, g2_ref, ygm_ref, yda_ref, ydn_ref, gate_ref,
                  wgm_ref, wda_ref, wdn_ref, wout_ref, x1_ref, h2_ref, *maybe_packed_ref):
    D = D_MODEL
    merged = (jax.nn.sigmoid(gate_ref[:, 0:D].astype(F32)) * _dot(ygm_ref[...], wgm_ref[...])
              + jax.nn.sigmoid(gate_ref[:, D:2 * D].astype(F32)) * _dot(yda_ref[...], wda_ref[...])
              + jax.nn.sigmoid(gate_ref[:, 2 * D:3 * D].astype(F32)) * _dot(ydn_ref[...], wdn_ref[...]))
    y = _dot(merged.astype(BF16), wout_ref[...])
    x1 = x_ref[...] + mod_ref[0, 2:3, :] * y
    x1_ref[...] = x1
    h2 = _rms(x1, g2_ref[...]) * (1.0 + mod_ref[0, 4:5, :]) + mod_ref[0, 3:4, :]
    h2_ref[...] = h2.astype(BF16)
    for packed_ref in maybe_packed_ref:
        packed_ref[...] = _pack_bf16_pair(h2[:, :D // 2], h2[:, D // 2:])


def _merge(x, mod, g2, y_gm, y_da, y_dn, gates, w_gm, w_da, w_dn, w_out, with_packed):
    T, D = x.shape
    tm = TM_MERGE
    per_b = SEQ // tm
    row = lambda i: (i, 0)
    const = lambda i: (0, 0)
    out_specs = [pl.BlockSpec((tm, D), row), pl.BlockSpec((tm, D), row)]
    out_shape = [jax.ShapeDtypeStruct((T, D), F32), jax.ShapeDtypeStruct((T, D), BF16)]
    if with_packed:
        out_specs.append(pl.BlockSpec((tm, D // 2), row))
        out_shape.append(jax.ShapeDtypeStruct((T, D // 2), jnp.uint32))
    return pl.pallas_call(
        _merge_kernel,
        grid=(T // tm,),
        in_specs=[pl.BlockSpec((tm, D), row),
                  pl.BlockSpec((1, 6, D), lambda i: (i // per_b, 0, 0)),
                  pl.BlockSpec((1, D), const),
                  pl.BlockSpec((tm, GM_WIDTH), row),
                  pl.BlockSpec((tm, GM_WIDTH), row),
                  pl.BlockSpec((tm, DN_WIDTH), row),
                  pl.BlockSpec((tm, 3 * D), row),
                  pl.BlockSpec((GM_WIDTH, D), const),
                  pl.BlockSpec((GM_WIDTH, D), const),
                  pl.BlockSpec((DN_WIDTH, D), const),
                  pl.BlockSpec((D, D), const)],
        out_specs=out_specs,
        out_shape=out_shape,
        compiler_params=_params("arbitrary"),
        name="merge",
    )(x, mod, g2, y_gm, y_da, y_dn, gates, w_gm, w_da, w_dn, w_out)


def _ffn_kernel(x_ref, h_ref, mod_ref, wg_ref, wu_ref, wd_ref, o_ref):
    h = h_ref[...]
    a = _dot(h, wg_ref[...])
    b = _dot(h, wu_ref[...])
    f = _dot((_silu(a) * b).astype(BF16), wd_ref[...])
    o_ref[...] = x_ref[...] + mod_ref[0, 5:6, :] * f


def _dense_ffn(x1, h2, mod, w_gate, w_up, w_down):
    T, D = x1.shape
    F = w_gate.shape[1]
    tm = TM_FFN
    per_b = SEQ // tm
    row = lambda i: (i, 0)
    const = lambda i: (0, 0)
    return pl.pallas_call(
        _ffn_kernel,
        grid=(T // tm,),
        in_specs=[pl.BlockSpec((tm, D), row),
                  pl.BlockSpec((tm, D), row),
                  pl.BlockSpec((1, 6, D), lambda i: (i // per_b, 0, 0)),
                  pl.BlockSpec((D, F), const, pipeline_mode=pl.Buffered(1)),
                  pl.BlockSpec((D, F), const, pipeline_mode=pl.Buffered(1)),
                  pl.BlockSpec((F, D), const, pipeline_mode=pl.Buffered(1))],
        out_specs=pl.BlockSpec((tm, D), row),
        out_shape=jax.ShapeDtypeStruct((T, D), F32),
        compiler_params=_params("arbitrary"),
        name="dense_ffn",
    )(x1, h2, mod, w_gate, w_up, w_down)


def _router_kernel(h_ref, wr_ref, e_ref, p_ref, r_ref, cnt_ref, base_ref):
    i = pl.program_id(0)
    tm = h_ref.shape[0]

    @pl.when(i == 0)
    def _():
        base_ref[...] = jnp.zeros_like(base_ref)

    logits = _dot_nt(wr_ref[...], h_ref[...])
    row = lax.broadcasted_iota(jnp.int32, logits.shape, 0)
    m1 = jnp.max(logits, axis=0, keepdims=True)
    i1 = jnp.min(jnp.where(logits == m1, row, N_EXPERTS), axis=0, keepdims=True)
    rest = jnp.where(row == i1, -jnp.inf, logits)
    m2 = jnp.max(rest, axis=0, keepdims=True)
    i2 = jnp.min(jnp.where(rest == m2, row, N_EXPERTS), axis=0, keepdims=True)
    e2 = jnp.exp(m2 - m1)
    w1 = 1.0 / (1.0 + e2)
    w2 = e2 / (1.0 + e2)
    oh1 = jnp.where(row == i1, 1.0, 0.0)
    oh2 = jnp.where(row == i2, 1.0, 0.0)
    both = oh1 + oh2
    ti = lax.broadcasted_iota(jnp.int32, (tm, tm), 0)
    tj = lax.broadcasted_iota(jnp.int32, (tm, tm), 1)
    before = jnp.where(ti < tj, 1.0, 0.0).astype(BF16)
    pos = base_ref[...] + _dot(both.astype(BF16), before)
    r1 = jnp.sum(oh1 * pos, axis=0, keepdims=True)
    r2 = jnp.sum(oh2 * pos, axis=0, keepdims=True)
    base_ref[...] = base_ref[...] + jnp.sum(both, axis=1, keepdims=True)
    zi = jnp.zeros((N_EXPERTS - 2, tm), jnp.int32)
    zf = jnp.zeros((N_EXPERTS - 2, tm), F32)
    e_ref[...] = jnp.concatenate([i1, i2, zi], axis=0)
    p_ref[...] = jnp.concatenate([w1, w2, zf], axis=0)
    r_ref[...] = jnp.concatenate([r1.astype(jnp.int32), r2.astype(jnp.int32), zi], axis=0)
    cnt_ref[...] = jnp.broadcast_to(base_ref[...], cnt_ref.shape)


def _router(h2, w_router_t):
    T, D = h2.shape
    tm = TM_ROUTE
    col = lambda i: (0, i)
    return pl.pallas_call(
        _router_kernel,
        grid=(T // tm,),
        in_specs=[pl.BlockSpec((tm, D), lambda i: (i, 0)),
                  pl.BlockSpec((N_EXPERTS, D), lambda i: (0, 0))],
        out_specs=[pl.BlockSpec((N_EXPERTS, tm), col)] * 3 + [pl.BlockSpec((N_EXPERTS, LANES), lambda i: (0, 0))],
        out_shape=[jax.ShapeDtypeStruct((N_EXPERTS, T), jnp.int32),
                   jax.ShapeDtypeStruct((N_EXPERTS, T), F32),
                   jax.ShapeDtypeStruct((N_EXPERTS, T), jnp.int32),
                   jax.ShapeDtypeStruct((N_EXPERTS, LANES), F32)],
        scratch_shapes=[pltpu.VMEM((N_EXPERTS, 1), F32)],
        compiler_params=_params("arbitrary"),
        name="moe_router",
    )(h2, w_router_t)


def _dispatch_kernel(dest_ref, h_ref, xs_in_ref, xs_ref, buf_ref, sem_in, sem_out):
    del xs_in_ref
    i = pl.program_id(0)
    n = pl.num_programs(0)
    tm = buf_ref.shape[1]

    def load(t):
        s = t % DISP_SLOTS
        return pltpu.make_async_copy(h_ref.at[pl.ds(pl.multiple_of(t * tm, tm), tm)], buf_ref.at[s],
                                     sem_in.at[s])

    def wait_rows(t):
        s = t % DISP_SLOTS
        for _ in range(2):
            pltpu.make_async_copy(buf_ref.at[s], xs_ref.at[pl.ds(0, tm)], sem_out.at[s]).wait()

    @pl.when(i == 0)
    def _():
        load(0).start()
        pl.when(n > 1)(lambda: load(1).start())

    load(i).wait()
    slot = i % DISP_SLOTS

    def issue(r, _):
        for k in range(2):
            pltpu.make_async_copy(buf_ref.at[slot, pl.ds(r, 1)],
                                  xs_ref.at[pl.ds(dest_ref[0, 0, k * tm + r], 1)], sem_out.at[slot]).start()
        return 0

    lax.fori_loop(0, tm, issue, 0, unroll=8)
    pl.when(i > 0)(lambda: wait_rows(i - 1))
    pl.when(i + 2 < n)(lambda: load(i + 2).start())
    pl.when(i == n - 1)(lambda: wait_rows(i))


def _dispatch(h2, dest, n_slots):
    T, D = h2.shape
    tm = TM_DISP
    dest_t = dest.reshape(2, T // tm, tm).transpose(1, 0, 2).reshape(T // tm, 1, 2 * tm)
    xs0 = jnp.zeros((n_slots, D), h2.dtype)
    return pl.pallas_call(
        _dispatch_kernel,
        grid=(T // tm,),
        in_specs=[pl.BlockSpec((1, 1, 2 * tm), lambda i: (i, 0, 0), memory_space=pltpu.SMEM),
                  pl.BlockSpec(memory_space=pl.ANY),
                  pl.BlockSpec(memory_space=pl.ANY)],
        out_specs=pl.BlockSpec(memory_space=pl.ANY),
        out_shape=jax.ShapeDtypeStruct((n_slots, D), h2.dtype),
        scratch_shapes=[pltpu.VMEM((DISP_SLOTS, tm, D), h2.dtype),
                        pltpu.SemaphoreType.DMA((DISP_SLOTS,)), pltpu.SemaphoreType.DMA((DISP_SLOTS,))],
        input_output_aliases={2: 0},
        compiler_params=_params("arbitrary"),
        name="moe_dispatch",
    )(dest_t, h2, xs0)


def _expert_kernel(be_ref, nu_ref, x_ref, wg_ref, wu_ref, wd_ref, o_ref):
    del be_ref
    used = pl.program_id(0) < nu_ref[0]
    half = D_MODEL // 2

    @pl.when(used)
    def _():
        lo, hi = _unpack_bf16_pair(x_ref[...])
        a = _dot(lo, wg_ref[0, :half, :]) + _dot(hi, wg_ref[0, half:, :])
        b = _dot(lo, wu_ref[0, :half, :]) + _dot(hi, wu_ref[0, half:, :])
        o_ref[...] = _dot((_silu(a) * b).astype(BF16), wd_ref[0])

    @pl.when(jnp.logical_not(used))
    def _():
        o_ref[...] = jnp.zeros_like(o_ref)


def _experts(xs, blk_e, n_used, w_gate, w_up, w_down):
    P = xs.shape[0]
    _, D, F = w_gate.shape
    tm = TM_MOE
    grid_spec = pltpu.PrefetchScalarGridSpec(
        num_scalar_prefetch=2,
        grid=(P // tm,),
        in_specs=[pl.BlockSpec((tm, D // 2), lambda i, be, nu: (i, 0)),
                  pl.BlockSpec((1, D, F), lambda i, be, nu: (be[i], 0, 0), pipeline_mode=pl.Buffered(1)),
                  pl.BlockSpec((1, D, F), lambda i, be, nu: (be[i], 0, 0), pipeline_mode=pl.Buffered(1)),
                  pl.BlockSpec((1, F, D), lambda i, be, nu: (be[i], 0, 0), pipeline_mode=pl.Buffered(1))],
        out_specs=pl.BlockSpec((tm, D), lambda i, be, nu: (i, 0)))
    return pl.pallas_call(
        _expert_kernel,
        grid_spec=grid_spec,
        out_shape=jax.ShapeDtypeStruct((P, D), F32),
        compiler_params=_params("arbitrary"),
        name="moe_experts",
    )(blk_e, n_used, xs, w_gate, w_up, w_down)


def _combine_kernel(dest_ref, dest_next_ref, x_ref, mod_ref, p_ref, g_ref, ys_ref, o_ref, ybuf_ref, sems):
    i = pl.program_id(0)
    tm = x_ref.shape[0]
    slot = i % 2

    def gather(d_ref, s):
        def issue(r, _):
            for k in range(2):
                pltpu.make_async_copy(ys_ref.at[pl.ds(d_ref[0, 0, k * tm + r], 1)],
                                      ybuf_ref.at[s, k, pl.ds(r, 1)], sems.at[s]).start()
            return 0

        lax.fori_loop(0, tm, issue, 0, unroll=8)

    pl.when(i == 0)(lambda: gather(dest_ref, 0))
    pl.when(i + 1 < pl.num_programs(0))(lambda: gather(dest_next_ref, 1 - slot))
    for k in range(2):
        pltpu.make_async_copy(ys_ref.at[pl.ds(0, tm)], ybuf_ref.at[slot, k], sems.at[slot]).wait()
    p = p_ref[...]
    f = ybuf_ref[slot, 0] * p[:, 0:1] + ybuf_ref[slot, 1] * p[:, 1:2]
    x2 = x_ref[...] + mod_ref[0, 5:6, :] * f
    o_ref[...] = _rms(x2, g_ref[...])


def _combine(x1, mod, probs, final_g, ys, dest):
    T, D = x1.shape
    tm = TM_COMB
    per_b = SEQ // tm
    n = T // tm
    dest_t = dest.reshape(2, n, tm).transpose(1, 0, 2).reshape(n, 1, 2 * tm)
    return pl.pallas_call(
        _combine_kernel,
        grid=(n,),
        in_specs=[pl.BlockSpec((1, 1, 2 * tm), lambda i: (i, 0, 0), memory_space=pltpu.SMEM),
                  pl.BlockSpec((1, 1, 2 * tm), lambda i: (jnp.minimum(i + 1, n - 1), 0, 0),
                               memory_space=pltpu.SMEM),
                  pl.BlockSpec((tm, D), lambda i: (i, 0)),
                  pl.BlockSpec((1, 6, D), lambda i: (i // per_b, 0, 0)),
                  pl.BlockSpec((tm, 2), lambda i: (i, 0)),
                  pl.BlockSpec((1, D), lambda i: (0, 0)),
                  pl.BlockSpec(memory_space=pl.ANY)],
        out_specs=pl.BlockSpec((tm, D), lambda i: (i, 0)),
        out_shape=jax.ShapeDtypeStruct((T, D), F32),
        scratch_shapes=[pltpu.VMEM((2, 2, tm, D), F32), pltpu.SemaphoreType.DMA((2,))],
        compiler_params=_params("arbitrary"),
        name="moe_combine",
    )(dest_t, dest_t, x1, mod, probs, final_g, ys)


def _moe_layer(x1, h2, h2_packed, mod, w_router, w_gate, w_up, w_down, final_g):
    T, D = x1.shape
    e_idx, probs, rank, counts = _router(h2, w_router.T.astype(BF16))
    counts = counts[:, 0].astype(jnp.int32)
    padded = (counts + TM_MOE - 1) // TM_MOE * TM_MOE
    pad_end = jnp.cumsum(padded)
    pad_start = pad_end - padded
    eid = jnp.arange(N_EXPERTS, dtype=jnp.int32)[:, None, None]
    dest = jnp.sum(jnp.where(e_idx[None, :2] == eid, pad_start[:, None, None], 0), axis=0) + rank[:2]
    n_slots = 2 * T + N_EXPERTS * TM_MOE
    n_blk = n_slots // TM_MOE
    blk_e = jnp.minimum(jnp.searchsorted(pad_end, jnp.arange(n_blk, dtype=jnp.int32) * TM_MOE,
                                         side='right'), N_EXPERTS - 1).astype(jnp.int32)
    n_used = (pad_end[-1:] // TM_MOE).astype(jnp.int32)
    xs = _dispatch(h2_packed, dest, n_slots)
    ys = _experts(xs, blk_e, n_used, w_gate.astype(BF16), w_up.astype(BF16), w_down.astype(BF16))
    return _combine(x1, mod, probs[:2].T, final_g.reshape(1, -1), ys, dest)


def _final_norm_kernel(x_ref, g_ref, o_ref):
    o_ref[...] = _rms(x_ref[...], g_ref[...])


def _final_norm(x, g):
    T, D = x.shape
    tm = 1024
    return pl.pallas_call(
        _final_norm_kernel,
        grid=(T // tm,),
        in_specs=[pl.BlockSpec((tm, D), lambda i: (i, 0)), pl.BlockSpec((1, D), lambda i: (0, 0))],
        out_specs=pl.BlockSpec((tm, D), lambda i: (i, 0)),
        out_shape=jax.ShapeDtypeStruct((T, D), F32),
        compiler_params=_params("arbitrary"),
        name="final_norm",
    )(x, g.reshape(1, -1))


def _regroup_w_in(w):
    ab = jnp.pad(w[:, 4608:4616], ((0, 0), (0, LANES - 8)))
    return jnp.concatenate([w[:, :4608], w[:, 4616:], ab], axis=1).astype(BF16)


def _lambda_init(layer):
    return 0.8 - 0.6 * math.exp(-0.3 * layer)


def kernel(x, c, positions, norm1_g, norm2_g, w_mod, b_mod, w_in, gm_ln_g, gm_ln_b, gm_w_s, gm_b_s, da_lambda, da_subln_g, dn_conv_w, dn_a_log, dn_dt_bias, dn_norm_g, w_br_gm, w_br_da, w_br_dn, w_out, ffn_w_gate, ffn_w_up, ffn_w_down, moe_w_router, moe_w_gate, moe_w_up, moe_w_down, final_g):
    B, S, D = x.shape
    T = B * S
    xt = x.reshape(T, D)
    mod_all = _modulation(c, w_mod, b_mod).reshape(DEPTH, B, 6, D)
    cos_t, sin_t = _rope_tables(positions)
    for layer in range(DEPTH):
        mod = mod_all[layer]
        gm, qk, v, dn, gates, ab = _in_projection(xt, mod, norm1_g[layer].reshape(1, D),
                                                  _regroup_w_in(w_in[layer]), cos_t, sin_t, dn_conv_w[layer])
        y_gm = _gmlp(gm, gm_ln_g[layer], gm_ln_b[layer], gm_w_s[layer], gm_b_s[layer])
        y_da = _diff_attention(qk, v, da_lambda[layer], da_subln_g[layer], _lambda_init(layer), B)
        y_dn = _deltanet(dn, ab, dn_a_log[layer], dn_dt_bias[layer], dn_norm_g[layer], B)
        moe = layer % 2 == 1
        x1, h2, *packed = _merge(xt, mod, norm2_g[layer].reshape(1, D), y_gm, y_da, y_dn, gates,
                                 w_br_gm[layer].astype(BF16), w_br_da[layer].astype(BF16),
                                 w_br_dn[layer].astype(BF16), w_out[layer].astype(BF16), moe)
        if moe:
            xt = _moe_layer(x1, h2, packed[0], mod, moe_w_router[layer // 2], moe_w_gate[layer // 2],
                            moe_w_up[layer // 2], moe_w_down[layer // 2], final_g)
        else:
            xt = _dense_ffn(x1, h2, mod, ffn_w_gate[layer // 2].astype(BF16),
                            ffn_w_up[layer // 2].astype(BF16), ffn_w_down[layer // 2].astype(BF16))
    if DEPTH % 2 == 1:
        xt = _final_norm(xt, final_g)
    return xt.reshape(B, S, D)
```

```python
import functools
import math

import jax
import jax.numpy as jnp
import numpy as np
from jax import lax
from jax.experimental import pallas as pl
from jax.experimental.pallas import tpu as pltpu

D_MODEL = 1024
SEQ = 2048
DEPTH = 2
CHUNK = 64
EPS = 1e-6
GM_WIDTH = D_MODEL // 2
GM_GROUPS = 4
GM_BLOCK = 128
DA_HEADS = 4
DA_HEAD_DIM = 64
DA_V_DIM = 2 * DA_HEAD_DIM
ROPE_THETA = 10000.0
DN_HEADS = 4
DN_HEAD_DIM = 128
DN_WIDTH = DN_HEADS * DN_HEAD_DIM
DN_CONV = 4
N_EXPERTS = 8
FF_EXPERT = 7 * D_MODEL // 2

LANES = 128
SUBLANES = 8
BF16_SUBLANES = 16
VMEM_LIMIT = 56 * 1024 * 1024
DMA_ISSUE_UNROLL = 8

AB_COLS = 2 * DN_HEADS
C_GM, C_QK, C_V, C_DN, C_GATE = 0, 1024, 2048, 2560, 4608
C_AB = C_GATE + 3 * D_MODEL
W_ALL = C_AB + LANES
PROJ_CHUNK = 512

TM_PROJ = 512
TM_GM = 512
TQ = 256
TK = 256
TM_MERGE = 512
TM_FFN = 512
TM_ROUTE = 512
TM_MOE = 512
TM_DISP = 512
DISP_SLOTS = 3
TM_COMB = 512
ATTN_LOOKAHEAD = 2
VT_ROWS = DA_V_DIM + BF16_SUBLANES
DN_SEQS = 2
DN_BLOCK = 256

BF16 = jnp.bfloat16
F32 = jnp.float32


def _params(*sem):
    return pltpu.CompilerParams(dimension_semantics=sem, vmem_limit_bytes=VMEM_LIMIT)


def _dot(a, b):
    return jnp.dot(a, b, preferred_element_type=F32)


def _dot_nt(a, b):
    return lax.dot_general(a, b, (((1,), (1,)), ((), ())), preferred_element_type=F32)


def _dot_tn(a, b):
    return lax.dot_general(a, b, (((0,), (0,)), ((), ())), preferred_element_type=F32)


def _rms(x, g):
    return x * lax.rsqrt(jnp.mean(x * x, axis=-1, keepdims=True) + EPS) * g


def _silu(x):
    return x * jax.nn.sigmoid(x)


def _mod_kernel(c_ref, w_ref, b_ref, o_ref):
    c = c_ref[...]
    o_ref[0] = _dot(_silu(c).astype(BF16), w_ref[0].astype(BF16)) + b_ref[0]


def _modulation(c, w_mod, b_mod):
    B, D = c.shape
    L, _, N = w_mod.shape
    tn = 1536
    return pl.pallas_call(
        _mod_kernel,
        grid=(L, N // tn),
        in_specs=[pl.BlockSpec((B, D), lambda l, j: (0, 0)),
                  pl.BlockSpec((1, D, tn), lambda l, j: (l, 0, j)),
                  pl.BlockSpec((1, 1, tn), lambda l, j: (l, 0, j))],
        out_specs=pl.BlockSpec((1, B, tn), lambda l, j: (l, 0, j)),
        out_shape=jax.ShapeDtypeStruct((L, B, N), F32),
        compiler_params=_params("arbitrary", "arbitrary"),
        name="modulation",
    )(c, w_mod, b_mod.reshape(L, 1, N))


def _rope_tab_kernel(pos_ref, inv_ref, sgn_ref, cos_ref, sin_ref):
    ang = pos_ref[...].astype(F32) * inv_ref[...]
    cos_ref[...] = jnp.cos(ang)
    sin_ref[...] = jnp.sin(ang) * sgn_ref[...]


def _rope_tables(positions):
    T = positions.size
    inv_freq = ROPE_THETA ** (-jnp.arange(0, DA_HEAD_DIM, 2, dtype=F32) / DA_HEAD_DIM)
    inv = jnp.tile(inv_freq, LANES // (DA_HEAD_DIM // 2)).reshape(1, LANES)
    half = DA_HEAD_DIM // 2
    sgn = np.tile(np.concatenate([-np.ones(half), np.ones(half)]), LANES // DA_HEAD_DIM)
    sgn = jnp.asarray(sgn, F32).reshape(1, LANES)
    tm = 1024
    return pl.pallas_call(
        _rope_tab_kernel,
        grid=(T // tm,),
        in_specs=[pl.BlockSpec((tm, 1), lambda i: (i, 0)),
                  pl.BlockSpec((1, LANES), lambda i: (0, 0)),
                  pl.BlockSpec((1, LANES), lambda i: (0, 0))],
        out_specs=[pl.BlockSpec((tm, LANES), lambda i: (i, 0))] * 2,
        out_shape=[jax.ShapeDtypeStruct((T, LANES), F32)] * 2,
        compiler_params=_params("arbitrary"),
        name="rope_tables",
    )(positions.reshape(T, 1), inv, sgn)


def _rope_rows(x, cos, sin):
    lane = lax.broadcasted_iota(jnp.int32, x.shape, 1)
    first = (lane % DA_HEAD_DIM) < (DA_HEAD_DIM // 2)
    partner = jnp.where(first, pltpu.roll(x, LANES - DA_HEAD_DIM // 2, axis=1),
                        pltpu.roll(x, DA_HEAD_DIM // 2, axis=1))
    return x * cos + partner * sin


def _proj_kernel(x_ref, mod_ref, g_ref, w_ref, cos_ref, sin_ref, cw_ref,
                 gm_ref, qk_ref, v_ref, dn_ref, gate_ref, ab_ref, prev_ref):
    tm = x_ref.shape[0]

    @pl.when(pl.program_id(0) % (SEQ // tm) == 0)
    def _():
        prev_ref[...] = jnp.zeros_like(prev_ref)

    def conv_silu(y, idx):
        cw = cw_ref[:, idx * PROJ_CHUNK:(idx + 1) * PROJ_CHUNK]
        ycat = jnp.concatenate([prev_ref[idx], y], axis=0)
        out = y * cw[DN_CONV - 1:DN_CONV, :]
        for j in range(DN_CONV - 1):
            out = out + pltpu.roll(ycat, DN_CONV - 1 - j, axis=0)[SUBLANES:, :] * cw[j:j + 1, :]
        prev_ref[idx] = y[tm - SUBLANES:, :]
        return _silu(out)

    def l2norm_heads(t, mult):
        parts = []
        for j in range(PROJ_CHUNK // DN_HEAD_DIM):
            seg = t[:, j * DN_HEAD_DIM:(j + 1) * DN_HEAD_DIM]
            parts.append(seg * (lax.rsqrt(jnp.sum(seg * seg, axis=-1, keepdims=True) + EPS) * mult))
        return jnp.concatenate(parts, axis=1)

    x = x_ref[...]
    shift = mod_ref[0, 0:1, :]
    scale = mod_ref[0, 1:2, :]
    h = (_rms(x, g_ref[...]) * (1.0 + scale) + shift).astype(BF16)
    cos = cos_ref[...]
    sin = sin_ref[...]
    q_scale = DA_HEAD_DIM ** -0.5 * math.log2(math.e)

    def rope_chunk(y, mult):
        parts = [_rope_rows(y[:, j * LANES:(j + 1) * LANES], cos, sin) * mult
                 for j in range(PROJ_CHUNK // LANES)]
        return jnp.concatenate(parts, axis=1)

    dn_chunks = list(range(C_DN, C_GATE, PROJ_CHUNK))
    gate_chunks = list(range(C_GATE, C_AB, PROJ_CHUNK))
    order = list(range(0, C_DN, PROJ_CHUNK))
    for j, c in enumerate(dn_chunks):
        order += [c] + gate_chunks[2 * j:2 * j + 2]
    assert sorted(order) == list(range(0, C_AB, PROJ_CHUNK))
    for c0 in order:
        y = _dot(h, w_ref[:, c0:c0 + PROJ_CHUNK])
        if c0 < C_QK:
            gm_ref[:, c0:c0 + PROJ_CHUNK] = y.astype(BF16)
        elif c0 < C_V:
            mult = q_scale if c0 == C_QK else 1.0
            qk_ref[:, c0 - C_QK:c0 - C_QK + PROJ_CHUNK] = rope_chunk(y, mult).astype(BF16)
        elif c0 < C_DN:
            v_ref[...] = y.astype(BF16)
        elif c0 < C_GATE:
            idx = (c0 - C_DN) // PROJ_CHUNK
            if idx < 3:
                y = conv_silu(y, idx)
            if idx < 2:
                y = l2norm_heads(y, DN_HEAD_DIM ** -0.5 if idx == 0 else 1.0)
            dn_ref[:, c0 - C_DN:c0 - C_DN + PROJ_CHUNK] = y.astype(BF16)
        else:
            gate_ref[:, c0 - C_GATE:c0 - C_GATE + PROJ_CHUNK] = y.astype(BF16)
    ab_ref[...] = _dot(h, w_ref[:, C_AB:W_ALL])


def _in_projection(x, mod, g, w_all, cos_t, sin_t, conv_w):
    T, D = x.shape
    tm = TM_PROJ
    per_b = SEQ // tm
    row = lambda i: (i, 0)
    widths = (C_QK - C_GM, C_V - C_QK, C_DN - C_V, C_GATE - C_DN, C_AB - C_GATE)
    return pl.pallas_call(
        _proj_kernel,
        grid=(T // tm,),
        in_specs=[pl.BlockSpec((tm, D), row),
                  pl.BlockSpec((1, 6, D), lambda i: (i // per_b, 0, 0)),
                  pl.BlockSpec((1, D), lambda i: (0, 0)),
                  pl.BlockSpec((D, W_ALL), lambda i: (0, 0), pipeline_mode=pl.Buffered(1)),
                  pl.BlockSpec((tm, LANES), row),
                  pl.BlockSpec((tm, LANES), row),
                  pl.BlockSpec((DN_CONV, 3 * DN_WIDTH), lambda i: (0, 0))],
        out_specs=[pl.BlockSpec((tm, w), row) for w in widths] + [pl.BlockSpec((tm, LANES), row)],
        out_shape=[jax.ShapeDtypeStruct((T, w), BF16) for w in widths]
                  + [jax.ShapeDtypeStruct((T, LANES), F32)],
        scratch_shapes=[pltpu.VMEM((3, SUBLANES, PROJ_CHUNK), F32)],
        compiler_params=_params("arbitrary"),
        name="in_projection",
    )(x, mod, g, w_all, cos_t, sin_t, conv_w)


def _gelu(x):
    return 0.5 * x * (1.0 + lax.erf(x * np.float32(math.sqrt(0.5))))


def _gmlp_kernel(uv_ref, lng_ref, lnb_ref, ws_ref, bs_ref, o_ref):
    u = _gelu(uv_ref[:, :GM_WIDTH].astype(F32))
    v = _gelu(uv_ref[:, GM_WIDTH:].astype(F32))
    mu = jnp.mean(v, axis=-1, keepdims=True)
    vc = v - mu
    var = jnp.mean(vc * vc, axis=-1, keepdims=True)
    v = (vc * lax.rsqrt(var + EPS) * lng_ref[...] + lnb_ref[...]).astype(BF16)
    ri = lax.broadcasted_iota(jnp.int32, (GM_BLOCK, GM_BLOCK), 0) // CHUNK
    ci = lax.broadcasted_iota(jnp.int32, (GM_BLOCK, GM_BLOCK), 1) // CHUNK
    allowed = ci <= ri
    gc = GM_WIDTH // GM_GROUPS
    for g in range(GM_GROUPS):
        w = jnp.where(allowed, ws_ref[g], 0.0).astype(BF16)
        bias = bs_ref[g]
        for r in range(uv_ref.shape[0] // GM_BLOCK):
            rows = slice(r * GM_BLOCK, (r + 1) * GM_BLOCK)
            cols = slice(g * gc, (g + 1) * gc)
            sv = _dot(w, v[rows, cols]) + bias
            o_ref[rows, cols] = (u[rows, cols] * sv).astype(BF16)


def _gmlp(uv, ln_g, ln_b, w_s, b_s):
    T = uv.shape[0]
    tm = TM_GM
    return pl.pallas_call(
        _gmlp_kernel,
        grid=(T // tm,),
        in_specs=[pl.BlockSpec((tm, 2 * GM_WIDTH), lambda i: (i, 0)),
                  pl.BlockSpec((1, GM_WIDTH), lambda i: (0, 0)),
                  pl.BlockSpec((1, GM_WIDTH), lambda i: (0, 0)),
                  pl.BlockSpec((GM_GROUPS, GM_BLOCK, GM_BLOCK), lambda i: (0, 0, 0)),
                  pl.BlockSpec((GM_GROUPS, GM_BLOCK, 1), lambda i: (0, 0, 0))],
        out_specs=pl.BlockSpec((tm, GM_WIDTH), lambda i: (i, 0)),
        out_shape=jax.ShapeDtypeStruct((T, GM_WIDTH), BF16),
        compiler_params=_params("arbitrary"),
        name="gmlp",
    )(uv, ln_g.reshape(1, -1), ln_b.reshape(1, -1), w_s, b_s.reshape(GM_GROUPS, GM_BLOCK, 1))


def _attn_kernel(lam_ref, g_ref, q_ref, k_ref, v_ref, o_ref, qz_ref, vt_ref, *stats, lambda_init):
    nc = 2 * DA_HEADS
    m_ref, acc_ref, s_ref = (stats[j * nc:(j + 1) * nc] for j in range(3))
    i = pl.program_id(1)
    lp = lam_ref[...]
    lam = (jnp.exp(jnp.sum(lp[0:1] * lp[1:2], axis=-1, keepdims=True))
           - jnp.exp(jnp.sum(lp[2:3] * lp[3:4], axis=-1, keepdims=True)) + lambda_init)
    @pl.when(i == 0)
    def _():
        for kb in range(SEQ // TK):
            for h in range(DA_HEADS):
                vt_ref[h * VT_ROWS:h * VT_ROWS + DA_V_DIM, kb * TK:(kb + 1) * TK] = (
                    v_ref[kb * TK:(kb + 1) * TK, h * LANES:(h + 1) * LANES].T)
        for h in range(DA_HEADS):
            vt_ref[h * VT_ROWS + DA_V_DIM:(h + 1) * VT_ROWS, :] = jnp.ones((VT_ROWS - DA_V_DIM, SEQ), BF16)

    row = lax.broadcasted_iota(jnp.int32, (LANES, TQ), 0)
    for h in range(DA_HEADS):
        qt = q_ref[:, h * LANES:(h + 1) * LANES].T
        zero = jnp.zeros_like(qt)
        qz_ref[2 * h] = jnp.where(row < DA_HEAD_DIM, qt, zero)
        qz_ref[2 * h + 1] = jnp.where(row >= DA_HEAD_DIM, qt, zero)
    for c in range(nc):
        m_ref[c][...] = jnp.full_like(m_ref[c], -jnp.inf)
        acc_ref[c][...] = jnp.zeros_like(acc_ref[c])

    def key_rows(kb):
        return pl.ds(pl.multiple_of(kb * TK, TK), TK)

    def scores(kb, c, masked):
        h = c // 2
        s = _dot(k_ref[key_rows(kb), h * LANES:(h + 1) * LANES], qz_ref[c])
        if masked:
            ki = lax.broadcasted_iota(jnp.int32, s.shape, 0) // CHUNK + kb * (TK // CHUNK)
            qi = lax.broadcasted_iota(jnp.int32, s.shape, 1) // CHUNK + i * (TQ // CHUNK)
            s = jnp.where(ki <= qi, s, -jnp.inf)
        return s

    def accumulate(kb, c, s):
        h = c // 2
        m_old = m_ref[c][...]
        m_new = jnp.maximum(m_old, jnp.max(s, axis=0, keepdims=True))
        alpha = jnp.exp2(m_old - m_new)
        e = jnp.exp2((s - m_new).astype(BF16))
        vt = vt_ref[h * VT_ROWS:(h + 1) * VT_ROWS, key_rows(kb)]
        acc_ref[c][...] = alpha * acc_ref[c][...] + _dot(vt, e)
        m_ref[c][...] = m_new

    def fold_and_prefetch(kb, masked_next):
        fresh = {}
        for c in range(nc + ATTN_LOOKAHEAD):
            if c < nc:
                fresh[c] = scores(kb + 1, c, masked_next)
            if c >= ATTN_LOOKAHEAD:
                cc = c - ATTN_LOOKAHEAD
                accumulate(kb, cc, s_ref[cc][...])
                s_ref[cc][...] = fresh.pop(cc)

    first_masked = i * (TQ // TK)

    @pl.when(i == 0)
    def _():
        for c in range(nc):
            s_ref[c][...] = scores(0, c, True)

    @pl.when(i > 0)
    def _():
        for c in range(nc):
            s_ref[c][...] = scores(0, c, False)

        def body(kb, carry):
            fold_and_prefetch(kb, False)
            return carry

        lax.fori_loop(0, first_masked - 1, body, 0)
        fold_and_prefetch(first_masked - 1, True)

    for r in range(TQ // TK - 1):
        fold_and_prefetch(first_masked + r, True)
    for c in range(nc):
        accumulate(first_masked + TQ // TK - 1, c, s_ref[c][...])
    for h in range(DA_HEADS):
        num0, den0 = acc_ref[2 * h][:DA_V_DIM, :], acc_ref[2 * h][DA_V_DIM:DA_V_DIM + 1, :]
        num1, den1 = acc_ref[2 * h + 1][:DA_V_DIM, :], acc_ref[2 * h + 1][DA_V_DIM:DA_V_DIM + 1, :]
        o = num0 / den0 - lam * (num1 / den1)
        ms = jnp.mean(o * o, axis=0, keepdims=True)
        o = o * lax.rsqrt(ms + EPS) * (g_ref[...] * (1.0 - lambda_init))
        o_ref[:, h * LANES:(h + 1) * LANES] = o.T.astype(BF16)


def _diff_attention(qk, v, lam_p, subln_g, lambda_init, batch):
    T = qk.shape[0]
    nq = SEQ // TQ
    W = DA_HEADS * DA_V_DIM
    kernel = functools.partial(_attn_kernel, lambda_init=lambda_init)
    return pl.pallas_call(
        kernel,
        grid=(batch, nq),
        in_specs=[pl.BlockSpec((4, DA_HEAD_DIM), lambda b, i: (0, 0)),
                  pl.BlockSpec((DA_V_DIM, 1), lambda b, i: (0, 0)),
                  pl.BlockSpec((TQ, W), lambda b, i: (b * nq + i, 0)),
                  pl.BlockSpec((SEQ, W), lambda b, i: (b, 1)),
                  pl.BlockSpec((SEQ, W), lambda b, i: (b, 0))],
        out_specs=pl.BlockSpec((TQ, W), lambda b, i: (b * nq + i, 0)),
        out_shape=jax.ShapeDtypeStruct((T, W), BF16),
        scratch_shapes=[pltpu.VMEM((2 * DA_HEADS, LANES, TQ), BF16),
                        pltpu.VMEM((DA_HEADS * VT_ROWS, SEQ), BF16)]
                       + [pltpu.VMEM((1, TQ), F32)] * (2 * DA_HEADS)
                       + [pltpu.VMEM((VT_ROWS, TQ), F32)] * (2 * DA_HEADS)
                       + [pltpu.VMEM((TK, TQ), F32)] * (2 * DA_HEADS),
        compiler_params=_params("arbitrary", "arbitrary"),
        name="diff_attention",
    )(lam_p, subln_g.reshape(-1, 1), qk, qk, v)


def _split3(x):
    hi = x.astype(BF16)
    r = x - hi.astype(F32)
    mid = r.astype(BF16)
    lo = (r - mid.astype(F32)).astype(BF16)
    return hi, mid, lo


def _dn_kernel(dn_ref, a_ref, at_ref, alogt_ref, dtbt_ref, ng_ref,
               o_ref, *state_ref):
    c = pl.program_id(1)
    R = DN_BLOCK
    G = dn_ref.shape[0]

    @pl.when(c == 0)
    def _():
        for ref in state_ref:
            ref[...] = jnp.zeros_like(ref)

    W3 = 3 * DN_WIDTH
    ri = lax.broadcasted_iota(jnp.int32, (R, R), 0)
    ci = lax.broadcasted_iota(jnp.int32, (R, R), 1)
    tril = ri >= ci
    strict = ri > ci
    ones_triu = jnp.where(ci >= ri, 1.0, 0.0).astype(BF16)
    eye = jnp.where(ri == ci, 1.0, 0.0)
    same = {n: (ri // n) == (ci // n) for n in (16, 32, 64, 128)}
    gc_col, gc_row, beta_all = [], [], []
    for j in range(G):
        beta_all.append(jax.nn.sigmoid(a_ref[j]))
        abt = at_ref[j, 0]
        g_row = -jnp.exp(alogt_ref[...]) * jax.nn.softplus(abt + dtbt_ref[...])
        gc_row.append(sum(_dot(part, ones_triu) for part in _split3(g_row)))
        gc_col.append(gc_row[-1].T)

    H = range(G * DN_HEADS)
    seq = [n // DN_HEADS for n in H]
    hd = [n % DN_HEADS for n in H]
    hsl = [slice(hd[n] * DN_HEAD_DIM, (hd[n] + 1) * DN_HEAD_DIM) for n in H]
    q16 = [dn_ref[seq[n], :, hsl[n]] for n in H]
    k16 = [dn_ref[seq[n], :, DN_WIDTH + hd[n] * DN_HEAD_DIM:DN_WIDTH + (hd[n] + 1) * DN_HEAD_DIM] for n in H]
    q = [t.astype(F32) for t in q16]
    k = [t.astype(F32) for t in k16]
    v = [dn_ref[seq[n], :, 2 * DN_WIDTH + hd[n] * DN_HEAD_DIM:2 * DN_WIDTH + (hd[n] + 1) * DN_HEAD_DIM]
         .astype(F32) for n in H]
    gc = [gc_col[seq[n]][:, hd[n]:hd[n] + 1] for n in H]
    beta = [beta_all[seq[n]][:, DN_HEADS + hd[n]:DN_HEADS + hd[n] + 1] for n in H]
    g_last = [gc_col[seq[n]][R - 1:R, hd[n]:hd[n] + 1] for n in H]
    decay = [jnp.exp(jnp.where(tril, gc[n] - gc_row[seq[n]][hd[n]:hd[n] + 1, :], -jnp.inf)) for n in H]
    eg = [jnp.exp(t) for t in gc]
    kb = [k[h] * beta[h] for h in H]
    s1 = [_dot_nt(jnp.concatenate([kb[h].astype(BF16), q16[h]], axis=0), k16[h]) for h in H]
    a_mat = [jnp.where(strict, s1[h][:R] * decay[h], 0.0) for h in H]
    qk = [(s1[h][R:] * decay[h]).astype(BF16) for h in H]
    p = [jnp.where(same[16], -t, 0.0) for t in a_mat]
    t_inv = [eye + t for t in p]
    for _ in range(3):
        p16 = [t.astype(BF16) for t in p]
        p = [_dot(t, t) for t in p16]
        t_inv = [t_inv[h] + _dot(t_inv[h].astype(BF16), p[h].astype(BF16)) for h in H]
    a16 = [t.astype(BF16) for t in a_mat]
    t16 = [t.astype(BF16) for t in t_inv]
    for n in (16, 32, 64, 128):
        off = jnp.logical_not(same[n])
        if 2 * n < R:
            off = jnp.logical_and(same[2 * n], off)
        off16 = jnp.where(off, 1.0, 0.0).astype(BF16)
        tl = [_dot(t16[h], a16[h] * off16).astype(BF16) for h in H]
        t16 = [t16[h] - _dot(tl[h], t16[h]).astype(BF16) * off16 for h in H]
    sol = [_dot(t16[h], jnp.concatenate([v[h] * beta[h], kb[h] * eg[h]], axis=1).astype(BF16))
           for h in H]
    state = [state_ref[h][...] for h in H]
    m1 = [_dot(jnp.concatenate([sol[h][:, DN_HEAD_DIM:], q[h] * eg[h]], axis=0).astype(BF16),
               state[h].astype(BF16)) for h in H]
    vn16 = [(sol[h][:, :DN_HEAD_DIM] - m1[h][:R]).astype(BF16) for h in H]
    o = [m1[h][R:] + _dot(qk[h], vn16[h]) for h in H]
    kd = [(k[h] * jnp.exp(g_last[h] - gc[h])).astype(BF16) for h in H]
    for h in H:
        state_ref[h][...] = state[h] * jnp.exp(g_last[h]) + _dot_tn(kd[h], vn16[h])
    for n in H:
        z = dn_ref[seq[n], :, W3 + hd[n] * DN_HEAD_DIM:W3 + (hd[n] + 1) * DN_HEAD_DIM].astype(F32)
        o_ref[seq[n], :, hsl[n]] = (_rms(o[n], ng_ref[...]) * _silu(z)).astype(BF16)


def _deltanet(dn, ab, a_log, dt_bias, norm_g, batch):
    T = dn.shape[0]
    R = DN_BLOCK
    n = SEQ // R
    G = DN_SEQS if batch % DN_SEQS == 0 else 1
    col8 = lambda t: jnp.zeros((AB_COLS, 1), F32).at[:DN_HEADS, 0].set(t)
    ab_t = ab[:, :AB_COLS].reshape(batch, n, R, AB_COLS).transpose(0, 1, 3, 2)
    const = lambda b, c: (0, 0)
    out = pl.pallas_call(
        _dn_kernel,
        grid=(batch // G, n),
        in_specs=[pl.BlockSpec((G, R, 4 * DN_WIDTH), lambda b, c: (b, c, 0)),
                  pl.BlockSpec((G, R, LANES), lambda b, c: (b, c, 0)),
                  pl.BlockSpec((G, 1, AB_COLS, R), lambda b, c: (b, c, 0, 0)),
                  pl.BlockSpec((AB_COLS, 1), const),
                  pl.BlockSpec((AB_COLS, 1), const),
                  pl.BlockSpec((1, DN_HEAD_DIM), const)],
        out_specs=pl.BlockSpec((G, R, DN_WIDTH), lambda b, c: (b, c, 0)),
        out_shape=jax.ShapeDtypeStruct((batch, SEQ, DN_WIDTH), BF16),
        scratch_shapes=[pltpu.VMEM((DN_HEAD_DIM, DN_HEAD_DIM), F32)] * (G * DN_HEADS),
        compiler_params=_params("arbitrary", "arbitrary"),
        name="deltanet",
    )(dn.reshape(batch, SEQ, -1), ab.reshape(batch, SEQ, -1), ab_t,
      col8(a_log), col8(dt_bias), norm_g.reshape(1, -1))
    return out.reshape(T, DN_WIDTH)


def _pack_bf16_pair(lo, hi):
    lo_bits = lax.bitcast_convert_type(lo.astype(BF16).astype(F32), jnp.uint32)
    hi_bits = lax.bitcast_convert_type(hi.astype(BF16).astype(F32), jnp.uint32)
    return (lo_bits >> 16) | (hi_bits & jnp.uint32(0xFFFF0000))


def _unpack_bf16_pair(words):
    lo = lax.bitcast_convert_type(words << 16, F32)
    hi = lax.bitcast_convert_type(words & jnp.uint32(0xFFFF0000), F32)
    return lo.astype(BF16), hi.astype(BF16)


def _merge_kernel(x_ref, mod_ref, g2_ref, ygm_ref, yda_ref, ydn_ref, gate_ref,
                  wgm_ref, wda_ref, wdn_ref, wout_ref, x1_ref, h2_ref, *maybe_packed_ref):
    D = D_MODEL
    merged = (jax.nn.sigmoid(gate_ref[:, 0:D].astype(F32)) * _dot(ygm_ref[...], wgm_ref[...])
              + jax.nn.sigmoid(gate_ref[:, D:2 * D].astype(F32)) * _dot(yda_ref[...], wda_ref[...])
              + jax.nn.sigmoid(gate_ref[:, 2 * D:3 * D].astype(F32)) * _dot(ydn_ref[...], wdn_ref[...]))
    y = _dot(merged.astype(BF16), wout_ref[...])
    x1 = x_ref[...] + mod_ref[0, 2:3, :] * y
    x1_ref[...] = x1
    h2 = _rms(x1, g2_ref[...]) * (1.0 + mod_ref[0, 4:5, :]) + mod_ref[0, 3:4, :]
    h2_ref[...] = h2.astype(BF16)
    for packed_ref in maybe_packed_ref:
        packed_ref[...] = _pack_bf16_pair(h2[:, :D // 2], h2[:, D // 2:])


def _merge(x, mod, g2, y_gm, y_da, y_dn, gates, w_gm, w_da, w_dn, w_out, with_packed):
    T, D = x.shape
    tm = TM_MERGE
    per_b = SEQ // tm
    row = lambda i: (i, 0)
    const = lambda i: (0, 0)
    out_specs = [pl.BlockSpec((tm, D), row), pl.BlockSpec((tm, D), row)]
    out_shape = [jax.ShapeDtypeStruct((T, D), F32), jax.ShapeDtypeStruct((T, D), BF16)]
    if with_packed:
        out_specs.append(pl.BlockSpec((tm, D // 2), row))
        out_shape.append(jax.ShapeDtypeStruct((T, D // 2), jnp.uint32))
    return pl.pallas_call(
        _merge_kernel,
        grid=(T // tm,),
        in_specs=[pl.BlockSpec((tm, D), row),
                  pl.BlockSpec((1, 6, D), lambda i: (i // per_b, 0, 0)),
                  pl.BlockSpec((1, D), const),
                  pl.BlockSpec((tm, GM_WIDTH), row),
                  pl.BlockSpec((tm, GM_WIDTH), row),
                  pl.BlockSpec((tm, DN_WIDTH), row),
                  pl.BlockSpec((tm, 3 * D), row),
                  pl.BlockSpec((GM_WIDTH, D), const),
                  pl.BlockSpec((GM_WIDTH, D), const),
                  pl.BlockSpec((DN_WIDTH, D), const),
                  pl.BlockSpec((D, D), const)],
        out_specs=out_specs,
        out_shape=out_shape,
        compiler_params=_params("arbitrary"),
        name="merge",
    )(x, mod, g2, y_gm, y_da, y_dn, gates, w_gm, w_da, w_dn, w_out)


def _ffn_kernel(x_ref, h_ref, mod_ref, wg_ref, wu_ref, wd_ref, o_ref):
    h = h_ref[...]
    a = _dot(h, wg_ref[...])
    b = _dot(h, wu_ref[...])
    f = _dot((_silu(a) * b).astype(BF16), wd_ref[...])
    o_ref[...] = x_ref[...] + mod_ref[0, 5:6, :] * f


def _dense_ffn(x1, h2, mod, w_gate, w_up, w_down):
    T, D = x1.shape
    F = w_gate.shape[1]
    tm = TM_FFN
    per_b = SEQ // tm
    row = lambda i: (i, 0)
    const = lambda i: (0, 0)
    return pl.pallas_call(
        _ffn_kernel,
        grid=(T // tm,),
        in_specs=[pl.BlockSpec((tm, D), row),
                  pl.BlockSpec((tm, D), row),
                  pl.BlockSpec((1, 6, D), lambda i: (i // per_b, 0, 0)),
                  pl.BlockSpec((D, F), const, pipeline_mode=pl.Buffered(1)),
                  pl.BlockSpec((D, F), const, pipeline_mode=pl.Buffered(1)),
                  pl.BlockSpec((F, D), const, pipeline_mode=pl.Buffered(1))],
        out_specs=pl.BlockSpec((tm, D), row),
        out_shape=jax.ShapeDtypeStruct((T, D), F32),
        compiler_params=_params("arbitrary"),
        name="dense_ffn",
    )(x1, h2, mod, w_gate, w_up, w_down)


def _router_kernel(h_ref, wr_ref, e_ref, p_ref, r_ref, cnt_ref, base_ref):
    i = pl.program_id(0)
    tm = h_ref.shape[0]

    @pl.when(i == 0)
    def _():
        base_ref[...] = jnp.zeros_like(base_ref)

    logits = _dot_nt(wr_ref[...], h_ref[...])
    row = lax.broadcasted_iota(jnp.int32, logits.shape, 0)
    m1 = jnp.max(logits, axis=0, keepdims=True)
    i1 = jnp.min(jnp.where(logits == m1, row, N_EXPERTS), axis=0, keepdims=True)
    rest = jnp.where(row == i1, -jnp.inf, logits)
    m2 = jnp.max(rest, axis=0, keepdims=True)
    i2 = jnp.min(jnp.where(rest == m2, row, N_EXPERTS), axis=0, keepdims=True)
    e2 = jnp.exp(m2 - m1)
    w1 = 1.0 / (1.0 + e2)
    w2 = e2 / (1.0 + e2)
    oh1 = jnp.where(row == i1, 1.0, 0.0)
    oh2 = jnp.where(row == i2, 1.0, 0.0)
    both = oh1 + oh2
    ti = lax.broadcasted_iota(jnp.int32, (tm, tm), 0)
    tj = lax.broadcasted_iota(jnp.int32, (tm, tm), 1)
    before = jnp.where(ti < tj, 1.0, 0.0).astype(BF16)
    pos = base_ref[...] + _dot(both.astype(BF16), before)
    r1 = jnp.sum(oh1 * pos, axis=0, keepdims=True)
    r2 = jnp.sum(oh2 * pos, axis=0, keepdims=True)
    base_ref[...] = base_ref[...] + jnp.sum(both, axis=1, keepdims=True)
    zi = jnp.zeros((N_EXPERTS - 2, tm), jnp.int32)
    zf = jnp.zeros((N_EXPERTS - 2, tm), F32)
    e_ref[...] = jnp.concatenate([i1, i2, zi], axis=0)
    p_ref[...] = jnp.concatenate([w1, w2, zf], axis=0)
    r_ref[...] = jnp.concatenate([r1.astype(jnp.int32), r2.astype(jnp.int32), zi], axis=0)
    cnt_ref[...] = jnp.broadcast_to(base_ref[...], cnt_ref.shape)


def _router(h2, w_router_t):
    T, D = h2.shape
    tm = TM_ROUTE
    col = lambda i: (0, i)
    return pl.pallas_call(
        _router_kernel,
        grid=(T // tm,),
        in_specs=[pl.BlockSpec((tm, D), lambda i: (i, 0)),
                  pl.BlockSpec((N_EXPERTS, D), lambda i: (0, 0))],
        out_specs=[pl.BlockSpec((N_EXPERTS, tm), col)] * 3 + [pl.BlockSpec((N_EXPERTS, LANES), lambda i: (0, 0))],
        out_shape=[jax.ShapeDtypeStruct((N_EXPERTS, T), jnp.int32),
                   jax.ShapeDtypeStruct((N_EXPERTS, T), F32),
                   jax.ShapeDtypeStruct((N_EXPERTS, T), jnp.int32),
                   jax.ShapeDtypeStruct((N_EXPERTS, LANES), F32)],
        scratch_shapes=[pltpu.VMEM((N_EXPERTS, 1), F32)],
        compiler_params=_params("arbitrary"),
        name="moe_router",
    )(h2, w_router_t)


def _dispatch_kernel(dest_ref, h_ref, xs_in_ref, xs_ref, buf_ref, sem_in, sem_out):
    del xs_in_ref
    i = pl.program_id(0)
    n = pl.num_programs(0)
    tm = buf_ref.shape[1]

    def load(t):
        s = t % DISP_SLOTS
        return pltpu.make_async_copy(h_ref.at[pl.ds(pl.multiple_of(t * tm, tm), tm)], buf_ref.at[s],
                                     sem_in.at[s])

    def wait_rows(t):
        s = t % DISP_SLOTS
        for _ in range(2):
            pltpu.make_async_copy(buf_ref.at[s], xs_ref.at[pl.ds(0, tm)], sem_out.at[s]).wait()

    @pl.when(i == 0)
    def _():
        load(0).start()
        pl.when(n > 1)(lambda: load(1).start())

    load(i).wait()
    slot = i % DISP_SLOTS

    def issue(r, _):
        for k in range(2):
            pltpu.make_async_copy(buf_ref.at[slot, pl.ds(r, 1)],
                                  xs_ref.at[pl.ds(dest_ref[0, 0, k * tm + r], 1)], sem_out.at[slot]).start()
        return 0

    lax.fori_loop(0, tm, issue, 0, unroll=DMA_ISSUE_UNROLL)
    pl.when(i > 0)(lambda: wait_rows(i - 1))
    pl.when(i + 2 < n)(lambda: load(i + 2).start())
    pl.when(i == n - 1)(lambda: wait_rows(i))


def _dispatch(h2, dest, n_slots):
    T, D = h2.shape
    tm = TM_DISP
    dest_t = dest.reshape(2, T // tm, tm).transpose(1, 0, 2).reshape(T // tm, 1, 2 * tm)
    xs0 = jnp.zeros((n_slots, D), h2.dtype)
    return pl.pallas_call(
        _dispatch_kernel,
        grid=(T // tm,),
        in_specs=[pl.BlockSpec((1, 1, 2 * tm), lambda i: (i, 0, 0), memory_space=pltpu.SMEM),
                  pl.BlockSpec(memory_space=pl.ANY),
                  pl.BlockSpec(memory_space=pl.ANY)],
        out_specs=pl.BlockSpec(memory_space=pl.ANY),
        out_shape=jax.ShapeDtypeStruct((n_slots, D), h2.dtype),
        scratch_shapes=[pltpu.VMEM((DISP_SLOTS, tm, D), h2.dtype),
                        pltpu.SemaphoreType.DMA((DISP_SLOTS,)), pltpu.SemaphoreType.DMA((DISP_SLOTS,))],
        input_output_aliases={2: 0},
        compiler_params=_params("arbitrary"),
        name="moe_dispatch",
    )(dest_t, h2, xs0)


def _expert_kernel(be_ref, nu_ref, x_ref, wg_ref, wu_ref, wd_ref, o_ref):
    del be_ref
    used = pl.program_id(0) < nu_ref[0]
    half = D_MODEL // 2

    @pl.when(used)
    def _():
        lo, hi = _unpack_bf16_pair(x_ref[...])
        a = _dot(lo, wg_ref[0, :half, :]) + _dot(hi, wg_ref[0, half:, :])
        b = _dot(lo, wu_ref[0, :half, :]) + _dot(hi, wu_ref[0, half:, :])
        o_ref[...] = _dot((_silu(a) * b).astype(BF16), wd_ref[0])

    @pl.when(jnp.logical_not(used))
    def _():
        o_ref[...] = jnp.zeros_like(o_ref)


def _experts(xs, blk_e, n_used, w_gate, w_up, w_down):
    P = xs.shape[0]
    _, D, F = w_gate.shape
    tm = TM_MOE
    grid_spec = pltpu.PrefetchScalarGridSpec(
        num_scalar_prefetch=2,
        grid=(P // tm,),
        in_specs=[pl.BlockSpec((tm, D // 2), lambda i, be, nu: (i, 0)),
                  pl.BlockSpec((1, D, F), lambda i, be, nu: (be[i], 0, 0), pipeline_mode=pl.Buffered(1)),
                  pl.BlockSpec((1, D, F), lambda i, be, nu: (be[i], 0, 0), pipeline_mode=pl.Buffered(1)),
                  pl.BlockSpec((1, F, D), lambda i, be, nu: (be[i], 0, 0), pipeline_mode=pl.Buffered(1))],
        out_specs=pl.BlockSpec((tm, D), lambda i, be, nu: (i, 0)))
    return pl.pallas_call(
        _expert_kernel,
        grid_spec=grid_spec,
        out_shape=jax.ShapeDtypeStruct((P, D), F32),
        compiler_params=_params("arbitrary"),
        name="moe_experts",
    )(blk_e, n_used, xs, w_gate, w_up, w_down)


def _combine_kernel(dest_ref, dest_next_ref, x_ref, mod_ref, p_ref, g_ref, ys_ref, o_ref, ybuf_ref, sems):
    i = pl.program_id(0)
    tm = x_ref.shape[0]
    slot = i % 2

    def gather(d_ref, s):
        def issue(r, _):
            for k in range(2):
                pltpu.make_async_copy(ys_ref.at[pl.ds(d_ref[0, 0, k * tm + r], 1)],
                                      ybuf_ref.at[s, k, pl.ds(r, 1)], sems.at[s]).start()
            return 0

        lax.fori_loop(0, tm, issue, 0, unroll=DMA_ISSUE_UNROLL)

    pl.when(i == 0)(lambda: gather(dest_ref, 0))
    pl.when(i + 1 < pl.num_programs(0))(lambda: gather(dest_next_ref, 1 - slot))
    for k in range(2):
        pltpu.make_async_copy(ys_ref.at[pl.ds(0, tm)], ybuf_ref.at[slot, k], sems.at[slot]).wait()
    p = p_ref[...]
    f = ybuf_ref[slot, 0] * p[:, 0:1] + ybuf_ref[slot, 1] * p[:, 1:2]
    x2 = x_ref[...] + mod_ref[0, 5:6, :] * f
    o_ref[...] = _rms(x2, g_ref[...])


def _combine(x1, mod, probs, final_g, ys, dest):
    T, D = x1.shape
    tm = TM_COMB
    per_b = SEQ // tm
    n = T // tm
    dest_t = dest.reshape(2, n, tm).transpose(1, 0, 2).reshape(n, 1, 2 * tm)
    return pl.pallas_call(
        _combine_kernel,
        grid=(n,),
        in_specs=[pl.BlockSpec((1, 1, 2 * tm), lambda i: (i, 0, 0), memory_space=pltpu.SMEM),
                  pl.BlockSpec((1, 1, 2 * tm), lambda i: (jnp.minimum(i + 1, n - 1), 0, 0),
                               memory_space=pltpu.SMEM),
                  pl.BlockSpec((tm, D), lambda i: (i, 0)),
                  pl.BlockSpec((1, 6, D), lambda i: (i // per_b, 0, 0)),
                  pl.BlockSpec((tm, 2), lambda i: (i, 0)),
                  pl.BlockSpec((1, D), lambda i: (0, 0)),
                  pl.BlockSpec(memory_space=pl.ANY)],
        out_specs=pl.BlockSpec((tm, D), lambda i: (i, 0)),
        out_shape=jax.ShapeDtypeStruct((T, D), F32),
        scratch_shapes=[pltpu.VMEM((2, 2, tm, D), F32), pltpu.SemaphoreType.DMA((2,))],
        compiler_params=_params("arbitrary"),
        name="moe_combine",
    )(dest_t, dest_t, x1, mod, probs, final_g, ys)


def _moe_layer(x1, h2, h2_packed, mod, w_router, w_gate, w_up, w_down, final_g):
    T, D = x1.shape
    e_idx, probs, rank, counts = _router(h2, w_router.T.astype(BF16))
    counts = counts[:, 0].astype(jnp.int32)
    padded = (counts + TM_MOE - 1) // TM_MOE * TM_MOE
    pad_end = jnp.cumsum(padded)
    pad_start = pad_end - padded
    eid = jnp.arange(N_EXPERTS, dtype=jnp.int32)[:, None, None]
    dest = jnp.sum(jnp.where(e_idx[None, :2] == eid, pad_start[:, None, None], 0), axis=0) + rank[:2]
    n_slots = 2 * T + N_EXPERTS * TM_MOE
    n_blk = n_slots // TM_MOE
    blk_e = jnp.minimum(jnp.searchsorted(pad_end, jnp.arange(n_blk, dtype=jnp.int32) * TM_MOE,
                                         side='right'), N_EXPERTS - 1).astype(jnp.int32)
    n_used = (pad_end[-1:] // TM_MOE).astype(jnp.int32)
    xs = _dispatch(h2_packed, dest, n_slots)
    ys = _experts(xs, blk_e, n_used, w_gate.astype(BF16), w_up.astype(BF16), w_down.astype(BF16))
    return _combine(x1, mod, probs[:2].T, final_g.reshape(1, -1), ys, dest)


def _final_norm_kernel(x_ref, g_ref, o_ref):
    o_ref[...] = _rms(x_ref[...], g_ref[...])


def _final_norm(x, g):
    T, D = x.shape
    tm = 1024
    return pl.pallas_call(
        _final_norm_kernel,
        grid=(T // tm,),
        in_specs=[pl.BlockSpec((tm, D), lambda i: (i, 0)), pl.BlockSpec((1, D), lambda i: (0, 0))],
        out_specs=pl.BlockSpec((tm, D), lambda i: (i, 0)),
        out_shape=jax.ShapeDtypeStruct((T, D), F32),
        compiler_params=_params("arbitrary"),
        name="final_norm",
    )(x, g.reshape(1, -1))


def _regroup_w_in(w):
    ab = jnp.pad(w[:, C_GATE:C_GATE + AB_COLS], ((0, 0), (0, LANES - AB_COLS)))
    return jnp.concatenate([w[:, :C_GATE], w[:, C_GATE + AB_COLS:], ab], axis=1).astype(BF16)


def _lambda_init(layer):
    return 0.8 - 0.6 * math.exp(-0.3 * layer)


def kernel(x, c, positions, norm1_g, norm2_g, w_mod, b_mod, w_in, gm_ln_g, gm_ln_b, gm_w_s, gm_b_s, da_lambda, da_subln_g, dn_conv_w, dn_a_log, dn_dt_bias, dn_norm_g, w_br_gm, w_br_da, w_br_dn, w_out, ffn_w_gate, ffn_w_up, ffn_w_down, moe_w_router, moe_w_gate, moe_w_up, moe_w_down, final_g):
    B, S, D = x.shape
    T = B * S
    xt = x.reshape(T, D)
    mod_all = _modulation(c, w_mod, b_mod).reshape(DEPTH, B, 6, D)
    cos_t, sin_t = _rope_tables(positions)
    for layer in range(DEPTH):
        mod = mod_all[layer]
        gm, qk, v, dn, gates, ab = _in_projection(xt, mod, norm1_g[layer].reshape(1, D),
                                                  _regroup_w_in(w_in[layer]), cos_t, sin_t, dn_conv_w[layer])
        y_gm = _gmlp(gm, gm_ln_g[layer], gm_ln_b[layer], gm_w_s[layer], gm_b_s[layer])
        y_da = _diff_attention(qk, v, da_lambda[layer], da_subln_g[layer], _lambda_init(layer), B)
        y_dn = _deltanet(dn, ab, dn_a_log[layer], dn_dt_bias[layer], dn_norm_g[layer], B)
        moe = layer % 2 == 1
        x1, h2, *packed = _merge(xt, mod, norm2_g[layer].reshape(1, D), y_gm, y_da, y_dn, gates,
                                 w_br_gm[layer].astype(BF16), w_br_da[layer].astype(BF16),
                                 w_br_dn[layer].astype(BF16), w_out[layer].astype(BF16), moe)
        if moe:
            xt = _moe_layer(x1, h2, packed[0], mod, moe_w_router[layer // 2], moe_w_gate[layer // 2],
                            moe_w_up[layer // 2], moe_w_down[layer // 2], final_g)
        else:
            xt = _dense_ffn(x1, h2, mod, ffn_w_gate[layer // 2].astype(BF16),
                            ffn_w_up[layer // 2].astype(BF16), ffn_w_down[layer // 2].astype(BF16))
    if DEPTH % 2 == 1:
        xt = _final_norm(xt, final_g)
    return xt.reshape(B, S, D)
```

```python
import functools
import math

import jax
import jax.numpy as jnp
import numpy as np
from jax import lax
from jax.experimental import pallas as pl
from jax.experimental.pallas import tpu as pltpu

D_MODEL = 1024
SEQ = 2048
DEPTH = 2
CHUNK = 64
EPS = 1e-6
GM_WIDTH = D_MODEL // 2
GM_GROUPS = 4
GM_BLOCK = 128
DA_HEADS = 4
DA_HEAD_DIM = 64
DA_V_DIM = 2 * DA_HEAD_DIM
ROPE_THETA = 10000.0
DN_HEADS = 4
DN_HEAD_DIM = 128
DN_WIDTH = DN_HEADS * DN_HEAD_DIM
DN_CONV = 4
N_EXPERTS = 8
FF_EXPERT = 7 * D_MODEL // 2

LANES = 128
SUBLANES = 8
BF16_SUBLANES = 16
VMEM_LIMIT = 56 * 1024 * 1024
DMA_ISSUE_UNROLL = 8

AB_COLS = 2 * DN_HEADS
C_GM, C_QK, C_V, C_DN, C_GATE = 0, 1024, 2048, 2560, 4608
C_AB = C_GATE + 3 * D_MODEL
W_ALL = C_AB + LANES
PROJ_CHUNK = 512

TM_PROJ = 512
TM_GM = 512
TQ = 256
TK = 256
TM_MERGE = 512
TM_FFN = 512
TM_ROUTE = 512
TM_MOE = 512
TM_DISP = 512
DISP_SLOTS = 3
TM_COMB = 512
ATTN_LOOKAHEAD = 2
VT_ROWS = DA_V_DIM + BF16_SUBLANES
DN_SEQS = 2
DN_BLOCK = 256

BF16 = jnp.bfloat16
F32 = jnp.float32


def _params(*sem):
    return pltpu.CompilerParams(dimension_semantics=sem, vmem_limit_bytes=VMEM_LIMIT)


def _dot(a, b):
    return jnp.dot(a, b, preferred_element_type=F32)


def _dot_nt(a, b):
    return lax.dot_general(a, b, (((1,), (1,)), ((), ())), preferred_element_type=F32)


def _dot_tn(a, b):
    return lax.dot_general(a, b, (((0,), (0,)), ((), ())), preferred_element_type=F32)


def _rms(x, g):
    return x * lax.rsqrt(jnp.mean(x * x, axis=-1, keepdims=True) + EPS) * g


def _silu(x):
    return x * jax.nn.sigmoid(x)


def _mod_kernel(c_ref, w_ref, b_ref, o_ref):
    c = c_ref[...]
    o_ref[0] = _dot(_silu(c).astype(BF16), w_ref[0].astype(BF16)) + b_ref[0]


def _modulation(c, w_mod, b_mod):
    B, D = c.shape
    L, _, N = w_mod.shape
    tn = 1536
    return pl.pallas_call(
        _mod_kernel,
        grid=(L, N // tn),
        in_specs=[pl.BlockSpec((B, D), lambda l, j: (0, 0)),
                  pl.BlockSpec((1, D, tn), lambda l, j: (l, 0, j)),
                  pl.BlockSpec((1, 1, tn), lambda l, j: (l, 0, j))],
        out_specs=pl.BlockSpec((1, B, tn), lambda l, j: (l, 0, j)),
        out_shape=jax.ShapeDtypeStruct((L, B, N), F32),
        compiler_params=_params("arbitrary", "arbitrary"),
        name="modulation",
    )(c, w_mod, b_mod.reshape(L, 1, N))


def _rope_tab_kernel(pos_ref, inv_ref, sgn_ref, cos_ref, sin_ref):
    ang = pos_ref[...].astype(F32) * inv_ref[...]
    cos_ref[...] = jnp.cos(ang)
    sin_ref[...] = jnp.sin(ang) * sgn_ref[...]


def _rope_tables(positions):
    T = positions.size
    inv_freq = ROPE_THETA ** (-jnp.arange(0, DA_HEAD_DIM, 2, dtype=F32) / DA_HEAD_DIM)
    inv = jnp.tile(inv_freq, LANES // (DA_HEAD_DIM // 2)).reshape(1, LANES)
    half = DA_HEAD_DIM // 2
    sgn = np.tile(np.concatenate([-np.ones(half), np.ones(half)]), LANES // DA_HEAD_DIM)
    sgn = jnp.asarray(sgn, F32).reshape(1, LANES)
    tm = 1024
    return pl.pallas_call(
        _rope_tab_kernel,
        grid=(T // tm,),
        in_specs=[pl.BlockSpec((tm, 1), lambda i: (i, 0)),
                  pl.BlockSpec((1, LANES), lambda i: (0, 0)),
                  pl.BlockSpec((1, LANES), lambda i: (0, 0))],
        out_specs=[pl.BlockSpec((tm, LANES), lambda i: (i, 0))] * 2,
        out_shape=[jax.ShapeDtypeStruct((T, LANES), F32)] * 2,
        compiler_params=_params("arbitrary"),
        name="rope_tables",
    )(positions.reshape(T, 1), inv, sgn)


def _rope_rows(x, cos, sin):
    lane = lax.broadcasted_iota(jnp.int32, x.shape, 1)
    first = (lane % DA_HEAD_DIM) < (DA_HEAD_DIM // 2)
    partner = jnp.where(first, pltpu.roll(x, LANES - DA_HEAD_DIM // 2, axis=1),
                        pltpu.roll(x, DA_HEAD_DIM // 2, axis=1))
    return x * cos + partner * sin


def _proj_kernel(x_ref, mod_ref, g_ref, w_ref, cos_ref, sin_ref, cw_ref,
                 gm_ref, qk_ref, v_ref, dn_ref, gate_ref, ab_ref, prev_ref):
    tm = x_ref.shape[0]

    @pl.when(pl.program_id(0) % (SEQ // tm) == 0)
    def _():
        prev_ref[...] = jnp.zeros_like(prev_ref)

    def conv_silu(y, idx):
        cw = cw_ref[:, idx * PROJ_CHUNK:(idx + 1) * PROJ_CHUNK]
        ycat = jnp.concatenate([prev_ref[idx], y], axis=0)
        out = y * cw[DN_CONV - 1:DN_CONV, :]
        for j in range(DN_CONV - 1):
            out = out + pltpu.roll(ycat, DN_CONV - 1 - j, axis=0)[SUBLANES:, :] * cw[j:j + 1, :]
        prev_ref[idx] = y[tm - SUBLANES:, :]
        return _silu(out)

    def l2norm_heads(t, mult):
        parts = []
        for j in range(PROJ_CHUNK // DN_HEAD_DIM):
            seg = t[:, j * DN_HEAD_DIM:(j + 1) * DN_HEAD_DIM]
            parts.append(seg * (lax.rsqrt(jnp.sum(seg * seg, axis=-1, keepdims=True) + EPS) * mult))
        return jnp.concatenate(parts, axis=1)

    x = x_ref[...]
    shift = mod_ref[0, 0:1, :]
    scale = mod_ref[0, 1:2, :]
    h = (_rms(x, g_ref[...]) * (1.0 + scale) + shift).astype(BF16)
    cos = cos_ref[...]
    sin = sin_ref[...]
    q_scale = DA_HEAD_DIM ** -0.5 * math.log2(math.e)

    def rope_chunk(y, mult):
        parts = [_rope_rows(y[:, j * LANES:(j + 1) * LANES], cos, sin) * mult
                 for j in range(PROJ_CHUNK // LANES)]
        return jnp.concatenate(parts, axis=1)

    dn_chunks = list(range(C_DN, C_GATE, PROJ_CHUNK))
    gate_chunks = list(range(C_GATE, C_AB, PROJ_CHUNK))
    order = list(range(0, C_DN, PROJ_CHUNK))
    for j, c in enumerate(dn_chunks):
        order += [c] + gate_chunks[2 * j:2 * j + 2]
    assert sorted(order) == list(range(0, C_AB, PROJ_CHUNK))
    for c0 in order:
        y = _dot(h, w_ref[:, c0:c0 + PROJ_CHUNK])
        if c0 < C_QK:
            gm_ref[:, c0:c0 + PROJ_CHUNK] = y.astype(BF16)
        elif c0 < C_V:
            mult = q_scale if c0 == C_QK else 1.0
            qk_ref[:, c0 - C_QK:c0 - C_QK + PROJ_CHUNK] = rope_chunk(y, mult).astype(BF16)
        elif c0 < C_DN:
            v_ref[...] = y.astype(BF16)
        elif c0 < C_GATE:
            idx = (c0 - C_DN) // PROJ_CHUNK
            if idx < 3:
                y = conv_silu(y, idx)
            if idx < 2:
                y = l2norm_heads(y, DN_HEAD_DIM ** -0.5 if idx == 0 else 1.0)
            dn_ref[:, c0 - C_DN:c0 - C_DN + PROJ_CHUNK] = y.astype(BF16)
        else:
            gate_ref[:, c0 - C_GATE:c0 - C_GATE + PROJ_CHUNK] = y.astype(BF16)
    ab_ref[...] = _dot(h, w_ref[:, C_AB:W_ALL])


def _in_projection(x, mod, g, w_all, cos_t, sin_t, conv_w):
    T, D = x.shape
    tm = TM_PROJ
    per_b = SEQ // tm
    row = lambda i: (i, 0)
    widths = (C_QK - C_GM, C_V - C_QK, C_DN - C_V, C_GATE - C_DN, C_AB - C_GATE)
    return pl.pallas_call(
        _proj_kernel,
        grid=(T // tm,),
        in_specs=[pl.BlockSpec((tm, D), row),
                  pl.BlockSpec((1, 6, D), lambda i: (i // per_b, 0, 0)),
                  pl.BlockSpec((1, D), lambda i: (0, 0)),
                  pl.BlockSpec((D, W_ALL), lambda i: (0, 0), pipeline_mode=pl.Buffered(1)),
                  pl.BlockSpec((tm, LANES), row),
                  pl.BlockSpec((tm, LANES), row),
                  pl.BlockSpec((DN_CONV, 3 * DN_WIDTH), lambda i: (0, 0))],
        out_specs=[pl.BlockSpec((tm, w), row) for w in widths] + [pl.BlockSpec((tm, LANES), row)],
        out_shape=[jax.ShapeDtypeStruct((T, w), BF16) for w in widths]
                  + [jax.ShapeDtypeStruct((T, LANES), F32)],
        scratch_shapes=[pltpu.VMEM((3, SUBLANES, PROJ_CHUNK), F32)],
        compiler_params=_params("arbitrary"),
        name="in_projection",
    )(x, mod, g, w_all, cos_t, sin_t, conv_w)


def _gelu(x):
    return 0.5 * x * (1.0 + lax.erf(x * np.float32(math.sqrt(0.5))))


def _gmlp_kernel(uv_ref, lng_ref, lnb_ref, ws_ref, bs_ref, o_ref):
    u = _gelu(uv_ref[:, :GM_WIDTH].astype(F32))
    v = _gelu(uv_ref[:, GM_WIDTH:].astype(F32))
    mu = jnp.mean(v, axis=-1, keepdims=True)
    vc = v - mu
    var = jnp.mean(vc * vc, axis=-1, keepdims=True)
    v = (vc * lax.rsqrt(var + EPS) * lng_ref[...] + lnb_ref[...]).astype(BF16)
    ri = lax.broadcasted_iota(jnp.int32, (GM_BLOCK, GM_BLOCK), 0) // CHUNK
    ci = lax.broadcasted_iota(jnp.int32, (GM_BLOCK, GM_BLOCK), 1) // CHUNK
    allowed = ci <= ri
    gc = GM_WIDTH // GM_GROUPS
    for g in range(GM_GROUPS):
        w = jnp.where(allowed, ws_ref[g], 0.0).astype(BF16)
        bias = bs_ref[g]
        for r in range(uv_ref.shape[0] // GM_BLOCK):
            rows = slice(r * GM_BLOCK, (r + 1) * GM_BLOCK)
            cols = slice(g * gc, (g + 1) * gc)
            sv = _dot(w, v[rows, cols]) + bias
            o_ref[rows, cols] = (u[rows, cols] * sv).astype(BF16)


def _gmlp(uv, ln_g, ln_b, w_s, b_s):
    T = uv.shape[0]
    tm = TM_GM
    return pl.pallas_call(
        _gmlp_kernel,
        grid=(T // tm,),
        in_specs=[pl.BlockSpec((tm, 2 * GM_WIDTH), lambda i: (i, 0)),
                  pl.BlockSpec((1, GM_WIDTH), lambda i: (0, 0)),
                  pl.BlockSpec((1, GM_WIDTH), lambda i: (0, 0)),
                  pl.BlockSpec((GM_GROUPS, GM_BLOCK, GM_BLOCK), lambda i: (0, 0, 0)),
                  pl.BlockSpec((GM_GROUPS, GM_BLOCK, 1), lambda i: (0, 0, 0))],
        out_specs=pl.BlockSpec((tm, GM_WIDTH), lambda i: (i, 0)),
        out_shape=jax.ShapeDtypeStruct((T, GM_WIDTH), BF16),
        compiler_params=_params("arbitrary"),
        name="gmlp",
    )(uv, ln_g.reshape(1, -1), ln_b.reshape(1, -1), w_s, b_s.reshape(GM_GROUPS, GM_BLOCK, 1))


def _attn_kernel(lam_ref, g_ref, q_ref, k_ref, v_ref, o_ref, qz_ref, vt_ref, *stats, lambda_init):
    nc = 2 * DA_HEADS
    m_ref, acc_ref, s_ref = (stats[j * nc:(j + 1) * nc] for j in range(3))
    i = pl.program_id(1)
    lp = lam_ref[...]
    lam = (jnp.exp(jnp.sum(lp[0:1] * lp[1:2], axis=-1, keepdims=True))
           - jnp.exp(jnp.sum(lp[2:3] * lp[3:4], axis=-1, keepdims=True)) + lambda_init)
    @pl.when(i == 0)
    def _():
        for kb in range(SEQ // TK):
            for h in range(DA_HEADS):
                vt_ref[h * VT_ROWS:h * VT_ROWS + DA_V_DIM, kb * TK:(kb + 1) * TK] = (
                    v_ref[kb * TK:(kb + 1) * TK, h * LANES:(h + 1) * LANES].T)
        for h in range(DA_HEADS):
            vt_ref[h * VT_ROWS + DA_V_DIM:(h + 1) * VT_ROWS, :] = jnp.ones((VT_ROWS - DA_V_DIM, SEQ), BF16)

    row = lax.broadcasted_iota(jnp.int32, (LANES, TQ), 0)
    for h in range(DA_HEADS):
        qt = q_ref[:, h * LANES:(h + 1) * LANES].T
        zero = jnp.zeros_like(qt)
        qz_ref[2 * h] = jnp.where(row < DA_HEAD_DIM, qt, zero)
        qz_ref[2 * h + 1] = jnp.where(row >= DA_HEAD_DIM, qt, zero)
    for c in range(nc):
        m_ref[c][...] = jnp.full_like(m_ref[c], -jnp.inf)
        acc_ref[c][...] = jnp.zeros_like(acc_ref[c])

    def key_rows(kb):
        return pl.ds(pl.multiple_of(kb * TK, TK), TK)

    def scores(kb, c, masked):
        h = c // 2
        s = _dot(k_ref[key_rows(kb), h * LANES:(h + 1) * LANES], qz_ref[c])
        if masked:
            ki = lax.broadcasted_iota(jnp.int32, s.shape, 0) // CHUNK + kb * (TK // CHUNK)
            qi = lax.broadcasted_iota(jnp.int32, s.shape, 1) // CHUNK + i * (TQ // CHUNK)
            s = jnp.where(ki <= qi, s, -jnp.inf)
        return s

    def accumulate(kb, c, s):
        h = c // 2
        m_old = m_ref[c][...]
        m_new = jnp.maximum(m_old, jnp.max(s, axis=0, keepdims=True))
        alpha = jnp.exp2(m_old - m_new)
        e = jnp.exp2((s - m_new).astype(BF16))
        vt = vt_ref[h * VT_ROWS:(h + 1) * VT_ROWS, key_rows(kb)]
        acc_ref[c][...] = alpha * acc_ref[c][...] + _dot(vt, e)
        m_ref[c][...] = m_new

    def fold_and_prefetch(kb, masked_next):
        fresh = {}
        for c in range(nc + ATTN_LOOKAHEAD):
            if c < nc:
                fresh[c] = scores(kb + 1, c, masked_next)
            if c >= ATTN_LOOKAHEAD:
                cc = c - ATTN_LOOKAHEAD
                accumulate(kb, cc, s_ref[cc][...])
                s_ref[cc][...] = fresh.pop(cc)

    first_masked = i * (TQ // TK)

    @pl.when(i == 0)
    def _():
        for c in range(nc):
            s_ref[c][...] = scores(0, c, True)

    @pl.when(i > 0)
    def _():
        for c in range(nc):
            s_ref[c][...] = scores(0, c, False)

        def body(kb, carry):
            fold_and_prefetch(kb, False)
            return carry

        lax.fori_loop(0, first_masked - 1, body, 0)
        fold_and_prefetch(first_masked - 1, True)

    for r in range(TQ // TK - 1):
        fold_and_prefetch(first_masked + r, True)
    for c in range(nc):
        accumulate(first_masked + TQ // TK - 1, c, s_ref[c][...])
    for h in range(DA_HEADS):
        num0, den0 = acc_ref[2 * h][:DA_V_DIM, :], acc_ref[2 * h][DA_V_DIM:DA_V_DIM + 1, :]
        num1, den1 = acc_ref[2 * h + 1][:DA_V_DIM, :], acc_ref[2 * h + 1][DA_V_DIM:DA_V_DIM + 1, :]
        o = num0 / den0 - lam * (num1 / den1)
        ms = jnp.mean(o * o, axis=0, keepdims=True)
        o = o * lax.rsqrt(ms + EPS) * (g_ref[...] * (1.0 - lambda_init))
        o_ref[:, h * LANES:(h + 1) * LANES] = o.T.astype(BF16)


def _diff_attention(qk, v, lam_p, subln_g, lambda_init, batch):
    T = qk.shape[0]
    nq = SEQ // TQ
    W = DA_HEADS * DA_V_DIM
    kernel = functools.partial(_attn_kernel, lambda_init=lambda_init)
    return pl.pallas_call(
        kernel,
        grid=(batch, nq),
        in_specs=[pl.BlockSpec((4, DA_HEAD_DIM), lambda b, i: (0, 0)),
                  pl.BlockSpec((DA_V_DIM, 1), lambda b, i: (0, 0)),
                  pl.BlockSpec((TQ, W), lambda b, i: (b * nq + i, 0)),
                  pl.BlockSpec((SEQ, W), lambda b, i: (b, 1)),
                  pl.BlockSpec((SEQ, W), lambda b, i: (b, 0))],
        out_specs=pl.BlockSpec((TQ, W), lambda b, i: (b * nq + i, 0)),
        out_shape=jax.ShapeDtypeStruct((T, W), BF16),
        scratch_shapes=[pltpu.VMEM((2 * DA_HEADS, LANES, TQ), BF16),
                        pltpu.VMEM((DA_HEADS * VT_ROWS, SEQ), BF16)]
                       + [pltpu.VMEM((1, TQ), F32)] * (2 * DA_HEADS)
                       + [pltpu.VMEM((VT_ROWS, TQ), F32)] * (2 * DA_HEADS)
                       + [pltpu.VMEM((TK, TQ), F32)] * (2 * DA_HEADS),
        compiler_params=_params("arbitrary", "arbitrary"),
        name="diff_attention",
    )(lam_p, subln_g.reshape(-1, 1), qk, qk, v)


def _split3(x):
    hi = x.astype(BF16)
    r = x - hi.astype(F32)
    mid = r.astype(BF16)
    lo = (r - mid.astype(F32)).astype(BF16)
    return hi, mid, lo


def _dn_kernel(dn_ref, a_ref, at_ref, alogt_ref, dtbt_ref, ng_ref,
               o_ref, *state_ref):
    c = pl.program_id(1)
    R = DN_BLOCK
    G = dn_ref.shape[0]

    @pl.when(c == 0)
    def _():
        for ref in state_ref:
            ref[...] = jnp.zeros_like(ref)

    W3 = 3 * DN_WIDTH
    ri = lax.broadcasted_iota(jnp.int32, (R, R), 0)
    ci = lax.broadcasted_iota(jnp.int32, (R, R), 1)
    tril = ri >= ci
    strict = ri > ci
    ones_triu = jnp.where(ci >= ri, 1.0, 0.0).astype(BF16)
    eye = jnp.where(ri == ci, 1.0, 0.0)
    same = {n: (ri // n) == (ci // n) for n in (16, 32, 64, 128)}
    gc_col, gc_row, beta_all = [], [], []
    for j in range(G):
        beta_all.append(jax.nn.sigmoid(a_ref[j]))
        abt = at_ref[j, 0]
        g_row = -jnp.exp(alogt_ref[...]) * jax.nn.softplus(abt + dtbt_ref[...])
        gc_row.append(sum(_dot(part, ones_triu) for part in _split3(g_row)))
        gc_col.append(gc_row[-1].T)

    H = range(G * DN_HEADS)
    seq = [n // DN_HEADS for n in H]
    hd = [n % DN_HEADS for n in H]
    hsl = [slice(hd[n] * DN_HEAD_DIM, (hd[n] + 1) * DN_HEAD_DIM) for n in H]
    q16 = [dn_ref[seq[n], :, hsl[n]] for n in H]
    k16 = [dn_ref[seq[n], :, DN_WIDTH + hd[n] * DN_HEAD_DIM:DN_WIDTH + (hd[n] + 1) * DN_HEAD_DIM] for n in H]
    q = [t.astype(F32) for t in q16]
    k = [t.astype(F32) for t in k16]
    v = [dn_ref[seq[n], :, 2 * DN_WIDTH + hd[n] * DN_HEAD_DIM:2 * DN_WIDTH + (hd[n] + 1) * DN_HEAD_DIM]
         .astype(F32) for n in H]
    gc = [gc_col[seq[n]][:, hd[n]:hd[n] + 1] for n in H]
    beta = [beta_all[seq[n]][:, DN_HEADS + hd[n]:DN_HEADS + hd[n] + 1] for n in H]
    g_last = [gc_col[seq[n]][R - 1:R, hd[n]:hd[n] + 1] for n in H]
    decay = [jnp.exp(jnp.where(tril, gc[n] - gc_row[seq[n]][hd[n]:hd[n] + 1, :], -jnp.inf)) for n in H]
    eg = [jnp.exp(t) for t in gc]
    kb = [k[h] * beta[h] for h in H]
    s1 = [_dot_nt(jnp.concatenate([kb[h].astype(BF16), q16[h]], axis=0), k16[h]) for h in H]
    a_mat = [jnp.where(strict, s1[h][:R] * decay[h], 0.0) for h in H]
    qk = [(s1[h][R:] * decay[h]).astype(BF16) for h in H]
    p = [jnp.where(same[16], -t, 0.0) for t in a_mat]
    t_inv = [eye + t for t in p]
    for _ in range(3):
        p16 = [t.astype(BF16) for t in p]
        p = [_dot(t, t) for t in p16]
        t_inv = [t_inv[h] + _dot(t_inv[h].astype(BF16), p[h].astype(BF16)) for h in H]
    a16 = [t.astype(BF16) for t in a_mat]
    t16 = [t.astype(BF16) for t in t_inv]
    for n in (16, 32, 64, 128):
        off = jnp.logical_not(same[n])
        if 2 * n < R:
            off = jnp.logical_and(same[2 * n], off)
        off16 = jnp.where(off, 1.0, 0.0).astype(BF16)
        tl = [_dot(t16[h], a16[h] * off16).astype(BF16) for h in H]
        t16 = [t16[h] - _dot(tl[h], t16[h]).astype(BF16) * off16 for h in H]
    sol = [_dot(t16[h], jnp.concatenate([v[h] * beta[h], kb[h] * eg[h]], axis=1).astype(BF16))
           for h in H]
    state = [state_ref[h][...] for h in H]
    m1 = [_dot(jnp.concatenate([sol[h][:, DN_HEAD_DIM:], q[h] * eg[h]], axis=0).astype(BF16),
               state[h].astype(BF16)) for h in H]
    vn16 = [(sol[h][:, :DN_HEAD_DIM] - m1[h][:R]).astype(BF16) for h in H]
    o = [m1[h][R:] + _dot(qk[h], vn16[h]) for h in H]
    kd = [(k[h] * jnp.exp(g_last[h] - gc[h])).astype(BF16) for h in H]
    for h in H:
        state_ref[h][...] = state[h] * jnp.exp(g_last[h]) + _dot_tn(kd[h], vn16[h])
    for n in H:
        z = dn_ref[seq[n], :, W3 + hd[n] * DN_HEAD_DIM:W3 + (hd[n] + 1) * DN_HEAD_DIM].astype(F32)
        o_ref[seq[n], :, hsl[n]] = (_rms(o[n], ng_ref[...]) * _silu(z)).astype(BF16)


def _deltanet(dn, ab, a_log, dt_bias, norm_g, batch):
    T = dn.shape[0]
    R = DN_BLOCK
    n = SEQ // R
    G = DN_SEQS if batch % DN_SEQS == 0 else 1
    col8 = lambda t: jnp.zeros((AB_COLS, 1), F32).at[:DN_HEADS, 0].set(t)
    ab_t = ab[:, :AB_COLS].reshape(batch, n, R, AB_COLS).transpose(0, 1, 3, 2)
    const = lambda b, c: (0, 0)
    out = pl.pallas_call(
        _dn_kernel,
        grid=(batch // G, n),
        in_specs=[pl.BlockSpec((G, R, 4 * DN_WIDTH), lambda b, c: (b, c, 0)),
                  pl.BlockSpec((G, R, LANES), lambda b, c: (b, c, 0)),
                  pl.BlockSpec((G, 1, AB_COLS, R), lambda b, c: (b, c, 0, 0)),
                  pl.BlockSpec((AB_COLS, 1), const),
                  pl.BlockSpec((AB_COLS, 1), const),
                  pl.BlockSpec((1, DN_HEAD_DIM), const)],
        out_specs=pl.BlockSpec((G, R, DN_WIDTH), lambda b, c: (b, c, 0)),
        out_shape=jax.ShapeDtypeStruct((batch, SEQ, DN_WIDTH), BF16),
        scratch_shapes=[pltpu.VMEM((DN_HEAD_DIM, DN_HEAD_DIM), F32)] * (G * DN_HEADS),
        compiler_params=_params("arbitrary", "arbitrary"),
        name="deltanet",
    )(dn.reshape(batch, SEQ, -1), ab.reshape(batch, SEQ, -1), ab_t,
      col8(a_log), col8(dt_bias), norm_g.reshape(1, -1))
    return out.reshape(T, DN_WIDTH)


def _pack_bf16_pair(lo, hi):
    lo_bits = lax.bitcast_convert_type(lo.astype(BF16).astype(F32), jnp.uint32)
    hi_bits = lax.bitcast_convert_type(hi.astype(BF16).astype(F32), jnp.uint32)
    return (lo_bits >> 16) | (hi_bits & jnp.uint32(0xFFFF0000))


def _unpack_bf16_pair(words):
    lo = lax.bitcast_convert_type(words << 16, F32)
    hi = lax.bitcast_convert_type(words & jnp.uint32(0xFFFF0000), F32)
    return lo.astype(BF16), hi.astype(BF16)


def _merge_kernel(x_ref, mod_ref, g2_ref, ygm_ref, yda_ref, ydn_ref, gate_ref,
                  wgm_ref, wda_ref, wdn_ref, wout_ref, x1_ref, h2_ref, *maybe_packed_ref):
    D = D_MODEL
    merged = (jax.nn.sigmoid(gate_ref[:, 0:D].astype(F32)) * _dot(ygm_ref[...], wgm_ref[...])
              + jax.nn.sigmoid(gate_ref[:, D:2 * D].astype(F32)) * _dot(yda_ref[...], wda_ref[...])
              + jax.nn.sigmoid(gate_ref[:, 2 * D:3 * D].astype(F32)) * _dot(ydn_ref[...], wdn_ref[...]))
    y = _dot(merged.astype(BF16), wout_ref[...])
    x1 = x_ref[...] + mod_ref[0, 2:3, :] * y
    x1_ref[...] = x1
    h2 = _rms(x1, g2_ref[...]) * (1.0 + mod_ref[0, 4:5, :]) + mod_ref[0, 3:4, :]
    h2_ref[...] = h2.astype(BF16)
    for packed_ref in maybe_packed_ref:
        packed_ref[...] = _pack_bf16_pair(h2[:, :D // 2], h2[:, D // 2:])


def _merge(x, mod, g2, y_gm, y_da, y_dn, gates, w_gm, w_da, w_dn, w_out, with_packed):
    T, D = x.shape
    tm = TM_MERGE
    per_b = SEQ // tm
    row = lambda i: (i, 0)
    const = lambda i: (0, 0)
    out_specs = [pl.BlockSpec((tm, D), row), pl.BlockSpec((tm, D), row)]
    out_shape = [jax.ShapeDtypeStruct((T, D), F32), jax.ShapeDtypeStruct((T, D), BF16)]
    if with_packed:
        out_specs.append(pl.BlockSpec((tm, D // 2), row))
        out_shape.append(jax.ShapeDtypeStruct((T, D // 2), jnp.uint32))
    return pl.pallas_call(
        _merge_kernel,
        grid=(T // tm,),
        in_specs=[pl.BlockSpec((tm, D), row),
                  pl.BlockSpec((1, 6, D), lambda i: (i // per_b, 0, 0)),
                  pl.BlockSpec((1, D), const),
                  pl.BlockSpec((tm, GM_WIDTH), row),
                  pl.BlockSpec((tm, GM_WIDTH), row),
                  pl.BlockSpec((tm, DN_WIDTH), row),
                  pl.BlockSpec((tm, 3 * D), row),
                  pl.BlockSpec((GM_WIDTH, D), const),
                  pl.BlockSpec((GM_WIDTH, D), const),
                  pl.BlockSpec((DN_WIDTH, D), const),
                  pl.BlockSpec((D, D), const)],
        out_specs=out_specs,
        out_shape=out_shape,
        compiler_params=_params("arbitrary"),
        name="merge",
    )(x, mod, g2, y_gm, y_da, y_dn, gates, w_gm, w_da, w_dn, w_out)


def _ffn_kernel(x_ref, h_ref, mod_ref, wg_ref, wu_ref, wd_ref, o_ref):
    h = h_ref[...]
    a = _dot(h, wg_ref[...])
    b = _dot(h, wu_ref[...])
    f = _dot((_silu(a) * b).astype(BF16), wd_ref[...])
    o_ref[...] = x_ref[...] + mod_ref[0, 5:6, :] * f


def _dense_ffn(x1, h2, mod, w_gate, w_up, w_down):
    T, D = x1.shape
    F = w_gate.shape[1]
    tm = TM_FFN
    per_b = SEQ // tm
    row = lambda i: (i, 0)
    const = lambda i: (0, 0)
    return pl.pallas_call(
        _ffn_kernel,
        grid=(T // tm,),
        in_specs=[pl.BlockSpec((tm, D), row),
                  pl.BlockSpec((tm, D), row),
                  pl.BlockSpec((1, 6, D), lambda i: (i // per_b, 0, 0)),
                  pl.BlockSpec((D, F), const, pipeline_mode=pl.Buffered(1)),
                  pl.BlockSpec((D, F), const, pipeline_mode=pl.Buffered(1)),
                  pl.BlockSpec((F, D), const, pipeline_mode=pl.Buffered(1))],
        out_specs=pl.BlockSpec((tm, D), row),
        out_shape=jax.ShapeDtypeStruct((T, D), F32),
        compiler_params=_params("arbitrary"),
        name="dense_ffn",
    )(x1, h2, mod, w_gate, w_up, w_down)


def _router_kernel(h_ref, wr_ref, e_ref, p_ref, r_ref, cnt_ref, base_ref):
    i = pl.program_id(0)
    tm = h_ref.shape[0]

    @pl.when(i == 0)
    def _():
        base_ref[...] = jnp.zeros_like(base_ref)

    logits = _dot_nt(wr_ref[...], h_ref[...])
    row = lax.broadcasted_iota(jnp.int32, logits.shape, 0)
    m1 = jnp.max(logits, axis=0, keepdims=True)
    i1 = jnp.min(jnp.where(logits == m1, row, N_EXPERTS), axis=0, keepdims=True)
    rest = jnp.where(row == i1, -jnp.inf, logits)
    m2 = jnp.max(rest, axis=0, keepdims=True)
    i2 = jnp.min(jnp.where(rest == m2, row, N_EXPERTS), axis=0, keepdims=True)
    e2 = jnp.exp(m2 - m1)
    w1 = 1.0 / (1.0 + e2)
    w2 = e2 / (1.0 + e2)
    oh1 = jnp.where(row == i1, 1.0, 0.0)
    oh2 = jnp.where(row == i2, 1.0, 0.0)
    both = oh1 + oh2
    ti = lax.broadcasted_iota(jnp.int32, (tm, tm), 0)
    tj = lax.broadcasted_iota(jnp.int32, (tm, tm), 1)
    before = jnp.where(ti < tj, 1.0, 0.0).astype(BF16)
    pos = base_ref[...] + _dot(both.astype(BF16), before)
    r1 = jnp.sum(oh1 * pos, axis=0, keepdims=True)
    r2 = jnp.sum(oh2 * pos, axis=0, keepdims=True)
    base_ref[...] = base_ref[...] + jnp.sum(both, axis=1, keepdims=True)
    zi = jnp.zeros((N_EXPERTS - 2, tm), jnp.int32)
    zf = jnp.zeros((N_EXPERTS - 2, tm), F32)
    e_ref[...] = jnp.concatenate([i1, i2, zi], axis=0)
    p_ref[...] = jnp.concatenate([w1, w2, zf], axis=0)
    r_ref[...] = jnp.concatenate([r1.astype(jnp.int32), r2.astype(jnp.int32), zi], axis=0)
    cnt_ref[...] = jnp.broadcast_to(base_ref[...], cnt_ref.shape)


def _router(h2, w_router_t):
    T, D = h2.shape
    tm = TM_ROUTE
    col = lambda i: (0, i)
    return pl.pallas_call(
        _router_kernel,
        grid=(T // tm,),
        in_specs=[pl.BlockSpec((tm, D), lambda i: (i, 0)),
                  pl.BlockSpec((N_EXPERTS, D), lambda i: (0, 0))],
        out_specs=[pl.BlockSpec((N_EXPERTS, tm), col)] * 3 + [pl.BlockSpec((N_EXPERTS, LANES), lambda i: (0, 0))],
        out_shape=[jax.ShapeDtypeStruct((N_EXPERTS, T), jnp.int32),
                   jax.ShapeDtypeStruct((N_EXPERTS, T), F32),
                   jax.ShapeDtypeStruct((N_EXPERTS, T), jnp.int32),
                   jax.ShapeDtypeStruct((N_EXPERTS, LANES), F32)],
        scratch_shapes=[pltpu.VMEM((N_EXPERTS, 1), F32)],
        compiler_params=_params("arbitrary"),
        name="moe_router",
    )(h2, w_router_t)


def _dispatch_kernel(dest_ref, h_ref, xs_in_ref, xs_ref, buf_ref, sem_in, sem_out):
    del xs_in_ref
    i = pl.program_id(0)
    n = pl.num_programs(0)
    tm = buf_ref.shape[1]

    def load(t):
        s = t % DISP_SLOTS
        return pltpu.make_async_copy(h_ref.at[pl.ds(pl.multiple_of(t * tm, tm), tm)], buf_ref.at[s],
                                     sem_in.at[s])

    def wait_rows(t):
        s = t % DISP_SLOTS
        for _ in range(2):
            pltpu.make_async_copy(buf_ref.at[s], xs_ref.at[pl.ds(0, tm)], sem_out.at[s]).wait()

    @pl.when(i == 0)
    def _():
        load(0).start()
        pl.when(n > 1)(lambda: load(1).start())

    load(i).wait()
    slot = i % DISP_SLOTS

    def issue(r, _):
        for k in range(2):
            pltpu.make_async_copy(buf_ref.at[slot, pl.ds(r, 1)],
                                  xs_ref.at[pl.ds(dest_ref[0, 0, k * tm + r], 1)], sem_out.at[slot]).start()
        return 0

    lax.fori_loop(0, tm, issue, 0, unroll=DMA_ISSUE_UNROLL)
    pl.when(i > 0)(lambda: wait_rows(i - 1))
    pl.when(i + 2 < n)(lambda: load(i + 2).start())
    pl.when(i == n - 1)(lambda: wait_rows(i))


def _dispatch(h2, dest, n_slots):
    T, D = h2.shape
    tm = TM_DISP
    dest_t = dest.reshape(2, T // tm, tm).transpose(1, 0, 2).reshape(T // tm, 1, 2 * tm)
    xs0 = jnp.zeros((n_slots, D), h2.dtype)
    return pl.pallas_call(
        _dispatch_kernel,
        grid=(T // tm,),
        in_specs=[pl.BlockSpec((1, 1, 2 * tm), lambda i: (i, 0, 0), memory_space=pltpu.SMEM),
                  pl.BlockSpec(memory_space=pl.ANY),
                  pl.BlockSpec(memory_space=pl.ANY)],
        out_specs=pl.BlockSpec(memory_space=pl.ANY),
        out_shape=jax.ShapeDtypeStruct((n_slots, D), h2.dtype),
        scratch_shapes=[pltpu.VMEM((DISP_SLOTS, tm, D), h2.dtype),
                        pltpu.SemaphoreType.DMA((DISP_SLOTS,)), pltpu.SemaphoreType.DMA((DISP_SLOTS,))],
        input_output_aliases={2: 0},
        compiler_params=_params("arbitrary"),
        name="moe_dispatch",
    )(dest_t, h2, xs0)


def _expert_kernel(be_ref, nu_ref, x_ref, wg_ref, wu_ref, wd_ref, o_ref):
    del be_ref
    used = pl.program_id(0) < nu_ref[0]
    half = D_MODEL // 2

    @pl.when(used)
    def _():
        lo, hi = _unpack_bf16_pair(x_ref[...])
        a = _dot(lo, wg_ref[0, :half, :]) + _dot(hi, wg_ref[0, half:, :])
        b = _dot(lo, wu_ref[0, :half, :]) + _dot(hi, wu_ref[0, half:, :])
        o_ref[...] = _dot((_silu(a) * b).astype(BF16), wd_ref[0])

    @pl.when(jnp.logical_not(used))
    def _():
        o_ref[...] = jnp.zeros_like(o_ref)


def _experts(xs, blk_e, n_used, w_gate, w_up, w_down):
    P = xs.shape[0]
    _, D, F = w_gate.shape
    tm = TM_MOE
    grid_spec = pltpu.PrefetchScalarGridSpec(
        num_scalar_prefetch=2,
        grid=(P // tm,),
        in_specs=[pl.BlockSpec((tm, D // 2), lambda i, be, nu: (i, 0)),
                  pl.BlockSpec((1, D, F), lambda i, be, nu: (be[i], 0, 0), pipeline_mode=pl.Buffered(1)),
                  pl.BlockSpec((1, D, F), lambda i, be, nu: (be[i], 0, 0), pipeline_mode=pl.Buffered(1)),
                  pl.BlockSpec((1, F, D), lambda i, be, nu: (be[i], 0, 0), pipeline_mode=pl.Buffered(1))],
        out_specs=pl.BlockSpec((tm, D), lambda i, be, nu: (i, 0)))
    return pl.pallas_call(
        _expert_kernel,
        grid_spec=grid_spec,
        out_shape=jax.ShapeDtypeStruct((P, D), F32),
        compiler_params=_params("arbitrary"),
        name="moe_experts",
    )(blk_e, n_used, xs, w_gate, w_up, w_down)


def _combine_kernel(dest_ref, dest_next_ref, x_ref, mod_ref, p_ref, g_ref, ys_ref, o_ref, ybuf_ref, sems):
    i = pl.program_id(0)
    tm = x_ref.shape[0]
    slot = i % 2

    def gather(d_ref, s):
        def issue(r, _):
            for k in range(2):
                pltpu.make_async_copy(ys_ref.at[pl.ds(d_ref[0, 0, k * tm + r], 1)],
                                      ybuf_ref.at[s, k, pl.ds(r, 1)], sems.at[s]).start()
            return 0

        lax.fori_loop(0, tm, issue, 0, unroll=DMA_ISSUE_UNROLL)

    pl.when(i == 0)(lambda: gather(dest_ref, 0))
    pl.when(i + 1 < pl.num_programs(0))(lambda: gather(dest_next_ref, 1 - slot))
    for k in range(2):
        pltpu.make_async_copy(ys_ref.at[pl.ds(0, tm)], ybuf_ref.at[slot, k], sems.at[slot]).wait()
    p = p_ref[...]
    f = ybuf_ref[slot, 0] * p[:, 0:1] + ybuf_ref[slot, 1] * p[:, 1:2]
    x2 = x_ref[...] + mod_ref[0, 5:6, :] * f
    o_ref[...] = _rms(x2, g_ref[...])


def _combine(x1, mod, probs, final_g, ys, dest):
    T, D = x1.shape
    tm = TM_COMB
    per_b = SEQ // tm
    n = T // tm
    dest_t = dest.reshape(2, n, tm).transpose(1, 0, 2).reshape(n, 1, 2 * tm)
    return pl.pallas_call(
        _combine_kernel,
        grid=(n,),
        in_specs=[pl.BlockSpec((1, 1, 2 * tm), lambda i: (i, 0, 0), memory_space=pltpu.SMEM),
                  pl.BlockSpec((1, 1, 2 * tm), lambda i: (jnp.minimum(i + 1, n - 1), 0, 0),
                               memory_space=pltpu.SMEM),
                  pl.BlockSpec((tm, D), lambda i: (i, 0)),
                  pl.BlockSpec((1, 6, D), lambda i: (i // per_b, 0, 0)),
                  pl.BlockSpec((tm, 2), lambda i: (i, 0)),
                  pl.BlockSpec((1, D), lambda i: (0, 0)),
                  pl.BlockSpec(memory_space=pl.ANY)],
        out_specs=pl.BlockSpec((tm, D), lambda i: (i, 0)),
        out_shape=jax.ShapeDtypeStruct((T, D), F32),
        scratch_shapes=[pltpu.VMEM((2, 2, tm, D), F32), pltpu.SemaphoreType.DMA((2,))],
        compiler_params=_params("arbitrary"),
        name="moe_combine",
    )(dest_t, dest_t, x1, mod, probs, final_g, ys)


def _moe_layer(x1, h2, h2_packed, mod, w_router, w_gate, w_up, w_down, final_g):
    T, D = x1.shape
    e_idx, probs, rank, counts = _router(h2, w_router.T.astype(BF16))
    counts = counts[:, 0].astype(jnp.int32)
    padded = (counts + TM_MOE - 1) // TM_MOE * TM_MOE
    pad_end = jnp.cumsum(padded)
    pad_start = pad_end - padded
    eid = jnp.arange(N_EXPERTS, dtype=jnp.int32)[:, None, None]
    dest = jnp.sum(jnp.where(e_idx[None, :2] == eid, pad_start[:, None, None], 0), axis=0) + rank[:2]
    n_slots = 2 * T + N_EXPERTS * TM_MOE
    n_blk = n_slots // TM_MOE
    blk_first = jnp.arange(n_blk, dtype=jnp.int32) * TM_MOE
    blk_e = jnp.minimum(jnp.sum(pad_end[None, :] <= blk_first[:, None], axis=1),
                        N_EXPERTS - 1).astype(jnp.int32)
    n_used = (pad_end[-1:] // TM_MOE).astype(jnp.int32)
    xs = _dispatch(h2_packed, dest, n_slots)
    ys = _experts(xs, blk_e, n_used, w_gate.astype(BF16), w_up.astype(BF16), w_down.astype(BF16))
    return _combine(x1, mod, probs[:2].T, final_g.reshape(1, -1), ys, dest)


def _final_norm_kernel(x_ref, g_ref, o_ref):
    o_ref[...] = _rms(x_ref[...], g_ref[...])


def _final_norm(x, g):
    T, D = x.shape
    tm = 1024
    return pl.pallas_call(
        _final_norm_kernel,
        grid=(T // tm,),
        in_specs=[pl.BlockSpec((tm, D), lambda i: (i, 0)), pl.BlockSpec((1, D), lambda i: (0, 0))],
        out_specs=pl.BlockSpec((tm, D), lambda i: (i, 0)),
        out_shape=jax.ShapeDtypeStruct((T, D), F32),
        compiler_params=_params("arbitrary"),
        name="final_norm",
    )(x, g.reshape(1, -1))


def _regroup_w_in(w):
    ab = jnp.pad(w[:, C_GATE:C_GATE + AB_COLS], ((0, 0), (0, LANES - AB_COLS)))
    return jnp.concatenate([w[:, :C_GATE], w[:, C_GATE + AB_COLS:], ab], axis=1).astype(BF16)


def _lambda_init(layer):
    return 0.8 - 0.6 * math.exp(-0.3 * layer)


def kernel(x, c, positions, norm1_g, norm2_g, w_mod, b_mod, w_in, gm_ln_g, gm_ln_b, gm_w_s, gm_b_s, da_lambda, da_subln_g, dn_conv_w, dn_a_log, dn_dt_bias, dn_norm_g, w_br_gm, w_br_da, w_br_dn, w_out, ffn_w_gate, ffn_w_up, ffn_w_down, moe_w_router, moe_w_gate, moe_w_up, moe_w_down, final_g):
    B, S, D = x.shape
    T = B * S
    xt = x.reshape(T, D)
    mod_all = _modulation(c, w_mod, b_mod).reshape(DEPTH, B, 6, D)
    cos_t, sin_t = _rope_tables(positions)
    for layer in range(DEPTH):
        mod = mod_all[layer]
        gm, qk, v, dn, gates, ab = _in_projection(xt, mod, norm1_g[layer].reshape(1, D),
                                                  _regroup_w_in(w_in[layer]), cos_t, sin_t, dn_conv_w[layer])
        y_gm = _gmlp(gm, gm_ln_g[layer], gm_ln_b[layer], gm_w_s[layer], gm_b_s[layer])
        y_da = _diff_attention(qk, v, da_lambda[layer], da_subln_g[layer], _lambda_init(layer), B)
        y_dn = _deltanet(dn, ab, dn_a_log[layer], dn_dt_bias[layer], dn_norm_g[layer], B)
        moe = layer % 2 == 1
        x1, h2, *packed = _merge(xt, mod, norm2_g[layer].reshape(1, D), y_gm, y_da, y_dn, gates,
                                 w_br_gm[layer].astype(BF16), w_br_da[layer].astype(BF16),
                                 w_br_dn[layer].astype(BF16), w_out[layer].astype(BF16), moe)
        if moe:
            xt = _moe_layer(x1, h2, packed[0], mod, moe_w_router[layer // 2], moe_w_gate[layer // 2],
                            moe_w_up[layer // 2], moe_w_down[layer // 2], final_g)
        else:
            xt = _dense_ffn(x1, h2, mod, ffn_w_gate[layer // 2].astype(BF16),
                            ffn_w_up[layer // 2].astype(BF16), ffn_w_down[layer // 2].astype(BF16))
    if DEPTH % 2 == 1:
        xt = _final_norm(xt, final_g)
    return xt.reshape(B, S, D)
```

```python
import functools
import math

import jax
import jax.numpy as jnp
import numpy as np
from jax import lax
from jax.experimental import pallas as pl
from jax.experimental.pallas import tpu as pltpu

D_MODEL = 1024
SEQ = 2048
DEPTH = 2
CHUNK = 64
EPS = 1e-6
GM_WIDTH = D_MODEL // 2
GM_GROUPS = 4
GM_BLOCK = 128
DA_HEADS = 4
DA_HEAD_DIM = 64
DA_V_DIM = 2 * DA_HEAD_DIM
ROPE_THETA = 10000.0
DN_HEADS = 4
DN_HEAD_DIM = 128
DN_WIDTH = DN_HEADS * DN_HEAD_DIM
DN_CONV = 4
N_EXPERTS = 8
FF_EXPERT = 7 * D_MODEL // 2

LANES = 128
SUBLANES = 8
BF16_SUBLANES = 16
VMEM_LIMIT = 56 * 1024 * 1024
DMA_ISSUE_UNROLL = 8

AB_COLS = 2 * DN_HEADS
C_GM, C_QK, C_V, C_DN, C_GATE = 0, 1024, 2048, 2560, 4608
C_AB = C_GATE + 3 * D_MODEL
W_ALL = C_AB + LANES
PROJ_CHUNK = 512

TM_PROJ = 512
TM_GM = 512
TQ = 256
TK = 256
TM_MERGE = 512
TM_FFN = 512
TM_ROUTE = 512
TM_MOE = 512
TM_DISP = 512
DISP_SLOTS = 3
TM_COMB = 512
ATTN_LOOKAHEAD = 2
VT_ROWS = DA_V_DIM + BF16_SUBLANES
DN_SEQS = 2
DN_BLOCK = 256

BF16 = jnp.bfloat16
F32 = jnp.float32


def _params(*sem):
    return pltpu.CompilerParams(dimension_semantics=sem, vmem_limit_bytes=VMEM_LIMIT)


def _dot(a, b):
    return jnp.dot(a, b, preferred_element_type=F32)


def _dot_nt(a, b):
    return lax.dot_general(a, b, (((1,), (1,)), ((), ())), preferred_element_type=F32)


def _dot_tn(a, b):
    return lax.dot_general(a, b, (((0,), (0,)), ((), ())), preferred_element_type=F32)


def _rms(x, g):
    return x * lax.rsqrt(jnp.mean(x * x, axis=-1, keepdims=True) + EPS) * g


def _silu(x):
    return x * jax.nn.sigmoid(x)


def _mod_kernel(c_ref, w_ref, b_ref, o_ref):
    c = c_ref[...]
    o_ref[0] = _dot(_silu(c).astype(BF16), w_ref[0].astype(BF16)) + b_ref[0]


def _modulation(c, w_mod, b_mod):
    B, D = c.shape
    L, _, N = w_mod.shape
    tn = 1536
    return pl.pallas_call(
        _mod_kernel,
        grid=(L, N // tn),
        in_specs=[pl.BlockSpec((B, D), lambda l, j: (0, 0)),
                  pl.BlockSpec((1, D, tn), lambda l, j: (l, 0, j)),
                  pl.BlockSpec((1, 1, tn), lambda l, j: (l, 0, j))],
        out_specs=pl.BlockSpec((1, B, tn), lambda l, j: (l, 0, j)),
        out_shape=jax.ShapeDtypeStruct((L, B, N), F32),
        compiler_params=_params("arbitrary", "arbitrary"),
        name="modulation",
    )(c, w_mod, b_mod.reshape(L, 1, N))


def _rope_tab_kernel(pos_ref, inv_ref, sgn_ref, cos_ref, sin_ref):
    ang = pos_ref[...].astype(F32) * inv_ref[...]
    cos_ref[...] = jnp.cos(ang)
    sin_ref[...] = jnp.sin(ang) * sgn_ref[...]


def _rope_tables(positions):
    T = positions.size
    inv_freq = ROPE_THETA ** (-jnp.arange(0, DA_HEAD_DIM, 2, dtype=F32) / DA_HEAD_DIM)
    inv = jnp.tile(inv_freq, LANES // (DA_HEAD_DIM // 2)).reshape(1, LANES)
    half = DA_HEAD_DIM // 2
    sgn = np.tile(np.concatenate([-np.ones(half), np.ones(half)]), LANES // DA_HEAD_DIM)
    sgn = jnp.asarray(sgn, F32).reshape(1, LANES)
    tm = 1024
    return pl.pallas_call(
        _rope_tab_kernel,
        grid=(T // tm,),
        in_specs=[pl.BlockSpec((tm, 1), lambda i: (i, 0)),
                  pl.BlockSpec((1, LANES), lambda i: (0, 0)),
                  pl.BlockSpec((1, LANES), lambda i: (0, 0))],
        out_specs=[pl.BlockSpec((tm, LANES), lambda i: (i, 0))] * 2,
        out_shape=[jax.ShapeDtypeStruct((T, LANES), F32)] * 2,
        compiler_params=_params("arbitrary"),
        name="rope_tables",
    )(positions.reshape(T, 1), inv, sgn)


def _rope_rows(x, cos, sin):
    lane = lax.broadcasted_iota(jnp.int32, x.shape, 1)
    first = (lane % DA_HEAD_DIM) < (DA_HEAD_DIM // 2)
    partner = jnp.where(first, pltpu.roll(x, LANES - DA_HEAD_DIM // 2, axis=1),
                        pltpu.roll(x, DA_HEAD_DIM // 2, axis=1))
    return x * cos + partner * sin


def _proj_kernel(x_ref, mod_ref, g_ref, w_ref, cos_ref, sin_ref, cw_ref,
                 gm_ref, qk_ref, v_ref, dn_ref, gate_ref, ab_ref, prev_ref):
    tm = x_ref.shape[0]

    @pl.when(pl.program_id(0) % (SEQ // tm) == 0)
    def _():
        prev_ref[...] = jnp.zeros_like(prev_ref)

    def conv_silu(y, idx):
        cw = cw_ref[:, idx * PROJ_CHUNK:(idx + 1) * PROJ_CHUNK]
        ycat = jnp.concatenate([prev_ref[idx], y], axis=0)
        out = y * cw[DN_CONV - 1:DN_CONV, :]
        for j in range(DN_CONV - 1):
            out = out + pltpu.roll(ycat, DN_CONV - 1 - j, axis=0)[SUBLANES:, :] * cw[j:j + 1, :]
        prev_ref[idx] = y[tm - SUBLANES:, :]
        return _silu(out)

    def l2norm_heads(t, mult):
        parts = []
        for j in range(PROJ_CHUNK // DN_HEAD_DIM):
            seg = t[:, j * DN_HEAD_DIM:(j + 1) * DN_HEAD_DIM]
            parts.append(seg * (lax.rsqrt(jnp.sum(seg * seg, axis=-1, keepdims=True) + EPS) * mult))
        return jnp.concatenate(parts, axis=1)

    x = x_ref[...]
    shift = mod_ref[0, 0:1, :]
    scale = mod_ref[0, 1:2, :]
    h = (_rms(x, g_ref[...]) * (1.0 + scale) + shift).astype(BF16)
    cos = cos_ref[...]
    sin = sin_ref[...]
    q_scale = DA_HEAD_DIM ** -0.5 * math.log2(math.e)

    def rope_chunk(y, mult):
        parts = [_rope_rows(y[:, j * LANES:(j + 1) * LANES], cos, sin) * mult
                 for j in range(PROJ_CHUNK // LANES)]
        return jnp.concatenate(parts, axis=1)

    dn_chunks = list(range(C_DN, C_GATE, PROJ_CHUNK))
    gate_chunks = list(range(C_GATE, C_AB, PROJ_CHUNK))
    order = list(range(0, C_DN, PROJ_CHUNK))
    for j, c in enumerate(dn_chunks):
        order += [c] + gate_chunks[2 * j:2 * j + 2]
    assert sorted(order) == list(range(0, C_AB, PROJ_CHUNK))
    for c0 in order:
        y = _dot(h, w_ref[:, c0:c0 + PROJ_CHUNK])
        if c0 < C_QK:
            gm_ref[:, c0:c0 + PROJ_CHUNK] = y.astype(BF16)
        elif c0 < C_V:
            mult = q_scale if c0 == C_QK else 1.0
            qk_ref[:, c0 - C_QK:c0 - C_QK + PROJ_CHUNK] = rope_chunk(y, mult).astype(BF16)
        elif c0 < C_DN:
            v_ref[...] = y.astype(BF16)
        elif c0 < C_GATE:
            idx = (c0 - C_DN) // PROJ_CHUNK
            if idx < 3:
                y = conv_silu(y, idx)
            if idx < 2:
                y = l2norm_heads(y, DN_HEAD_DIM ** -0.5 if idx == 0 else 1.0)
            dn_ref[:, c0 - C_DN:c0 - C_DN + PROJ_CHUNK] = y.astype(BF16)
        else:
            gate_ref[:, c0 - C_GATE:c0 - C_GATE + PROJ_CHUNK] = y.astype(BF16)
    ab_ref[...] = _dot(h, w_ref[:, C_AB:W_ALL])


def _in_projection(x, mod, g, w_all, cos_t, sin_t, conv_w):
    T, D = x.shape
    tm = TM_PROJ
    per_b = SEQ // tm
    row = lambda i: (i, 0)
    widths = (C_QK - C_GM, C_V - C_QK, C_DN - C_V, C_GATE - C_DN, C_AB - C_GATE)
    return pl.pallas_call(
        _proj_kernel,
        grid=(T // tm,),
        in_specs=[pl.BlockSpec((tm, D), row),
                  pl.BlockSpec((1, 6, D), lambda i: (i // per_b, 0, 0)),
                  pl.BlockSpec((1, D), lambda i: (0, 0)),
                  pl.BlockSpec((D, W_ALL), lambda i: (0, 0), pipeline_mode=pl.Buffered(1)),
                  pl.BlockSpec((tm, LANES), row),
                  pl.BlockSpec((tm, LANES), row),
                  pl.BlockSpec((DN_CONV, 3 * DN_WIDTH), lambda i: (0, 0))],
        out_specs=[pl.BlockSpec((tm, w), row) for w in widths] + [pl.BlockSpec((tm, LANES), row)],
        out_shape=[jax.ShapeDtypeStruct((T, w), BF16) for w in widths]
                  + [jax.ShapeDtypeStruct((T, LANES), F32)],
        scratch_shapes=[pltpu.VMEM((3, SUBLANES, PROJ_CHUNK), F32)],
        compiler_params=_params("arbitrary"),
        name="in_projection",
    )(x, mod, g, w_all, cos_t, sin_t, conv_w)


def _gelu(x):
    return 0.5 * x * (1.0 + lax.erf(x * np.float32(math.sqrt(0.5))))


def _gmlp_kernel(uv_ref, lng_ref, lnb_ref, ws_ref, bs_ref, o_ref):
    u = _gelu(uv_ref[:, :GM_WIDTH].astype(F32))
    v = _gelu(uv_ref[:, GM_WIDTH:].astype(F32))
    mu = jnp.mean(v, axis=-1, keepdims=True)
    vc = v - mu
    var = jnp.mean(vc * vc, axis=-1, keepdims=True)
    v = (vc * lax.rsqrt(var + EPS) * lng_ref[...] + lnb_ref[...]).astype(BF16)
    ri = lax.broadcasted_iota(jnp.int32, (GM_BLOCK, GM_BLOCK), 0) // CHUNK
    ci = lax.broadcasted_iota(jnp.int32, (GM_BLOCK, GM_BLOCK), 1) // CHUNK
    allowed = ci <= ri
    gc = GM_WIDTH // GM_GROUPS
    for g in range(GM_GROUPS):
        w = jnp.where(allowed, ws_ref[g], 0.0).astype(BF16)
        bias = bs_ref[g]
        for r in range(uv_ref.shape[0] // GM_BLOCK):
            rows = slice(r * GM_BLOCK, (r + 1) * GM_BLOCK)
            cols = slice(g * gc, (g + 1) * gc)
            sv = _dot(w, v[rows, cols]) + bias
            o_ref[rows, cols] = (u[rows, cols] * sv).astype(BF16)


def _gmlp(uv, ln_g, ln_b, w_s, b_s):
    T = uv.shape[0]
    tm = TM_GM
    return pl.pallas_call(
        _gmlp_kernel,
        grid=(T // tm,),
        in_specs=[pl.BlockSpec((tm, 2 * GM_WIDTH), lambda i: (i, 0)),
                  pl.BlockSpec((1, GM_WIDTH), lambda i: (0, 0)),
                  pl.BlockSpec((1, GM_WIDTH), lambda i: (0, 0)),
                  pl.BlockSpec((GM_GROUPS, GM_BLOCK, GM_BLOCK), lambda i: (0, 0, 0)),
                  pl.BlockSpec((GM_GROUPS, GM_BLOCK, 1), lambda i: (0, 0, 0))],
        out_specs=pl.BlockSpec((tm, GM_WIDTH), lambda i: (i, 0)),
        out_shape=jax.ShapeDtypeStruct((T, GM_WIDTH), BF16),
        compiler_params=_params("arbitrary"),
        name="gmlp",
    )(uv, ln_g.reshape(1, -1), ln_b.reshape(1, -1), w_s, b_s.reshape(GM_GROUPS, GM_BLOCK, 1))


def _attn_kernel(lam_ref, g_ref, q_ref, k_ref, v_ref, o_ref, qz_ref, vt_ref, *stats, lambda_init):
    nc = 2 * DA_HEADS
    m_ref, acc_ref, s_ref = (stats[j * nc:(j + 1) * nc] for j in range(3))
    i = pl.program_id(1)
    lp = lam_ref[...]
    lam = (jnp.exp(jnp.sum(lp[0:1] * lp[1:2], axis=-1, keepdims=True))
           - jnp.exp(jnp.sum(lp[2:3] * lp[3:4], axis=-1, keepdims=True)) + lambda_init)
    @pl.when(i == 0)
    def _():
        for kb in range(SEQ // TK):
            for h in range(DA_HEADS):
                vt_ref[h * VT_ROWS:h * VT_ROWS + DA_V_DIM, kb * TK:(kb + 1) * TK] = (
                    v_ref[kb * TK:(kb + 1) * TK, h * LANES:(h + 1) * LANES].T)
        for h in range(DA_HEADS):
            vt_ref[h * VT_ROWS + DA_V_DIM:(h + 1) * VT_ROWS, :] = jnp.ones((VT_ROWS - DA_V_DIM, SEQ), BF16)

    row = lax.broadcasted_iota(jnp.int32, (LANES, TQ), 0)
    for h in range(DA_HEADS):
        qt = q_ref[:, h * LANES:(h + 1) * LANES].T
        zero = jnp.zeros_like(qt)
        qz_ref[2 * h] = jnp.where(row < DA_HEAD_DIM, qt, zero)
        qz_ref[2 * h + 1] = jnp.where(row >= DA_HEAD_DIM, qt, zero)
    for c in range(nc):
        m_ref[c][...] = jnp.full_like(m_ref[c], -jnp.inf)
        acc_ref[c][...] = jnp.zeros_like(acc_ref[c])

    def key_rows(kb):
        return pl.ds(pl.multiple_of(kb * TK, TK), TK)

    def scores(kb, c, masked):
        h = c // 2
        s = _dot(k_ref[key_rows(kb), h * LANES:(h + 1) * LANES], qz_ref[c])
        if masked:
            ki = lax.broadcasted_iota(jnp.int32, s.shape, 0) // CHUNK + kb * (TK // CHUNK)
            qi = lax.broadcasted_iota(jnp.int32, s.shape, 1) // CHUNK + i * (TQ // CHUNK)
            s = jnp.where(ki <= qi, s, -jnp.inf)
        return s

    def accumulate(kb, c, s):
        h = c // 2
        m_old = m_ref[c][...]
        m_new = jnp.maximum(m_old, jnp.max(s, axis=0, keepdims=True))
        alpha = jnp.exp2(m_old - m_new)
        e = jnp.exp2((s - m_new).astype(BF16))
        vt = vt_ref[h * VT_ROWS:(h + 1) * VT_ROWS, key_rows(kb)]
        acc_ref[c][...] = alpha * acc_ref[c][...] + _dot(vt, e)
        m_ref[c][...] = m_new

    def fold_and_prefetch(kb, masked_next):
        fresh = {}
        for c in range(nc + ATTN_LOOKAHEAD):
            if c < nc:
                fresh[c] = scores(kb + 1, c, masked_next)
            if c >= ATTN_LOOKAHEAD:
                cc = c - ATTN_LOOKAHEAD
                accumulate(kb, cc, s_ref[cc][...])
                s_ref[cc][...] = fresh.pop(cc)

    first_masked = i * (TQ // TK)

    @pl.when(i == 0)
    def _():
        for c in range(nc):
            s_ref[c][...] = scores(0, c, True)

    @pl.when(i > 0)
    def _():
        for c in range(nc):
            s_ref[c][...] = scores(0, c, False)

        def body(kb, carry):
            fold_and_prefetch(kb, False)
            return carry

        lax.fori_loop(0, first_masked - 1, body, 0)
        fold_and_prefetch(first_masked - 1, True)

    for r in range(TQ // TK - 1):
        fold_and_prefetch(first_masked + r, True)
    for c in range(nc):
        accumulate(first_masked + TQ // TK - 1, c, s_ref[c][...])
    for h in range(DA_HEADS):
        num0, den0 = acc_ref[2 * h][:DA_V_DIM, :], acc_ref[2 * h][DA_V_DIM:DA_V_DIM + 1, :]
        num1, den1 = acc_ref[2 * h + 1][:DA_V_DIM, :], acc_ref[2 * h + 1][DA_V_DIM:DA_V_DIM + 1, :]
        o = num0 / den0 - lam * (num1 / den1)
        ms = jnp.mean(o * o, axis=0, keepdims=True)
        o = o * lax.rsqrt(ms + EPS) * (g_ref[...] * (1.0 - lambda_init))
        o_ref[:, h * LANES:(h + 1) * LANES] = o.T.astype(BF16)


def _diff_attention(qk, v, lam_p, subln_g, lambda_init, batch):
    T = qk.shape[0]
    nq = SEQ // TQ
    W = DA_HEADS * DA_V_DIM
    kernel = functools.partial(_attn_kernel, lambda_init=lambda_init)
    return pl.pallas_call(
        kernel,
        grid=(batch, nq),
        in_specs=[pl.BlockSpec((4, DA_HEAD_DIM), lambda b, i: (0, 0)),
                  pl.BlockSpec((DA_V_DIM, 1), lambda b, i: (0, 0)),
                  pl.BlockSpec((TQ, W), lambda b, i: (b * nq + i, 0)),
                  pl.BlockSpec((SEQ, W), lambda b, i: (b, 1)),
                  pl.BlockSpec((SEQ, W), lambda b, i: (b, 0))],
        out_specs=pl.BlockSpec((TQ, W), lambda b, i: (b * nq + i, 0)),
        out_shape=jax.ShapeDtypeStruct((T, W), BF16),
        scratch_shapes=[pltpu.VMEM((2 * DA_HEADS, LANES, TQ), BF16),
                        pltpu.VMEM((DA_HEADS * VT_ROWS, SEQ), BF16)]
                       + [pltpu.VMEM((1, TQ), F32)] * (2 * DA_HEADS)
                       + [pltpu.VMEM((VT_ROWS, TQ), F32)] * (2 * DA_HEADS)
                       + [pltpu.VMEM((TK, TQ), F32)] * (2 * DA_HEADS),
        compiler_params=_params("arbitrary", "arbitrary"),
        name="diff_attention",
    )(lam_p, subln_g.reshape(-1, 1), qk, qk, v)


def _split3(x):
    hi = x.astype(BF16)
    r = x - hi.astype(F32)
    mid = r.astype(BF16)
    lo = (r - mid.astype(F32)).astype(BF16)
    return hi, mid, lo


def _dn_kernel(dn_ref, a_ref, at_ref, alogt_ref, dtbt_ref, ng_ref,
               o_ref, *state_ref):
    c = pl.program_id(1)
    R = DN_BLOCK
    G = dn_ref.shape[0]

    @pl.when(c == 0)
    def _():
        for ref in state_ref:
            ref[...] = jnp.zeros_like(ref)

    W3 = 3 * DN_WIDTH
    ri = lax.broadcasted_iota(jnp.int32, (R, R), 0)
    ci = lax.broadcasted_iota(jnp.int32, (R, R), 1)
    tril = ri >= ci
    strict = ri > ci
    ones_triu = jnp.where(ci >= ri, 1.0, 0.0).astype(BF16)
    eye = jnp.where(ri == ci, 1.0, 0.0)
    same = {n: (ri // n) == (ci // n) for n in (16, 32, 64, 128)}
    gc_col, gc_row, beta_all = [], [], []
    for j in range(G):
        beta_all.append(jax.nn.sigmoid(a_ref[j]))
        abt = at_ref[j, 0]
        g_row = -jnp.exp(alogt_ref[...]) * jax.nn.softplus(abt + dtbt_ref[...])
        gc_row.append(sum(_dot(part, ones_triu) for part in _split3(g_row)))
        gc_col.append(gc_row[-1].T)

    H = range(G * DN_HEADS)
    seq = [n // DN_HEADS for n in H]
    hd = [n % DN_HEADS for n in H]
    hsl = [slice(hd[n] * DN_HEAD_DIM, (hd[n] + 1) * DN_HEAD_DIM) for n in H]
    q16 = [dn_ref[seq[n], :, hsl[n]] for n in H]
    k16 = [dn_ref[seq[n], :, DN_WIDTH + hd[n] * DN_HEAD_DIM:DN_WIDTH + (hd[n] + 1) * DN_HEAD_DIM] for n in H]
    q = [t.astype(F32) for t in q16]
    k = [t.astype(F32) for t in k16]
    v = [dn_ref[seq[n], :, 2 * DN_WIDTH + hd[n] * DN_HEAD_DIM:2 * DN_WIDTH + (hd[n] + 1) * DN_HEAD_DIM]
         .astype(F32) for n in H]
    gc = [gc_col[seq[n]][:, hd[n]:hd[n] + 1] for n in H]
    beta = [beta_all[seq[n]][:, DN_HEADS + hd[n]:DN_HEADS + hd[n] + 1] for n in H]
    g_last = [gc_col[seq[n]][R - 1:R, hd[n]:hd[n] + 1] for n in H]
    decay = [jnp.exp(jnp.where(tril, gc[n] - gc_row[seq[n]][hd[n]:hd[n] + 1, :], -jnp.inf)) for n in H]
    eg = [jnp.exp(t) for t in gc]
    kb = [k[h] * beta[h] for h in H]
    s1 = [_dot_nt(jnp.concatenate([kb[h].astype(BF16), q16[h]], axis=0), k16[h]) for h in H]
    a_mat = [jnp.where(strict, s1[h][:R] * decay[h], 0.0) for h in H]
    qk = [(s1[h][R:] * decay[h]).astype(BF16) for h in H]
    p = [jnp.where(same[16], -t, 0.0) for t in a_mat]
    t_inv = [eye + t for t in p]
    for _ in range(3):
        p16 = [t.astype(BF16) for t in p]
        p = [_dot(t, t) for t in p16]
        t_inv = [t_inv[h] + _dot(t_inv[h].astype(BF16), p[h].astype(BF16)) for h in H]
    a16 = [t.astype(BF16) for t in a_mat]
    t16 = [t.astype(BF16) for t in t_inv]
    for n in (16, 32, 64, 128):
        off = jnp.logical_not(same[n])
        if 2 * n < R:
            off = jnp.logical_and(same[2 * n], off)
        off16 = jnp.where(off, 1.0, 0.0).astype(BF16)
        tl = [_dot(t16[h], a16[h] * off16).astype(BF16) for h in H]
        t16 = [t16[h] - _dot(tl[h], t16[h]).astype(BF16) * off16 for h in H]
    sol = [_dot(t16[h], jnp.concatenate([v[h] * beta[h], kb[h] * eg[h]], axis=1).astype(BF16))
           for h in H]
    state = [state_ref[h][...] for h in H]
    m1 = [_dot(jnp.concatenate([sol[h][:, DN_HEAD_DIM:], q[h] * eg[h]], axis=0).astype(BF16),
               state[h].astype(BF16)) for h in H]
    vn16 = [(sol[h][:, :DN_HEAD_DIM] - m1[h][:R]).astype(BF16) for h in H]
    o = [m1[h][R:] + _dot(qk[h], vn16[h]) for h in H]
    kd = [(k[h] * jnp.exp(g_last[h] - gc[h])).astype(BF16) for h in H]
    for h in H:
        state_ref[h][...] = state[h] * jnp.exp(g_last[h]) + _dot_tn(kd[h], vn16[h])
    for n in H:
        z = dn_ref[seq[n], :, W3 + hd[n] * DN_HEAD_DIM:W3 + (hd[n] + 1) * DN_HEAD_DIM].astype(F32)
        o_ref[seq[n], :, hsl[n]] = (_rms(o[n], ng_ref[...]) * _silu(z)).astype(BF16)


def _deltanet(dn, ab, a_log, dt_bias, norm_g, batch):
    T = dn.shape[0]
    R = DN_BLOCK
    n = SEQ // R
    G = DN_SEQS if batch % DN_SEQS == 0 else 1
    col8 = lambda t: jnp.zeros((AB_COLS, 1), F32).at[:DN_HEADS, 0].set(t)
    ab_t = ab[:, :AB_COLS].reshape(batch, n, R, AB_COLS).transpose(0, 1, 3, 2)
    const = lambda b, c: (0, 0)
    out = pl.pallas_call(
        _dn_kernel,
        grid=(batch // G, n),
        in_specs=[pl.BlockSpec((G, R, 4 * DN_WIDTH), lambda b, c: (b, c, 0)),
                  pl.BlockSpec((G, R, LANES), lambda b, c: (b, c, 0)),
                  pl.BlockSpec((G, 1, AB_COLS, R), lambda b, c: (b, c, 0, 0)),
                  pl.BlockSpec((AB_COLS, 1), const),
                  pl.BlockSpec((AB_COLS, 1), const),
                  pl.BlockSpec((1, DN_HEAD_DIM), const)],
        out_specs=pl.BlockSpec((G, R, DN_WIDTH), lambda b, c: (b, c, 0)),
        out_shape=jax.ShapeDtypeStruct((batch, SEQ, DN_WIDTH), BF16),
        scratch_shapes=[pltpu.VMEM((DN_HEAD_DIM, DN_HEAD_DIM), F32)] * (G * DN_HEADS),
        compiler_params=_params("arbitrary", "arbitrary"),
        name="deltanet",
    )(dn.reshape(batch, SEQ, -1), ab.reshape(batch, SEQ, -1), ab_t,
      col8(a_log), col8(dt_bias), norm_g.reshape(1, -1))
    return out.reshape(T, DN_WIDTH)


def _pack_bf16_pair(lo, hi):
    lo_bits = lax.bitcast_convert_type(lo.astype(BF16).astype(F32), jnp.uint32)
    hi_bits = lax.bitcast_convert_type(hi.astype(BF16).astype(F32), jnp.uint32)
    return (lo_bits >> 16) | (hi_bits & jnp.uint32(0xFFFF0000))


def _unpack_bf16_pair(words):
    lo = lax.bitcast_convert_type(words << 16, F32)
    hi = lax.bitcast_convert_type(words & jnp.uint32(0xFFFF0000), F32)
    return lo.astype(BF16), hi.astype(BF16)


def _merge_kernel(x_ref, mod_ref, g2_ref, ygm_ref, yda_ref, ydn_ref, gate_ref,
                  wgm_ref, wda_ref, wdn_ref, wout_ref, x1_ref, h2_ref, *maybe_packed_ref):
    D = D_MODEL
    def gate(j):
        return 0.5 * jnp.tanh(0.5 * gate_ref[:, j * D:(j + 1) * D].astype(F32)) + 0.5

    merged = (gate(0) * _dot(ygm_ref[...], wgm_ref[...])
              + gate(1) * _dot(yda_ref[...], wda_ref[...])
              + gate(2) * _dot(ydn_ref[...], wdn_ref[...]))
    y = _dot(merged.astype(BF16), wout_ref[...])
    x1 = x_ref[...] + mod_ref[0, 2:3, :] * y
    x1_ref[...] = x1
    h2 = _rms(x1, g2_ref[...]) * (1.0 + mod_ref[0, 4:5, :]) + mod_ref[0, 3:4, :]
    h2_ref[...] = h2.astype(BF16)
    for packed_ref in maybe_packed_ref:
        packed_ref[...] = _pack_bf16_pair(h2[:, :D // 2], h2[:, D // 2:])


def _merge(x, mod, g2, y_gm, y_da, y_dn, gates, w_gm, w_da, w_dn, w_out, with_packed):
    T, D = x.shape
    tm = TM_MERGE
    per_b = SEQ // tm
    row = lambda i: (i, 0)
    const = lambda i: (0, 0)
    out_specs = [pl.BlockSpec((tm, D), row), pl.BlockSpec((tm, D), row)]
    out_shape = [jax.ShapeDtypeStruct((T, D), F32), jax.ShapeDtypeStruct((T, D), BF16)]
    if with_packed:
        out_specs.append(pl.BlockSpec((tm, D // 2), row))
        out_shape.append(jax.ShapeDtypeStruct((T, D // 2), jnp.uint32))
    return pl.pallas_call(
        _merge_kernel,
        grid=(T // tm,),
        in_specs=[pl.BlockSpec((tm, D), row),
                  pl.BlockSpec((1, 6, D), lambda i: (i // per_b, 0, 0)),
                  pl.BlockSpec((1, D), const),
                  pl.BlockSpec((tm, GM_WIDTH), row),
                  pl.BlockSpec((tm, GM_WIDTH), row),
                  pl.BlockSpec((tm, DN_WIDTH), row),
                  pl.BlockSpec((tm, 3 * D), row),
                  pl.BlockSpec((GM_WIDTH, D), const),
                  pl.BlockSpec((GM_WIDTH, D), const),
                  pl.BlockSpec((DN_WIDTH, D), const),
                  pl.BlockSpec((D, D), const)],
        out_specs=out_specs,
        out_shape=out_shape,
        compiler_params=_params("arbitrary"),
        name="merge",
    )(x, mod, g2, y_gm, y_da, y_dn, gates, w_gm, w_da, w_dn, w_out)


def _ffn_kernel(x_ref, h_ref, mod_ref, wg_ref, wu_ref, wd_ref, o_ref):
    h = h_ref[...]
    a = _dot(h, wg_ref[...])
    b = _dot(h, wu_ref[...])
    f = _dot((_silu(a) * b).astype(BF16), wd_ref[...])
    o_ref[...] = x_ref[...] + mod_ref[0, 5:6, :] * f


def _dense_ffn(x1, h2, mod, w_gate, w_up, w_down):
    T, D = x1.shape
    F = w_gate.shape[1]
    tm = TM_FFN
    per_b = SEQ // tm
    row = lambda i: (i, 0)
    const = lambda i: (0, 0)
    return pl.pallas_call(
        _ffn_kernel,
        grid=(T // tm,),
        in_specs=[pl.BlockSpec((tm, D), row),
                  pl.BlockSpec((tm, D), row),
                  pl.BlockSpec((1, 6, D), lambda i: (i // per_b, 0, 0)),
                  pl.BlockSpec((D, F), const, pipeline_mode=pl.Buffered(1)),
                  pl.BlockSpec((D, F), const, pipeline_mode=pl.Buffered(1)),
                  pl.BlockSpec((F, D), const, pipeline_mode=pl.Buffered(1))],
        out_specs=pl.BlockSpec((tm, D), row),
        out_shape=jax.ShapeDtypeStruct((T, D), F32),
        compiler_params=_params("arbitrary"),
        name="dense_ffn",
    )(x1, h2, mod, w_gate, w_up, w_down)


def _router_kernel(h_ref, wr_ref, e_ref, p_ref, r_ref, cnt_ref, base_ref):
    i = pl.program_id(0)
    tm = h_ref.shape[0]

    @pl.when(i == 0)
    def _():
        base_ref[...] = jnp.zeros_like(base_ref)

    logits = _dot_nt(wr_ref[...], h_ref[...])
    row = lax.broadcasted_iota(jnp.int32, logits.shape, 0)
    m1 = jnp.max(logits, axis=0, keepdims=True)
    i1 = jnp.min(jnp.where(logits == m1, row, N_EXPERTS), axis=0, keepdims=True)
    rest = jnp.where(row == i1, -jnp.inf, logits)
    m2 = jnp.max(rest, axis=0, keepdims=True)
    i2 = jnp.min(jnp.where(rest == m2, row, N_EXPERTS), axis=0, keepdims=True)
    e2 = jnp.exp(m2 - m1)
    w1 = 1.0 / (1.0 + e2)
    w2 = e2 / (1.0 + e2)
    oh1 = jnp.where(row == i1, 1.0, 0.0)
    oh2 = jnp.where(row == i2, 1.0, 0.0)
    both = oh1 + oh2
    ti = lax.broadcasted_iota(jnp.int32, (tm, tm), 0)
    tj = lax.broadcasted_iota(jnp.int32, (tm, tm), 1)
    before = jnp.where(ti < tj, 1.0, 0.0).astype(BF16)
    pos = base_ref[...] + _dot(both.astype(BF16), before)
    r1 = jnp.sum(oh1 * pos, axis=0, keepdims=True)
    r2 = jnp.sum(oh2 * pos, axis=0, keepdims=True)
    base_ref[...] = base_ref[...] + jnp.sum(both, axis=1, keepdims=True)
    zi = jnp.zeros((N_EXPERTS - 2, tm), jnp.int32)
    zf = jnp.zeros((N_EXPERTS - 2, tm), F32)
    e_ref[...] = jnp.concatenate([i1, i2, zi], axis=0)
    p_ref[...] = jnp.concatenate([w1, w2, zf], axis=0)
    r_ref[...] = jnp.concatenate([r1.astype(jnp.int32), r2.astype(jnp.int32), zi], axis=0)
    cnt_ref[...] = jnp.broadcast_to(base_ref[...], cnt_ref.shape)


def _router(h2, w_router_t):
    T, D = h2.shape
    tm = TM_ROUTE
    col = lambda i: (0, i)
    return pl.pallas_call(
        _router_kernel,
        grid=(T // tm,),
        in_specs=[pl.BlockSpec((tm, D), lambda i: (i, 0)),
                  pl.BlockSpec((N_EXPERTS, D), lambda i: (0, 0))],
        out_specs=[pl.BlockSpec((N_EXPERTS, tm), col)] * 3 + [pl.BlockSpec((N_EXPERTS, LANES), lambda i: (0, 0))],
        out_shape=[jax.ShapeDtypeStruct((N_EXPERTS, T), jnp.int32),
                   jax.ShapeDtypeStruct((N_EXPERTS, T), F32),
                   jax.ShapeDtypeStruct((N_EXPERTS, T), jnp.int32),
                   jax.ShapeDtypeStruct((N_EXPERTS, LANES), F32)],
        scratch_shapes=[pltpu.VMEM((N_EXPERTS, 1), F32)],
        compiler_params=_params("arbitrary"),
        name="moe_router",
    )(h2, w_router_t)


def _dispatch_kernel(dest_ref, h_ref, xs_in_ref, xs_ref, buf_ref, sem_in, sem_out):
    del xs_in_ref
    i = pl.program_id(0)
    n = pl.num_programs(0)
    tm = buf_ref.shape[1]

    def load(t):
        s = t % DISP_SLOTS
        return pltpu.make_async_copy(h_ref.at[pl.ds(pl.multiple_of(t * tm, tm), tm)], buf_ref.at[s],
                                     sem_in.at[s])

    def wait_rows(t):
        s = t % DISP_SLOTS
        for _ in range(2):
            pltpu.make_async_copy(buf_ref.at[s], xs_ref.at[pl.ds(0, tm)], sem_out.at[s]).wait()

    @pl.when(i == 0)
    def _():
        load(0).start()
        pl.when(n > 1)(lambda: load(1).start())

    load(i).wait()
    slot = i % DISP_SLOTS

    def issue(r, _):
        for k in range(2):
            pltpu.make_async_copy(buf_ref.at[slot, pl.ds(r, 1)],
                                  xs_ref.at[pl.ds(dest_ref[0, 0, k * tm + r], 1)], sem_out.at[slot]).start()
        return 0

    lax.fori_loop(0, tm, issue, 0, unroll=DMA_ISSUE_UNROLL)
    pl.when(i > 0)(lambda: wait_rows(i - 1))
    pl.when(i + 2 < n)(lambda: load(i + 2).start())
    pl.when(i == n - 1)(lambda: wait_rows(i))


def _dispatch(h2, dest, n_slots):
    T, D = h2.shape
    tm = TM_DISP
    dest_t = dest.reshape(2, T // tm, tm).transpose(1, 0, 2).reshape(T // tm, 1, 2 * tm)
    xs0 = jnp.zeros((n_slots, D), h2.dtype)
    return pl.pallas_call(
        _dispatch_kernel,
        grid=(T // tm,),
        in_specs=[pl.BlockSpec((1, 1, 2 * tm), lambda i: (i, 0, 0), memory_space=pltpu.SMEM),
                  pl.BlockSpec(memory_space=pl.ANY),
                  pl.BlockSpec(memory_space=pl.ANY)],
        out_specs=pl.BlockSpec(memory_space=pl.ANY),
        out_shape=jax.ShapeDtypeStruct((n_slots, D), h2.dtype),
        scratch_shapes=[pltpu.VMEM((DISP_SLOTS, tm, D), h2.dtype),
                        pltpu.SemaphoreType.DMA((DISP_SLOTS,)), pltpu.SemaphoreType.DMA((DISP_SLOTS,))],
        input_output_aliases={2: 0},
        compiler_params=_params("arbitrary"),
        name="moe_dispatch",
    )(dest_t, h2, xs0)


def _expert_kernel(be_ref, nu_ref, x_ref, wg_ref, wu_ref, wd_ref, o_ref):
    del be_ref
    used = pl.program_id(0) < nu_ref[0]
    half = D_MODEL // 2

    @pl.when(used)
    def _():
        lo, hi = _unpack_bf16_pair(x_ref[...])
        a = _dot(lo, wg_ref[0, :half, :]) + _dot(hi, wg_ref[0, half:, :])
        b = _dot(lo, wu_ref[0, :half, :]) + _dot(hi, wu_ref[0, half:, :])
        o_ref[...] = _dot((_silu(a) * b).astype(BF16), wd_ref[0])

    @pl.when(jnp.logical_not(used))
    def _():
        o_ref[...] = jnp.zeros_like(o_ref)


def _experts(xs, blk_e, n_used, w_gate, w_up, w_down):
    P = xs.shape[0]
    _, D, F = w_gate.shape
    tm = TM_MOE
    grid_spec = pltpu.PrefetchScalarGridSpec(
        num_scalar_prefetch=2,
        grid=(P // tm,),
        in_specs=[pl.BlockSpec((tm, D // 2), lambda i, be, nu: (i, 0)),
                  pl.BlockSpec((1, D, F), lambda i, be, nu: (be[i], 0, 0), pipeline_mode=pl.Buffered(1)),
                  pl.BlockSpec((1, D, F), lambda i, be, nu: (be[i], 0, 0), pipeline_mode=pl.Buffered(1)),
                  pl.BlockSpec((1, F, D), lambda i, be, nu: (be[i], 0, 0), pipeline_mode=pl.Buffered(1))],
        out_specs=pl.BlockSpec((tm, D), lambda i, be, nu: (i, 0)))
    return pl.pallas_call(
        _expert_kernel,
        grid_spec=grid_spec,
        out_shape=jax.ShapeDtypeStruct((P, D), F32),
        compiler_params=_params("arbitrary"),
        name="moe_experts",
    )(blk_e, n_used, xs, w_gate, w_up, w_down)


def _combine_kernel(dest_ref, dest_next_ref, x_ref, mod_ref, p_ref, g_ref, ys_ref, o_ref, ybuf_ref, sems):
    i = pl.program_id(0)
    tm = x_ref.shape[0]
    slot = i % 2

    def gather(d_ref, s):
        def issue(r, _):
            for k in range(2):
                pltpu.make_async_copy(ys_ref.at[pl.ds(d_ref[0, 0, k * tm + r], 1)],
                                      ybuf_ref.at[s, k, pl.ds(r, 1)], sems.at[s]).start()
            return 0

        lax.fori_loop(0, tm, issue, 0, unroll=DMA_ISSUE_UNROLL)

    pl.when(i == 0)(lambda: gather(dest_ref, 0))
    pl.when(i + 1 < pl.num_programs(0))(lambda: gather(dest_next_ref, 1 - slot))
    for k in range(2):
        pltpu.make_async_copy(ys_ref.at[pl.ds(0, tm)], ybuf_ref.at[slot, k], sems.at[slot]).wait()
    p = p_ref[...]
    f = ybuf_ref[slot, 0] * p[:, 0:1] + ybuf_ref[slot, 1] * p[:, 1:2]
    x2 = x_ref[...] + mod_ref[0, 5:6, :] * f
    o_ref[...] = _rms(x2, g_ref[...])


def _combine(x1, mod, probs, final_g, ys, dest):
    T, D = x1.shape
    tm = TM_COMB
    per_b = SEQ // tm
    n = T // tm
    dest_t = dest.reshape(2, n, tm).transpose(1, 0, 2).reshape(n, 1, 2 * tm)
    return pl.pallas_call(
        _combine_kernel,
        grid=(n,),
        in_specs=[pl.BlockSpec((1, 1, 2 * tm), lambda i: (i, 0, 0), memory_space=pltpu.SMEM),
                  pl.BlockSpec((1, 1, 2 * tm), lambda i: (jnp.minimum(i + 1, n - 1), 0, 0),
                               memory_space=pltpu.SMEM),
                  pl.BlockSpec((tm, D), lambda i: (i, 0)),
                  pl.BlockSpec((1, 6, D), lambda i: (i // per_b, 0, 0)),
                  pl.BlockSpec((tm, 2), lambda i: (i, 0)),
                  pl.BlockSpec((1, D), lambda i: (0, 0)),
                  pl.BlockSpec(memory_space=pl.ANY)],
        out_specs=pl.BlockSpec((tm, D), lambda i: (i, 0)),
        out_shape=jax.ShapeDtypeStruct((T, D), F32),
        scratch_shapes=[pltpu.VMEM((2, 2, tm, D), F32), pltpu.SemaphoreType.DMA((2,))],
        compiler_params=_params("arbitrary"),
        name="moe_combine",
    )(dest_t, dest_t, x1, mod, probs, final_g, ys)


def _moe_layer(x1, h2, h2_packed, mod, w_router, w_gate, w_up, w_down, final_g):
    T, D = x1.shape
    e_idx, probs, rank, counts = _router(h2, w_router.T.astype(BF16))
    counts = counts[:, 0].astype(jnp.int32)
    padded = (counts + TM_MOE - 1) // TM_MOE * TM_MOE
    pad_end = jnp.cumsum(padded)
    pad_start = pad_end - padded
    eid = jnp.arange(N_EXPERTS, dtype=jnp.int32)[:, None, None]
    dest = jnp.sum(jnp.where(e_idx[None, :2] == eid, pad_start[:, None, None], 0), axis=0) + rank[:2]
    n_slots = 2 * T + N_EXPERTS * TM_MOE
    n_blk = n_slots // TM_MOE
    blk_first = jnp.arange(n_blk, dtype=jnp.int32) * TM_MOE
    blk_e = jnp.minimum(jnp.sum(pad_end[None, :] <= blk_first[:, None], axis=1),
                        N_EXPERTS - 1).astype(jnp.int32)
    n_used = (pad_end[-1:] // TM_MOE).astype(jnp.int32)
    xs = _dispatch(h2_packed, dest, n_slots)
    ys = _experts(xs, blk_e, n_used, w_gate.astype(BF16), w_up.astype(BF16), w_down.astype(BF16))
    return _combine(x1, mod, probs[:2].T, final_g.reshape(1, -1), ys, dest)


def _final_norm_kernel(x_ref, g_ref, o_ref):
    o_ref[...] = _rms(x_ref[...], g_ref[...])


def _final_norm(x, g):
    T, D = x.shape
    tm = 1024
    return pl.pallas_call(
        _final_norm_kernel,
        grid=(T // tm,),
        in_specs=[pl.BlockSpec((tm, D), lambda i: (i, 0)), pl.BlockSpec((1, D), lambda i: (0, 0))],
        out_specs=pl.BlockSpec((tm, D), lambda i: (i, 0)),
        out_shape=jax.ShapeDtypeStruct((T, D), F32),
        compiler_params=_params("arbitrary"),
        name="final_norm",
    )(x, g.reshape(1, -1))


def _regroup_w_in(w):
    ab = jnp.pad(w[:, C_GATE:C_GATE + AB_COLS], ((0, 0), (0, LANES - AB_COLS)))
    return jnp.concatenate([w[:, :C_GATE], w[:, C_GATE + AB_COLS:], ab], axis=1).astype(BF16)


def _lambda_init(layer):
    return 0.8 - 0.6 * math.exp(-0.3 * layer)


def kernel(x, c, positions, norm1_g, norm2_g, w_mod, b_mod, w_in, gm_ln_g, gm_ln_b, gm_w_s, gm_b_s, da_lambda, da_subln_g, dn_conv_w, dn_a_log, dn_dt_bias, dn_norm_g, w_br_gm, w_br_da, w_br_dn, w_out, ffn_w_gate, ffn_w_up, ffn_w_down, moe_w_router, moe_w_gate, moe_w_up, moe_w_down, final_g):
    B, S, D = x.shape
    T = B * S
    xt = x.reshape(T, D)
    mod_all = _modulation(c, w_mod, b_mod).reshape(DEPTH, B, 6, D)
    cos_t, sin_t = _rope_tables(positions)
    for layer in range(DEPTH):
        mod = mod_all[layer]
        gm, qk, v, dn, gates, ab = _in_projection(xt, mod, norm1_g[layer].reshape(1, D),
                                                  _regroup_w_in(w_in[layer]), cos_t, sin_t, dn_conv_w[layer])
        y_gm = _gmlp(gm, gm_ln_g[layer], gm_ln_b[layer], gm_w_s[layer], gm_b_s[layer])
        y_da = _diff_attention(qk, v, da_lambda[layer], da_subln_g[layer], _lambda_init(layer), B)
        y_dn = _deltanet(dn, ab, dn_a_log[layer], dn_dt_bias[layer], dn_norm_g[layer], B)
        moe = layer % 2 == 1
        x1, h2, *packed = _merge(xt, mod, norm2_g[layer].reshape(1, D), y_gm, y_da, y_dn, gates,
                                 w_br_gm[layer].astype(BF16), w_br_da[layer].astype(BF16),
                                 w_br_dn[layer].astype(BF16), w_out[layer].astype(BF16), moe)
        if moe:
            xt = _moe_layer(x1, h2, packed[0], mod, moe_w_router[layer // 2], moe_w_gate[layer // 2],
                            moe_w_up[layer // 2], moe_w_down[layer // 2], final_g)
        else:
            xt = _dense_ffn(x1, h2, mod, ffn_w_gate[layer // 2].astype(BF16),
                            ffn_w_up[layer // 2].astype(BF16), ffn_w_down[layer // 2].astype(BF16))
    if DEPTH % 2 == 1:
        xt = _final_norm(xt, final_g)
    return xt.reshape(B, S, D)
```

```python
import functools
import math

import jax
import jax.numpy as jnp
import numpy as np
from jax import lax
from jax.experimental import pallas as pl
from jax.experimental.pallas import tpu as pltpu

D_MODEL = 1024
SEQ = 2048
DEPTH = 2
CHUNK = 64
EPS = 1e-6
GM_WIDTH = D_MODEL // 2
GM_GROUPS = 4
GM_BLOCK = 128
DA_HEADS = 4
DA_HEAD_DIM = 64
DA_V_DIM = 2 * DA_HEAD_DIM
ROPE_THETA = 10000.0
DN_HEADS = 4
DN_HEAD_DIM = 128
DN_WIDTH = DN_HEADS * DN_HEAD_DIM
DN_CONV = 4
N_EXPERTS = 8
FF_EXPERT = 7 * D_MODEL // 2

LANES = 128
SUBLANES = 8
BF16_SUBLANES = 16
VMEM_LIMIT = 56 * 1024 * 1024
DMA_ISSUE_UNROLL = 8

AB_COLS = 2 * DN_HEADS
C_GM, C_QK, C_V, C_DN, C_GATE = 0, 1024, 2048, 2560, 4608
C_AB = C_GATE + 3 * D_MODEL
W_ALL = C_AB + LANES
PROJ_CHUNK = 512

TM_PROJ = 512
TM_GM = 512
TQ = 256
TK = 256
TM_MERGE = 512
TM_FFN = 512
TM_ROUTE = 512
TM_MOE = 512
TM_DISP = 512
DISP_SLOTS = 3
TM_COMB = 512
ATTN_LOOKAHEAD = 2
VT_ROWS = DA_V_DIM + BF16_SUBLANES
DN_SEQS = 2
DN_BLOCK = 256

BF16 = jnp.bfloat16
F32 = jnp.float32


def _params(*sem):
    return pltpu.CompilerParams(dimension_semantics=sem, vmem_limit_bytes=VMEM_LIMIT)


def _dot(a, b):
    return jnp.dot(a, b, preferred_element_type=F32)


def _dot_nt(a, b):
    return lax.dot_general(a, b, (((1,), (1,)), ((), ())), preferred_element_type=F32)


def _dot_tn(a, b):
    return lax.dot_general(a, b, (((0,), (0,)), ((), ())), preferred_element_type=F32)


def _rms(x, g):
    return x * lax.rsqrt(jnp.mean(x * x, axis=-1, keepdims=True) + EPS) * g


def _silu(x):
    return x * jax.nn.sigmoid(x)


def _mod_kernel(c_ref, w_ref, b_ref, o_ref):
    c = c_ref[...]
    o_ref[0] = _dot(_silu(c).astype(BF16), w_ref[0].astype(BF16)) + b_ref[0]


def _modulation(c, w_mod, b_mod):
    B, D = c.shape
    L, _, N = w_mod.shape
    tn = 1536
    return pl.pallas_call(
        _mod_kernel,
        grid=(L, N // tn),
        in_specs=[pl.BlockSpec((B, D), lambda l, j: (0, 0)),
                  pl.BlockSpec((1, D, tn), lambda l, j: (l, 0, j)),
                  pl.BlockSpec((1, 1, tn), lambda l, j: (l, 0, j))],
        out_specs=pl.BlockSpec((1, B, tn), lambda l, j: (l, 0, j)),
        out_shape=jax.ShapeDtypeStruct((L, B, N), F32),
        compiler_params=_params("arbitrary", "arbitrary"),
        name="modulation",
    )(c, w_mod, b_mod.reshape(L, 1, N))


def _rope_tab_kernel(pos_ref, inv_ref, sgn_ref, cos_ref, sin_ref):
    ang = pos_ref[...].astype(F32) * inv_ref[...]
    cos_ref[...] = jnp.cos(ang)
    sin_ref[...] = jnp.sin(ang) * sgn_ref[...]


def _rope_tables(positions):
    T = positions.size
    inv_freq = ROPE_THETA ** (-jnp.arange(0, DA_HEAD_DIM, 2, dtype=F32) / DA_HEAD_DIM)
    inv = jnp.tile(inv_freq, LANES // (DA_HEAD_DIM // 2)).reshape(1, LANES)
    half = DA_HEAD_DIM // 2
    sgn = np.tile(np.concatenate([-np.ones(half), np.ones(half)]), LANES // DA_HEAD_DIM)
    sgn = jnp.asarray(sgn, F32).reshape(1, LANES)
    tm = 1024
    return pl.pallas_call(
        _rope_tab_kernel,
        grid=(T // tm,),
        in_specs=[pl.BlockSpec((tm, 1), lambda i: (i, 0)),
                  pl.BlockSpec((1, LANES), lambda i: (0, 0)),
                  pl.BlockSpec((1, LANES), lambda i: (0, 0))],
        out_specs=[pl.BlockSpec((tm, LANES), lambda i: (i, 0))] * 2,
        out_shape=[jax.ShapeDtypeStruct((T, LANES), F32)] * 2,
        compiler_params=_params("arbitrary"),
        name="rope_tables",
    )(positions.reshape(T, 1), inv, sgn)


def _rope_rows(x, cos, sin):
    lane = lax.broadcasted_iota(jnp.int32, x.shape, 1)
    first = (lane % DA_HEAD_DIM) < (DA_HEAD_DIM // 2)
    partner = jnp.where(first, pltpu.roll(x, LANES - DA_HEAD_DIM // 2, axis=1),
                        pltpu.roll(x, DA_HEAD_DIM // 2, axis=1))
    return x * cos + partner * sin


def _proj_kernel(x_ref, mod_ref, g_ref, w_ref, cos_ref, sin_ref, cw_ref,
                 gm_ref, qk_ref, v_ref, dn_ref, gate_ref, ab_ref, prev_ref):
    tm = x_ref.shape[0]

    @pl.when(pl.program_id(0) % (SEQ // tm) == 0)
    def _():
        prev_ref[...] = jnp.zeros_like(prev_ref)

    def conv_silu(y, idx):
        cw = cw_ref[:, idx * PROJ_CHUNK:(idx + 1) * PROJ_CHUNK]
        ycat = jnp.concatenate([prev_ref[idx], y], axis=0)
        out = y * cw[DN_CONV - 1:DN_CONV, :]
        for j in range(DN_CONV - 1):
            out = out + pltpu.roll(ycat, DN_CONV - 1 - j, axis=0)[SUBLANES:, :] * cw[j:j + 1, :]
        prev_ref[idx] = y[tm - SUBLANES:, :]
        return _silu(out)

    def l2norm_heads(t, mult):
        parts = []
        for j in range(PROJ_CHUNK // DN_HEAD_DIM):
            seg = t[:, j * DN_HEAD_DIM:(j + 1) * DN_HEAD_DIM]
            parts.append(seg * (lax.rsqrt(jnp.sum(seg * seg, axis=-1, keepdims=True) + EPS) * mult))
        return jnp.concatenate(parts, axis=1)

    x = x_ref[...]
    shift = mod_ref[0, 0:1, :]
    scale = mod_ref[0, 1:2, :]
    h = (_rms(x, g_ref[...]) * (1.0 + scale) + shift).astype(BF16)
    cos = cos_ref[...]
    sin = sin_ref[...]
    q_scale = DA_HEAD_DIM ** -0.5 * math.log2(math.e)

    def rope_chunk(y, mult):
        parts = [_rope_rows(y[:, j * LANES:(j + 1) * LANES], cos, sin) * mult
                 for j in range(PROJ_CHUNK // LANES)]
        return jnp.concatenate(parts, axis=1)

    dn_chunks = list(range(C_DN, C_GATE, PROJ_CHUNK))
    gate_chunks = list(range(C_GATE, C_AB, PROJ_CHUNK))
    order = list(range(0, C_DN, PROJ_CHUNK))
    for j, c in enumerate(dn_chunks):
        order += [c] + gate_chunks[2 * j:2 * j + 2]
    assert sorted(order) == list(range(0, C_AB, PROJ_CHUNK))
    for c0 in order:
        y = _dot(h, w_ref[:, c0:c0 + PROJ_CHUNK])
        if c0 < C_QK:
            gm_ref[:, c0:c0 + PROJ_CHUNK] = y.astype(BF16)
        elif c0 < C_V:
            mult = q_scale if c0 == C_QK else 1.0
            qk_ref[:, c0 - C_QK:c0 - C_QK + PROJ_CHUNK] = rope_chunk(y, mult).astype(BF16)
        elif c0 < C_DN:
            v_ref[...] = y.astype(BF16)
        elif c0 < C_GATE:
            idx = (c0 - C_DN) // PROJ_CHUNK
            if idx < 3:
                y = conv_silu(y, idx)
            if idx < 2:
                y = l2norm_heads(y, DN_HEAD_DIM ** -0.5 if idx == 0 else 1.0)
            dn_ref[:, c0 - C_DN:c0 - C_DN + PROJ_CHUNK] = y.astype(BF16)
        else:
            gate_ref[:, c0 - C_GATE:c0 - C_GATE + PROJ_CHUNK] = y.astype(BF16)
    ab_ref[...] = _dot(h, w_ref[:, C_AB:W_ALL])


def _in_projection(x, mod, g, w_all, cos_t, sin_t, conv_w):
    T, D = x.shape
    tm = TM_PROJ
    per_b = SEQ // tm
    row = lambda i: (i, 0)
    widths = (C_QK - C_GM, C_V - C_QK, C_DN - C_V, C_GATE - C_DN, C_AB - C_GATE)
    return pl.pallas_call(
        _proj_kernel,
        grid=(T // tm,),
        in_specs=[pl.BlockSpec((tm, D), row),
                  pl.BlockSpec((1, 6, D), lambda i: (i // per_b, 0, 0)),
                  pl.BlockSpec((1, D), lambda i: (0, 0)),
                  pl.BlockSpec((D, W_ALL), lambda i: (0, 0), pipeline_mode=pl.Buffered(1)),
                  pl.BlockSpec((tm, LANES), row),
                  pl.BlockSpec((tm, LANES), row),
                  pl.BlockSpec((DN_CONV, 3 * DN_WIDTH), lambda i: (0, 0))],
        out_specs=[pl.BlockSpec((tm, w), row) for w in widths] + [pl.BlockSpec((tm, LANES), row)],
        out_shape=[jax.ShapeDtypeStruct((T, w), BF16) for w in widths]
                  + [jax.ShapeDtypeStruct((T, LANES), F32)],
        scratch_shapes=[pltpu.VMEM((3, SUBLANES, PROJ_CHUNK), F32)],
        compiler_params=_params("arbitrary"),
        name="in_projection",
    )(x, mod, g, w_all, cos_t, sin_t, conv_w)


def _gelu(x):
    return 0.5 * x * (1.0 + lax.erf(x * np.float32(math.sqrt(0.5))))


def _gmlp_kernel(uv_ref, lng_ref, lnb_ref, ws_ref, bs_ref, o_ref):
    u = _gelu(uv_ref[:, :GM_WIDTH].astype(F32))
    v = _gelu(uv_ref[:, GM_WIDTH:].astype(F32))
    mu = jnp.mean(v, axis=-1, keepdims=True)
    vc = v - mu
    var = jnp.mean(vc * vc, axis=-1, keepdims=True)
    v = (vc * lax.rsqrt(var + EPS) * lng_ref[...] + lnb_ref[...]).astype(BF16)
    ri = lax.broadcasted_iota(jnp.int32, (GM_BLOCK, GM_BLOCK), 0) // CHUNK
    ci = lax.broadcasted_iota(jnp.int32, (GM_BLOCK, GM_BLOCK), 1) // CHUNK
    allowed = ci <= ri
    gc = GM_WIDTH // GM_GROUPS
    for g in range(GM_GROUPS):
        w = jnp.where(allowed, ws_ref[g], 0.0).astype(BF16)
        bias = bs_ref[g]
        for r in range(uv_ref.shape[0] // GM_BLOCK):
            rows = slice(r * GM_BLOCK, (r + 1) * GM_BLOCK)
            cols = slice(g * gc, (g + 1) * gc)
            sv = _dot(w, v[rows, cols]) + bias
            o_ref[rows, cols] = (u[rows, cols] * sv).astype(BF16)


def _gmlp(uv, ln_g, ln_b, w_s, b_s):
    T = uv.shape[0]
    tm = TM_GM
    return pl.pallas_call(
        _gmlp_kernel,
        grid=(T // tm,),
        in_specs=[pl.BlockSpec((tm, 2 * GM_WIDTH), lambda i: (i, 0)),
                  pl.BlockSpec((1, GM_WIDTH), lambda i: (0, 0)),
                  pl.BlockSpec((1, GM_WIDTH), lambda i: (0, 0)),
                  pl.BlockSpec((GM_GROUPS, GM_BLOCK, GM_BLOCK), lambda i: (0, 0, 0)),
                  pl.BlockSpec((GM_GROUPS, GM_BLOCK, 1), lambda i: (0, 0, 0))],
        out_specs=pl.BlockSpec((tm, GM_WIDTH), lambda i: (i, 0)),
        out_shape=jax.ShapeDtypeStruct((T, GM_WIDTH), BF16),
        compiler_params=_params("arbitrary"),
        name="gmlp",
    )(uv, ln_g.reshape(1, -1), ln_b.reshape(1, -1), w_s, b_s.reshape(GM_GROUPS, GM_BLOCK, 1))


def _attn_kernel(lam_ref, g_ref, q_ref, k_ref, v_ref, o_ref, qz_ref, vt_ref, *stats, lambda_init):
    nc = 2 * DA_HEADS
    m_ref, acc_ref, s_ref = (stats[j * nc:(j + 1) * nc] for j in range(3))
    i = pl.program_id(1)
    lp = lam_ref[...]
    lam = (jnp.exp(jnp.sum(lp[0:1] * lp[1:2], axis=-1, keepdims=True))
           - jnp.exp(jnp.sum(lp[2:3] * lp[3:4], axis=-1, keepdims=True)) + lambda_init)
    @pl.when(i == 0)
    def _():
        for kb in range(SEQ // TK):
            for h in range(DA_HEADS):
                vt_ref[h * VT_ROWS:h * VT_ROWS + DA_V_DIM, kb * TK:(kb + 1) * TK] = (
                    v_ref[kb * TK:(kb + 1) * TK, h * LANES:(h + 1) * LANES].T)
        for h in range(DA_HEADS):
            vt_ref[h * VT_ROWS + DA_V_DIM:(h + 1) * VT_ROWS, :] = jnp.ones((VT_ROWS - DA_V_DIM, SEQ), BF16)

    row = lax.broadcasted_iota(jnp.int32, (LANES, TQ), 0)
    for h in range(DA_HEADS):
        qt = q_ref[:, h * LANES:(h + 1) * LANES].T
        zero = jnp.zeros_like(qt)
        qz_ref[2 * h] = jnp.where(row < DA_HEAD_DIM, qt, zero)
        qz_ref[2 * h + 1] = jnp.where(row >= DA_HEAD_DIM, qt, zero)
    for c in range(nc):
        m_ref[c][...] = jnp.full_like(m_ref[c], -jnp.inf)
        acc_ref[c][...] = jnp.zeros_like(acc_ref[c])

    def key_rows(kb):
        return pl.ds(pl.multiple_of(kb * TK, TK), TK)

    def scores(kb, c, masked):
        h = c // 2
        s = _dot(k_ref[key_rows(kb), h * LANES:(h + 1) * LANES], qz_ref[c])
        if masked:
            ki = lax.broadcasted_iota(jnp.int32, s.shape, 0) // CHUNK + kb * (TK // CHUNK)
            qi = lax.broadcasted_iota(jnp.int32, s.shape, 1) // CHUNK + i * (TQ // CHUNK)
            s = jnp.where(ki <= qi, s, -jnp.inf)
        return s

    def accumulate(kb, c, s):
        h = c // 2
        m_old = m_ref[c][...]
        m_new = jnp.maximum(m_old, jnp.max(s, axis=0, keepdims=True))
        alpha = jnp.exp2(m_old - m_new)
        e = jnp.exp2((s - m_new).astype(BF16))
        vt = vt_ref[h * VT_ROWS:(h + 1) * VT_ROWS, key_rows(kb)]
        acc_ref[c][...] = alpha * acc_ref[c][...] + _dot(vt, e)
        m_ref[c][...] = m_new

    def fold_and_prefetch(kb, masked_next):
        fresh = {}
        for c in range(nc + ATTN_LOOKAHEAD):
            if c < nc:
                fresh[c] = scores(kb + 1, c, masked_next)
            if c >= ATTN_LOOKAHEAD:
                cc = c - ATTN_LOOKAHEAD
                accumulate(kb, cc, s_ref[cc][...])
                s_ref[cc][...] = fresh.pop(cc)

    first_masked = i * (TQ // TK)

    @pl.when(i == 0)
    def _():
        for c in range(nc):
            s_ref[c][...] = scores(0, c, True)

    @pl.when(i > 0)
    def _():
        for c in range(nc):
            s_ref[c][...] = scores(0, c, False)

        def body(kb, carry):
            fold_and_prefetch(kb, False)
            return carry

        lax.fori_loop(0, first_masked - 1, body, 0)
        fold_and_prefetch(first_masked - 1, True)

    for r in range(TQ // TK - 1):
        fold_and_prefetch(first_masked + r, True)
    for c in range(nc):
        accumulate(first_masked + TQ // TK - 1, c, s_ref[c][...])
    for h in range(DA_HEADS):
        num0, den0 = acc_ref[2 * h][:DA_V_DIM, :], acc_ref[2 * h][DA_V_DIM:DA_V_DIM + 1, :]
        num1, den1 = acc_ref[2 * h + 1][:DA_V_DIM, :], acc_ref[2 * h + 1][DA_V_DIM:DA_V_DIM + 1, :]
        o = num0 / den0 - lam * (num1 / den1)
        ms = jnp.mean(o * o, axis=0, keepdims=True)
        o = o * lax.rsqrt(ms + EPS) * (g_ref[...] * (1.0 - lambda_init))
        o_ref[:, h * LANES:(h + 1) * LANES] = o.T.astype(BF16)


def _diff_attention(qk, v, lam_p, subln_g, lambda_init, batch):
    T = qk.shape[0]
    nq = SEQ // TQ
    W = DA_HEADS * DA_V_DIM
    kernel = functools.partial(_attn_kernel, lambda_init=lambda_init)
    return pl.pallas_call(
        kernel,
        grid=(batch, nq),
        in_specs=[pl.BlockSpec((4, DA_HEAD_DIM), lambda b, i: (0, 0)),
                  pl.BlockSpec((DA_V_DIM, 1), lambda b, i: (0, 0)),
                  pl.BlockSpec((TQ, W), lambda b, i: (b * nq + i, 0)),
                  pl.BlockSpec((SEQ, W), lambda b, i: (b, 1)),
                  pl.BlockSpec((SEQ, W), lambda b, i: (b, 0))],
        out_specs=pl.BlockSpec((TQ, W), lambda b, i: (b * nq + i, 0)),
        out_shape=jax.ShapeDtypeStruct((T, W), BF16),
        scratch_shapes=[pltpu.VMEM((2 * DA_HEADS, LANES, TQ), BF16),
                        pltpu.VMEM((DA_HEADS * VT_ROWS, SEQ), BF16)]
                       + [pltpu.VMEM((1, TQ), F32)] * (2 * DA_HEADS)
                       + [pltpu.VMEM((VT_ROWS, TQ), F32)] * (2 * DA_HEADS)
                       + [pltpu.VMEM((TK, TQ), F32)] * (2 * DA_HEADS),
        compiler_params=_params("arbitrary", "arbitrary"),
        name="diff_attention",
    )(lam_p, subln_g.reshape(-1, 1), qk, qk, v)


def _split3(x):
    hi = x.astype(BF16)
    r = x - hi.astype(F32)
    mid = r.astype(BF16)
    lo = (r - mid.astype(F32)).astype(BF16)
    return hi, mid, lo


def _dn_kernel(dn_ref, a_ref, at_ref, alogt_ref, dtbt_ref, ng_ref,
               o_ref, *state_ref):
    c = pl.program_id(1)
    R = DN_BLOCK
    G = dn_ref.shape[0]

    @pl.when(c == 0)
    def _():
        for ref in state_ref:
            ref[...] = jnp.zeros_like(ref)

    W3 = 3 * DN_WIDTH
    ri = lax.broadcasted_iota(jnp.int32, (R, R), 0)
    ci = lax.broadcasted_iota(jnp.int32, (R, R), 1)
    tril = ri >= ci
    strict = ri > ci
    ones_triu = jnp.where(ci >= ri, 1.0, 0.0).astype(BF16)
    eye = jnp.where(ri == ci, 1.0, 0.0)
    same = {n: (ri // n) == (ci // n) for n in (16, 32, 64, 128)}
    gc_col, gc_row, beta_all = [], [], []
    for j in range(G):
        beta_all.append(jax.nn.sigmoid(a_ref[j]))
        abt = at_ref[j, 0]
        g_row = -jnp.exp(alogt_ref[...]) * jax.nn.softplus(abt + dtbt_ref[...])
        gc_row.append(sum(_dot(part, ones_triu) for part in _split3(g_row)))
        gc_col.append(gc_row[-1].T)

    H = range(G * DN_HEADS)
    seq = [n // DN_HEADS for n in H]
    hd = [n % DN_HEADS for n in H]
    hsl = [slice(hd[n] * DN_HEAD_DIM, (hd[n] + 1) * DN_HEAD_DIM) for n in H]
    q16 = [dn_ref[seq[n], :, hsl[n]] for n in H]
    k16 = [dn_ref[seq[n], :, DN_WIDTH + hd[n] * DN_HEAD_DIM:DN_WIDTH + (hd[n] + 1) * DN_HEAD_DIM] for n in H]
    q = [t.astype(F32) for t in q16]
    k = [t.astype(F32) for t in k16]
    v = [dn_ref[seq[n], :, 2 * DN_WIDTH + hd[n] * DN_HEAD_DIM:2 * DN_WIDTH + (hd[n] + 1) * DN_HEAD_DIM]
         .astype(F32) for n in H]
    gc = [gc_col[seq[n]][:, hd[n]:hd[n] + 1] for n in H]
    beta = [beta_all[seq[n]][:, DN_HEADS + hd[n]:DN_HEADS + hd[n] + 1] for n in H]
    g_last = [gc_col[seq[n]][R - 1:R, hd[n]:hd[n] + 1] for n in H]
    decay = [jnp.exp(jnp.where(tril, gc[n] - gc_row[seq[n]][hd[n]:hd[n] + 1, :], -jnp.inf)) for n in H]
    eg = [jnp.exp(t) for t in gc]
    kb = [k[h] * beta[h] for h in H]
    s1 = [_dot_nt(jnp.concatenate([kb[h].astype(BF16), q16[h]], axis=0), k16[h]) for h in H]
    a_mat = [jnp.where(strict, s1[h][:R] * decay[h], 0.0) for h in H]
    qk = [(s1[h][R:] * decay[h]).astype(BF16) for h in H]
    p = [jnp.where(same[16], -t, 0.0) for t in a_mat]
    t_inv = [eye + t for t in p]
    for _ in range(3):
        p16 = [t.astype(BF16) for t in p]
        p = [_dot(t, t) for t in p16]
        t_inv = [t_inv[h] + _dot(t_inv[h].astype(BF16), p[h].astype(BF16)) for h in H]
    a16 = [t.astype(BF16) for t in a_mat]
    t16 = [t.astype(BF16) for t in t_inv]
    for n in (16, 32, 64, 128):
        off = jnp.logical_not(same[n])
        if 2 * n < R:
            off = jnp.logical_and(same[2 * n], off)
        off16 = jnp.where(off, 1.0, 0.0).astype(BF16)
        tl = [_dot(t16[h], a16[h] * off16).astype(BF16) for h in H]
        t16 = [t16[h] - _dot(tl[h], t16[h]).astype(BF16) * off16 for h in H]
    sol = [_dot(t16[h], jnp.concatenate([v[h] * beta[h], kb[h] * eg[h]], axis=1).astype(BF16))
           for h in H]
    state = [state_ref[h][...] for h in H]
    m1 = [_dot(jnp.concatenate([sol[h][:, DN_HEAD_DIM:], q[h] * eg[h]], axis=0).astype(BF16),
               state[h].astype(BF16)) for h in H]
    vn16 = [(sol[h][:, :DN_HEAD_DIM] - m1[h][:R]).astype(BF16) for h in H]
    o = [m1[h][R:] + _dot(qk[h], vn16[h]) for h in H]
    kd = [(k[h] * jnp.exp(g_last[h] - gc[h])).astype(BF16) for h in H]
    for h in H:
        state_ref[h][...] = state[h] * jnp.exp(g_last[h]) + _dot_tn(kd[h], vn16[h])
    for n in H:
        z = dn_ref[seq[n], :, W3 + hd[n] * DN_HEAD_DIM:W3 + (hd[n] + 1) * DN_HEAD_DIM].astype(F32)
        o_ref[seq[n], :, hsl[n]] = (_rms(o[n], ng_ref[...]) * _silu(z)).astype(BF16)


def _deltanet(dn, ab, a_log, dt_bias, norm_g, batch):
    T = dn.shape[0]
    R = DN_BLOCK
    n = SEQ // R
    G = DN_SEQS if batch % DN_SEQS == 0 else 1
    col8 = lambda t: jnp.zeros((AB_COLS, 1), F32).at[:DN_HEADS, 0].set(t)
    ab_t = ab[:, :AB_COLS].reshape(batch, n, R, AB_COLS).transpose(0, 1, 3, 2)
    const = lambda b, c: (0, 0)
    out = pl.pallas_call(
        _dn_kernel,
        grid=(batch // G, n),
        in_specs=[pl.BlockSpec((G, R, 4 * DN_WIDTH), lambda b, c: (b, c, 0)),
                  pl.BlockSpec((G, R, LANES), lambda b, c: (b, c, 0)),
                  pl.BlockSpec((G, 1, AB_COLS, R), lambda b, c: (b, c, 0, 0)),
                  pl.BlockSpec((AB_COLS, 1), const),
                  pl.BlockSpec((AB_COLS, 1), const),
                  pl.BlockSpec((1, DN_HEAD_DIM), const)],
        out_specs=pl.BlockSpec((G, R, DN_WIDTH), lambda b, c: (b, c, 0)),
        out_shape=jax.ShapeDtypeStruct((batch, SEQ, DN_WIDTH), BF16),
        scratch_shapes=[pltpu.VMEM((DN_HEAD_DIM, DN_HEAD_DIM), F32)] * (G * DN_HEADS),
        compiler_params=_params("arbitrary", "arbitrary"),
        name="deltanet",
    )(dn.reshape(batch, SEQ, -1), ab.reshape(batch, SEQ, -1), ab_t,
      col8(a_log), col8(dt_bias), norm_g.reshape(1, -1))
    return out.reshape(T, DN_WIDTH)


def _pack_bf16_pair(lo, hi):
    lo_bits = lax.bitcast_convert_type(lo.astype(BF16).astype(F32), jnp.uint32)
    hi_bits = lax.bitcast_convert_type(hi.astype(BF16).astype(F32), jnp.uint32)
    return (lo_bits >> 16) | (hi_bits & jnp.uint32(0xFFFF0000))


def _unpack_pair_f32(words):
    lo = lax.bitcast_convert_type(words << 16, F32)
    hi = lax.bitcast_convert_type(words & jnp.uint32(0xFFFF0000), F32)
    return lo, hi


def _unpack_bf16_pair(words):
    lo, hi = _unpack_pair_f32(words)
    return lo.astype(BF16), hi.astype(BF16)


def _merge_kernel(x_ref, mod_ref, g2_ref, ygm_ref, yda_ref, ydn_ref, gate_ref,
                  wgm_ref, wda_ref, wdn_ref, wout_ref, x1_ref, h2_ref, *maybe_packed_ref):
    D = D_MODEL
    def gate(j):
        return 0.5 * jnp.tanh(0.5 * gate_ref[:, j * D:(j + 1) * D].astype(F32)) + 0.5

    merged = (gate(0) * _dot(ygm_ref[...], wgm_ref[...])
              + gate(1) * _dot(yda_ref[...], wda_ref[...])
              + gate(2) * _dot(ydn_ref[...], wdn_ref[...]))
    y = _dot(merged.astype(BF16), wout_ref[...])
    x1 = x_ref[...] + mod_ref[0, 2:3, :] * y
    x1_ref[...] = x1
    h2 = _rms(x1, g2_ref[...]) * (1.0 + mod_ref[0, 4:5, :]) + mod_ref[0, 3:4, :]
    h2_ref[...] = h2.astype(BF16)
    for packed_ref in maybe_packed_ref:
        packed_ref[...] = _pack_bf16_pair(h2[:, :D // 2], h2[:, D // 2:])


def _merge(x, mod, g2, y_gm, y_da, y_dn, gates, w_gm, w_da, w_dn, w_out, with_packed):
    T, D = x.shape
    tm = TM_MERGE
    per_b = SEQ // tm
    row = lambda i: (i, 0)
    const = lambda i: (0, 0)
    out_specs = [pl.BlockSpec((tm, D), row), pl.BlockSpec((tm, D), row)]
    out_shape = [jax.ShapeDtypeStruct((T, D), F32), jax.ShapeDtypeStruct((T, D), BF16)]
    if with_packed:
        out_specs.append(pl.BlockSpec((tm, D // 2), row))
        out_shape.append(jax.ShapeDtypeStruct((T, D // 2), jnp.uint32))
    return pl.pallas_call(
        _merge_kernel,
        grid=(T // tm,),
        in_specs=[pl.BlockSpec((tm, D), row),
                  pl.BlockSpec((1, 6, D), lambda i: (i // per_b, 0, 0)),
                  pl.BlockSpec((1, D), const),
                  pl.BlockSpec((tm, GM_WIDTH), row),
                  pl.BlockSpec((tm, GM_WIDTH), row),
                  pl.BlockSpec((tm, DN_WIDTH), row),
                  pl.BlockSpec((tm, 3 * D), row),
                  pl.BlockSpec((GM_WIDTH, D), const),
                  pl.BlockSpec((GM_WIDTH, D), const),
                  pl.BlockSpec((DN_WIDTH, D), const),
                  pl.BlockSpec((D, D), const)],
        out_specs=out_specs,
        out_shape=out_shape,
        compiler_params=_params("arbitrary"),
        name="merge",
    )(x, mod, g2, y_gm, y_da, y_dn, gates, w_gm, w_da, w_dn, w_out)


def _ffn_kernel(x_ref, h_ref, mod_ref, wg_ref, wu_ref, wd_ref, o_ref):
    h = h_ref[...]
    a = _dot(h, wg_ref[...])
    b = _dot(h, wu_ref[...])
    f = _dot((_silu(a) * b).astype(BF16), wd_ref[...])
    o_ref[...] = x_ref[...] + mod_ref[0, 5:6, :] * f


def _dense_ffn(x1, h2, mod, w_gate, w_up, w_down):
    T, D = x1.shape
    F = w_gate.shape[1]
    tm = TM_FFN
    per_b = SEQ // tm
    row = lambda i: (i, 0)
    const = lambda i: (0, 0)
    return pl.pallas_call(
        _ffn_kernel,
        grid=(T // tm,),
        in_specs=[pl.BlockSpec((tm, D), row),
                  pl.BlockSpec((tm, D), row),
                  pl.BlockSpec((1, 6, D), lambda i: (i // per_b, 0, 0)),
                  pl.BlockSpec((D, F), const, pipeline_mode=pl.Buffered(1)),
                  pl.BlockSpec((D, F), const, pipeline_mode=pl.Buffered(1)),
                  pl.BlockSpec((F, D), const, pipeline_mode=pl.Buffered(1))],
        out_specs=pl.BlockSpec((tm, D), row),
        out_shape=jax.ShapeDtypeStruct((T, D), F32),
        compiler_params=_params("arbitrary"),
        name="dense_ffn",
    )(x1, h2, mod, w_gate, w_up, w_down)


def _router_kernel(h_ref, wr_ref, e_ref, p_ref, r_ref, cnt_ref, base_ref):
    i = pl.program_id(0)
    tm = h_ref.shape[0]

    @pl.when(i == 0)
    def _():
        base_ref[...] = jnp.zeros_like(base_ref)

    logits = _dot_nt(wr_ref[...], h_ref[...])
    row = lax.broadcasted_iota(jnp.int32, logits.shape, 0)
    m1 = jnp.max(logits, axis=0, keepdims=True)
    i1 = jnp.min(jnp.where(logits == m1, row, N_EXPERTS), axis=0, keepdims=True)
    rest = jnp.where(row == i1, -jnp.inf, logits)
    m2 = jnp.max(rest, axis=0, keepdims=True)
    i2 = jnp.min(jnp.where(rest == m2, row, N_EXPERTS), axis=0, keepdims=True)
    e2 = jnp.exp(m2 - m1)
    w1 = 1.0 / (1.0 + e2)
    w2 = e2 / (1.0 + e2)
    oh1 = jnp.where(row == i1, 1.0, 0.0)
    oh2 = jnp.where(row == i2, 1.0, 0.0)
    both = oh1 + oh2
    ti = lax.broadcasted_iota(jnp.int32, (tm, tm), 0)
    tj = lax.broadcasted_iota(jnp.int32, (tm, tm), 1)
    before = jnp.where(ti < tj, 1.0, 0.0).astype(BF16)
    pos = base_ref[...] + _dot(both.astype(BF16), before)
    r1 = jnp.sum(oh1 * pos, axis=0, keepdims=True)
    r2 = jnp.sum(oh2 * pos, axis=0, keepdims=True)
    base_ref[...] = base_ref[...] + jnp.sum(both, axis=1, keepdims=True)
    zi = jnp.zeros((N_EXPERTS - 2, tm), jnp.int32)
    zf = jnp.zeros((N_EXPERTS - 2, tm), F32)
    e_ref[...] = jnp.concatenate([i1, i2, zi], axis=0)
    p_ref[...] = jnp.concatenate([w1, w2, zf], axis=0)
    r_ref[...] = jnp.concatenate([r1.astype(jnp.int32), r2.astype(jnp.int32), zi], axis=0)
    cnt_ref[...] = jnp.broadcast_to(base_ref[...], cnt_ref.shape)


def _router(h2, w_router_t):
    T, D = h2.shape
    tm = TM_ROUTE
    col = lambda i: (0, i)
    return pl.pallas_call(
        _router_kernel,
        grid=(T // tm,),
        in_specs=[pl.BlockSpec((tm, D), lambda i: (i, 0)),
                  pl.BlockSpec((N_EXPERTS, D), lambda i: (0, 0))],
        out_specs=[pl.BlockSpec((N_EXPERTS, tm), col)] * 3 + [pl.BlockSpec((N_EXPERTS, LANES), lambda i: (0, 0))],
        out_shape=[jax.ShapeDtypeStruct((N_EXPERTS, T), jnp.int32),
                   jax.ShapeDtypeStruct((N_EXPERTS, T), F32),
                   jax.ShapeDtypeStruct((N_EXPERTS, T), jnp.int32),
                   jax.ShapeDtypeStruct((N_EXPERTS, LANES), F32)],
        scratch_shapes=[pltpu.VMEM((N_EXPERTS, 1), F32)],
        compiler_params=_params("arbitrary"),
        name="moe_router",
    )(h2, w_router_t)


def _dispatch_kernel(dest_ref, h_ref, xs_in_ref, xs_ref, buf_ref, sem_in, sem_out):
    del xs_in_ref
    i = pl.program_id(0)
    n = pl.num_programs(0)
    tm = buf_ref.shape[1]

    def load(t):
        s = t % DISP_SLOTS
        return pltpu.make_async_copy(h_ref.at[pl.ds(pl.multiple_of(t * tm, tm), tm)], buf_ref.at[s],
                                     sem_in.at[s])

    def wait_rows(t):
        s = t % DISP_SLOTS
        for _ in range(2):
            pltpu.make_async_copy(buf_ref.at[s], xs_ref.at[pl.ds(0, tm)], sem_out.at[s]).wait()

    @pl.when(i == 0)
    def _():
        load(0).start()
        pl.when(n > 1)(lambda: load(1).start())

    load(i).wait()
    slot = i % DISP_SLOTS

    def issue(r, _):
        for k in range(2):
            pltpu.make_async_copy(buf_ref.at[slot, pl.ds(r, 1)],
                                  xs_ref.at[pl.ds(dest_ref[0, 0, k * tm + r], 1)], sem_out.at[slot]).start()
        return 0

    lax.fori_loop(0, tm, issue, 0, unroll=DMA_ISSUE_UNROLL)
    pl.when(i > 0)(lambda: wait_rows(i - 1))
    pl.when(i + 2 < n)(lambda: load(i + 2).start())
    pl.when(i == n - 1)(lambda: wait_rows(i))


def _dispatch(h2, dest, n_slots):
    T, D = h2.shape
    tm = TM_DISP
    dest_t = dest.reshape(2, T // tm, tm).transpose(1, 0, 2).reshape(T // tm, 1, 2 * tm)
    xs0 = jnp.zeros((n_slots, D), h2.dtype)
    return pl.pallas_call(
        _dispatch_kernel,
        grid=(T // tm,),
        in_specs=[pl.BlockSpec((1, 1, 2 * tm), lambda i: (i, 0, 0), memory_space=pltpu.SMEM),
                  pl.BlockSpec(memory_space=pl.ANY),
                  pl.BlockSpec(memory_space=pl.ANY)],
        out_specs=pl.BlockSpec(memory_space=pl.ANY),
        out_shape=jax.ShapeDtypeStruct((n_slots, D), h2.dtype),
        scratch_shapes=[pltpu.VMEM((DISP_SLOTS, tm, D), h2.dtype),
                        pltpu.SemaphoreType.DMA((DISP_SLOTS,)), pltpu.SemaphoreType.DMA((DISP_SLOTS,))],
        input_output_aliases={2: 0},
        compiler_params=_params("arbitrary"),
        name="moe_dispatch",
    )(dest_t, h2, xs0)


def _expert_kernel(be_ref, nu_ref, x_ref, wg_ref, wu_ref, wd_ref, o_ref):
    del be_ref
    used = pl.program_id(0) < nu_ref[0]
    half = D_MODEL // 2

    @pl.when(used)
    def _():
        lo, hi = _unpack_bf16_pair(x_ref[...])
        a = _dot(lo, wg_ref[0, :half, :]) + _dot(hi, wg_ref[0, half:, :])
        b = _dot(lo, wu_ref[0, :half, :]) + _dot(hi, wu_ref[0, half:, :])
        y = _dot((_silu(a) * b).astype(BF16), wd_ref[0])
        o_ref[...] = _pack_bf16_pair(y[:, :half], y[:, half:])

    @pl.when(jnp.logical_not(used))
    def _():
        o_ref[...] = jnp.zeros_like(o_ref)


def _experts(xs, blk_e, n_used, w_gate, w_up, w_down):
    P = xs.shape[0]
    _, D, F = w_gate.shape
    tm = TM_MOE
    grid_spec = pltpu.PrefetchScalarGridSpec(
        num_scalar_prefetch=2,
        grid=(P // tm,),
        in_specs=[pl.BlockSpec((tm, D // 2), lambda i, be, nu: (i, 0)),
                  pl.BlockSpec((1, D, F), lambda i, be, nu: (be[i], 0, 0), pipeline_mode=pl.Buffered(1)),
                  pl.BlockSpec((1, D, F), lambda i, be, nu: (be[i], 0, 0), pipeline_mode=pl.Buffered(1)),
                  pl.BlockSpec((1, F, D), lambda i, be, nu: (be[i], 0, 0), pipeline_mode=pl.Buffered(1))],
        out_specs=pl.BlockSpec((tm, D // 2), lambda i, be, nu: (i, 0)))
    return pl.pallas_call(
        _expert_kernel,
        grid_spec=grid_spec,
        out_shape=jax.ShapeDtypeStruct((P, D // 2), jnp.uint32),
        compiler_params=_params("arbitrary"),
        name="moe_experts",
    )(blk_e, n_used, xs, w_gate, w_up, w_down)


def _combine_kernel(dest_ref, dest_next_ref, x_ref, mod_ref, p_ref, g_ref, ys_ref, o_ref, ybuf_ref, sems):
    i = pl.program_id(0)
    tm = x_ref.shape[0]
    slot = i % 2

    def gather(d_ref, s):
        def issue(r, _):
            for k in range(2):
                pltpu.make_async_copy(ys_ref.at[pl.ds(d_ref[0, 0, k * tm + r], 1)],
                                      ybuf_ref.at[s, k, pl.ds(r, 1)], sems.at[s]).start()
            return 0

        lax.fori_loop(0, tm, issue, 0, unroll=DMA_ISSUE_UNROLL)

    pl.when(i == 0)(lambda: gather(dest_ref, 0))
    pl.when(i + 1 < pl.num_programs(0))(lambda: gather(dest_next_ref, 1 - slot))
    for k in range(2):
        pltpu.make_async_copy(ys_ref.at[pl.ds(0, tm)], ybuf_ref.at[slot, k], sems.at[slot]).wait()
    p = p_ref[...]
    lo0, hi0 = _unpack_pair_f32(ybuf_ref[slot, 0])
    lo1, hi1 = _unpack_pair_f32(ybuf_ref[slot, 1])
    f = jnp.concatenate([lo0 * p[:, 0:1] + lo1 * p[:, 1:2], hi0 * p[:, 0:1] + hi1 * p[:, 1:2]], axis=1)
    x2 = x_ref[...] + mod_ref[0, 5:6, :] * f
    o_ref[...] = _rms(x2, g_ref[...])


def _combine(x1, mod, probs, final_g, ys, dest):
    T, D = x1.shape
    tm = TM_COMB
    per_b = SEQ // tm
    n = T // tm
    dest_t = dest.reshape(2, n, tm).transpose(1, 0, 2).reshape(n, 1, 2 * tm)
    return pl.pallas_call(
        _combine_kernel,
        grid=(n,),
        in_specs=[pl.BlockSpec((1, 1, 2 * tm), lambda i: (i, 0, 0), memory_space=pltpu.SMEM),
                  pl.BlockSpec((1, 1, 2 * tm), lambda i: (jnp.minimum(i + 1, n - 1), 0, 0),
                               memory_space=pltpu.SMEM),
                  pl.BlockSpec((tm, D), lambda i: (i, 0)),
                  pl.BlockSpec((1, 6, D), lambda i: (i // per_b, 0, 0)),
                  pl.BlockSpec((tm, 2), lambda i: (i, 0)),
                  pl.BlockSpec((1, D), lambda i: (0, 0)),
                  pl.BlockSpec(memory_space=pl.ANY)],
        out_specs=pl.BlockSpec((tm, D), lambda i: (i, 0)),
        out_shape=jax.ShapeDtypeStruct((T, D), F32),
        scratch_shapes=[pltpu.VMEM((2, 2, tm, D // 2), jnp.uint32), pltpu.SemaphoreType.DMA((2,))],
        compiler_params=_params("arbitrary"),
        name="moe_combine",
    )(dest_t, dest_t, x1, mod, probs, final_g, ys)


def _moe_layer(x1, h2, h2_packed, mod, w_router, w_gate, w_up, w_down, final_g):
    T, D = x1.shape
    e_idx, probs, rank, counts = _router(h2, w_router.T.astype(BF16))
    counts = counts[:, 0].astype(jnp.int32)
    padded = (counts + TM_MOE - 1) // TM_MOE * TM_MOE
    pad_end = jnp.cumsum(padded)
    pad_start = pad_end - padded
    eid = jnp.arange(N_EXPERTS, dtype=jnp.int32)[:, None, None]
    dest = jnp.sum(jnp.where(e_idx[None, :2] == eid, pad_start[:, None, None], 0), axis=0) + rank[:2]
    n_slots = 2 * T + N_EXPERTS * TM_MOE
    n_blk = n_slots // TM_MOE
    blk_first = jnp.arange(n_blk, dtype=jnp.int32) * TM_MOE
    blk_e = jnp.minimum(jnp.sum(pad_end[None, :] <= blk_first[:, None], axis=1),
                        N_EXPERTS - 1).astype(jnp.int32)
    n_used = (pad_end[-1:] // TM_MOE).astype(jnp.int32)
    xs = _dispatch(h2_packed, dest, n_slots)
    ys = _experts(xs, blk_e, n_used, w_gate.astype(BF16), w_up.astype(BF16), w_down.astype(BF16))
    return _combine(x1, mod, probs[:2].T, final_g.reshape(1, -1), ys, dest)


def _final_norm_kernel(x_ref, g_ref, o_ref):
    o_ref[...] = _rms(x_ref[...], g_ref[...])


def _final_norm(x, g):
    T, D = x.shape
    tm = 1024
    return pl.pallas_call(
        _final_norm_kernel,
        grid=(T // tm,),
        in_specs=[pl.BlockSpec((tm, D), lambda i: (i, 0)), pl.BlockSpec((1, D), lambda i: (0, 0))],
        out_specs=pl.BlockSpec((tm, D), lambda i: (i, 0)),
        out_shape=jax.ShapeDtypeStruct((T, D), F32),
        compiler_params=_params("arbitrary"),
        name="final_norm",
    )(x, g.reshape(1, -1))


def _regroup_w_in(w):
    ab = jnp.pad(w[:, C_GATE:C_GATE + AB_COLS], ((0, 0), (0, LANES - AB_COLS)))
    return jnp.concatenate([w[:, :C_GATE], w[:, C_GATE + AB_COLS:], ab], axis=1).astype(BF16)


def _lambda_init(layer):
    return 0.8 - 0.6 * math.exp(-0.3 * layer)


def kernel(x, c, positions, norm1_g, norm2_g, w_mod, b_mod, w_in, gm_ln_g, gm_ln_b, gm_w_s, gm_b_s, da_lambda, da_subln_g, dn_conv_w, dn_a_log, dn_dt_bias, dn_norm_g, w_br_gm, w_br_da, w_br_dn, w_out, ffn_w_gate, ffn_w_up, ffn_w_down, moe_w_router, moe_w_gate, moe_w_up, moe_w_down, final_g):
    B, S, D = x.shape
    T = B * S
    xt = x.reshape(T, D)
    mod_all = _modulation(c, w_mod, b_mod).reshape(DEPTH, B, 6, D)
    cos_t, sin_t = _rope_tables(positions)
    for layer in range(DEPTH):
        mod = mod_all[layer]
        gm, qk, v, dn, gates, ab = _in_projection(xt, mod, norm1_g[layer].reshape(1, D),
                                                  _regroup_w_in(w_in[layer]), cos_t, sin_t, dn_conv_w[layer])
        y_gm = _gmlp(gm, gm_ln_g[layer], gm_ln_b[layer], gm_w_s[layer], gm_b_s[layer])
        y_da = _diff_attention(qk, v, da_lambda[layer], da_subln_g[layer], _lambda_init(layer), B)
        y_dn = _deltanet(dn, ab, dn_a_log[layer], dn_dt_bias[layer], dn_norm_g[layer], B)
        moe = layer % 2 == 1
        x1, h2, *packed = _merge(xt, mod, norm2_g[layer].reshape(1, D), y_gm, y_da, y_dn, gates,
                                 w_br_gm[layer].astype(BF16), w_br_da[layer].astype(BF16),
                                 w_br_dn[layer].astype(BF16), w_out[layer].astype(BF16), moe)
        if moe:
            xt = _moe_layer(x1, h2, packed[0], mod, moe_w_router[layer // 2], moe_w_gate[layer // 2],
                            moe_w_up[layer // 2], moe_w_down[layer // 2], final_g)
        else:
            xt = _dense_ffn(x1, h2, mod, ffn_w_gate[layer // 2].astype(BF16),
                            ffn_w_up[layer // 2].astype(BF16), ffn_w_down[layer // 2].astype(BF16))
    if DEPTH % 2 == 1:
        xt = _final_norm(xt, final_g)
    return xt.reshape(B, S, D)
```

```python
import functools
import math

import jax
import jax.numpy as jnp
import numpy as np
from jax import lax
from jax.experimental import pallas as pl
from jax.experimental.pallas import tpu as pltpu

D_MODEL = 1024
SEQ = 2048
DEPTH = 2
CHUNK = 64
EPS = 1e-6
GM_WIDTH = D_MODEL // 2
GM_GROUPS = 4
GM_BLOCK = 128
DA_HEADS = 4
DA_HEAD_DIM = 64
DA_V_DIM = 2 * DA_HEAD_DIM
ROPE_THETA = 10000.0
DN_HEADS = 4
DN_HEAD_DIM = 128
DN_WIDTH = DN_HEADS * DN_HEAD_DIM
DN_CONV = 4
N_EXPERTS = 8
FF_EXPERT = 7 * D_MODEL // 2

LANES = 128
SUBLANES = 8
BF16_SUBLANES = 16
VMEM_LIMIT = 56 * 1024 * 1024
DMA_ISSUE_UNROLL = 8

AB_COLS = 2 * DN_HEADS
C_GM, C_QK, C_V, C_DN, C_GATE = 0, 1024, 2048, 2560, 4608
C_AB = C_GATE + 3 * D_MODEL
W_ALL = C_AB + LANES
PROJ_CHUNK = 512

TM_PROJ = 512
TM_GM = 512
TQ = 256
TK = 256
TM_MERGE = 512
TM_FFN = 512
TM_ROUTE = 512
TM_MOE = 512
TM_DISP = 512
DISP_SLOTS = 3
TM_COMB = 512
ATTN_LOOKAHEAD = 2
VT_ROWS = DA_V_DIM + BF16_SUBLANES
DN_SEQS = 2
DN_BLOCK = 256

BF16 = jnp.bfloat16
F32 = jnp.float32


def _params(*sem):
    return pltpu.CompilerParams(dimension_semantics=sem, vmem_limit_bytes=VMEM_LIMIT)


def _dot(a, b):
    return jnp.dot(a, b, preferred_element_type=F32)


def _dot_nt(a, b):
    return lax.dot_general(a, b, (((1,), (1,)), ((), ())), preferred_element_type=F32)


def _dot_tn(a, b):
    return lax.dot_general(a, b, (((0,), (0,)), ((), ())), preferred_element_type=F32)


def _rms(x, g):
    return x * lax.rsqrt(jnp.mean(x * x, axis=-1, keepdims=True) + EPS) * g


def _silu(x):
    return x * jax.nn.sigmoid(x)


def _mod_kernel(c_ref, w_ref, b_ref, o_ref):
    c = c_ref[...]
    o_ref[0] = _dot(_silu(c).astype(BF16), w_ref[0].astype(BF16)) + b_ref[0]


def _modulation(c, w_mod, b_mod):
    B, D = c.shape
    L, _, N = w_mod.shape
    tn = 1536
    return pl.pallas_call(
        _mod_kernel,
        grid=(L, N // tn),
        in_specs=[pl.BlockSpec((B, D), lambda l, j: (0, 0)),
                  pl.BlockSpec((1, D, tn), lambda l, j: (l, 0, j)),
                  pl.BlockSpec((1, 1, tn), lambda l, j: (l, 0, j))],
        out_specs=pl.BlockSpec((1, B, tn), lambda l, j: (l, 0, j)),
        out_shape=jax.ShapeDtypeStruct((L, B, N), F32),
        compiler_params=_params("arbitrary", "arbitrary"),
        name="modulation",
    )(c, w_mod, b_mod.reshape(L, 1, N))


def _rope_tab_kernel(pos_ref, inv_ref, sgn_ref, cos_ref, sin_ref):
    ang = pos_ref[...].astype(F32) * inv_ref[...]
    cos_ref[...] = jnp.cos(ang)
    sin_ref[...] = jnp.sin(ang) * sgn_ref[...]


def _rope_tables(positions):
    T = positions.size
    inv_freq = ROPE_THETA ** (-jnp.arange(0, DA_HEAD_DIM, 2, dtype=F32) / DA_HEAD_DIM)
    inv = jnp.tile(inv_freq, LANES // (DA_HEAD_DIM // 2)).reshape(1, LANES)
    half = DA_HEAD_DIM // 2
    sgn = np.tile(np.concatenate([-np.ones(half), np.ones(half)]), LANES // DA_HEAD_DIM)
    sgn = jnp.asarray(sgn, F32).reshape(1, LANES)
    tm = 1024
    return pl.pallas_call(
        _rope_tab_kernel,
        grid=(T // tm,),
        in_specs=[pl.BlockSpec((tm, 1), lambda i: (i, 0)),
                  pl.BlockSpec((1, LANES), lambda i: (0, 0)),
                  pl.BlockSpec((1, LANES), lambda i: (0, 0))],
        out_specs=[pl.BlockSpec((tm, LANES), lambda i: (i, 0))] * 2,
        out_shape=[jax.ShapeDtypeStruct((T, LANES), F32)] * 2,
        compiler_params=_params("arbitrary"),
        name="rope_tables",
    )(positions.reshape(T, 1), inv, sgn)


def _rope_rows(x, cos, sin):
    lane = lax.broadcasted_iota(jnp.int32, x.shape, 1)
    first = (lane % DA_HEAD_DIM) < (DA_HEAD_DIM // 2)
    partner = jnp.where(first, pltpu.roll(x, LANES - DA_HEAD_DIM // 2, axis=1),
                        pltpu.roll(x, DA_HEAD_DIM // 2, axis=1))
    return x * cos + partner * sin


def _proj_kernel(x_ref, mod_ref, g_ref, w_ref, cos_ref, sin_ref, cw_ref,
                 gm_ref, qk_ref, v_ref, dn_ref, gate_ref, ab_ref, prev_ref):
    tm = x_ref.shape[0]

    @pl.when(pl.program_id(0) % (SEQ // tm) == 0)
    def _():
        prev_ref[...] = jnp.zeros_like(prev_ref)

    def conv_silu(y, idx):
        cw = cw_ref[:, idx * PROJ_CHUNK:(idx + 1) * PROJ_CHUNK]
        ycat = jnp.concatenate([prev_ref[idx], y], axis=0)
        out = y * cw[DN_CONV - 1:DN_CONV, :]
        for j in range(DN_CONV - 1):
            out = out + pltpu.roll(ycat, DN_CONV - 1 - j, axis=0)[SUBLANES:, :] * cw[j:j + 1, :]
        prev_ref[idx] = y[tm - SUBLANES:, :]
        return _silu(out)

    def l2norm_heads(t, mult):
        parts = []
        for j in range(PROJ_CHUNK // DN_HEAD_DIM):
            seg = t[:, j * DN_HEAD_DIM:(j + 1) * DN_HEAD_DIM]
            parts.append(seg * (lax.rsqrt(jnp.sum(seg * seg, axis=-1, keepdims=True) + EPS) * mult))
        return jnp.concatenate(parts, axis=1)

    x = x_ref[...]
    shift = mod_ref[0, 0:1, :]
    scale = mod_ref[0, 1:2, :]
    h = (_rms(x, g_ref[...]) * (1.0 + scale) + shift).astype(BF16)
    cos = cos_ref[...]
    sin = sin_ref[...]
    q_scale = DA_HEAD_DIM ** -0.5 * math.log2(math.e)

    def rope_chunk(y, mult):
        parts = [_rope_rows(y[:, j * LANES:(j + 1) * LANES], cos, sin) * mult
                 for j in range(PROJ_CHUNK // LANES)]
        return jnp.concatenate(parts, axis=1)

    dn_chunks = list(range(C_DN, C_GATE, PROJ_CHUNK))
    gate_chunks = list(range(C_GATE, C_AB, PROJ_CHUNK))
    order = list(range(0, C_DN, PROJ_CHUNK))
    for j, c in enumerate(dn_chunks):
        order += [c] + gate_chunks[2 * j:2 * j + 2]
    assert sorted(order) == list(range(0, C_AB, PROJ_CHUNK))
    for c0 in order:
        y = _dot(h, w_ref[:, c0:c0 + PROJ_CHUNK])
        if c0 < C_QK:
            gm_ref[:, c0:c0 + PROJ_CHUNK] = y.astype(BF16)
        elif c0 < C_V:
            mult = q_scale if c0 == C_QK else 1.0
            qk_ref[:, c0 - C_QK:c0 - C_QK + PROJ_CHUNK] = rope_chunk(y, mult).astype(BF16)
        elif c0 < C_DN:
            v_ref[...] = y.astype(BF16)
        elif c0 < C_GATE:
            idx = (c0 - C_DN) // PROJ_CHUNK
            if idx < 3:
                y = conv_silu(y, idx)
            if idx < 2:
                y = l2norm_heads(y, DN_HEAD_DIM ** -0.5 if idx == 0 else 1.0)
            dn_ref[:, c0 - C_DN:c0 - C_DN + PROJ_CHUNK] = y.astype(BF16)
        else:
            gate_ref[:, c0 - C_GATE:c0 - C_GATE + PROJ_CHUNK] = y.astype(BF16)
    ab_ref[...] = _dot(h, w_ref[:, C_AB:W_ALL])


def _in_projection(x, mod, g, w_all, cos_t, sin_t, conv_w):
    T, D = x.shape
    tm = TM_PROJ
    per_b = SEQ // tm
    row = lambda i: (i, 0)
    widths = (C_QK - C_GM, C_V - C_QK, C_DN - C_V, C_GATE - C_DN, C_AB - C_GATE)
    return pl.pallas_call(
        _proj_kernel,
        grid=(T // tm,),
        in_specs=[pl.BlockSpec((tm, D), row),
                  pl.BlockSpec((1, 6, D), lambda i: (i // per_b, 0, 0)),
                  pl.BlockSpec((1, D), lambda i: (0, 0)),
                  pl.BlockSpec((D, W_ALL), lambda i: (0, 0), pipeline_mode=pl.Buffered(1)),
                  pl.BlockSpec((tm, LANES), row),
                  pl.BlockSpec((tm, LANES), row),
                  pl.BlockSpec((DN_CONV, 3 * DN_WIDTH), lambda i: (0, 0))],
        out_specs=[pl.BlockSpec((tm, w), row) for w in widths] + [pl.BlockSpec((tm, LANES), row)],
        out_shape=[jax.ShapeDtypeStruct((T, w), BF16) for w in widths]
                  + [jax.ShapeDtypeStruct((T, LANES), F32)],
        scratch_shapes=[pltpu.VMEM((3, SUBLANES, PROJ_CHUNK), F32)],
        compiler_params=_params("arbitrary"),
        name="in_projection",
    )(x, mod, g, w_all, cos_t, sin_t, conv_w)


def _gelu(x):
    return 0.5 * x * (1.0 + lax.erf(x * np.float32(math.sqrt(0.5))))


def _gmlp_kernel(uv_ref, lng_ref, lnb_ref, ws_ref, bs_ref, o_ref):
    u = _gelu(uv_ref[:, :GM_WIDTH].astype(F32))
    v = _gelu(uv_ref[:, GM_WIDTH:].astype(F32))
    mu = jnp.mean(v, axis=-1, keepdims=True)
    vc = v - mu
    var = jnp.mean(vc * vc, axis=-1, keepdims=True)
    v = (vc * lax.rsqrt(var + EPS) * lng_ref[...] + lnb_ref[...]).astype(BF16)
    ri = lax.broadcasted_iota(jnp.int32, (GM_BLOCK, GM_BLOCK), 0) // CHUNK
    ci = lax.broadcasted_iota(jnp.int32, (GM_BLOCK, GM_BLOCK), 1) // CHUNK
    allowed = ci <= ri
    gc = GM_WIDTH // GM_GROUPS
    for g in range(GM_GROUPS):
        w = jnp.where(allowed, ws_ref[g], 0.0).astype(BF16)
        bias = bs_ref[g]
        for r in range(uv_ref.shape[0] // GM_BLOCK):
            rows = slice(r * GM_BLOCK, (r + 1) * GM_BLOCK)
            cols = slice(g * gc, (g + 1) * gc)
            sv = _dot(w, v[rows, cols]) + bias
            o_ref[rows, cols] = (u[rows, cols] * sv).astype(BF16)


def _gmlp(uv, ln_g, ln_b, w_s, b_s):
    T = uv.shape[0]
    tm = TM_GM
    return pl.pallas_call(
        _gmlp_kernel,
        grid=(T // tm,),
        in_specs=[pl.BlockSpec((tm, 2 * GM_WIDTH), lambda i: (i, 0)),
                  pl.BlockSpec((1, GM_WIDTH), lambda i: (0, 0)),
                  pl.BlockSpec((1, GM_WIDTH), lambda i: (0, 0)),
                  pl.BlockSpec((GM_GROUPS, GM_BLOCK, GM_BLOCK), lambda i: (0, 0, 0)),
                  pl.BlockSpec((GM_GROUPS, GM_BLOCK, 1), lambda i: (0, 0, 0))],
        out_specs=pl.BlockSpec((tm, GM_WIDTH), lambda i: (i, 0)),
        out_shape=jax.ShapeDtypeStruct((T, GM_WIDTH), BF16),
        compiler_params=_params("arbitrary"),
        name="gmlp",
    )(uv, ln_g.reshape(1, -1), ln_b.reshape(1, -1), w_s, b_s.reshape(GM_GROUPS, GM_BLOCK, 1))


def _attn_kernel(lam_ref, g_ref, q_ref, k_ref, v_ref, o_ref, qz_ref, vt_ref, *stats, lambda_init):
    nc = 2 * DA_HEADS
    m_ref, acc_ref, s_ref = (stats[j * nc:(j + 1) * nc] for j in range(3))
    i = pl.program_id(1)
    lp = lam_ref[...]
    lam = (jnp.exp(jnp.sum(lp[0:1] * lp[1:2], axis=-1, keepdims=True))
           - jnp.exp(jnp.sum(lp[2:3] * lp[3:4], axis=-1, keepdims=True)) + lambda_init)
    @pl.when(i == 0)
    def _():
        for kb in range(SEQ // TK):
            for h in range(DA_HEADS):
                vt_ref[h * VT_ROWS:h * VT_ROWS + DA_V_DIM, kb * TK:(kb + 1) * TK] = (
                    v_ref[kb * TK:(kb + 1) * TK, h * LANES:(h + 1) * LANES].T)
        for h in range(DA_HEADS):
            vt_ref[h * VT_ROWS + DA_V_DIM:(h + 1) * VT_ROWS, :] = jnp.ones((VT_ROWS - DA_V_DIM, SEQ), BF16)

    row = lax.broadcasted_iota(jnp.int32, (LANES, TQ), 0)
    for h in range(DA_HEADS):
        qt = q_ref[:, h * LANES:(h + 1) * LANES].T
        zero = jnp.zeros_like(qt)
        qz_ref[2 * h] = jnp.where(row < DA_HEAD_DIM, qt, zero)
        qz_ref[2 * h + 1] = jnp.where(row >= DA_HEAD_DIM, qt, zero)
    for c in range(nc):
        m_ref[c][...] = jnp.full_like(m_ref[c], -jnp.inf)
        acc_ref[c][...] = jnp.zeros_like(acc_ref[c])

    def key_rows(kb):
        return pl.ds(pl.multiple_of(kb * TK, TK), TK)

    def scores(kb, c, masked):
        h = c // 2
        s = _dot(k_ref[key_rows(kb), h * LANES:(h + 1) * LANES], qz_ref[c])
        if masked:
            ki = lax.broadcasted_iota(jnp.int32, s.shape, 0) // CHUNK + kb * (TK // CHUNK)
            qi = lax.broadcasted_iota(jnp.int32, s.shape, 1) // CHUNK + i * (TQ // CHUNK)
            s = jnp.where(ki <= qi, s, -jnp.inf)
        return s

    def accumulate(kb, c, s):
        h = c // 2
        m_old = m_ref[c][...]
        m_new = jnp.maximum(m_old, jnp.max(s, axis=0, keepdims=True))
        alpha = jnp.exp2(m_old - m_new)
        e = jnp.exp2((s - m_new).astype(BF16))
        vt = vt_ref[h * VT_ROWS:(h + 1) * VT_ROWS, key_rows(kb)]
        acc_ref[c][...] = alpha * acc_ref[c][...] + _dot(vt, e)
        m_ref[c][...] = m_new

    def fold_and_prefetch(kb, masked_next):
        fresh = {}
        for c in range(nc + ATTN_LOOKAHEAD):
            if c < nc:
                fresh[c] = scores(kb + 1, c, masked_next)
            if c >= ATTN_LOOKAHEAD:
                cc = c - ATTN_LOOKAHEAD
                accumulate(kb, cc, s_ref[cc][...])
                s_ref[cc][...] = fresh.pop(cc)

    first_masked = i * (TQ // TK)

    @pl.when(i == 0)
    def _():
        for c in range(nc):
            s_ref[c][...] = scores(0, c, True)

    @pl.when(i > 0)
    def _():
        for c in range(nc):
            s_ref[c][...] = scores(0, c, False)

        def body(kb, carry):
            fold_and_prefetch(kb, False)
            return carry

        lax.fori_loop(0, first_masked - 1, body, 0)
        fold_and_prefetch(first_masked - 1, True)

    for r in range(TQ // TK - 1):
        fold_and_prefetch(first_masked + r, True)
    for c in range(nc):
        accumulate(first_masked + TQ // TK - 1, c, s_ref[c][...])
    for h in range(DA_HEADS):
        num0, den0 = acc_ref[2 * h][:DA_V_DIM, :], acc_ref[2 * h][DA_V_DIM:DA_V_DIM + 1, :]
        num1, den1 = acc_ref[2 * h + 1][:DA_V_DIM, :], acc_ref[2 * h + 1][DA_V_DIM:DA_V_DIM + 1, :]
        o = num0 / den0 - lam * (num1 / den1)
        ms = jnp.mean(o * o, axis=0, keepdims=True)
        o = o * lax.rsqrt(ms + EPS) * (g_ref[...] * (1.0 - lambda_init))
        o_ref[:, h * LANES:(h + 1) * LANES] = o.T.astype(BF16)


def _diff_attention(qk, v, lam_p, subln_g, lambda_init, batch):
    T = qk.shape[0]
    nq = SEQ // TQ
    W = DA_HEADS * DA_V_DIM
    kernel = functools.partial(_attn_kernel, lambda_init=lambda_init)
    return pl.pallas_call(
        kernel,
        grid=(batch, nq),
        in_specs=[pl.BlockSpec((4, DA_HEAD_DIM), lambda b, i: (0, 0)),
                  pl.BlockSpec((DA_V_DIM, 1), lambda b, i: (0, 0)),
                  pl.BlockSpec((TQ, W), lambda b, i: (b * nq + i, 0)),
                  pl.BlockSpec((SEQ, W), lambda b, i: (b, 1)),
                  pl.BlockSpec((SEQ, W), lambda b, i: (b, 0))],
        out_specs=pl.BlockSpec((TQ, W), lambda b, i: (b * nq + i, 0)),
        out_shape=jax.ShapeDtypeStruct((T, W), BF16),
        scratch_shapes=[pltpu.VMEM((2 * DA_HEADS, LANES, TQ), BF16),
                        pltpu.VMEM((DA_HEADS * VT_ROWS, SEQ), BF16)]
                       + [pltpu.VMEM((1, TQ), F32)] * (2 * DA_HEADS)
                       + [pltpu.VMEM((VT_ROWS, TQ), F32)] * (2 * DA_HEADS)
                       + [pltpu.VMEM((TK, TQ), F32)] * (2 * DA_HEADS),
        compiler_params=_params("arbitrary", "arbitrary"),
        name="diff_attention",
    )(lam_p, subln_g.reshape(-1, 1), qk, qk, v)


def _split3(x):
    hi = x.astype(BF16)
    r = x - hi.astype(F32)
    mid = r.astype(BF16)
    lo = (r - mid.astype(F32)).astype(BF16)
    return hi, mid, lo


def _dn_kernel(dn_ref, a_ref, at_ref, alogt_ref, dtbt_ref, ng_ref,
               o_ref, *state_ref):
    c = pl.program_id(1)
    R = DN_BLOCK
    G = dn_ref.shape[0]

    @pl.when(c == 0)
    def _():
        for ref in state_ref:
            ref[...] = jnp.zeros_like(ref)

    W3 = 3 * DN_WIDTH
    ri = lax.broadcasted_iota(jnp.int32, (R, R), 0)
    ci = lax.broadcasted_iota(jnp.int32, (R, R), 1)
    tril = ri >= ci
    strict = ri > ci
    ones_triu = jnp.where(ci >= ri, 1.0, 0.0).astype(BF16)
    eye = jnp.where(ri == ci, 1.0, 0.0)
    same = {n: (ri // n) == (ci // n) for n in (16, 32, 64, 128)}
    gc_col, gc_row, beta_all = [], [], []
    for j in range(G):
        beta_all.append(jax.nn.sigmoid(a_ref[j]))
        abt = at_ref[j, 0]
        g_row = -jnp.exp(alogt_ref[...]) * jax.nn.softplus(abt + dtbt_ref[...])
        gc_row.append(sum(_dot(part, ones_triu) for part in _split3(g_row)))
        gc_col.append(gc_row[-1].T)

    H = range(G * DN_HEADS)
    seq = [n // DN_HEADS for n in H]
    hd = [n % DN_HEADS for n in H]
    hsl = [slice(hd[n] * DN_HEAD_DIM, (hd[n] + 1) * DN_HEAD_DIM) for n in H]
    q16 = [dn_ref[seq[n], :, hsl[n]] for n in H]
    k16 = [dn_ref[seq[n], :, DN_WIDTH + hd[n] * DN_HEAD_DIM:DN_WIDTH + (hd[n] + 1) * DN_HEAD_DIM] for n in H]
    q = [t.astype(F32) for t in q16]
    k = [t.astype(F32) for t in k16]
    v = [dn_ref[seq[n], :, 2 * DN_WIDTH + hd[n] * DN_HEAD_DIM:2 * DN_WIDTH + (hd[n] + 1) * DN_HEAD_DIM]
         .astype(F32) for n in H]
    gc = [gc_col[seq[n]][:, hd[n]:hd[n] + 1] for n in H]
    beta = [beta_all[seq[n]][:, DN_HEADS + hd[n]:DN_HEADS + hd[n] + 1] for n in H]
    g_last = [gc_col[seq[n]][R - 1:R, hd[n]:hd[n] + 1] for n in H]
    decay = [jnp.exp(jnp.where(tril, gc[n] - gc_row[seq[n]][hd[n]:hd[n] + 1, :], -jnp.inf)) for n in H]
    eg = [jnp.exp(t) for t in gc]
    kb = [k[h] * beta[h] for h in H]
    s1 = [_dot_nt(jnp.concatenate([kb[h].astype(BF16), q16[h]], axis=0), k16[h]) for h in H]
    a_mat = [jnp.where(strict, s1[h][:R] * decay[h], 0.0) for h in H]
    qk = [(s1[h][R:] * decay[h]).astype(BF16) for h in H]
    p = [jnp.where(same[16], -t, 0.0) for t in a_mat]
    t_inv = [eye + t for t in p]
    for _ in range(3):
        p16 = [t.astype(BF16) for t in p]
        p = [_dot(t, t) for t in p16]
        t_inv = [t_inv[h] + _dot(t_inv[h].astype(BF16), p[h].astype(BF16)) for h in H]
    a16 = [t.astype(BF16) for t in a_mat]
    t16 = [t.astype(BF16) for t in t_inv]
    for n in (16, 32, 64, 128):
        off = jnp.logical_not(same[n])
        if 2 * n < R:
            off = jnp.logical_and(same[2 * n], off)
        off16 = jnp.where(off, 1.0, 0.0).astype(BF16)
        tl = [_dot(t16[h], a16[h] * off16).astype(BF16) for h in H]
        t16 = [t16[h] - _dot(tl[h], t16[h]).astype(BF16) * off16 for h in H]
    sol = [_dot(t16[h], jnp.concatenate([v[h] * beta[h], kb[h] * eg[h]], axis=1).astype(BF16))
           for h in H]
    state = [state_ref[h][...] for h in H]
    m1 = [_dot(jnp.concatenate([sol[h][:, DN_HEAD_DIM:], q[h] * eg[h]], axis=0).astype(BF16),
               state[h].astype(BF16)) for h in H]
    vn16 = [(sol[h][:, :DN_HEAD_DIM] - m1[h][:R]).astype(BF16) for h in H]
    o = [m1[h][R:] + _dot(qk[h], vn16[h]) for h in H]
    kd = [(k[h] * jnp.exp(g_last[h] - gc[h])).astype(BF16) for h in H]
    for h in H:
        state_ref[h][...] = state[h] * jnp.exp(g_last[h]) + _dot_tn(kd[h], vn16[h])
    for n in H:
        z = dn_ref[seq[n], :, W3 + hd[n] * DN_HEAD_DIM:W3 + (hd[n] + 1) * DN_HEAD_DIM].astype(F32)
        o_ref[seq[n], :, hsl[n]] = (_rms(o[n], ng_ref[...]) * _silu(z)).astype(BF16)


def _deltanet(dn, ab, a_log, dt_bias, norm_g, batch):
    T = dn.shape[0]
    R = DN_BLOCK
    n = SEQ // R
    G = DN_SEQS if batch % DN_SEQS == 0 else 1
    col8 = lambda t: jnp.zeros((AB_COLS, 1), F32).at[:DN_HEADS, 0].set(t)
    ab_t = ab[:, :AB_COLS].reshape(batch, n, R, AB_COLS).transpose(0, 1, 3, 2)
    const = lambda b, c: (0, 0)
    out = pl.pallas_call(
        _dn_kernel,
        grid=(batch // G, n),
        in_specs=[pl.BlockSpec((G, R, 4 * DN_WIDTH), lambda b, c: (b, c, 0)),
                  pl.BlockSpec((G, R, LANES), lambda b, c: (b, c, 0)),
                  pl.BlockSpec((G, 1, AB_COLS, R), lambda b, c: (b, c, 0, 0)),
                  pl.BlockSpec((AB_COLS, 1), const),
                  pl.BlockSpec((AB_COLS, 1), const),
                  pl.BlockSpec((1, DN_HEAD_DIM), const)],
        out_specs=pl.BlockSpec((G, R, DN_WIDTH), lambda b, c: (b, c, 0)),
        out_shape=jax.ShapeDtypeStruct((batch, SEQ, DN_WIDTH), BF16),
        scratch_shapes=[pltpu.VMEM((DN_HEAD_DIM, DN_HEAD_DIM), F32)] * (G * DN_HEADS),
        compiler_params=_params("arbitrary", "arbitrary"),
        name="deltanet",
    )(dn.reshape(batch, SEQ, -1), ab.reshape(batch, SEQ, -1), ab_t,
      col8(a_log), col8(dt_bias), norm_g.reshape(1, -1))
    return out.reshape(T, DN_WIDTH)


def _pack_bf16_pair(lo, hi):
    lo_bits = lax.bitcast_convert_type(lo.astype(BF16).astype(F32), jnp.uint32)
    hi_bits = lax.bitcast_convert_type(hi.astype(BF16).astype(F32), jnp.uint32)
    return (lo_bits >> 16) | (hi_bits & jnp.uint32(0xFFFF0000))


def _unpack_bf16_pair(words):
    lo = lax.bitcast_convert_type(words << 16, F32)
    hi = lax.bitcast_convert_type(words & jnp.uint32(0xFFFF0000), F32)
    return lo.astype(BF16), hi.astype(BF16)


def _merge_kernel(x_ref, mod_ref, g2_ref, ygm_ref, yda_ref, ydn_ref, gate_ref,
                  wgm_ref, wda_ref, wdn_ref, wout_ref, x1_ref, h2_ref, *maybe_packed_ref):
    D = D_MODEL
    def gate(j):
        return 0.5 * jnp.tanh(0.5 * gate_ref[:, j * D:(j + 1) * D].astype(F32)) + 0.5

    merged = (gate(0) * _dot(ygm_ref[...], wgm_ref[...])
              + gate(1) * _dot(yda_ref[...], wda_ref[...])
              + gate(2) * _dot(ydn_ref[...], wdn_ref[...]))
    y = _dot(merged.astype(BF16), wout_ref[...])
    x1 = x_ref[...] + mod_ref[0, 2:3, :] * y
    x1_ref[...] = x1
    h2 = _rms(x1, g2_ref[...]) * (1.0 + mod_ref[0, 4:5, :]) + mod_ref[0, 3:4, :]
    h2_ref[...] = h2.astype(BF16)
    for packed_ref in maybe_packed_ref:
        packed_ref[...] = _pack_bf16_pair(h2[:, :D // 2], h2[:, D // 2:])


def _merge(x, mod, g2, y_gm, y_da, y_dn, gates, w_gm, w_da, w_dn, w_out, with_packed):
    T, D = x.shape
    tm = TM_MERGE
    per_b = SEQ // tm
    row = lambda i: (i, 0)
    const = lambda i: (0, 0)
    out_specs = [pl.BlockSpec((tm, D), row), pl.BlockSpec((tm, D), row)]
    out_shape = [jax.ShapeDtypeStruct((T, D), F32), jax.ShapeDtypeStruct((T, D), BF16)]
    if with_packed:
        out_specs.append(pl.BlockSpec((tm, D // 2), row))
        out_shape.append(jax.ShapeDtypeStruct((T, D // 2), jnp.uint32))
    return pl.pallas_call(
        _merge_kernel,
        grid=(T // tm,),
        in_specs=[pl.BlockSpec((tm, D), row),
                  pl.BlockSpec((1, 6, D), lambda i: (i // per_b, 0, 0)),
                  pl.BlockSpec((1, D), const),
                  pl.BlockSpec((tm, GM_WIDTH), row),
                  pl.BlockSpec((tm, GM_WIDTH), row),
                  pl.BlockSpec((tm, DN_WIDTH), row),
                  pl.BlockSpec((tm, 3 * D), row),
                  pl.BlockSpec((GM_WIDTH, D), const),
                  pl.BlockSpec((GM_WIDTH, D), const),
                  pl.BlockSpec((DN_WIDTH, D), const),
                  pl.BlockSpec((D, D), const)],
        out_specs=out_specs,
        out_shape=out_shape,
        compiler_params=_params("arbitrary"),
        name="merge",
    )(x, mod, g2, y_gm, y_da, y_dn, gates, w_gm, w_da, w_dn, w_out)


def _ffn_kernel(x_ref, h_ref, mod_ref, wg_ref, wu_ref, wd_ref, o_ref):
    h = h_ref[...]
    a = _dot(h, wg_ref[...])
    b = _dot(h, wu_ref[...])
    f = _dot((_silu(a) * b).astype(BF16), wd_ref[...])
    o_ref[...] = x_ref[...] + mod_ref[0, 5:6, :] * f


def _dense_ffn(x1, h2, mod, w_gate, w_up, w_down):
    T, D = x1.shape
    F = w_gate.shape[1]
    tm = TM_FFN
    per_b = SEQ // tm
    row = lambda i: (i, 0)
    const = lambda i: (0, 0)
    return pl.pallas_call(
        _ffn_kernel,
        grid=(T // tm,),
        in_specs=[pl.BlockSpec((tm, D), row),
                  pl.BlockSpec((tm, D), row),
                  pl.BlockSpec((1, 6, D), lambda i: (i // per_b, 0, 0)),
                  pl.BlockSpec((D, F), const, pipeline_mode=pl.Buffered(1)),
                  pl.BlockSpec((D, F), const, pipeline_mode=pl.Buffered(1)),
                  pl.BlockSpec((F, D), const, pipeline_mode=pl.Buffered(1))],
        out_specs=pl.BlockSpec((tm, D), row),
        out_shape=jax.ShapeDtypeStruct((T, D), F32),
        compiler_params=_params("arbitrary"),
        name="dense_ffn",
    )(x1, h2, mod, w_gate, w_up, w_down)


def _router_kernel(h_ref, wr_ref, e_ref, p_ref, r_ref, cnt_ref, base_ref):
    i = pl.program_id(0)
    tm = h_ref.shape[0]

    @pl.when(i == 0)
    def _():
        base_ref[...] = jnp.zeros_like(base_ref)

    logits = _dot_nt(wr_ref[...], h_ref[...])
    row = lax.broadcasted_iota(jnp.int32, logits.shape, 0)
    m1 = jnp.max(logits, axis=0, keepdims=True)
    i1 = jnp.min(jnp.where(logits == m1, row, N_EXPERTS), axis=0, keepdims=True)
    rest = jnp.where(row == i1, -jnp.inf, logits)
    m2 = jnp.max(rest, axis=0, keepdims=True)
    i2 = jnp.min(jnp.where(rest == m2, row, N_EXPERTS), axis=0, keepdims=True)
    e2 = jnp.exp(m2 - m1)
    w1 = 1.0 / (1.0 + e2)
    w2 = e2 / (1.0 + e2)
    oh1 = jnp.where(row == i1, 1.0, 0.0)
    oh2 = jnp.where(row == i2, 1.0, 0.0)
    both = oh1 + oh2
    ti = lax.broadcasted_iota(jnp.int32, (tm, tm), 0)
    tj = lax.broadcasted_iota(jnp.int32, (tm, tm), 1)
    before = jnp.where(ti < tj, 1.0, 0.0).astype(BF16)
    pos = base_ref[...] + _dot(both.astype(BF16), before)
    r1 = jnp.sum(oh1 * pos, axis=0, keepdims=True)
    r2 = jnp.sum(oh2 * pos, axis=0, keepdims=True)
    base_ref[...] = base_ref[...] + jnp.sum(both, axis=1, keepdims=True)
    zi = jnp.zeros((N_EXPERTS - 2, tm), jnp.int32)
    zf = jnp.zeros((N_EXPERTS - 2, tm), F32)
    e_ref[...] = jnp.concatenate([i1, i2, zi], axis=0)
    p_ref[...] = jnp.concatenate([w1, w2, zf], axis=0)
    r_ref[...] = jnp.concatenate([r1.astype(jnp.int32), r2.astype(jnp.int32), zi], axis=0)
    cnt_ref[...] = jnp.broadcast_to(base_ref[...], cnt_ref.shape)


def _router(h2, w_router_t):
    T, D = h2.shape
    tm = TM_ROUTE
    col = lambda i: (0, i)
    return pl.pallas_call(
        _router_kernel,
        grid=(T // tm,),
        in_specs=[pl.BlockSpec((tm, D), lambda i: (i, 0)),
                  pl.BlockSpec((N_EXPERTS, D), lambda i: (0, 0))],
        out_specs=[pl.BlockSpec((N_EXPERTS, tm), col)] * 3 + [pl.BlockSpec((N_EXPERTS, LANES), lambda i: (0, 0))],
        out_shape=[jax.ShapeDtypeStruct((N_EXPERTS, T), jnp.int32),
                   jax.ShapeDtypeStruct((N_EXPERTS, T), F32),
                   jax.ShapeDtypeStruct((N_EXPERTS, T), jnp.int32),
                   jax.ShapeDtypeStruct((N_EXPERTS, LANES), F32)],
        scratch_shapes=[pltpu.VMEM((N_EXPERTS, 1), F32)],
        compiler_params=_params("arbitrary"),
        name="moe_router",
    )(h2, w_router_t)


def _dispatch_kernel(dest_ref, h_ref, xs_in_ref, xs_ref, buf_ref, sem_in, sem_out):
    del xs_in_ref
    i = pl.program_id(0)
    n = pl.num_programs(0)
    tm = buf_ref.shape[1]

    def load(t):
        s = t % DISP_SLOTS
        return pltpu.make_async_copy(h_ref.at[pl.ds(pl.multiple_of(t * tm, tm), tm)], buf_ref.at[s],
                                     sem_in.at[s])

    def wait_rows(t):
        s = t % DISP_SLOTS
        for _ in range(2):
            pltpu.make_async_copy(buf_ref.at[s], xs_ref.at[pl.ds(0, tm)], sem_out.at[s]).wait()

    @pl.when(i == 0)
    def _():
        load(0).start()
        pl.when(n > 1)(lambda: load(1).start())

    load(i).wait()
    slot = i % DISP_SLOTS

    def issue(r, _):
        for k in range(2):
            pltpu.make_async_copy(buf_ref.at[slot, pl.ds(r, 1)],
                                  xs_ref.at[pl.ds(dest_ref[0, 0, k * tm + r], 1)], sem_out.at[slot]).start(priority=k)
        return 0

    lax.fori_loop(0, tm, issue, 0, unroll=DMA_ISSUE_UNROLL)
    pl.when(i > 0)(lambda: wait_rows(i - 1))
    pl.when(i + 2 < n)(lambda: load(i + 2).start())
    pl.when(i == n - 1)(lambda: wait_rows(i))


def _dispatch(h2, dest, n_slots):
    T, D = h2.shape
    tm = TM_DISP
    dest_t = dest.reshape(2, T // tm, tm).transpose(1, 0, 2).reshape(T // tm, 1, 2 * tm)
    xs0 = jnp.zeros((n_slots, D), h2.dtype)
    return pl.pallas_call(
        _dispatch_kernel,
        grid=(T // tm,),
        in_specs=[pl.BlockSpec((1, 1, 2 * tm), lambda i: (i, 0, 0), memory_space=pltpu.SMEM),
                  pl.BlockSpec(memory_space=pl.ANY),
                  pl.BlockSpec(memory_space=pl.ANY)],
        out_specs=pl.BlockSpec(memory_space=pl.ANY),
        out_shape=jax.ShapeDtypeStruct((n_slots, D), h2.dtype),
        scratch_shapes=[pltpu.VMEM((DISP_SLOTS, tm, D), h2.dtype),
                        pltpu.SemaphoreType.DMA((DISP_SLOTS,)), pltpu.SemaphoreType.DMA((DISP_SLOTS,))],
        input_output_aliases={2: 0},
        compiler_params=_params("arbitrary"),
        name="moe_dispatch",
    )(dest_t, h2, xs0)


def _expert_kernel(be_ref, nu_ref, x_ref, wg_ref, wu_ref, wd_ref, o_ref):
    del be_ref
    used = pl.program_id(0) < nu_ref[0]
    half = D_MODEL // 2

    @pl.when(used)
    def _():
        lo, hi = _unpack_bf16_pair(x_ref[...])
        a = _dot(lo, wg_ref[0, :half, :]) + _dot(hi, wg_ref[0, half:, :])
        b = _dot(lo, wu_ref[0, :half, :]) + _dot(hi, wu_ref[0, half:, :])
        o_ref[...] = _dot((_silu(a) * b).astype(BF16), wd_ref[0])

    @pl.when(jnp.logical_not(used))
    def _():
        o_ref[...] = jnp.zeros_like(o_ref)


def _experts(xs, blk_e, n_used, w_gate, w_up, w_down):
    P = xs.shape[0]
    _, D, F = w_gate.shape
    tm = TM_MOE
    grid_spec = pltpu.PrefetchScalarGridSpec(
        num_scalar_prefetch=2,
        grid=(P // tm,),
        in_specs=[pl.BlockSpec((tm, D // 2), lambda i, be, nu: (i, 0)),
                  pl.BlockSpec((1, D, F), lambda i, be, nu: (be[i], 0, 0), pipeline_mode=pl.Buffered(1)),
                  pl.BlockSpec((1, D, F), lambda i, be, nu: (be[i], 0, 0), pipeline_mode=pl.Buffered(1)),
                  pl.BlockSpec((1, F, D), lambda i, be, nu: (be[i], 0, 0), pipeline_mode=pl.Buffered(1))],
        out_specs=pl.BlockSpec((tm, D), lambda i, be, nu: (i, 0)))
    return pl.pallas_call(
        _expert_kernel,
        grid_spec=grid_spec,
        out_shape=jax.ShapeDtypeStruct((P, D), F32),
        compiler_params=_params("arbitrary"),
        name="moe_experts",
    )(blk_e, n_used, xs, w_gate, w_up, w_down)


def _combine_kernel(dest_ref, dest_next_ref, x_ref, mod_ref, p_ref, g_ref, ys_ref, o_ref, ybuf_ref, sems):
    i = pl.program_id(0)
    tm = x_ref.shape[0]
    slot = i % 2

    def gather(d_ref, s):
        def issue(r, _):
            for k in range(2):
                pltpu.make_async_copy(ys_ref.at[pl.ds(d_ref[0, 0, k * tm + r], 1)],
                                      ybuf_ref.at[s, k, pl.ds(r, 1)], sems.at[s]).start(priority=k)
            return 0

        lax.fori_loop(0, tm, issue, 0, unroll=DMA_ISSUE_UNROLL)

    pl.when(i == 0)(lambda: gather(dest_ref, 0))
    pl.when(i + 1 < pl.num_programs(0))(lambda: gather(dest_next_ref, 1 - slot))
    for k in range(2):
        pltpu.make_async_copy(ys_ref.at[pl.ds(0, tm)], ybuf_ref.at[slot, k], sems.at[slot]).wait()
    p = p_ref[...]
    f = ybuf_ref[slot, 0] * p[:, 0:1] + ybuf_ref[slot, 1] * p[:, 1:2]
    x2 = x_ref[...] + mod_ref[0, 5:6, :] * f
    o_ref[...] = _rms(x2, g_ref[...])


def _combine(x1, mod, probs, final_g, ys, dest):
    T, D = x1.shape
    tm = TM_COMB
    per_b = SEQ // tm
    n = T // tm
    dest_t = dest.reshape(2, n, tm).transpose(1, 0, 2).reshape(n, 1, 2 * tm)
    return pl.pallas_call(
        _combine_kernel,
        grid=(n,),
        in_specs=[pl.BlockSpec((1, 1, 2 * tm), lambda i: (i, 0, 0), memory_space=pltpu.SMEM),
                  pl.BlockSpec((1, 1, 2 * tm), lambda i: (jnp.minimum(i + 1, n - 1), 0, 0),
                               memory_space=pltpu.SMEM),
                  pl.BlockSpec((tm, D), lambda i: (i, 0)),
                  pl.BlockSpec((1, 6, D), lambda i: (i // per_b, 0, 0)),
                  pl.BlockSpec((tm, 2), lambda i: (i, 0)),
                  pl.BlockSpec((1, D), lambda i: (0, 0)),
                  pl.BlockSpec(memory_space=pl.ANY)],
        out_specs=pl.BlockSpec((tm, D), lambda i: (i, 0)),
        out_shape=jax.ShapeDtypeStruct((T, D), F32),
        scratch_shapes=[pltpu.VMEM((2, 2, tm, D), F32), pltpu.SemaphoreType.DMA((2,))],
        compiler_params=_params("arbitrary"),
        name="moe_combine",
    )(dest_t, dest_t, x1, mod, probs, final_g, ys)


def _moe_layer(x1, h2, h2_packed, mod, w_router, w_gate, w_up, w_down, final_g):
    T, D = x1.shape
    e_idx, probs, rank, counts = _router(h2, w_router.T.astype(BF16))
    counts = counts[:, 0].astype(jnp.int32)
    padded = (counts + TM_MOE - 1) // TM_MOE * TM_MOE
    pad_end = jnp.cumsum(padded)
    pad_start = pad_end - padded
    eid = jnp.arange(N_EXPERTS, dtype=jnp.int32)[:, None, None]
    dest = jnp.sum(jnp.where(e_idx[None, :2] == eid, pad_start[:, None, None], 0), axis=0) + rank[:2]
    n_slots = 2 * T + N_EXPERTS * TM_MOE
    n_blk = n_slots // TM_MOE
    blk_first = jnp.arange(n_blk, dtype=jnp.int32) * TM_MOE
    blk_e = jnp.minimum(jnp.sum(pad_end[None, :] <= blk_first[:, None], axis=1),
                        N_EXPERTS - 1).astype(jnp.int32)
    n_used = (pad_end[-1:] // TM_MOE).astype(jnp.int32)
    xs = _dispatch(h2_packed, dest, n_slots)
    ys = _experts(xs, blk_e, n_used, w_gate.astype(BF16), w_up.astype(BF16), w_down.astype(BF16))
    return _combine(x1, mod, probs[:2].T, final_g.reshape(1, -1), ys, dest)


def _final_norm_kernel(x_ref, g_ref, o_ref):
    o_ref[...] = _rms(x_ref[...], g_ref[...])


def _final_norm(x, g):
    T, D = x.shape
    tm = 1024
    return pl.pallas_call(
        _final_norm_kernel,
        grid=(T // tm,),
        in_specs=[pl.BlockSpec((tm, D), lambda i: (i, 0)), pl.BlockSpec((1, D), lambda i: (0, 0))],
        out_specs=pl.BlockSpec((tm, D), lambda i: (i, 0)),
        out_shape=jax.ShapeDtypeStruct((T, D), F32),
        compiler_params=_params("arbitrary"),
        name="final_norm",
    )(x, g.reshape(1, -1))


def _regroup_w_in(w):
    ab = jnp.pad(w[:, C_GATE:C_GATE + AB_COLS], ((0, 0), (0, LANES - AB_COLS)))
    return jnp.concatenate([w[:, :C_GATE], w[:, C_GATE + AB_COLS:], ab], axis=1).astype(BF16)


def _lambda_init(layer):
    return 0.8 - 0.6 * math.exp(-0.3 * layer)


def kernel(x, c, positions, norm1_g, norm2_g, w_mod, b_mod, w_in, gm_ln_g, gm_ln_b, gm_w_s, gm_b_s, da_lambda, da_subln_g, dn_conv_w, dn_a_log, dn_dt_bias, dn_norm_g, w_br_gm, w_br_da, w_br_dn, w_out, ffn_w_gate, ffn_w_up, ffn_w_down, moe_w_router, moe_w_gate, moe_w_up, moe_w_down, final_g):
    B, S, D = x.shape
    T = B * S
    xt = x.reshape(T, D)
    mod_all = _modulation(c, w_mod, b_mod).reshape(DEPTH, B, 6, D)
    cos_t, sin_t = _rope_tables(positions)
    for layer in range(DEPTH):
        mod = mod_all[layer]
        gm, qk, v, dn, gates, ab = _in_projection(xt, mod, norm1_g[layer].reshape(1, D),
                                                  _regroup_w_in(w_in[layer]), cos_t, sin_t, dn_conv_w[layer])
        y_gm = _gmlp(gm, gm_ln_g[layer], gm_ln_b[layer], gm_w_s[layer], gm_b_s[layer])
        y_da = _diff_attention(qk, v, da_lambda[layer], da_subln_g[layer], _lambda_init(layer), B)
        y_dn = _deltanet(dn, ab, dn_a_log[layer], dn_dt_bias[layer], dn_norm_g[layer], B)
        moe = layer % 2 == 1
        x1, h2, *packed = _merge(xt, mod, norm2_g[layer].reshape(1, D), y_gm, y_da, y_dn, gates,
                                 w_br_gm[layer].astype(BF16), w_br_da[layer].astype(BF16),
                                 w_br_dn[layer].astype(BF16), w_out[layer].astype(BF16), moe)
        if moe:
            xt = _moe_layer(x1, h2, packed[0], mod, moe_w_router[layer // 2], moe_w_gate[layer // 2],
                            moe_w_up[layer // 2], moe_w_down[layer // 2], final_g)
        else:
            xt = _dense_ffn(x1, h2, mod, ffn_w_gate[layer // 2].astype(BF16),
                            ffn_w_up[layer // 2].astype(BF16), ffn_w_down[layer // 2].astype(BF16))
    if DEPTH % 2 == 1:
        xt = _final_norm(xt, final_g)
    return xt.reshape(B, S, D)
```
